```python
import numpy as np
import jax
import jax.numpy as jnp
from jax import lax

D_MODEL = 4096
BATCH = 4
SEQ = 2048
DEPTH = 4
DEC_BATCH = 8
DEC_SEQ = 1
PAST_LEN = 8192
PAGE_SIZE = 128

N_MIXERS = 4
N_LAYERS_A = (DEPTH + 3) // N_MIXERS
N_LAYERS_B = (DEPTH + 2) // N_MIXERS
N_LAYERS_C = (DEPTH + 1) // N_MIXERS
N_LAYERS_D = DEPTH // N_MIXERS
EPS = 1e-6
NEG_INF = -1e30
F32 = jnp.float32

BR_WIDTH = 3 * D_MODEL // 2

A_GROUPS = ((128, 1), (512, 4), (2048, 16))
A_N_GROUPS = len(A_GROUPS)
A_HEAD_DIM = 128
A_HEADS = D_MODEL // A_HEAD_DIM
A_WIDTH = A_HEADS * A_HEAD_DIM
A_QKV = 3 * A_N_GROUPS * A_WIDTH
A_IN = A_QKV + A_WIDTH
A_BLOCK = 128
NUM_BUCKETS = 32
MAX_DISTANCE = 2048

LRU_BLOCKS = 16
LRU_BS = BR_WIDTH // LRU_BLOCKS
CONV_W = 4
C_RG = 8.0

RET_DK = 256
RET_DV = 512
RET_HEADS = BR_WIDTH // RET_DV
RET_QK = RET_HEADS * RET_DK
RET_CHUNK = 128
RET_IN = 2 * RET_QK + 2 * BR_WIDTH
ROPE_BASE = 10000.0

GLA_DK = 256
GLA_DV = 512
GLA_HEADS = BR_WIDTH // GLA_DV
GLA_QK = GLA_HEADS * GLA_DK
GLA_RANK = 16
GLA_TAU = 16.0
GLA_CHUNK = 64
GLA_IN = 2 * GLA_QK + 2 * BR_WIDTH + GLA_RANK

kernel_name = 'hybrid_dilated_lru_retnet_gla_step'


def rmsnorm(x, gain):
    xf = x.astype(F32)
    y = xf * lax.rsqrt(jnp.mean(xf * xf, axis=-1, keepdims=True) + EPS)
    return (y * gain.astype(F32)).astype(x.dtype)


def head_rmsnorm(o, gain):
    y = o * lax.rsqrt(jnp.mean(o * o, axis=-1, keepdims=True) + EPS)
    return y * gain.astype(F32).reshape(o.shape[-2:])


def head_groupnorm(o, gain):
    c = o - jnp.mean(o, axis=-1, keepdims=True)
    y = c * lax.rsqrt(jnp.mean(c * c, axis=-1, keepdims=True) + EPS)
    return y * gain.astype(F32).reshape(o.shape[-2:])


def t5_bucket(dist):
    n = np.asarray(dist, dtype=np.int64)
    max_exact = NUM_BUCKETS // 2
    ratio = np.log(np.maximum(n, 1) / max_exact) / np.log(MAX_DISTANCE / max_exact)
    large = np.minimum(max_exact + (ratio * (NUM_BUCKETS - max_exact)).astype(np.int64), NUM_BUCKETS - 1)
    return np.where(n < max_exact, n, large).astype(np.int32)


def dilated_attn_prompt(q, k, v, bias, dilation, n_keys):
    b, s, h, dh = q.shape
    L = s // dilation
    nb = -(-L // A_BLOCK)
    lp = nb * A_BLOCK
    bd = b * dilation

    def to_blocks(x):
        x = x.astype(F32).reshape(b, L, dilation, h, dh).transpose(0, 2, 1, 3, 4).reshape(bd, L, h, dh)
        x = jnp.pad(x, ((0, 0), (0, lp - L), (0, 0), (0, 0)))
        return x.reshape(bd, nb, A_BLOCK, h, dh)

    def with_prev(x):
        prev = jnp.pad(x[:, :-1], ((0, 0), (1, 0), (0, 0), (0, 0), (0, 0)))
        return jnp.concatenate([prev, x], axis=2)

    qb = to_blocks(q)
    kk = with_prev(to_blocks(k))
    vv = with_prev(to_blocks(v))
    i = np.arange(A_BLOCK)[:, None]
    j = np.arange(2 * A_BLOCK)[None, :]
    rel = i + A_BLOCK - j
    in_band = (rel >= 0) & (rel <= n_keys)
    blk = np.arange(nb)[:, None, None]
    valid = in_band[None] & ((blk > 0) | (j >= A_BLOCK)[None])
    bias_bl = bias[:, np.clip(rel, 0, n_keys)]
    logits = jnp.einsum('bnihd,bnjhd->bnhij', qb, kk) * (dh ** -0.5) + bias_bl[None, None]
    logits = jnp.where(valid[None, :, None], logits, NEG_INF)
    m = jnp.max(logits, axis=-1)
    p = jnp.exp(logits - m[..., None])
    den = jnp.sum(p, axis=-1)
    num = jnp.einsum('bnhij,bnjhd->bnihd', p, vv)

    def from_blocks(x):
        rest = x.shape[3:]
        x = x.reshape((bd, lp) + rest)[:, :L]
        x = jnp.moveaxis(x.reshape((b, dilation, L) + rest), 1, 2)
        return x.reshape((b, s) + rest)

    return (from_blocks(num), from_blocks(m.transpose(0, 1, 3, 2)), from_blocks(den.transpose(0, 1, 3, 2)))


def dilated_attn_sample(q, k, v, k_buf, v_buf, bias, dilation, n_keys):
    lb = k_buf.shape[1]
    t = q.shape[1]
    kk = jnp.concatenate([k_buf.astype(F32), k.astype(F32)], axis=1)
    vv = jnp.concatenate([v_buf.astype(F32), v.astype(F32)], axis=1)
    idx = lb + np.arange(t)[:, None] - dilation * np.arange(n_keys + 1)[None, :]
    valid = idx >= 0
    idx = np.maximum(idx, 0)
    kg = kk[:, idx]
    vg = vv[:, idx]
    dh = q.shape[-1]
    logits = jnp.einsum('bthd,btkhd->bhtk', q.astype(F32), kg) * (dh ** -0.5) + bias[:, None, :]
    logits = jnp.where(valid[None, None], logits, NEG_INF)
    m = jnp.max(logits, axis=-1)
    p = jnp.exp(logits - m[..., None])
    den = jnp.sum(p, axis=-1)
    num = jnp.einsum('bhtk,btkhd->bthd', p, vg)
    return (num, m.transpose(0, 2, 1), den.transpose(0, 2, 1))


def mixer_dilated(h, w_in, w_out, rel_bias, k_bufs, v_bufs):
    b, t, _ = h.shape
    proj = h @ w_in
    qkv = proj[..., :A_QKV].reshape(b, t, 3, A_N_GROUPS, A_HEADS, A_HEAD_DIM)
    gate = proj[..., A_QKV:]
    parts, new_k, new_v = [], [], []
    for g, (window, dil) in enumerate(A_GROUPS):
        n_keys = window // dil
        q, k, v = qkv[:, :, 0, g], qkv[:, :, 1, g], qkv[:, :, 2, g]
        bias = rel_bias[t5_bucket(dil * np.arange(n_keys + 1)), g * A_HEADS:(g + 1) * A_HEADS].T.astype(F32)
        if k_bufs is None:
            parts.append(dilated_attn_prompt(q, k, v, bias, dil, n_keys))
            keep = min(window, t)
            new_k.append(k[:, t - keep:])
            new_v.append(v[:, t - keep:])
        else:
            parts.append(dilated_attn_sample(q, k, v, k_bufs[g], v_bufs[g], bias, dil, n_keys))
            new_k.append(k)
            new_v.append(v)
    nums = jnp.stack([p_[0] for p_ in parts])
    ms = jnp.stack([p_[1] for p_ in parts])
    dens = jnp.stack([p_[2] for p_ in parts])
    w = jnp.exp(ms - jnp.max(ms, axis=0, keepdims=True))
    o = jnp.sum(w[..., None] * nums, axis=0) / jnp.sum(w * dens, axis=0)[..., None]
    y = (o.reshape(b, t, A_WIDTH) * jax.nn.silu(gate.astype(F32))).astype(h.dtype)
    return y @ w_out, new_k, new_v


def lin_rec_combine(c1, c2):
    a1, b1 = c1
    a2, b2 = c2
    return a1 * a2, a2 * b1 + b2


def mixer_rglru(h, w_in, conv_w, conv_b, gate_a_w, gate_a_b, gate_x_w, gate_x_b, lam, w_out, h0, conv_buf, start_pos):
    b, t, _ = h.shape
    proj = h @ w_in
    xb, gate = proj[..., :BR_WIDTH], proj[..., BR_WIDTH:]
    if conv_buf is None:
        conv_buf = jnp.zeros((b, CONV_W - 1, BR_WIDTH), xb.dtype)
        h0 = jnp.zeros((b, BR_WIDTH), F32)
    xpad = jnp.concatenate([conv_buf.astype(xb.dtype), xb], axis=1)
    xpf = xpad.astype(F32)
    xc = conv_b.astype(F32) + sum(conv_w[w].astype(F32) * xpf[:, w:w + t] for w in range(CONV_W))
    new_buf = xpad[:, t:]
    xblk = xc.reshape(b, t, LRU_BLOCKS, LRU_BS)
    r = jax.nn.sigmoid(jnp.einsum('btni,nij->btnj', xblk, gate_a_w.astype(F32)).reshape(b, t, BR_WIDTH) + gate_a_b.astype(F32))
    ig = jax.nn.sigmoid(jnp.einsum('btni,nij->btnj', xblk, gate_x_w.astype(F32)).reshape(b, t, BR_WIDTH) + gate_x_b.astype(F32))
    log_a = -C_RG * r * jax.nn.softplus(-lam.astype(F32))
    a = jnp.exp(log_a)
    mult = jnp.sqrt(-jnp.expm1(2.0 * log_a))
    is_start = (jnp.arange(t) + start_pos) == 0
    mult = jnp.where(is_start[None, :, None], 1.0, mult)
    bx = mult * ig * xc
    bx = bx.at[:, 0].add(a[:, 0] * h0.astype(F32))
    _, hs = lax.associative_scan(lin_rec_combine, (a, bx), axis=1)
    y = (hs * jax.nn.silu(gate.astype(F32))).astype(h.dtype)
    return y @ w_out, hs[:, -1].astype(h.dtype), new_buf


def rope(x, pos):
    half = x.shape[-1] // 2
    inv_freq = ROPE_BASE ** (-jnp.arange(half, dtype=F32) / half)
    ang = pos[:, None] * inv_freq[None, :]
    cos = jnp.cos(ang)[None, :, None, :]
    sin = jnp.sin(ang)[None, :, None, :]
    x1, x2 = x[..., :half], x[..., half:]
    return jnp.concatenate([x1 * cos - x2 * sin, x1 * sin + x2 * cos], axis=-1)


def split_chunks(x, chunk):
    b, t = x.shape[:2]
    return jnp.moveaxis(x.reshape((b, t // chunk, chunk) + x.shape[2:]), 1, 0)


def retention_chunks(q, k, v, log_gamma, s0, chunk):
    b, t, nh, _ = q.shape
    dv = v.shape[-1]
    idx = jnp.arange(chunk, dtype=F32)
    diff = idx[:, None] - idx[None, :]
    decay = jnp.where(diff[None] >= 0, jnp.exp(diff[None] * log_gamma[:, None, None]), 0.0)
    q_dec = jnp.exp((idx[:, None] + 1.0) * log_gamma[None])
    k_dec = jnp.exp((chunk - 1.0 - idx)[:, None] * log_gamma[None])
    chunk_dec = jnp.exp(chunk * log_gamma)

    def step(s, xs):
        qc, kc, vc = xs
        scores = jnp.einsum('bihd,bjhd->bhij', qc, kc) * decay
        o = jnp.einsum('bhij,bjhe->bihe', scores, vc) + jnp.einsum('bihd,bhde->bihe', qc * q_dec[..., None], s)
        s = chunk_dec[:, None, None] * s + jnp.einsum('bjhd,bjhe->bhde', kc * k_dec[..., None], vc)
        return s, o

    s, o = lax.scan(step, s0, (split_chunks(q, chunk), split_chunks(k, chunk), split_chunks(v, chunk)))
    return jnp.moveaxis(o, 0, 1).reshape(b, t, nh, dv), s


def mixer_retention(h, w_in, gn_gain, w_out, s0, start_pos):
    b, t, _ = h.shape
    proj = h @ w_in
    q = proj[..., :RET_QK].reshape(b, t, RET_HEADS, RET_DK).astype(F32)
    k = proj[..., RET_QK:2 * RET_QK].reshape(b, t, RET_HEADS, RET_DK).astype(F32)
    v = proj[..., 2 * RET_QK:2 * RET_QK + BR_WIDTH].reshape(b, t, RET_HEADS, RET_DV).astype(F32)
    gate = proj[..., 2 * RET_QK + BR_WIDTH:]
    pos = jnp.arange(t, dtype=F32) + start_pos
    q = rope(q, pos)
    k = rope(k, pos) * (RET_DK ** -0.5)
    log_gamma = jnp.log1p(-jnp.exp2(-5.0 - jnp.arange(RET_HEADS, dtype=F32)))
    if s0 is None:
        s0 = jnp.zeros((b, RET_HEADS, RET_DK, RET_DV), F32)
    chunk = RET_CHUNK if t % RET_CHUNK == 0 else t
    o, s = retention_chunks(q, k, v, log_gamma, s0.astype(F32), chunk)
    y = (head_groupnorm(o, gn_gain).reshape(b, t, BR_WIDTH) * jax.nn.silu(gate.astype(F32))).astype(h.dtype)
    return y @ w_out, s.astype(h.dtype)


def gla_chunks(q, k, v, log_alpha, s0, chunk):
    b, t, nh, _ = q.shape
    dv = v.shape[-1]
    causal = np.tril(np.ones((chunk, chunk), dtype=bool))

    def step(s, xs):
        qc, kc, vc, gc = xs
        bcum = jnp.cumsum(gc, axis=1)
        qe = qc * jnp.exp(bcum)
        ke = kc * jnp.exp(-bcum)
        scores = jnp.where(causal[None, None], jnp.einsum('bihd,bjhd->bhij', qe, ke), 0.0)
        o = jnp.einsum('bhij,bjhe->bihe', scores, vc) + jnp.einsum('bihd,bhde->bihe', qe, s)
        blast = bcum[:, -1]
        s = jnp.exp(blast)[..., None] * s + jnp.einsum('bjhd,bjhe->bhde', kc * jnp.exp(blast[:, None] - bcum), vc)
        return s, o

    s, o = lax.scan(step, s0, (split_chunks(q, chunk), split_chunks(k, chunk), split_chunks(v, chunk), split_chunks(log_alpha, chunk)))
    return jnp.moveaxis(o, 0, 1).reshape(b, t, nh, dv), s


def mixer_gla(h, w_in, gate_w, gate_b, gn_gain, w_out, s0):
    b, t, _ = h.shape
    proj = h @ w_in
    q = proj[..., :GLA_QK].reshape(b, t, GLA_HEADS, GLA_DK).astype(F32) * (GLA_DK ** -0.5)
    k = proj[..., GLA_QK:2 * GLA_QK].reshape(b, t, GLA_HEADS, GLA_DK).astype(F32)
    v = proj[..., 2 * GLA_QK:2 * GLA_QK + BR_WIDTH].reshape(b, t, GLA_HEADS, GLA_DV).astype(F32)
    gate = proj[..., 2 * GLA_QK + BR_WIDTH:2 * GLA_QK + 2 * BR_WIDTH]
    low = proj[..., 2 * GLA_QK + 2 * BR_WIDTH:].astype(F32)
    z = low @ gate_w.astype(F32) + gate_b.astype(F32)
    log_alpha = (jax.nn.log_sigmoid(z) / GLA_TAU).reshape(b, t, GLA_HEADS, GLA_DK)
    if s0 is None:
        s0 = jnp.zeros((b, GLA_HEADS, GLA_DK, GLA_DV), F32)
    chunk = GLA_CHUNK if t % GLA_CHUNK == 0 else t
    o, s = gla_chunks(q, k, v, log_alpha, s0.astype(F32), chunk)
    y = (head_rmsnorm(o, gn_gain).reshape(b, t, BR_WIDTH) * jax.nn.silu(gate.astype(F32))).astype(h.dtype)
    return y @ w_out, s.astype(h.dtype)


def setup_inputs(seed: int = 0) -> dict:
    key = jax.random.key(seed)
    keys = jax.random.split(key, 48)
    ctr = [0]

    def nk():
        ctr[0] += 1
        return keys[ctr[0] - 1]

    def nrm(shape, scale):
        return jax.random.normal(nk(), shape, F32) * scale

    lb = [min(w, PAST_LEN) for w, _ in A_GROUPS]
    inputs = {}
    inputs['x_prompt'] = nrm((BATCH, SEQ, D_MODEL), 1.0)
    inputs['x_sample'] = nrm((DEC_BATCH, DEC_SEQ, D_MODEL), 1.0)
    for (w, _), n in zip(A_GROUPS, lb):
        inputs['cache_k_w%d' % w] = nrm((N_LAYERS_A, DEC_BATCH, n, A_HEADS, A_HEAD_DIM), 1.0)
        inputs['cache_v_w%d' % w] = nrm((N_LAYERS_A, DEC_BATCH, n, A_HEADS, A_HEAD_DIM), 1.0)
    inputs['state_lru_h'] = nrm((N_LAYERS_B, DEC_BATCH, BR_WIDTH), 1.0)
    inputs['state_lru_conv'] = nrm((N_LAYERS_B, DEC_BATCH, CONV_W - 1, BR_WIDTH), 1.0)
    inputs['state_ret'] = nrm((N_LAYERS_C, DEC_BATCH, RET_HEADS, RET_DK, RET_DV), 1.0)
    inputs['state_gla'] = nrm((N_LAYERS_D, DEC_BATCH, GLA_HEADS, GLA_DK, GLA_DV), 1.0)
    inputs['norm_pre'] = 1.0 + nrm((DEPTH, D_MODEL), 0.05)
    inputs['norm_post'] = 1.0 + nrm((DEPTH, D_MODEL), 0.05)
    inputs['rel_bias'] = nrm((NUM_BUCKETS, A_N_GROUPS * A_HEADS), 0.1)
    inputs['a_w_in'] = nrm((N_LAYERS_A, D_MODEL, A_IN), D_MODEL ** -0.5)
    inputs['a_w_out'] = nrm((N_LAYERS_A, A_WIDTH, D_MODEL), A_WIDTH ** -0.5)
    inputs['b_w_in'] = nrm((N_LAYERS_B, D_MODEL, 2 * BR_WIDTH), D_MODEL ** -0.5)
    inputs['b_conv_w'] = nrm((N_LAYERS_B, CONV_W, BR_WIDTH), CONV_W ** -0.5)
    inputs['b_conv_b'] = nrm((N_LAYERS_B, BR_WIDTH), 0.01)
    inputs['b_gate_a_w'] = nrm((N_LAYERS_B, LRU_BLOCKS, LRU_BS, LRU_BS), LRU_BS ** -0.5)
    inputs['b_gate_a_b'] = nrm((N_LAYERS_B, BR_WIDTH), 0.01)
    inputs['b_gate_x_w'] = nrm((N_LAYERS_B, LRU_BLOCKS, LRU_BS, LRU_BS), LRU_BS ** -0.5)
    inputs['b_gate_x_b'] = nrm((N_LAYERS_B, BR_WIDTH), 0.01)
    u = jax.random.uniform(nk(), (N_LAYERS_B, BR_WIDTH), F32, 0.9, 0.999)
    a_base = u ** (1.0 / C_RG)
    inputs['b_lambda'] = jnp.log(a_base) - jnp.log1p(-a_base)
    inputs['b_w_out'] = nrm((N_LAYERS_B, BR_WIDTH, D_MODEL), BR_WIDTH ** -0.5)
    inputs['c_w_in'] = nrm((N_LAYERS_C, D_MODEL, RET_IN), D_MODEL ** -0.5)
    inputs['c_norm'] = 1.0 + nrm((N_LAYERS_C, BR_WIDTH), 0.05)
    inputs['c_w_out'] = nrm((N_LAYERS_C, BR_WIDTH, D_MODEL), BR_WIDTH ** -0.5)
    inputs['d_w_in'] = nrm((N_LAYERS_D, D_MODEL, GLA_IN), D_MODEL ** -0.5)
    inputs['d_gate_w'] = nrm((N_LAYERS_D, GLA_RANK, GLA_QK), GLA_RANK ** -0.5)
    inputs['d_gate_b'] = nrm((N_LAYERS_D, GLA_QK), 0.01)
    inputs['d_norm'] = 1.0 + nrm((N_LAYERS_D, BR_WIDTH), 0.05)
    inputs['d_w_out'] = nrm((N_LAYERS_D, BR_WIDTH, D_MODEL), BR_WIDTH ** -0.5)
    return inputs


def reference(x_prompt, x_sample, cache_k_w128, cache_v_w128, cache_k_w512, cache_v_w512,
              cache_k_w2048, cache_v_w2048, state_lru_h, state_lru_conv, state_ret, state_gla,
              norm_pre, norm_post, rel_bias, a_w_in, a_w_out,
              b_w_in, b_conv_w, b_conv_b, b_gate_a_w, b_gate_a_b, b_gate_x_w, b_gate_x_b, b_lambda, b_w_out,
              c_w_in, c_norm, c_w_out, d_w_in, d_gate_w, d_gate_b, d_norm, d_w_out):
    k_caches = (cache_k_w128, cache_k_w512, cache_k_w2048)
    v_caches = (cache_v_w128, cache_v_w512, cache_v_w2048)
    xp, xs = x_prompt, x_sample
    kp_rows = [[] for _ in A_GROUPS]
    vp_rows = [[] for _ in A_GROUPS]
    ks_rows = [[] for _ in A_GROUPS]
    vs_rows = [[] for _ in A_GROUPS]
    lru_h_p, lru_h_s, lru_c_p, lru_c_s = [], [], [], []
    ret_p, ret_s, gla_p, gla_s = [], [], [], []
    for i in range(DEPTH):
        kind, j = i % N_MIXERS, i // N_MIXERS
        hp = rmsnorm(xp, norm_pre[i])
        hs = rmsnorm(xs, norm_pre[i])
        if kind == 0:
            yp, kn, vn = mixer_dilated(hp, a_w_in[j], a_w_out[j], rel_bias, None, None)
            ys, ksn, vsn = mixer_dilated(hs, a_w_in[j], a_w_out[j], rel_bias,
                                         tuple(c[j] for c in k_caches), tuple(c[j] for c in v_caches))
            for g in range(A_N_GROUPS):
                kp_rows[g].append(kn[g])
                vp_rows[g].append(vn[g])
                ks_rows[g].append(ksn[g])
                vs_rows[g].append(vsn[g])
        elif kind == 1:
            prm = (b_w_in[j], b_conv_w[j], b_conv_b[j], b_gate_a_w[j], b_gate_a_b[j],
                   b_gate_x_w[j], b_gate_x_b[j], b_lambda[j], b_w_out[j])
            yp, hn, cn = mixer_rglru(hp, *prm, None, None, 0)
            ys, hsn, csn = mixer_rglru(hs, *prm, state_lru_h[j], state_lru_conv[j], PAST_LEN)
            lru_h_p.append(hn)
            lru_c_p.append(cn)
            lru_h_s.append(hsn)
            lru_c_s.append(csn)
        elif kind == 2:
            yp, sn = mixer_retention(hp, c_w_in[j], c_norm[j], c_w_out[j], None, 0)
            ys, ssn = mixer_retention(hs, c_w_in[j], c_norm[j], c_w_out[j], state_ret[j], PAST_LEN)
            ret_p.append(sn)
            ret_s.append(ssn)
        else:
            yp, sn = mixer_gla(hp, d_w_in[j], d_gate_w[j], d_gate_b[j], d_norm[j], d_w_out[j], None)
            ys, ssn = mixer_gla(hs, d_w_in[j], d_gate_w[j], d_gate_b[j], d_norm[j], d_w_out[j], state_gla[j])
            gla_p.append(sn)
            gla_s.append(ssn)
        xp = xp + rmsnorm(yp, norm_post[i])
        xs = xs + rmsnorm(ys, norm_post[i])
    return (xp, xs,
            jnp.stack(kp_rows[0]), jnp.stack(ks_rows[0]), jnp.stack(vp_rows[0]), jnp.stack(vs_rows[0]),
            jnp.stack(kp_rows[1]), jnp.stack(ks_rows[1]), jnp.stack(vp_rows[1]), jnp.stack(vs_rows[1]),
            jnp.stack(kp_rows[2]), jnp.stack(ks_rows[2]), jnp.stack(vp_rows[2]), jnp.stack(vs_rows[2]),
            jnp.stack(lru_h_p), jnp.stack(lru_h_s), jnp.stack(lru_c_p), jnp.stack(lru_c_s),
            jnp.stack(ret_p), jnp.stack(ret_s), jnp.stack(gla_p), jnp.stack(gla_s))
```

```python
import functools

import numpy as np
import jax
import jax.numpy as jnp
from jax import lax
from jax.experimental import pallas as pl
from jax.experimental.pallas import tpu as pltpu

F32 = jnp.float32
BF16 = jnp.bfloat16

PAST_LEN = 8192
EPS = 1e-6
NEG_INF = -1e30
A_GROUPS = ((128, 1), (512, 4), (2048, 16))
A_HEAD_DIM = 128
A_BLOCK = 128
NUM_BUCKETS = 32
MAX_DISTANCE = 2048
C_RG = 8.0
RET_DK = 256
RET_DV = 512
RET_CHUNK = 128
ROPE_BASE = 10000.0
GLA_DK = 256
GLA_DV = 512
GLA_TAU = 16.0
GLA_CHUNK = 64

LANES = 128
SUBLANES = 8
V7X_VMEM_LIMIT_BYTES = 56 * 1024 * 1024

_ARB = pltpu.ARBITRARY


def _params(n_grid):
    return pltpu.CompilerParams(dimension_semantics=(_ARB,) * n_grid,
                                vmem_limit_bytes=V7X_VMEM_LIMIT_BYTES)


def _dot(a, b):
    return jnp.dot(a, b, preferred_element_type=F32)


def _dot_nt(a, b):
    return lax.dot_general(a, b, (((1,), (1,)), ((), ())), preferred_element_type=F32)


def _dot_tn(a, b):
    return lax.dot_general(a, b, (((0,), (0,)), ((), ())), preferred_element_type=F32)


def _sigmoid(x):
    return 1.0 / (1.0 + jnp.exp(-x))


def _silu(x):
    return x * _sigmoid(x)


def _softplus(x):
    return jnp.maximum(x, 0.0) + jnp.log1p(jnp.exp(-jnp.abs(x)))


def _rms(x, g):
    return x * lax.rsqrt(jnp.mean(x * x, axis=-1, keepdims=True) + EPS) * g


def _col_bcast(row, n):
    return jnp.broadcast_to(row, (n, n)).T


def _norm_kernel(x_ref, g_ref, h_ref):
    h_ref[...] = _rms(x_ref[...], g_ref[...]).astype(BF16)


def _resnorm_kernel(x_ref, y_ref, gpost_ref, gnext_ref, xo_ref, h_ref):
    xn = x_ref[...] + _rms(y_ref[...], gpost_ref[...])
    xo_ref[...] = xn
    h_ref[...] = _rms(xn, gnext_ref[...]).astype(BF16)


def _res_kernel(x_ref, y_ref, gpost_ref, xo_ref):
    xo_ref[...] = x_ref[...] + _rms(y_ref[...], gpost_ref[...])


def _row_tile(m):
    return min(m, 256)


def norm_cast(x, g):
    m, d = x.shape
    tm = _row_tile(m)
    return pl.pallas_call(
        _norm_kernel,
        grid=(m // tm,),
        in_specs=[pl.BlockSpec((tm, d), lambda i: (i, 0)),
                  pl.BlockSpec((1, d), lambda i: (0, 0))],
        out_specs=pl.BlockSpec((tm, d), lambda i: (i, 0)),
        out_shape=jax.ShapeDtypeStruct((m, d), BF16),
        compiler_params=_params(1),
        name="norm_cast",
    )(x, g.reshape(1, d))


def residual_norm(x, y, g_post, g_next):
    m, d = x.shape
    tm = _row_tile(m)
    row = pl.BlockSpec((tm, d), lambda i: (i, 0))
    vec = pl.BlockSpec((1, d), lambda i: (0, 0))
    if g_next is None:
        return pl.pallas_call(
            _res_kernel, grid=(m // tm,),
            in_specs=[row, row, vec], out_specs=row,
            out_shape=jax.ShapeDtypeStruct((m, d), F32),
            compiler_params=_params(1), name="residual",
        )(x, y, g_post.reshape(1, d)), None
    return pl.pallas_call(
        _resnorm_kernel, grid=(m // tm,),
        in_specs=[row, row, vec, vec], out_specs=[row, row],
        out_shape=[jax.ShapeDtypeStruct((m, d), F32), jax.ShapeDtypeStruct((m, d), BF16)],
        compiler_params=_params(1), name="residual_norm",
    )(x, y, g_post.reshape(1, d), g_next.reshape(1, d))


def _mm_kernel(x_ref, w_ref, xs_ref, o_ref, os_ref, wbf_ref):
    @pl.when(pl.program_id(1) == 0)
    def _():
        wbf_ref[...] = w_ref[...].astype(BF16)
        os_ref[...] = _dot(xs_ref[...], wbf_ref[...])

    o_ref[...] = _dot(x_ref[...], wbf_ref[...])


def matmul(x, xs, w):
    m, k = x.shape
    ms = xs.shape[0]
    n = w.shape[1]
    tm = min(m, 512)
    tn = 512 if k <= 4096 else 256
    return pl.pallas_call(
        _mm_kernel,
        grid=(pl.cdiv(n, tn), m // tm),
        in_specs=[pl.BlockSpec((tm, k), lambda j, i: (i, 0)),
                  pl.BlockSpec((k, tn), lambda j, i: (0, j)),
                  pl.BlockSpec((ms, k), lambda j, i: (0, 0))],
        out_specs=[pl.BlockSpec((tm, tn), lambda j, i: (i, j)),
                   pl.BlockSpec((ms, tn), lambda j, i: (0, j))],
        out_shape=[jax.ShapeDtypeStruct((m, n), F32), jax.ShapeDtypeStruct((ms, n), F32)],
        scratch_shapes=[pltpu.VMEM((k, tn), BF16)],
        compiler_params=_params(2),
        name="proj_matmul",
    )(x, w, xs)


def _t5_bucket(dist):
    n = np.asarray(dist, dtype=np.int64)
    max_exact = NUM_BUCKETS // 2
    ratio = np.log(np.maximum(n, 1) / max_exact) / np.log(MAX_DISTANCE / max_exact)
    large = np.minimum(max_exact + (ratio * (NUM_BUCKETS - max_exact)).astype(np.int64), NUM_BUCKETS - 1)
    return np.where(n < max_exact, n, large).astype(np.int32)


def _attn_prompt_kernel(*refs, dilations, seq):
    ng = len(dilations)
    qkv = refs[:3 * ng]
    gate_ref, bias_ref, y_ref, num_s, m_s, den_s = refs[3 * ng:]
    blk = A_BLOCK
    scale = A_HEAD_DIM ** -0.5
    ii = lax.broadcasted_iota(jnp.int32, (blk, blk), 0)
    jj = lax.broadcasted_iota(jnp.int32, (blk, blk), 1)
    cur_ok = jj <= ii
    prev_ok = jj >= ii

    for g, dil in enumerate(dilations):
        q_ref, k_ref, v_ref = qkv[3 * g:3 * g + 3]
        nb = seq // (dil * blk)
        b_prev = bias_ref[g, :, :blk]
        b_cur = bias_ref[g, :, blk:]

        def rows_at(start, dil=dil):
            if dil == 1:
                return pl.ds(pl.multiple_of(start, blk), blk)
            return pl.ds(start, blk, stride=dil)

        def body(idx, carry, g=g, dil=dil, nb=nb, q_ref=q_ref, k_ref=k_ref, v_ref=v_ref,
                 b_prev=b_prev, b_cur=b_cur, rows_at=rows_at):
            r = idx % dil
            bi = idx // dil
            start = r + bi * (blk * dil)
            rows = rows_at(start)
            q = q_ref[rows, :].astype(BF16)
            kc = k_ref[rows, :].astype(BF16)
            vc = v_ref[rows, :].astype(BF16)
            s_cur = jnp.where(cur_ok, _dot_nt(q, kc) * scale + b_cur, NEG_INF)
            m = jnp.max(s_cur, axis=-1, keepdims=True)
            if nb > 1:
                prows = rows_at(jnp.maximum(start - blk * dil, 0))
                kp = k_ref[prows, :].astype(BF16)
                vp = v_ref[prows, :].astype(BF16)
                s_prev = jnp.where(prev_ok & (bi > 0), _dot_nt(q, kp) * scale + b_prev, NEG_INF)
                m = jnp.maximum(m, jnp.max(s_prev, axis=-1, keepdims=True))
            p_cur = jnp.exp(s_cur - m)
            den = jnp.sum(p_cur, axis=-1, keepdims=True)
            num = _dot(p_cur.astype(BF16), vc)
            if nb > 1:
                p_prev = jnp.exp(s_prev - m)
                den = den + jnp.sum(p_prev, axis=-1, keepdims=True)
                num = num + _dot(p_prev.astype(BF16), vp)
            num_s[g, rows, :] = num
            m_s[g, rows, :] = jnp.broadcast_to(m, (blk, A_HEAD_DIM))
            den_s[g, rows, :] = jnp.broadcast_to(den, (blk, A_HEAD_DIM))
            return carry

        lax.fori_loop(0, seq // blk, body, 0)

    tr = 256

    def merge(c, carry):
        rows = pl.ds(pl.multiple_of(c * tr, tr), tr)
        ms = [m_s[g, rows, :] for g in range(ng)]
        mx = functools.reduce(jnp.maximum, ms)
        num = jnp.zeros((tr, A_HEAD_DIM), F32)
        den = jnp.zeros((tr, A_HEAD_DIM), F32)
        for g in range(ng):
            w = jnp.exp(ms[g] - mx)
            num = num + w * num_s[g, rows, :]
            den = den + w * den_s[g, rows, :]
        y_ref[rows, :] = ((num / den) * _silu(gate_ref[rows, :])).astype(BF16)
        return carry

    lax.fori_loop(0, seq // tr, merge, 0)


def attn_prompt(proj, bias_tab, n_heads):
    b, s, _ = proj.shape
    ng = len(A_GROUPS)
    dh = A_HEAD_DIM
    in_specs = []
    args = []
    for g in range(ng):
        for which in range(3):
            base = (which * ng + g) * n_heads
            in_specs.append(pl.BlockSpec((None, s, dh), lambda bi, hi, base=base: (bi, 0, base + hi)))
            args.append(proj)
    in_specs.append(pl.BlockSpec((None, s, dh), lambda bi, hi: (bi, 0, 3 * ng * n_heads + hi)))
    args.append(proj)
    in_specs.append(pl.BlockSpec((None, ng, A_BLOCK, 2 * A_BLOCK), lambda bi, hi: (hi, 0, 0, 0)))
    args.append(bias_tab)
    kern = functools.partial(_attn_prompt_kernel, dilations=tuple(d for _, d in A_GROUPS), seq=s)
    return pl.pallas_call(
        kern,
        grid=(b, n_heads),
        in_specs=in_specs,
        out_specs=pl.BlockSpec((None, s, dh), lambda bi, hi: (bi, 0, hi)),
        out_shape=jax.ShapeDtypeStruct((b, s, n_heads * dh), BF16),
        scratch_shapes=[pltpu.VMEM((ng, s, dh), F32)] * 3,
        compiler_params=_params(2),
        name="attn_prompt",
    )(*args)


def _attn_sample_kernel(*refs, n_heads):
    ng = len(A_GROUPS)
    p_ref = refs[0]
    caches = refs[1:1 + 2 * ng]
    bpast_ref, bself_ref, y_ref = refs[1 + 2 * ng:]
    dh = A_HEAD_DIM
    width = n_heads * dh
    scale = dh ** -0.5
    for h in range(n_heads):
        lanes = slice(h * dh, (h + 1) * dh)
        parts = []
        for g in range(ng):
            kc_ref, vc_ref = caches[2 * g], caches[2 * g + 1]
            q = p_ref[:, (0 * ng + g) * width + h * dh:(0 * ng + g) * width + (h + 1) * dh]
            kn = p_ref[:, (1 * ng + g) * width + h * dh:(1 * ng + g) * width + (h + 1) * dh]
            vn = p_ref[:, (2 * ng + g) * width + h * dh:(2 * ng + g) * width + (h + 1) * dh]
            lp = jnp.sum(kc_ref[:, lanes] * q, axis=-1, keepdims=True) * scale + bpast_ref[g, :, h:h + 1]
            ls = jnp.sum(kn * q, axis=-1, keepdims=True) * scale + bself_ref[g, :, h:h + 1]
            m = jnp.maximum(jnp.max(lp, axis=0, keepdims=True), ls)
            p = jnp.exp(lp - m)
            ps = jnp.exp(ls - m)
            den = jnp.sum(p, axis=0, keepdims=True) + ps
            num = jnp.sum(p * vc_ref[:, lanes], axis=0, keepdims=True) + ps * vn
            parts.append((num, m, den))
        mx = functools.reduce(jnp.maximum, [p_[1] for p_ in parts])
        num = jnp.zeros((1, dh), F32)
        den = jnp.zeros((1, 1), F32)
        for num_g, m_g, den_g in parts:
            w = jnp.exp(m_g - mx)
            num = num + w * num_g
            den = den + w * den_g
        gate = p_ref[:, 3 * ng * width + h * dh:3 * ng * width + (h + 1) * dh]
        y_ref[:, lanes] = ((num / den) * _silu(gate)).astype(BF16)


def attn_sample(proj_s, k_caches, v_caches, bias_past, bias_self, n_heads):
    bs, a_in = proj_s.shape
    width = n_heads * A_HEAD_DIM
    args = [proj_s.reshape(bs, 1, a_in)]
    in_specs = [pl.BlockSpec((None, 1, a_in), lambda bi: (bi, 0, 0))]
    for (window, dil), kc, vc in zip(A_GROUPS, k_caches, v_caches):
        n_keys = window // dil
        for c in (kc, vc):
            args.append(c.reshape(bs, n_keys, dil * width))
            in_specs.append(pl.BlockSpec((None, n_keys, width), lambda bi: (bi, 0, 0)))
    args += [bias_past, bias_self]
    in_specs += [pl.BlockSpec(bias_past.shape, lambda bi: (0, 0, 0)),
                 pl.BlockSpec(bias_self.shape, lambda bi: (0, 0, 0))]
    y = pl.pallas_call(
        functools.partial(_attn_sample_kernel, n_heads=n_heads),
        grid=(bs,),
        in_specs=in_specs,
        out_specs=pl.BlockSpec((None, 1, width), lambda bi: (bi, 0, 0)),
        out_shape=jax.ShapeDtypeStruct((bs, 1, width), BF16),
        compiler_params=_params(1),
        name="attn_sample",
    )(*args)
    return y.reshape(bs, width)


def _lru_gates(xc, wa_ref, ba_ref, wx_ref, bx_ref, lam_ref):
    xb = xc.astype(BF16)
    r = _sigmoid(_dot(xb, wa_ref[...].astype(BF16)) + ba_ref[...])
    ig = _sigmoid(_dot(xb, wx_ref[...].astype(BF16)) + bx_ref[...])
    log_a = -C_RG * r * _softplus(-lam_ref[...])
    a = jnp.exp(log_a)
    mult = jnp.sqrt(-jnp.tanh(log_a) * (jnp.exp(2.0 * log_a) + 1.0))
    return a, mult, ig


def _lru_prompt_kernel(x_ref, gate_ref, cw_ref, cb_ref, wa_ref, ba_ref, wx_ref, bx_ref, lam_ref,
                       y_ref, hl_ref, hc_ref, xp_ref, *, tt, conv_w):
    t = pl.program_id(2)

    @pl.when(t == 0)
    def _():
        hc_ref[...] = jnp.zeros_like(hc_ref)
        xp_ref[...] = jnp.zeros_like(xp_ref)

    x = x_ref[...]
    c = x.shape[1]
    row = lax.broadcasted_iota(jnp.int32, (tt, c), 0)
    row8 = lax.broadcasted_iota(jnp.int32, (SUBLANES, c), 0)
    prev8 = xp_ref[...]
    xc = cb_ref[...] + cw_ref[conv_w - 1:conv_w, :] * x
    for k in range(1, conv_w):
        xr = pltpu.roll(x, k, axis=0)
        first = jnp.where(row8 < k, pltpu.roll(prev8, k, axis=0), xr[:SUBLANES])
        xs = jnp.concatenate([first, xr[SUBLANES:]], axis=0)
        xc = xc + cw_ref[conv_w - 1 - k:conv_w - k, :] * xs
    xp_ref[...] = x[tt - SUBLANES:]

    a, mult, ig = _lru_gates(xc, wa_ref, ba_ref, wx_ref, bx_ref, lam_ref)
    mult = jnp.where(row + t * tt == 0, 1.0, mult)
    bx = mult * ig * xc
    bx = bx + jnp.where(row == 0, a * hc_ref[...], 0.0)
    d = 1
    while d < tt:
        keep = row >= d
        a_sh = jnp.where(keep, pltpu.roll(a, d, axis=0), 1.0)
        b_sh = jnp.where(keep, pltpu.roll(bx, d, axis=0), 0.0)
        bx = a * b_sh + bx
        a = a * a_sh
        d *= 2
    y_ref[...] = (bx * _silu(gate_ref[...])).astype(BF16)
    last = bx[tt - 1:tt]
    hc_ref[...] = last
    hl_ref[...] = last


def lru_prompt(proj, conv_w, conv_b, wa, ba, wx, bx, lam):
    b, s, two_br = proj.shape
    br = two_br // 2
    nblk, bs_, _ = wa.shape
    cw = conv_w.shape[0]
    tt = min(s, 256)
    vec = lambda a: a.reshape(1, br)
    vspec = pl.BlockSpec((1, bs_), lambda bi, ni, ti: (0, ni))
    wspec = pl.BlockSpec((None, bs_, bs_), lambda bi, ni, ti: (ni, 0, 0))
    y, hl = pl.pallas_call(
        functools.partial(_lru_prompt_kernel, tt=tt, conv_w=cw),
        grid=(b, nblk, s // tt),
        in_specs=[pl.BlockSpec((None, tt, bs_), lambda bi, ni, ti: (bi, ti, ni)),
                  pl.BlockSpec((None, tt, bs_), lambda bi, ni, ti: (bi, ti, nblk + ni)),
                  pl.BlockSpec((cw, bs_), lambda bi, ni, ti: (0, ni)),
                  vspec, wspec, vspec, wspec, vspec, vspec],
        out_specs=[pl.BlockSpec((None, tt, bs_), lambda bi, ni, ti: (bi, ti, ni)),
                   pl.BlockSpec((None, 1, bs_), lambda bi, ni, ti: (bi, 0, ni))],
        out_shape=[jax.ShapeDtypeStruct((b, s, br), BF16), jax.ShapeDtypeStruct((b, 1, br), F32)],
        scratch_shapes=[pltpu.VMEM((1, bs_), F32), pltpu.VMEM((SUBLANES, bs_), F32)],
        compiler_params=_params(3),
        name="lru_prompt",
    )(proj, proj, conv_w, vec(conv_b), wa, vec(ba), wx, vec(bx), vec(lam))
    return y, hl.reshape(b, br)


def _lru_sample_kernel(x_ref, gate_ref, buf_ref, h0_ref, cw_ref, cb_ref, wa_ref, ba_ref, wx_ref, bx_ref,
                       lam_ref, y_ref, h_ref, *, conv_w):
    x = x_ref[...]
    xc = cb_ref[...] + cw_ref[conv_w - 1:conv_w, :] * x
    for w in range(conv_w - 1):
        xc = xc + cw_ref[w:w + 1, :] * buf_ref[:, w, :]
    a, mult, ig = _lru_gates(xc, wa_ref, ba_ref, wx_ref, bx_ref, lam_ref)
    h = a * h0_ref[...] + mult * ig * xc
    h_ref[...] = h
    y_ref[...] = (h * _silu(gate_ref[...])).astype(BF16)


def lru_sample(proj_s, conv_buf, h0, conv_w, conv_b, wa, ba, wx, bx, lam):
    bs, two_br = proj_s.shape
    br = two_br // 2
    nblk, bs_, _ = wa.shape
    cw = conv_w.shape[0]
    vec = lambda a: a.reshape(1, br)
    vspec = pl.BlockSpec((1, bs_), lambda ni: (0, ni))
    wspec = pl.BlockSpec((None, bs_, bs_), lambda ni: (ni, 0, 0))
    rspec = pl.BlockSpec((bs, bs_), lambda ni: (0, ni))
    return pl.pallas_call(
        functools.partial(_lru_sample_kernel, conv_w=cw),
        grid=(nblk,),
        in_specs=[rspec,
                  pl.BlockSpec((bs, bs_), lambda ni: (0, nblk + ni)),
                  pl.BlockSpec((bs, cw - 1, bs_), lambda ni: (0, 0, ni)),
                  rspec,
                  pl.BlockSpec((cw, bs_), lambda ni: (0, ni)),
                  vspec, wspec, vspec, wspec, vspec, vspec],
        out_specs=[rspec, rspec],
        out_shape=[jax.ShapeDtypeStruct((bs, br), BF16), jax.ShapeDtypeStruct((bs, br), F32)],
        compiler_params=_params(1),
        name="lru_sample",
    )(proj_s, proj_s, conv_buf, h0, conv_w, vec(conv_b), wa, vec(ba), wx, vec(bx), vec(lam))


def _rope(x, cos, sin):
    half = x.shape[-1] // 2
    x1, x2 = x[:, :half], x[:, half:]
    return jnp.concatenate([x1 * cos - x2 * sin, x1 * sin + x2 * cos], axis=-1)


def _groupnorm(o, gain):
    c = o - jnp.mean(o, axis=-1, keepdims=True)
    return c * lax.rsqrt(jnp.mean(c * c, axis=-1, keepdims=True) + EPS) * gain


def _ret_prompt_kernel(q_ref, k_ref, v_ref, gate_ref, cos_ref, sin_ref, lg_ref, gain_ref,
                       y_ref, s_ref, st_ref, *, chunk, n_chunks):
    lg = lg_ref[:, :1]
    idx = lax.broadcasted_iota(jnp.int32, (chunk, 1), 0).astype(F32)
    ii = lax.broadcasted_iota(jnp.int32, (chunk, chunk), 0)
    jj = lax.broadcasted_iota(jnp.int32, (chunk, chunk), 1)
    diff = (ii - jj).astype(F32)
    decay = jnp.where(diff >= 0, jnp.exp(diff * lg), 0.0)
    q_dec = jnp.exp((idx + 1.0) * lg)
    k_dec = jnp.exp((chunk - 1.0 - idx) * lg)
    chunk_dec = jnp.exp(chunk * lg)
    gain = gain_ref[...]
    st_ref[...] = jnp.zeros_like(st_ref)

    def body(c, carry):
        rows = pl.ds(pl.multiple_of(c * chunk, chunk), chunk)
        cos = cos_ref[rows, :]
        sin = sin_ref[rows, :]
        qc = _rope(q_ref[rows, :], cos, sin)
        kc = _rope(k_ref[rows, :], cos, sin) * (RET_DK ** -0.5)
        vc = v_ref[rows, :].astype(BF16)
        st = st_ref[...]
        scores = _dot_nt(qc.astype(BF16), kc.astype(BF16)) * decay
        o = _dot(scores.astype(BF16), vc) + _dot((qc * q_dec).astype(BF16), st.astype(BF16))
        st_ref[...] = chunk_dec * st + _dot_tn((kc * k_dec).astype(BF16), vc)
        y_ref[rows, :] = (_groupnorm(o, gain) * _silu(gate_ref[rows, :])).astype(BF16)
        return carry

    lax.fori_loop(0, n_chunks, body, 0)
    s_ref[...] = st_ref[...]


def _ret_log_gamma(n_heads):
    lg = np.log1p(-np.exp2(-5.0 - np.arange(n_heads, dtype=np.float32))).astype(np.float32)
    return jnp.asarray(np.broadcast_to(lg[:, None, None], (n_heads, 1, LANES)).copy())


def _rope_tables(pos):
    half = RET_DK // 2
    inv_freq = ROPE_BASE ** (-jnp.arange(half, dtype=F32) / half)
    ang = pos[:, None] * inv_freq[None, :]
    return jnp.cos(ang), jnp.sin(ang)


def ret_prompt(proj, gain, n_heads):
    b, s, _ = proj.shape
    dk, dv = RET_DK, RET_DV
    qk = n_heads * dk
    chunk = RET_CHUNK if s % RET_CHUNK == 0 else s
    cos, sin = _rope_tables(jnp.arange(s, dtype=F32))
    half = dk // 2
    koff = qk // dk
    voff = 2 * qk // dv
    goff = (2 * qk + n_heads * dv) // dv
    return pl.pallas_call(
        functools.partial(_ret_prompt_kernel, chunk=chunk, n_chunks=s // chunk),
        grid=(b, n_heads),
        in_specs=[pl.BlockSpec((None, s, dk), lambda bi, hi: (bi, 0, hi)),
                  pl.BlockSpec((None, s, dk), lambda bi, hi: (bi, 0, koff + hi)),
                  pl.BlockSpec((None, s, dv), lambda bi, hi: (bi, 0, voff + hi)),
                  pl.BlockSpec((None, s, dv), lambda bi, hi: (bi, 0, goff + hi)),
                  pl.BlockSpec((s, half), lambda bi, hi: (0, 0)),
                  pl.BlockSpec((s, half), lambda bi, hi: (0, 0)),
                  pl.BlockSpec((None, 1, LANES), lambda bi, hi: (hi, 0, 0)),
                  pl.BlockSpec((1, dv), lambda bi, hi: (0, hi))],
        out_specs=[pl.BlockSpec((None, s, dv), lambda bi, hi: (bi, 0, hi)),
                   pl.BlockSpec((None, None, dk, dv), lambda bi, hi: (bi, hi, 0, 0))],
        out_shape=[jax.ShapeDtypeStruct((b, s, n_heads * dv), BF16),
                   jax.ShapeDtypeStruct((b, n_heads, dk, dv), F32)],
        scratch_shapes=[pltpu.VMEM((dk, dv), F32)],
        compiler_params=_params(2),
        name="ret_prompt",
    )(proj, proj, proj, proj, cos, sin, _ret_log_gamma(n_heads), gain.reshape(1, n_heads * dv))


def _ret_sample_kernel(q_ref, k_ref, v_ref, gate_ref, cos_ref, sin_ref, lg_ref, gain_ref, s0_ref,
                       y_ref, s_ref):
    dk, dv = RET_DK, RET_DV
    gamma = jnp.exp(lg_ref[:, :1])
    cos = cos_ref[...]
    sin = sin_ref[...]
    q = _rope(q_ref[...], cos, sin)
    k = _rope(k_ref[...], cos, sin) * (dk ** -0.5)
    v = v_ref[...]
    s0 = s0_ref[...]
    score = jnp.sum(q * k, axis=-1, keepdims=True)
    q8 = jnp.broadcast_to(q * gamma, (SUBLANES, dk)).astype(BF16)
    o = score * v + _dot(q8, s0.astype(BF16))[:1]
    kcol = _col_bcast(k, dk)
    for j in range(dv // dk):
        cols = slice(j * dk, (j + 1) * dk)
        s_ref[:, cols] = gamma * s0[:, cols] + kcol * v[:, cols]
    y_ref[...] = (_groupnorm(o, gain_ref[...]) * _silu(gate_ref[...])).astype(BF16)


def ret_sample(proj_s, gain, s0, n_heads):
    bs, ret_in = proj_s.shape
    dk, dv = RET_DK, RET_DV
    qk = n_heads * dk
    cos, sin = _rope_tables(jnp.full((1,), PAST_LEN, F32))
    half = dk // 2
    koff = qk // dk
    voff = 2 * qk // dv
    goff = (2 * qk + n_heads * dv) // dv
    p3 = proj_s.reshape(bs, 1, ret_in)
    y, s = pl.pallas_call(
        _ret_sample_kernel,
        grid=(bs, n_heads),
        in_specs=[pl.BlockSpec((None, 1, dk), lambda bi, hi: (bi, 0, hi)),
                  pl.BlockSpec((None, 1, dk), lambda bi, hi: (bi, 0, koff + hi)),
                  pl.BlockSpec((None, 1, dv), lambda bi, hi: (bi, 0, voff + hi)),
                  pl.BlockSpec((None, 1, dv), lambda bi, hi: (bi, 0, goff + hi)),
                  pl.BlockSpec((1, half), lambda bi, hi: (0, 0)),
                  pl.BlockSpec((1, half), lambda bi, hi: (0, 0)),
                  pl.BlockSpec((None, 1, LANES), lambda bi, hi: (hi, 0, 0)),
                  pl.BlockSpec((1, dv), lambda bi, hi: (0, hi)),
                  pl.BlockSpec((None, None, dk, dv), lambda bi, hi: (bi, hi, 0, 0))],
        out_specs=[pl.BlockSpec((None, 1, dv), lambda bi, hi: (bi, 0, hi)),
                   pl.BlockSpec((None, None, dk, dv), lambda bi, hi: (bi, hi, 0, 0))],
        out_shape=[jax.ShapeDtypeStruct((bs, 1, n_heads * dv), BF16),
                   jax.ShapeDtypeStruct((bs, n_heads, dk, dv), F32)],
        compiler_params=_params(2),
        name="ret_sample",
    )(p3, p3, p3, p3, cos, sin, _ret_log_gamma(n_heads), gain.reshape(1, n_heads * dv), s0)
    return y.reshape(bs, n_heads * dv), s


def _headnorm(o, gain):
    return o * lax.rsqrt(jnp.mean(o * o, axis=-1, keepdims=True) + EPS) * gain


def _gla_log_alpha(low_ref_val, gw_ref, gb_ref, rank):
    lane = lax.broadcasted_iota(jnp.int32, low_ref_val.shape, 1)
    low = jnp.where(lane < rank, low_ref_val, 0.0).astype(BF16)
    z = _dot(low, gw_ref[...].astype(BF16)) + gb_ref[...]
    return (jnp.minimum(z, 0.0) - jnp.log1p(jnp.exp(-jnp.abs(z)))) / GLA_TAU


def _gla_prompt_kernel(q_ref, k_ref, v_ref, gate_ref, low_ref, gw_ref, gb_ref, gain_ref,
                       y_ref, s_ref, st_ref, *, chunk, n_chunks, rank):
    dk, dv = GLA_DK, GLA_DV
    row = lax.broadcasted_iota(jnp.int32, (chunk, dk), 0)
    ii = lax.broadcasted_iota(jnp.int32, (chunk, chunk), 0)
    jj = lax.broadcasted_iota(jnp.int32, (chunk, chunk), 1)
    causal = jj <= ii
    gain = gain_ref[...]
    st_ref[...] = jnp.zeros_like(st_ref)

    def body(c, carry):
        rows = pl.ds(pl.multiple_of(c * chunk, chunk), chunk)
        bcum = _gla_log_alpha(low_ref[rows, :], gw_ref, gb_ref, rank)
        d = 1
        while d < chunk:
            bcum = bcum + jnp.where(row >= d, pltpu.roll(bcum, d, axis=0), 0.0)
            d *= 2
        qc = q_ref[rows, :] * (dk ** -0.5)
        kc = k_ref[rows, :]
        vc = v_ref[rows, :].astype(BF16)
        qe = (qc * jnp.exp(bcum)).astype(BF16)
        ke = (kc * jnp.exp(-bcum)).astype(BF16)
        st = st_ref[...]
        scores = jnp.where(causal, _dot_nt(qe, ke), 0.0)
        o = _dot(scores.astype(BF16), vc) + _dot(qe, st.astype(BF16))
        blast = bcum[chunk - 1:chunk]
        upd = _dot_tn((kc * jnp.exp(blast - bcum)).astype(BF16), vc)
        dec = _col_bcast(jnp.exp(blast), dk)
        for j in range(dv // dk):
            cols = slice(j * dk, (j + 1) * dk)
            st_ref[:, cols] = dec * st[:, cols] + upd[:, cols]
        y_ref[rows, :] = (_headnorm(o, gain) * _silu(gate_ref[rows, :])).astype(BF16)
        return carry

    lax.fori_loop(0, n_chunks, body, 0)
    s_ref[...] = st_ref[...]


def _gla_offsets(n_heads):
    dk, dv = GLA_DK, GLA_DV
    qk = n_heads * dk
    koff = qk // dk
    voff = 2 * qk // dv
    goff = (2 * qk + n_heads * dv) // dv
    loff = (2 * qk + 2 * n_heads * dv) // LANES
    return koff, voff, goff, loff


def _pad_rank(gate_w):
    rank = gate_w.shape[0]
    return jnp.pad(gate_w, ((0, LANES - rank), (0, 0)))


def gla_prompt(proj, gate_w, gate_b, gain, n_heads):
    b, s, _ = proj.shape
    dk, dv = GLA_DK, GLA_DV
    rank = gate_w.shape[0]
    chunk = GLA_CHUNK if s % GLA_CHUNK == 0 else s
    koff, voff, goff, loff = _gla_offsets(n_heads)
    return pl.pallas_call(
        functools.partial(_gla_prompt_kernel, chunk=chunk, n_chunks=s // chunk, rank=rank),
        grid=(b, n_heads),
        in_specs=[pl.BlockSpec((None, s, dk), lambda bi, hi: (bi, 0, hi)),
                  pl.BlockSpec((None, s, dk), lambda bi, hi: (bi, 0, koff + hi)),
                  pl.BlockSpec((None, s, dv), lambda bi, hi: (bi, 0, voff + hi)),
                  pl.BlockSpec((None, s, dv), lambda bi, hi: (bi, 0, goff + hi)),
                  pl.BlockSpec((None, s, LANES), lambda bi, hi: (bi, 0, loff)),
                  pl.BlockSpec((LANES, dk), lambda bi, hi: (0, hi)),
                  pl.BlockSpec((1, dk), lambda bi, hi: (0, hi)),
                  pl.BlockSpec((1, dv), lambda bi, hi: (0, hi))],
        out_specs=[pl.BlockSpec((None, s, dv), lambda bi, hi: (bi, 0, hi)),
                   pl.BlockSpec((None, None, dk, dv), lambda bi, hi: (bi, hi, 0, 0))],
        out_shape=[jax.ShapeDtypeStruct((b, s, n_heads * dv), BF16),
                   jax.ShapeDtypeStruct((b, n_heads, dk, dv), F32)],
        scratch_shapes=[pltpu.VMEM((dk, dv), F32)],
        compiler_params=_params(2),
        name="gla_prompt",
    )(proj, proj, proj, proj, proj, _pad_rank(gate_w), gate_b.reshape(1, -1), gain.reshape(1, -1))


def _gla_sample_kernel(q_ref, k_ref, v_ref, gate_ref, low_ref, gw_ref, gb_ref, gain_ref, s0_ref,
                       y_ref, s_ref, *, rank):
    dk, dv = GLA_DK, GLA_DV
    low8 = jnp.broadcast_to(low_ref[...], (SUBLANES, LANES))
    g = _gla_log_alpha(low8, gw_ref, gb_ref, rank)[:1]
    q = q_ref[...] * (dk ** -0.5)
    k = k_ref[...]
    v = v_ref[...]
    s0 = s0_ref[...]
    qe = q * jnp.exp(g)
    ke = k * jnp.exp(-g)
    score = jnp.sum(qe * ke, axis=-1, keepdims=True)
    qe8 = jnp.broadcast_to(qe, (SUBLANES, dk)).astype(BF16)
    o = score * v + _dot(qe8, s0.astype(BF16))[:1]
    dec = _col_bcast(jnp.exp(g), dk)
    kcol = _col_bcast(k * jnp.exp(g - g), dk)
    for j in range(dv // dk):
        cols = slice(j * dk, (j + 1) * dk)
        s_ref[:, cols] = dec * s0[:, cols] + kcol * v[:, cols]
    y_ref[...] = (_headnorm(o, gain_ref[...]) * _silu(gate_ref[...])).astype(BF16)


def gla_sample(proj_s, gate_w, gate_b, gain, s0, n_heads):
    bs, gla_in = proj_s.shape
    dk, dv = GLA_DK, GLA_DV
    rank = gate_w.shape[0]
    koff, voff, goff, loff = _gla_offsets(n_heads)
    p3 = proj_s.reshape(bs, 1, gla_in)
    y, s = pl.pallas_call(
        functools.partial(_gla_sample_kernel, rank=rank),
        grid=(bs, n_heads),
        in_specs=[pl.BlockSpec((None, 1, dk), lambda bi, hi: (bi, 0, hi)),
                  pl.BlockSpec((None, 1, dk), lambda bi, hi: (bi, 0, koff + hi)),
                  pl.BlockSpec((None, 1, dv), lambda bi, hi: (bi, 0, voff + hi)),
                  pl.BlockSpec((None, 1, dv), lambda bi, hi: (bi, 0, goff + hi)),
                  pl.BlockSpec((None, 1, LANES), lambda bi, hi: (bi, 0, loff)),
                  pl.BlockSpec((LANES, dk), lambda bi, hi: (0, hi)),
                  pl.BlockSpec((1, dk), lambda bi, hi: (0, hi)),
                  pl.BlockSpec((1, dv), lambda bi, hi: (0, hi)),
                  pl.BlockSpec((None, None, dk, dv), lambda bi, hi: (bi, hi, 0, 0))],
        out_specs=[pl.BlockSpec((None, 1, dv), lambda bi, hi: (bi, 0, hi)),
                   pl.BlockSpec((None, None, dk, dv), lambda bi, hi: (bi, hi, 0, 0))],
        out_shape=[jax.ShapeDtypeStruct((bs, 1, n_heads * dv), BF16),
                   jax.ShapeDtypeStruct((bs, n_heads, dk, dv), F32)],
        compiler_params=_params(2),
        name="gla_sample",
    )(p3, p3, p3, p3, p3, _pad_rank(gate_w), gate_b.reshape(1, -1), gain.reshape(1, -1), s0)
    return y.reshape(bs, n_heads * dv), s


def _attention_bias_tables(rel_bias, n_heads):
    blk = A_BLOCK
    rel = np.arange(blk)[:, None] + blk - np.arange(2 * blk)[None, :]
    prompt, past, self_ = [], [], []
    for g, (window, dil) in enumerate(A_GROUPS):
        n_keys = window // dil
        assert n_keys == blk, "one 128-key band per dilated stream is assumed"
        bias = rel_bias[_t5_bucket(dil * np.arange(n_keys + 1)), g * n_heads:(g + 1) * n_heads].T.astype(F32)
        prompt.append(bias[:, np.clip(rel, 0, n_keys)])
        past.append(bias[:, n_keys - np.arange(n_keys)].T)
        self_.append(bias[:, :1].T)
    return jnp.stack(prompt, axis=1), jnp.stack(past), jnp.stack(self_)


def kernel(x_prompt, x_sample, cache_k_w128, cache_v_w128, cache_k_w512, cache_v_w512, cache_k_w2048, cache_v_w2048, state_lru_h, state_lru_conv, state_ret, state_gla, norm_pre, norm_post, rel_bias, a_w_in, a_w_out, b_w_in, b_conv_w, b_conv_b, b_gate_a_w, b_gate_a_b, b_gate_x_w, b_gate_x_b, b_lambda, b_w_out, c_w_in, c_norm, c_w_out, d_w_in, d_gate_w, d_gate_b, d_norm, d_w_out):
    b, s, d = x_prompt.shape
    bs = x_sample.shape[0]
    assert x_sample.shape[1] == 1, "one new token per sequence"
    depth = norm_pre.shape[0]
    k_caches = (cache_k_w128, cache_k_w512, cache_k_w2048)
    v_caches = (cache_v_w128, cache_v_w512, cache_v_w2048)
    ng = len(A_GROUPS)
    for (window, dil), kc in zip(A_GROUPS, k_caches):
        assert kc.shape[2] == window and s % (dil * A_BLOCK) == 0

    xp = x_prompt.reshape(b * s, d)
    xs = x_sample.reshape(bs, d)
    hp = norm_cast(xp, norm_pre[0])
    hs = norm_cast(xs, norm_pre[0])

    kp_rows = [[] for _ in A_GROUPS]
    vp_rows = [[] for _ in A_GROUPS]
    ks_rows = [[] for _ in A_GROUPS]
    vs_rows = [[] for _ in A_GROUPS]
    lru_h_p, lru_h_s, lru_c_p, lru_c_s = [], [], [], []
    ret_p, ret_s, gla_p, gla_s = [], [], [], []

    for i in range(depth):
        kind, j = i % 4, i // 4
        if kind == 0:
            n_heads = a_w_out.shape[1] // A_HEAD_DIM
            width = n_heads * A_HEAD_DIM
            proj, proj_s = matmul(hp, hs, a_w_in[j])
            bias_tab, bias_past, bias_self = _attention_bias_tables(rel_bias, n_heads)
            proj3 = proj.reshape(b, s, -1)
            yp = attn_prompt(proj3, bias_tab, n_heads).reshape(b * s, width)
            ys = attn_sample(proj_s, [c[j] for c in k_caches], [c[j] for c in v_caches],
                             bias_past, bias_self, n_heads)
            for g, (window, _) in enumerate(A_GROUPS):
                keep = min(window, s)
                k_off = (ng + g) * width
                v_off = (2 * ng + g) * width
                kp_rows[g].append(proj3[:, s - keep:, k_off:k_off + width].reshape(b, keep, n_heads, A_HEAD_DIM))
                vp_rows[g].append(proj3[:, s - keep:, v_off:v_off + width].reshape(b, keep, n_heads, A_HEAD_DIM))
                ks_rows[g].append(proj_s[:, k_off:k_off + width].reshape(bs, 1, n_heads, A_HEAD_DIM))
                vs_rows[g].append(proj_s[:, v_off:v_off + width].reshape(bs, 1, n_heads, A_HEAD_DIM))
            w_out = a_w_out[j]
        elif kind == 1:
            br = b_w_out.shape[1]
            proj, proj_s = matmul(hp, hs, b_w_in[j])
            prm = (b_conv_w[j], b_conv_b[j], b_gate_a_w[j], b_gate_a_b[j], b_gate_x_w[j], b_gate_x_b[j], b_lambda[j])
            proj3 = proj.reshape(b, s, -1)
            yp, h_last = lru_prompt(proj3, *prm)
            yp = yp.reshape(b * s, br)
            ys, h_new = lru_sample(proj_s, state_lru_conv[j], state_lru_h[j], *prm)
            cw = b_conv_w.shape[1]
            lru_h_p.append(h_last)
            lru_c_p.append(proj3[:, s - (cw - 1):, :br])
            lru_h_s.append(h_new)
            lru_c_s.append(jnp.concatenate([state_lru_conv[j], proj_s[:, None, :br]], axis=1)[:, 1:])
            w_out = b_w_out[j]
        elif kind == 2:
            br = c_w_out.shape[1]
            n_heads = br // RET_DV
            proj, proj_s = matmul(hp, hs, c_w_in[j])
            yp, st = ret_prompt(proj.reshape(b, s, -1), c_norm[j], n_heads)
            yp = yp.reshape(b * s, br)
            ys, st_s = ret_sample(proj_s, c_norm[j], state_ret[j], n_heads)
            ret_p.append(st)
            ret_s.append(st_s)
            w_out = c_w_out[j]
        else:
            br = d_w_out.shape[1]
            n_heads = br // GLA_DV
            proj, proj_s = matmul(hp, hs, d_w_in[j])
            yp, st = gla_prompt(proj.reshape(b, s, -1), d_gate_w[j], d_gate_b[j], d_norm[j], n_heads)
            yp = yp.reshape(b * s, br)
            ys, st_s = gla_sample(proj_s, d_gate_w[j], d_gate_b[j], d_norm[j], state_gla[j], n_heads)
            gla_p.append(st)
            gla_s.append(st_s)
            w_out = d_w_out[j]
        op, os_ = matmul(yp, ys, w_out)
        g_next = norm_pre[i + 1] if i + 1 < depth else None
        xp, hp = residual_norm(xp, op, norm_post[i], g_next)
        xs, hs = residual_norm(xs, os_, norm_post[i], g_next)

    return (xp.reshape(b, s, d), xs.reshape(bs, 1, d),
            jnp.stack(kp_rows[0]), jnp.stack(ks_rows[0]), jnp.stack(vp_rows[0]), jnp.stack(vs_rows[0]),
            jnp.stack(kp_rows[1]), jnp.stack(ks_rows[1]), jnp.stack(vp_rows[1]), jnp.stack(vs_rows[1]),
            jnp.stack(kp_rows[2]), jnp.stack(ks_rows[2]), jnp.stack(vp_rows[2]), jnp.stack(vs_rows[2]),
            jnp.stack(lru_h_p), jnp.stack(lru_h_s), jnp.stack(lru_c_p), jnp.stack(lru_c_s),
            jnp.stack(ret_p), jnp.stack(ret_s), jnp.stack(gla_p), jnp.stack(gla_s))
```

```python
import functools

import numpy as np
import jax
import jax.numpy as jnp
from jax import lax
from jax.experimental import pallas as pl
from jax.experimental.pallas import tpu as pltpu

F32 = jnp.float32
BF16 = jnp.bfloat16

PAST_LEN = 8192
EPS = 1e-6
NEG_INF = -1e30
A_GROUPS = ((128, 1), (512, 4), (2048, 16))
A_HEAD_DIM = 128
A_BLOCK = 128
ATTN_UNROLL = 8
NUM_BUCKETS = 32
MAX_DISTANCE = 2048
C_RG = 8.0
RET_DK = 256
RET_DV = 512
RET_CHUNK = 128
ROPE_BASE = 10000.0
GLA_DK = 256
GLA_DV = 512
GLA_TAU = 16.0
GLA_CHUNK = 64

LANES = 128
SUBLANES = 8
V7X_VMEM_LIMIT_BYTES = 56 * 1024 * 1024

_ARB = pltpu.ARBITRARY


def _params(n_grid):
    return pltpu.CompilerParams(dimension_semantics=(_ARB,) * n_grid,
                                vmem_limit_bytes=V7X_VMEM_LIMIT_BYTES)


def _dot(a, b):
    return jnp.dot(a, b, preferred_element_type=F32)


def _dot_nt(a, b):
    return lax.dot_general(a, b, (((1,), (1,)), ((), ())), preferred_element_type=F32)


def _dot_tn(a, b):
    return lax.dot_general(a, b, (((0,), (0,)), ((), ())), preferred_element_type=F32)


def _sigmoid(x):
    return 0.5 * jnp.tanh(0.5 * x) + 0.5


def _silu(x):
    return x * _sigmoid(x)


def _softplus(x):
    return jnp.maximum(x, 0.0) + jnp.log1p(jnp.exp(-jnp.abs(x)))


def _rms(x, g):
    return x * lax.rsqrt(jnp.mean(x * x, axis=-1, keepdims=True) + EPS) * g


def _col_bcast(row, n):
    return jnp.broadcast_to(row, (n, n)).T


def _norm_kernel(x_ref, g_ref, h_ref):
    h_ref[...] = _rms(x_ref[...], g_ref[...]).astype(BF16)


def _resnorm_kernel(x_ref, y_ref, gpost_ref, gnext_ref, xo_ref, h_ref):
    xn = x_ref[...] + _rms(y_ref[...], gpost_ref[...])
    xo_ref[...] = xn
    h_ref[...] = _rms(xn, gnext_ref[...]).astype(BF16)


def _res_kernel(x_ref, y_ref, gpost_ref, xo_ref):
    xo_ref[...] = x_ref[...] + _rms(y_ref[...], gpost_ref[...])


def _row_tile(m):
    return min(m, 256)


def norm_cast(x, g):
    m, d = x.shape
    tm = _row_tile(m)
    return pl.pallas_call(
        _norm_kernel,
        grid=(m // tm,),
        in_specs=[pl.BlockSpec((tm, d), lambda i: (i, 0)),
                  pl.BlockSpec((1, d), lambda i: (0, 0))],
        out_specs=pl.BlockSpec((tm, d), lambda i: (i, 0)),
        out_shape=jax.ShapeDtypeStruct((m, d), BF16),
        compiler_params=_params(1),
        name="norm_cast",
    )(x, g.reshape(1, d))


def residual_norm(x, y, g_post, g_next):
    m, d = x.shape
    tm = _row_tile(m)
    row = pl.BlockSpec((tm, d), lambda i: (i, 0))
    vec = pl.BlockSpec((1, d), lambda i: (0, 0))
    if g_next is None:
        return pl.pallas_call(
            _res_kernel, grid=(m // tm,),
            in_specs=[row, row, vec], out_specs=row,
            out_shape=jax.ShapeDtypeStruct((m, d), F32),
            compiler_params=_params(1), name="residual",
        )(x, y, g_post.reshape(1, d)), None
    return pl.pallas_call(
        _resnorm_kernel, grid=(m // tm,),
        in_specs=[row, row, vec, vec], out_specs=[row, row],
        out_shape=[jax.ShapeDtypeStruct((m, d), F32), jax.ShapeDtypeStruct((m, d), BF16)],
        compiler_params=_params(1), name="residual_norm",
    )(x, y, g_post.reshape(1, d), g_next.reshape(1, d))


def _mm_kernel(x_ref, w_ref, xs_ref, o_ref, os_ref, *, transposed):
    w = w_ref[...].astype(BF16)
    dot = _dot_nt if transposed else _dot
    o_ref[...] = dot(x_ref[...], w)

    @pl.when(pl.program_id(0) == 0)
    def _():
        os_ref[...] = dot(xs_ref[...], w)


MM_COL_TILE = 512
MM_ROW_TILE = 2048
MM_ROW_TILE_LONG_K = 1024
MM_LONG_K = 4096


def matmul(x, xs, w, layer, col0=0, ncols=None, transposed=False):
    m, k = x.shape
    ms = xs.shape[0]
    n = w.shape[1] if transposed else w.shape[2]
    ncols = n - col0 if ncols is None else ncols
    tn = MM_COL_TILE
    assert col0 % tn == 0
    tm = min(m, MM_ROW_TILE if k <= MM_LONG_K else MM_ROW_TILE_LONG_K)
    nj = pl.cdiv(ncols, tn)
    c0 = col0 // tn
    if transposed:
        w_spec = pl.BlockSpec((None, tn, k), lambda i, j: (layer, c0 + j, 0))
    else:
        w_spec = pl.BlockSpec((None, k, tn), lambda i, j: (layer, 0, c0 + j))
    return pl.pallas_call(
        functools.partial(_mm_kernel, transposed=transposed),
        grid=(m // tm, nj),
        in_specs=[pl.BlockSpec((tm, k), lambda i, j: (i, 0), pipeline_mode=pl.Buffered(1)),
                  w_spec,
                  pl.BlockSpec((ms, k), lambda i, j: (0, 0))],
        out_specs=[pl.BlockSpec((tm, tn), lambda i, j: (i, j)),
                   pl.BlockSpec((ms, tn), lambda i, j: (0, jnp.where(i == 0, j, nj - 1)))],
        out_shape=[jax.ShapeDtypeStruct((m, ncols), F32), jax.ShapeDtypeStruct((ms, ncols), F32)],
        compiler_params=_params(2),
        name="proj_matmul",
    )(x, w, xs)


def _t5_bucket(dist):
    n = np.asarray(dist, dtype=np.int64)
    max_exact = NUM_BUCKETS // 2
    ratio = np.log(np.maximum(n, 1) / max_exact) / np.log(MAX_DISTANCE / max_exact)
    large = np.minimum(max_exact + (ratio * (NUM_BUCKETS - max_exact)).astype(np.int64), NUM_BUCKETS - 1)
    return np.where(n < max_exact, n, large).astype(np.int32)


def _attn_prompt_kernel(*refs, dilations, seq):
    ng = len(dilations)
    qkv = refs[:3 * ng]
    gate_ref, bias_ref, y_ref, o_s, lse_s = refs[3 * ng:]
    blk = A_BLOCK
    scale = A_HEAD_DIM ** -0.5
    ii = lax.broadcasted_iota(jnp.int32, (blk, blk), 0)
    jj = lax.broadcasted_iota(jnp.int32, (blk, blk), 1)
    cur_ok = jj <= ii
    prev_ok = jj >= ii

    for g, dil in enumerate(dilations):
        q_ref, k_ref, v_ref = qkv[3 * g:3 * g + 3]
        nb = seq // (dil * blk)
        band = pltpu.roll(jnp.broadcast_to(bias_ref[g:g + 1, :], (blk, 2 * blk)), 0, axis=1,
                          stride=1, stride_axis=0)
        b_prev = band[:, :blk]
        b_cur = band[:, blk:]

        def rows_at(start, dil=dil):
            if dil == 1:
                return pl.ds(pl.multiple_of(start, blk), blk)
            return pl.ds(start, blk, stride=dil)

        def scores(idx, dil=dil, nb=nb, q_ref=q_ref, k_ref=k_ref, b_prev=b_prev, b_cur=b_cur,
                   rows_at=rows_at):
            r = idx % dil
            bi = idx // dil
            start = r + bi * (blk * dil)
            rows = rows_at(start)
            q = q_ref[rows, :].astype(BF16)
            s = [jnp.where(cur_ok, _dot_nt(q, k_ref[rows, :].astype(BF16)) * scale + b_cur, NEG_INF)]
            prows = None
            if nb > 1:
                prows = rows_at(jnp.maximum(start - blk * dil, 0))
                s.append(jnp.where(prev_ok & (bi > 0),
                                   _dot_nt(q, k_ref[prows, :].astype(BF16)) * scale + b_prev, NEG_INF))
            return rows, prows, s

        def softmax(s):
            m = jnp.max(functools.reduce(jnp.maximum, s), axis=-1, keepdims=True)
            p = [jnp.exp(x - m) for x in s]
            den = jnp.sum(functools.reduce(jnp.add, p), axis=-1, keepdims=True)
            return m, den, [x.astype(BF16) for x in p]

        def values(rows, prows, p, v_ref=v_ref):
            num = _dot(p[0], v_ref[rows, :].astype(BF16))
            if prows is not None:
                num = num + _dot(p[1], v_ref[prows, :].astype(BF16))
            return num

        def body(it, carry, g=g, scores=scores, softmax=softmax, values=values):
            blocks = [scores(it * ATTN_UNROLL + u) for u in range(ATTN_UNROLL)]
            probs = [softmax(s) for _, _, s in blocks]
            nums = [values(rows, prows, p) for (rows, prows, _), (_, _, p) in zip(blocks, probs)]
            for (rows, _, _), (m, den, _), num in zip(blocks, probs, nums):
                o_s[g, rows, :] = num / den
                lse_s[g, rows, :] = jnp.broadcast_to(m + jnp.log(den), (blk, A_HEAD_DIM))
            return carry

        lax.fori_loop(0, seq // (blk * ATTN_UNROLL), body, 0)

    tr = 256

    def merge(c, carry):
        rows = pl.ds(pl.multiple_of(c * tr, tr), tr)
        lses = [lse_s[g, rows, :] for g in range(ng)]
        mx = functools.reduce(jnp.maximum, lses)
        num = jnp.zeros((tr, A_HEAD_DIM), F32)
        den = jnp.zeros((tr, A_HEAD_DIM), F32)
        for g in range(ng):
            w = jnp.exp(lses[g] - mx)
            num = num + w * o_s[g, rows, :]
            den = den + w
        y_ref[rows, :] = ((num / den) * _silu(gate_ref[rows, :])).astype(BF16)
        return carry

    lax.fori_loop(0, seq // tr, merge, 0)


def attn_prompt(q, ks, vs, gate, bias_tab, n_heads):
    b, s, _ = gate.shape
    ng = len(A_GROUPS)
    dh = A_HEAD_DIM
    in_specs = []
    args = []
    for g in range(ng):
        in_specs.append(pl.BlockSpec((None, s, dh), lambda bi, hi, g=g: (bi, 0, g * n_heads + hi)))
        args.append(q)
        for a in (ks[g], vs[g]):
            in_specs.append(pl.BlockSpec((None, s, dh), lambda bi, hi: (bi, 0, hi)))
            args.append(a)
    in_specs.append(pl.BlockSpec((None, s, dh), lambda bi, hi: (bi, 0, hi)))
    args.append(gate)
    in_specs.append(pl.BlockSpec((None, ng, 2 * A_BLOCK), lambda bi, hi: (hi, 0, 0)))
    args.append(bias_tab)
    kern = functools.partial(_attn_prompt_kernel, dilations=tuple(d for _, d in A_GROUPS), seq=s)
    return pl.pallas_call(
        kern,
        grid=(b, n_heads),
        in_specs=in_specs,
        out_specs=pl.BlockSpec((None, s, dh), lambda bi, hi: (bi, 0, hi)),
        out_shape=jax.ShapeDtypeStruct((b, s, n_heads * dh), BF16),
        scratch_shapes=[pltpu.VMEM((ng, s, dh), F32)] * 2,
        compiler_params=_params(2),
        name="attn_prompt",
    )(*args)


def _attn_sample_kernel(*refs, n_heads):
    ng = len(A_GROUPS)
    q_ref, gate_ref = refs[0], refs[1]
    new = refs[2:2 + 2 * ng]
    caches = refs[2 + 2 * ng:2 + 4 * ng]
    bpast_ref, bself_ref, y_ref = refs[2 + 4 * ng:]
    dh = A_HEAD_DIM
    width = n_heads * dh
    scale = dh ** -0.5
    for h in range(n_heads):
        lanes = slice(h * dh, (h + 1) * dh)
        parts = []
        for g in range(ng):
            kc_ref, vc_ref = caches[2 * g], caches[2 * g + 1]
            q = q_ref[:, g * width + h * dh:g * width + (h + 1) * dh]
            kn = new[2 * g][:, lanes]
            vn = new[2 * g + 1][:, lanes]
            lp = jnp.sum(kc_ref[:, h, :] * q, axis=-1, keepdims=True) * scale + bpast_ref[g, :, h:h + 1]
            ls = jnp.sum(kn * q, axis=-1, keepdims=True) * scale + bself_ref[g, :, h:h + 1]
            m = jnp.maximum(jnp.max(lp, axis=0, keepdims=True), ls)
            p = jnp.exp(lp - m)
            ps = jnp.exp(ls - m)
            den = jnp.sum(p, axis=0, keepdims=True) + ps
            num = jnp.sum(p * vc_ref[:, h, :], axis=0, keepdims=True) + ps * vn
            parts.append((num, m, den))
        mx = functools.reduce(jnp.maximum, [p_[1] for p_ in parts])
        num = jnp.zeros((1, dh), F32)
        den = jnp.zeros((1, 1), F32)
        for num_g, m_g, den_g in parts:
            w = jnp.exp(m_g - mx)
            num = num + w * num_g
            den = den + w * den_g
        y_ref[:, lanes] = ((num / den) * _silu(gate_ref[:, lanes])).astype(BF16)


def attn_sample(q, ks, vs, gate, k_caches, v_caches, bias_past, bias_self, n_heads):
    bs, width = gate.shape
    dh = A_HEAD_DIM
    row = lambda a: a.reshape(bs, 1, a.shape[-1])
    row_spec = lambda a: pl.BlockSpec((None, 1, a.shape[-1]), lambda bi: (bi, 0, 0))
    args = [row(q), row(gate)]
    in_specs = [row_spec(q), row_spec(gate)]
    for g in range(len(A_GROUPS)):
        for a in (ks[g], vs[g]):
            args.append(row(a))
            in_specs.append(row_spec(a))
    for (window, dil), kc, vc in zip(A_GROUPS, k_caches, v_caches):
        n_keys = window // dil
        for c in (kc, vc):
            args.append(c.reshape(bs, n_keys, dil, n_heads, dh))
            in_specs.append(pl.BlockSpec((None, n_keys, None, n_heads, dh), lambda bi: (bi, 0, 0, 0, 0)))
    args += [bias_past, bias_self]
    in_specs += [pl.BlockSpec(bias_past.shape, lambda bi: (0, 0, 0)),
                 pl.BlockSpec(bias_self.shape, lambda bi: (0, 0, 0))]
    y = pl.pallas_call(
        functools.partial(_attn_sample_kernel, n_heads=n_heads),
        grid=(bs,),
        in_specs=in_specs,
        out_specs=pl.BlockSpec((None, 1, width), lambda bi: (bi, 0, 0)),
        out_shape=jax.ShapeDtypeStruct((bs, 1, width), BF16),
        compiler_params=_params(1),
        name="attn_sample",
    )(*args)
    return y.reshape(bs, width)


def _lru_gates(xc, wa_ref, ba_ref, wx_ref, bx_ref, lam_ref):
    xb = xc.astype(BF16)
    r = _sigmoid(_dot(xb, wa_ref[...].astype(BF16)) + ba_ref[...])
    ig = _sigmoid(_dot(xb, wx_ref[...].astype(BF16)) + bx_ref[...])
    log_a = -C_RG * r * _softplus(-lam_ref[...])
    a = jnp.exp(log_a)
    mult = jnp.sqrt(-jnp.tanh(log_a) * (a * a + 1.0))
    return a, mult, ig


def _lru_prompt_kernel(x_ref, gate_ref, cw_ref, cb_ref, wa_ref, ba_ref, wx_ref, bx_ref, lam_ref,
                       y_ref, hl_ref, hc_ref, xp_ref, *, tt, conv_w):
    t = pl.program_id(2)

    @pl.when(t == 0)
    def _():
        hc_ref[...] = jnp.zeros_like(hc_ref)
        xp_ref[...] = jnp.zeros_like(xp_ref)

    x = x_ref[...]
    c = x.shape[1]
    row = lax.broadcasted_iota(jnp.int32, (tt, c), 0)
    row8 = lax.broadcasted_iota(jnp.int32, (SUBLANES, c), 0)
    prev8 = xp_ref[...]
    xc = cb_ref[...] + cw_ref[conv_w - 1:conv_w, :] * x
    for k in range(1, conv_w):
        xr = pltpu.roll(x, k, axis=0)
        first = jnp.where(row8 < k, pltpu.roll(prev8, k, axis=0), xr[:SUBLANES])
        xs = jnp.concatenate([first, xr[SUBLANES:]], axis=0)
        xc = xc + cw_ref[conv_w - 1 - k:conv_w - k, :] * xs
    xp_ref[...] = x[tt - SUBLANES:]

    a, mult, ig = _lru_gates(xc, wa_ref, ba_ref, wx_ref, bx_ref, lam_ref)
    mult = jnp.where(row + t * tt == 0, 1.0, mult)
    bx = mult * ig * xc
    sub = row % SUBLANES
    d = 1
    while d < SUBLANES:
        keep = sub >= d
        a_sh = jnp.where(keep, pltpu.roll(a, d, axis=0), 1.0)
        b_sh = jnp.where(keep, pltpu.roll(bx, d, axis=0), 0.0)
        bx = a * b_sh + bx
        a = a * a_sh
        d *= 2
    h = hc_ref[...]
    groups = []
    for g in range(tt // SUBLANES):
        rows = slice(g * SUBLANES, (g + 1) * SUBLANES)
        groups.append(a[rows] * h + bx[rows])
        h = groups[-1][SUBLANES - 1:]
    y_ref[...] = (jnp.concatenate(groups, axis=0) * _silu(gate_ref[...])).astype(BF16)
    hc_ref[...] = h
    hl_ref[...] = h


def lru_prompt(proj, conv_w, conv_b, wa, ba, wx, bx, lam):
    b, s, two_br = proj.shape
    br = two_br // 2
    nblk, bs_, _ = wa.shape
    cw = conv_w.shape[0]
    tt = min(s, 256)
    vec = lambda a: a.reshape(1, br)
    vspec = pl.BlockSpec((1, bs_), lambda bi, ni, ti: (0, ni))
    wspec = pl.BlockSpec((None, bs_, bs_), lambda bi, ni, ti: (ni, 0, 0))
    y, hl = pl.pallas_call(
        functools.partial(_lru_prompt_kernel, tt=tt, conv_w=cw),
        grid=(b, nblk, s // tt),
        in_specs=[pl.BlockSpec((None, tt, bs_), lambda bi, ni, ti: (bi, ti, ni)),
                  pl.BlockSpec((None, tt, bs_), lambda bi, ni, ti: (bi, ti, nblk + ni)),
                  pl.BlockSpec((cw, bs_), lambda bi, ni, ti: (0, ni)),
                  vspec, wspec, vspec, wspec, vspec, vspec],
        out_specs=[pl.BlockSpec((None, tt, bs_), lambda bi, ni, ti: (bi, ti, ni)),
                   pl.BlockSpec((None, 1, bs_), lambda bi, ni, ti: (bi, 0, ni))],
        out_shape=[jax.ShapeDtypeStruct((b, s, br), BF16), jax.ShapeDtypeStruct((b, 1, br), F32)],
        scratch_shapes=[pltpu.VMEM((1, bs_), F32), pltpu.VMEM((SUBLANES, bs_), F32)],
        compiler_params=_params(3),
        name="lru_prompt",
    )(proj, proj, conv_w, vec(conv_b), wa, vec(ba), wx, vec(bx), vec(lam))
    return y, hl.reshape(b, br)


def _lru_sample_kernel(x_ref, gate_ref, buf_ref, h0_ref, cw_ref, cb_ref, wa_ref, ba_ref, wx_ref, bx_ref,
                       lam_ref, y_ref, h_ref, *, conv_w):
    x = x_ref[...]
    xc = cb_ref[...] + cw_ref[conv_w - 1:conv_w, :] * x
    for w in range(conv_w - 1):
        xc = xc + cw_ref[w:w + 1, :] * buf_ref[:, w, :]
    a, mult, ig = _lru_gates(xc, wa_ref, ba_ref, wx_ref, bx_ref, lam_ref)
    h = a * h0_ref[...] + mult * ig * xc
    h_ref[...] = h
    y_ref[...] = (h * _silu(gate_ref[...])).astype(BF16)


def lru_sample(proj_s, conv_buf, h0, conv_w, conv_b, wa, ba, wx, bx, lam):
    bs, two_br = proj_s.shape
    br = two_br // 2
    nblk, bs_, _ = wa.shape
    cw = conv_w.shape[0]
    vec = lambda a: a.reshape(1, br)
    vspec = pl.BlockSpec((1, bs_), lambda ni: (0, ni))
    wspec = pl.BlockSpec((None, bs_, bs_), lambda ni: (ni, 0, 0))
    rspec = pl.BlockSpec((bs, bs_), lambda ni: (0, ni))
    return pl.pallas_call(
        functools.partial(_lru_sample_kernel, conv_w=cw),
        grid=(nblk,),
        in_specs=[rspec,
                  pl.BlockSpec((bs, bs_), lambda ni: (0, nblk + ni)),
                  pl.BlockSpec((bs, cw - 1, bs_), lambda ni: (0, 0, ni)),
                  rspec,
                  pl.BlockSpec((cw, bs_), lambda ni: (0, ni)),
                  vspec, wspec, vspec, wspec, vspec, vspec],
        out_specs=[rspec, rspec],
        out_shape=[jax.ShapeDtypeStruct((bs, br), BF16), jax.ShapeDtypeStruct((bs, br), F32)],
        compiler_params=_params(1),
        name="lru_sample",
    )(proj_s, proj_s, conv_buf, h0, conv_w, vec(conv_b), wa, vec(ba), wx, vec(bx), vec(lam))


def _rope(x, cos, sin):
    half = x.shape[-1] // 2
    x1, x2 = x[:, :half], x[:, half:]
    return jnp.concatenate([x1 * cos - x2 * sin, x1 * sin + x2 * cos], axis=-1)


def _groupnorm(o, gain):
    c = o - jnp.mean(o, axis=-1, keepdims=True)
    return c * lax.rsqrt(jnp.mean(c * c, axis=-1, keepdims=True) + EPS) * gain


def _ret_prompt_kernel(q_ref, k_ref, v_ref, gate_ref, cos_ref, sin_ref, lg_ref, gain_ref,
                       y_ref, s_ref, st_ref, *, chunk, n_chunks):
    lg = lg_ref[:, :1]
    idx = lax.broadcasted_iota(jnp.int32, (chunk, 1), 0).astype(F32)
    ii = lax.broadcasted_iota(jnp.int32, (chunk, chunk), 0)
    jj = lax.broadcasted_iota(jnp.int32, (chunk, chunk), 1)
    diff = (ii - jj).astype(F32)
    decay = jnp.where(diff >= 0, jnp.exp(diff * lg), 0.0)
    q_dec = jnp.exp((idx + 1.0) * lg)
    k_dec = jnp.exp((chunk - 1.0 - idx) * lg)
    chunk_dec = jnp.exp(chunk * lg)
    gain = gain_ref[...]
    st_ref[...] = jnp.zeros_like(st_ref)

    def body(c, carry):
        rows = pl.ds(pl.multiple_of(c * chunk, chunk), chunk)
        cos = cos_ref[rows, :]
        sin = sin_ref[rows, :]
        qc = _rope(q_ref[rows, :], cos, sin)
        kc = _rope(k_ref[rows, :], cos, sin) * (RET_DK ** -0.5)
        vc = v_ref[rows, :].astype(BF16)
        st = st_ref[...]
        scores = _dot_nt(qc.astype(BF16), kc.astype(BF16)) * decay
        o = _dot(scores.astype(BF16), vc) + _dot((qc * q_dec).astype(BF16), st.astype(BF16))
        st_ref[...] = chunk_dec * st + _dot_tn((kc * k_dec).astype(BF16), vc)
        y_ref[rows, :] = (_groupnorm(o, gain) * _silu(gate_ref[rows, :])).astype(BF16)
        return carry

    lax.fori_loop(0, n_chunks, body, 0, unroll=2)
    s_ref[...] = st_ref[...]


def _ret_log_gamma(n_heads):
    lg = np.log1p(-np.exp2(-5.0 - np.arange(n_heads, dtype=np.float32))).astype(np.float32)
    return jnp.asarray(np.broadcast_to(lg[:, None, None], (n_heads, 1, LANES)).copy())


def _rope_tables(pos):
    half = RET_DK // 2
    inv_freq = ROPE_BASE ** (-jnp.arange(half, dtype=F32) / half)
    ang = pos[:, None] * inv_freq[None, :]
    return jnp.cos(ang), jnp.sin(ang)


def ret_prompt(proj, gain, n_heads):
    b, s, _ = proj.shape
    dk, dv = RET_DK, RET_DV
    qk = n_heads * dk
    chunk = RET_CHUNK if s % RET_CHUNK == 0 else s
    cos, sin = _rope_tables(jnp.arange(s, dtype=F32))
    half = dk // 2
    koff = qk // dk
    voff = 2 * qk // dv
    goff = (2 * qk + n_heads * dv) // dv
    return pl.pallas_call(
        functools.partial(_ret_prompt_kernel, chunk=chunk, n_chunks=s // chunk),
        grid=(b, n_heads),
        in_specs=[pl.BlockSpec((None, s, dk), lambda bi, hi: (bi, 0, hi)),
                  pl.BlockSpec((None, s, dk), lambda bi, hi: (bi, 0, koff + hi)),
                  pl.BlockSpec((None, s, dv), lambda bi, hi: (bi, 0, voff + hi)),
                  pl.BlockSpec((None, s, dv), lambda bi, hi: (bi, 0, goff + hi)),
                  pl.BlockSpec((s, half), lambda bi, hi: (0, 0)),
                  pl.BlockSpec((s, half), lambda bi, hi: (0, 0)),
                  pl.BlockSpec((None, 1, LANES), lambda bi, hi: (hi, 0, 0)),
                  pl.BlockSpec((1, dv), lambda bi, hi: (0, hi))],
        out_specs=[pl.BlockSpec((None, s, dv), lambda bi, hi: (bi, 0, hi)),
                   pl.BlockSpec((None, None, dk, dv), lambda bi, hi: (bi, hi, 0, 0))],
        out_shape=[jax.ShapeDtypeStruct((b, s, n_heads * dv), BF16),
                   jax.ShapeDtypeStruct((b, n_heads, dk, dv), F32)],
        scratch_shapes=[pltpu.VMEM((dk, dv), F32)],
        compiler_params=_params(2),
        name="ret_prompt",
    )(proj, proj, proj, proj, cos, sin, _ret_log_gamma(n_heads), gain.reshape(1, n_heads * dv))


def _ret_sample_kernel(q_ref, k_ref, v_ref, gate_ref, cos_ref, sin_ref, lg_ref, gain_ref, s0_ref,
                       y_ref, s_ref):
    dk, dv = RET_DK, RET_DV
    gamma = jnp.exp(lg_ref[:, :1])
    cos = cos_ref[...]
    sin = sin_ref[...]
    q = _rope(q_ref[...], cos, sin)
    k = _rope(k_ref[...], cos, sin) * (dk ** -0.5)
    v = v_ref[...]
    s0 = s0_ref[...]
    score = jnp.sum(q * k, axis=-1, keepdims=True)
    q8 = jnp.broadcast_to(q * gamma, (SUBLANES, dk)).astype(BF16)
    o = score * v + _dot(q8, s0.astype(BF16))[:1]
    kcol = _col_bcast(k, dk)
    for j in range(dv // dk):
        cols = slice(j * dk, (j + 1) * dk)
        s_ref[:, cols] = gamma * s0[:, cols] + kcol * v[:, cols]
    y_ref[...] = (_groupnorm(o, gain_ref[...]) * _silu(gate_ref[...])).astype(BF16)


def ret_sample(proj_s, gain, s0, n_heads):
    bs, ret_in = proj_s.shape
    dk, dv = RET_DK, RET_DV
    qk = n_heads * dk
    cos, sin = _rope_tables(jnp.full((1,), PAST_LEN, F32))
    half = dk // 2
    koff = qk // dk
    voff = 2 * qk // dv
    goff = (2 * qk + n_heads * dv) // dv
    p3 = proj_s.reshape(bs, 1, ret_in)
    y, s = pl.pallas_call(
        _ret_sample_kernel,
        grid=(bs, n_heads),
        in_specs=[pl.BlockSpec((None, 1, dk), lambda bi, hi: (bi, 0, hi)),
                  pl.BlockSpec((None, 1, dk), lambda bi, hi: (bi, 0, koff + hi)),
                  pl.BlockSpec((None, 1, dv), lambda bi, hi: (bi, 0, voff + hi)),
                  pl.BlockSpec((None, 1, dv), lambda bi, hi: (bi, 0, goff + hi)),
                  pl.BlockSpec((1, half), lambda bi, hi: (0, 0)),
                  pl.BlockSpec((1, half), lambda bi, hi: (0, 0)),
                  pl.BlockSpec((None, 1, LANES), lambda bi, hi: (hi, 0, 0)),
                  pl.BlockSpec((1, dv), lambda bi, hi: (0, hi)),
                  pl.BlockSpec((None, None, dk, dv), lambda bi, hi: (bi, hi, 0, 0))],
        out_specs=[pl.BlockSpec((None, 1, dv), lambda bi, hi: (bi, 0, hi)),
                   pl.BlockSpec((None, None, dk, dv), lambda bi, hi: (bi, hi, 0, 0))],
        out_shape=[jax.ShapeDtypeStruct((bs, 1, n_heads * dv), BF16),
                   jax.ShapeDtypeStruct((bs, n_heads, dk, dv), F32)],
        compiler_params=_params(2),
        name="ret_sample",
    )(p3, p3, p3, p3, cos, sin, _ret_log_gamma(n_heads), gain.reshape(1, n_heads * dv), s0)
    return y.reshape(bs, n_heads * dv), s


def _headnorm(o, gain):
    return o * lax.rsqrt(jnp.mean(o * o, axis=-1, keepdims=True) + EPS) * gain


def _gla_log_alpha(low_ref_val, gw_ref, gb_ref, rank):
    lane = lax.broadcasted_iota(jnp.int32, low_ref_val.shape, 1)
    low = jnp.where(lane < rank, low_ref_val, 0.0).astype(BF16)
    z = _dot(low, gw_ref[...].astype(BF16)) + gb_ref[...]
    return (jnp.minimum(z, 0.0) - jnp.log1p(jnp.exp(-jnp.abs(z)))) / GLA_TAU


def _gla_prompt_kernel(q_ref, k_ref, v_ref, gate_ref, low_ref, gw_ref, gb_ref, gain_ref,
                       y_ref, s_ref, st_ref, *, chunk, n_chunks, rank):
    dk, dv = GLA_DK, GLA_DV
    row = lax.broadcasted_iota(jnp.int32, (chunk, dk), 0)
    ii = lax.broadcasted_iota(jnp.int32, (chunk, chunk), 0)
    jj = lax.broadcasted_iota(jnp.int32, (chunk, chunk), 1)
    causal = jj <= ii
    gain = gain_ref[...]
    st_ref[...] = jnp.zeros_like(st_ref)

    def body(c, carry):
        rows = pl.ds(pl.multiple_of(c * chunk, chunk), chunk)
        bcum = _gla_log_alpha(low_ref[rows, :], gw_ref, gb_ref, rank)
        d = 1
        while d < chunk:
            bcum = bcum + jnp.where(row >= d, pltpu.roll(bcum, d, axis=0), 0.0)
            d *= 2
        qc = q_ref[rows, :] * (dk ** -0.5)
        kc = k_ref[rows, :]
        vc = v_ref[rows, :].astype(BF16)
        qe = (qc * jnp.exp(bcum)).astype(BF16)
        ke = (kc * jnp.exp(-bcum)).astype(BF16)
        st = st_ref[...]
        scores = jnp.where(causal, _dot_nt(qe, ke), 0.0)
        o = _dot(scores.astype(BF16), vc) + _dot(qe, st.astype(BF16))
        blast = bcum[chunk - 1:chunk]
        upd = _dot_tn((kc * jnp.exp(blast - bcum)).astype(BF16), vc)
        dec = _col_bcast(jnp.exp(blast), dk)
        for j in range(dv // dk):
            cols = slice(j * dk, (j + 1) * dk)
            st_ref[:, cols] = dec * st[:, cols] + upd[:, cols]
        y_ref[rows, :] = (_headnorm(o, gain) * _silu(gate_ref[rows, :])).astype(BF16)
        return carry

    lax.fori_loop(0, n_chunks, body, 0, unroll=2)
    s_ref[...] = st_ref[...]


def _gla_offsets(n_heads):
    dk, dv = GLA_DK, GLA_DV
    qk = n_heads * dk
    koff = qk // dk
    voff = 2 * qk // dv
    goff = (2 * qk + n_heads * dv) // dv
    loff = (2 * qk + 2 * n_heads * dv) // LANES
    return koff, voff, goff, loff


def _pad_rank(gate_w):
    rank = gate_w.shape[0]
    return jnp.pad(gate_w, ((0, LANES - rank), (0, 0)))


def gla_prompt(proj, gate_w, gate_b, gain, n_heads):
    b, s, _ = proj.shape
    dk, dv = GLA_DK, GLA_DV
    rank = gate_w.shape[0]
    chunk = GLA_CHUNK if s % GLA_CHUNK == 0 else s
    koff, voff, goff, loff = _gla_offsets(n_heads)
    return pl.pallas_call(
        functools.partial(_gla_prompt_kernel, chunk=chunk, n_chunks=s // chunk, rank=rank),
        grid=(b, n_heads),
        in_specs=[pl.BlockSpec((None, s, dk), lambda bi, hi: (bi, 0, hi)),
                  pl.BlockSpec((None, s, dk), lambda bi, hi: (bi, 0, koff + hi)),
                  pl.BlockSpec((None, s, dv), lambda bi, hi: (bi, 0, voff + hi)),
                  pl.BlockSpec((None, s, dv), lambda bi, hi: (bi, 0, goff + hi)),
                  pl.BlockSpec((None, s, LANES), lambda bi, hi: (bi, 0, loff)),
                  pl.BlockSpec((LANES, dk), lambda bi, hi: (0, hi)),
                  pl.BlockSpec((1, dk), lambda bi, hi: (0, hi)),
                  pl.BlockSpec((1, dv), lambda bi, hi: (0, hi))],
        out_specs=[pl.BlockSpec((None, s, dv), lambda bi, hi: (bi, 0, hi)),
                   pl.BlockSpec((None, None, dk, dv), lambda bi, hi: (bi, hi, 0, 0))],
        out_shape=[jax.ShapeDtypeStruct((b, s, n_heads * dv), BF16),
                   jax.ShapeDtypeStruct((b, n_heads, dk, dv), F32)],
        scratch_shapes=[pltpu.VMEM((dk, dv), F32)],
        compiler_params=_params(2),
        name="gla_prompt",
    )(proj, proj, proj, proj, proj, _pad_rank(gate_w), gate_b.reshape(1, -1), gain.reshape(1, -1))


def _gla_sample_kernel(q_ref, k_ref, v_ref, gate_ref, low_ref, gw_ref, gb_ref, gain_ref, s0_ref,
                       y_ref, s_ref, *, rank):
    dk, dv = GLA_DK, GLA_DV
    low8 = jnp.broadcast_to(low_ref[...], (SUBLANES, LANES))
    g = _gla_log_alpha(low8, gw_ref, gb_ref, rank)[:1]
    q = q_ref[...] * (dk ** -0.5)
    k = k_ref[...]
    v = v_ref[...]
    s0 = s0_ref[...]
    qe = q * jnp.exp(g)
    ke = k * jnp.exp(-g)
    score = jnp.sum(qe * ke, axis=-1, keepdims=True)
    qe8 = jnp.broadcast_to(qe, (SUBLANES, dk)).astype(BF16)
    o = score * v + _dot(qe8, s0.astype(BF16))[:1]
    dec = _col_bcast(jnp.exp(g), dk)
    kcol = _col_bcast(k * jnp.exp(g - g), dk)
    for j in range(dv // dk):
        cols = slice(j * dk, (j + 1) * dk)
        s_ref[:, cols] = dec * s0[:, cols] + kcol * v[:, cols]
    y_ref[...] = (_headnorm(o, gain_ref[...]) * _silu(gate_ref[...])).astype(BF16)


def gla_sample(proj_s, gate_w, gate_b, gain, s0, n_heads):
    bs, gla_in = proj_s.shape
    dk, dv = GLA_DK, GLA_DV
    rank = gate_w.shape[0]
    koff, voff, goff, loff = _gla_offsets(n_heads)
    p3 = proj_s.reshape(bs, 1, gla_in)
    y, s = pl.pallas_call(
        functools.partial(_gla_sample_kernel, rank=rank),
        grid=(bs, n_heads),
        in_specs=[pl.BlockSpec((None, 1, dk), lambda bi, hi: (bi, 0, hi)),
                  pl.BlockSpec((None, 1, dk), lambda bi, hi: (bi, 0, koff + hi)),
                  pl.BlockSpec((None, 1, dv), lambda bi, hi: (bi, 0, voff + hi)),
                  pl.BlockSpec((None, 1, dv), lambda bi, hi: (bi, 0, goff + hi)),
                  pl.BlockSpec((None, 1, LANES), lambda bi, hi: (bi, 0, loff)),
                  pl.BlockSpec((LANES, dk), lambda bi, hi: (0, hi)),
                  pl.BlockSpec((1, dk), lambda bi, hi: (0, hi)),
                  pl.BlockSpec((1, dv), lambda bi, hi: (0, hi)),
                  pl.BlockSpec((None, None, dk, dv), lambda bi, hi: (bi, hi, 0, 0))],
        out_specs=[pl.BlockSpec((None, 1, dv), lambda bi, hi: (bi, 0, hi)),
                   pl.BlockSpec((None, None, dk, dv), lambda bi, hi: (bi, hi, 0, 0))],
        out_shape=[jax.ShapeDtypeStruct((bs, 1, n_heads * dv), BF16),
                   jax.ShapeDtypeStruct((bs, n_heads, dk, dv), F32)],
        compiler_params=_params(2),
        name="gla_sample",
    )(p3, p3, p3, p3, p3, _pad_rank(gate_w), gate_b.reshape(1, -1), gain.reshape(1, -1), s0)
    return y.reshape(bs, n_heads * dv), s


def _attention_bias_tables(rel_bias, n_heads):
    blk = A_BLOCK
    ng = len(A_GROUPS)
    onehot = np.zeros((ng, 2 * blk, NUM_BUCKETS), np.float32)
    for g, (window, dil) in enumerate(A_GROUPS):
        n_keys = window // dil
        assert n_keys == blk, "one 128-key band per dilated stream is assumed"
        u = np.arange(n_keys + 1)
        onehot[g, u, _t5_bucket(dil * (n_keys - u))] = 1.0
    tab = jnp.einsum("gub,bgh->guh", jnp.asarray(onehot), rel_bias.astype(F32).reshape(NUM_BUCKETS, ng, n_heads),
                     precision=lax.Precision.HIGHEST)
    return tab.transpose(2, 0, 1), tab[:, :blk, :], tab[:, blk:blk + 1, :]


def kernel(x_prompt, x_sample, cache_k_w128, cache_v_w128, cache_k_w512, cache_v_w512, cache_k_w2048, cache_v_w2048, state_lru_h, state_lru_conv, state_ret, state_gla, norm_pre, norm_post, rel_bias, a_w_in, a_w_out, b_w_in, b_conv_w, b_conv_b, b_gate_a_w, b_gate_a_b, b_gate_x_w, b_gate_x_b, b_lambda, b_w_out, c_w_in, c_norm, c_w_out, d_w_in, d_gate_w, d_gate_b, d_norm, d_w_out):
    b, s, d = x_prompt.shape
    bs = x_sample.shape[0]
    assert x_sample.shape[1] == 1, "one new token per sequence"
    depth = norm_pre.shape[0]
    k_caches = (cache_k_w128, cache_k_w512, cache_k_w2048)
    v_caches = (cache_v_w128, cache_v_w512, cache_v_w2048)
    ng = len(A_GROUPS)
    for (window, dil), kc in zip(A_GROUPS, k_caches):
        assert kc.shape[2] == window and s % (dil * A_BLOCK) == 0

    xp = x_prompt.reshape(b * s, d)
    xs = x_sample.reshape(bs, d)
    hp = norm_cast(xp, norm_pre[0])
    hs = norm_cast(xs, norm_pre[0])

    kp_rows = [[] for _ in A_GROUPS]
    vp_rows = [[] for _ in A_GROUPS]
    ks_rows = [[] for _ in A_GROUPS]
    vs_rows = [[] for _ in A_GROUPS]
    lru_h_p, lru_h_s, lru_c_p, lru_c_s = [], [], [], []
    ret_p, ret_s, gla_p, gla_s = [], [], [], []

    for i in range(depth):
        kind, j = i % 4, i // 4
        if kind == 0:
            n_heads = a_w_out.shape[1] // A_HEAD_DIM
            width = n_heads * A_HEAD_DIM
            seg = lambda c0, nc: matmul(hp, hs, a_w_in, j, col0=c0 * width, ncols=nc * width)
            q_p, q_s = seg(0, ng)
            k_ps = [seg(ng + g, 1) for g in range(ng)]
            v_ps = [seg(2 * ng + g, 1) for g in range(ng)]
            gate_p, gate_s = seg(3 * ng, 1)
            bias_tab, bias_past, bias_self = _attention_bias_tables(rel_bias, n_heads)
            r3 = lambda a: a.reshape(b, s, a.shape[-1])
            yp = attn_prompt(r3(q_p), [r3(k[0]) for k in k_ps], [r3(v[0]) for v in v_ps], r3(gate_p),
                             bias_tab, n_heads).reshape(b * s, width)
            ys = attn_sample(q_s, [k[1] for k in k_ps], [v[1] for v in v_ps], gate_s,
                             [c[j] for c in k_caches], [c[j] for c in v_caches],
                             bias_past, bias_self, n_heads)
            for g, (window, _) in enumerate(A_GROUPS):
                keep = min(window, s)
                heads = lambda a, t: a.reshape(-1, t, n_heads, A_HEAD_DIM)
                kp_rows[g].append(heads(k_ps[g][0], s)[:, s - keep:])
                vp_rows[g].append(heads(v_ps[g][0], s)[:, s - keep:])
                ks_rows[g].append(heads(k_ps[g][1], 1))
                vs_rows[g].append(heads(v_ps[g][1], 1))
            w_out = a_w_out
        elif kind == 1:
            br = b_w_out.shape[1]
            proj, proj_s = matmul(hp, hs, b_w_in, j)
            prm = (b_conv_w[j], b_conv_b[j], b_gate_a_w[j], b_gate_a_b[j], b_gate_x_w[j], b_gate_x_b[j], b_lambda[j])
            proj3 = proj.reshape(b, s, -1)
            yp, h_last = lru_prompt(proj3, *prm)
            yp = yp.reshape(b * s, br)
            ys, h_new = lru_sample(proj_s, state_lru_conv[j], state_lru_h[j], *prm)
            cw = b_conv_w.shape[1]
            lru_h_p.append(h_last)
            lru_c_p.append(proj3[:, s - (cw - 1):, :br])
            lru_h_s.append(h_new)
            lru_c_s.append(jnp.concatenate([state_lru_conv[j], proj_s[:, None, :br]], axis=1)[:, 1:])
            w_out = b_w_out
        elif kind == 2:
            br = c_w_out.shape[1]
            n_heads = br // RET_DV
            proj, proj_s = matmul(hp, hs, c_w_in, j)
            yp, st = ret_prompt(proj.reshape(b, s, -1), c_norm[j], n_heads)
            yp = yp.reshape(b * s, br)
            ys, st_s = ret_sample(proj_s, c_norm[j], state_ret[j], n_heads)
            ret_p.append(st)
            ret_s.append(st_s)
            w_out = c_w_out
        else:
            br = d_w_out.shape[1]
            n_heads = br // GLA_DV
            proj, proj_s = matmul(hp, hs, jnp.swapaxes(d_w_in, 1, 2), j, transposed=True)
            yp, st = gla_prompt(proj.reshape(b, s, -1), d_gate_w[j], d_gate_b[j], d_norm[j], n_heads)
            yp = yp.reshape(b * s, br)
            ys, st_s = gla_sample(proj_s, d_gate_w[j], d_gate_b[j], d_norm[j], state_gla[j], n_heads)
            gla_p.append(st)
            gla_s.append(st_s)
            w_out = d_w_out
        op, os_ = matmul(yp, ys, w_out, j)
        g_next = norm_pre[i + 1] if i + 1 < depth else None
        xp, hp = residual_norm(xp, op, norm_post[i], g_next)
        xs, hs = residual_norm(xs, os_, norm_post[i], g_next)

    return (xp.reshape(b, s, d), xs.reshape(bs, 1, d),
            jnp.stack(kp_rows[0]), jnp.stack(ks_rows[0]), jnp.stack(vp_rows[0]), jnp.stack(vs_rows[0]),
            jnp.stack(kp_rows[1]), jnp.stack(ks_rows[1]), jnp.stack(vp_rows[1]), jnp.stack(vs_rows[1]),
            jnp.stack(kp_rows[2]), jnp.stack(ks_rows[2]), jnp.stack(vp_rows[2]), jnp.stack(vs_rows[2]),
            jnp.stack(lru_h_p), jnp.stack(lru_h_s), jnp.stack(lru_c_p), jnp.stack(lru_c_s),
            jnp.stack(ret_p), jnp.stack(ret_s), jnp.stack(gla_p), jnp.stack(gla_s))
```

```python
import functools

import numpy as np
import jax
import jax.numpy as jnp
from jax import lax
from jax.experimental import pallas as pl
from jax.experimental.pallas import tpu as pltpu

F32 = jnp.float32
BF16 = jnp.bfloat16

PAST_LEN = 8192
EPS = 1e-6
NEG_INF = -1e30
A_GROUPS = ((128, 1), (512, 4), (2048, 16))
A_HEAD_DIM = 128
A_BLOCK = 128
ATTN_UNROLL = 8
NUM_BUCKETS = 32
MAX_DISTANCE = 2048
C_RG = 8.0
RET_DK = 256
RET_DV = 512
RET_CHUNK = 128
ROPE_BASE = 10000.0
GLA_DK = 256
GLA_DV = 512
GLA_TAU = 16.0
GLA_CHUNK = 64
GLA_GROUP = 4
RET_GROUP = 2

LANES = 128
SUBLANES = 8
V7X_VMEM_LIMIT_BYTES = 56 * 1024 * 1024

_ARB = pltpu.ARBITRARY


def _params(n_grid):
    return pltpu.CompilerParams(dimension_semantics=(_ARB,) * n_grid,
                                vmem_limit_bytes=V7X_VMEM_LIMIT_BYTES)


def _dot(a, b):
    return jnp.dot(a, b, preferred_element_type=F32)


def _dot_nt(a, b):
    return lax.dot_general(a, b, (((1,), (1,)), ((), ())), preferred_element_type=F32)


def _dot_tn(a, b):
    return lax.dot_general(a, b, (((0,), (0,)), ((), ())), preferred_element_type=F32)


def _sigmoid(x):
    return 0.5 * jnp.tanh(0.5 * x) + 0.5


def _silu(x):
    return x * _sigmoid(x)


def _softplus(x):
    return jnp.maximum(x, 0.0) + jnp.log1p(jnp.exp(-jnp.abs(x)))


def _rms(x, g):
    return x * lax.rsqrt(jnp.mean(x * x, axis=-1, keepdims=True) + EPS) * g


def _chunk_group(n_chunks, want):
    return want if n_chunks % want == 0 else 1


def _col_bcast(row, n):
    return jnp.broadcast_to(row, (n, n)).T


def _norm_kernel(x_ref, g_ref, h_ref):
    h_ref[...] = _rms(x_ref[...], g_ref[...]).astype(BF16)


def _resnorm_kernel(x_ref, y_ref, gpost_ref, gnext_ref, xo_ref, h_ref):
    xn = x_ref[...] + _rms(y_ref[...], gpost_ref[...])
    xo_ref[...] = xn
    h_ref[...] = _rms(xn, gnext_ref[...]).astype(BF16)


def _res_kernel(x_ref, y_ref, gpost_ref, xo_ref):
    xo_ref[...] = x_ref[...] + _rms(y_ref[...], gpost_ref[...])


def _row_tile(m):
    return min(m, 256)


def norm_cast(x, g):
    m, d = x.shape
    tm = _row_tile(m)
    return pl.pallas_call(
        _norm_kernel,
        grid=(m // tm,),
        in_specs=[pl.BlockSpec((tm, d), lambda i: (i, 0)),
                  pl.BlockSpec((1, d), lambda i: (0, 0))],
        out_specs=pl.BlockSpec((tm, d), lambda i: (i, 0)),
        out_shape=jax.ShapeDtypeStruct((m, d), BF16),
        compiler_params=_params(1),
        name="norm_cast",
    )(x, g.reshape(1, d))


def residual_norm(x, y, g_post, g_next):
    m, d = x.shape
    tm = _row_tile(m)
    row = pl.BlockSpec((tm, d), lambda i: (i, 0))
    vec = pl.BlockSpec((1, d), lambda i: (0, 0))
    if g_next is None:
        return pl.pallas_call(
            _res_kernel, grid=(m // tm,),
            in_specs=[row, row, vec], out_specs=row,
            out_shape=jax.ShapeDtypeStruct((m, d), F32),
            compiler_params=_params(1), name="residual",
        )(x, y, g_post.reshape(1, d)), None
    return pl.pallas_call(
        _resnorm_kernel, grid=(m // tm,),
        in_specs=[row, row, vec, vec], out_specs=[row, row],
        out_shape=[jax.ShapeDtypeStruct((m, d), F32), jax.ShapeDtypeStruct((m, d), BF16)],
        compiler_params=_params(1), name="residual_norm",
    )(x, y, g_post.reshape(1, d), g_next.reshape(1, d))


def _mm_kernel(*refs, n_parts, transposed):
    x_parts = refs[:n_parts]
    w_ref, xs_ref, o_ref, os_ref = refs[n_parts:]
    w = w_ref[...].astype(BF16)
    dot = _dot_nt if transposed else _dot
    x = jnp.concatenate([p[...] for p in x_parts], axis=1)
    o_ref[...] = dot(x, w)

    @pl.when(pl.program_id(0) == 0)
    def _():
        os_ref[...] = dot(xs_ref[...], w)


MM_COL_TILE = 512
MM_ROW_TILE = 2048
MM_ROW_TILE_LONG_K = 1024
MM_LONG_K = 4096
MM_X_PARTS = 4


def matmul(x, xs, w, layer, col0=0, ncols=None, transposed=False):
    m, k = x.shape
    ms = xs.shape[0]
    n = w.shape[1] if transposed else w.shape[2]
    ncols = n - col0 if ncols is None else ncols
    tn = MM_COL_TILE
    assert col0 % tn == 0
    tm = min(m, MM_ROW_TILE if k <= MM_LONG_K else MM_ROW_TILE_LONG_K)
    nj = pl.cdiv(ncols, tn)
    c0 = col0 // tn
    n_parts = MM_X_PARTS if k % (MM_X_PARTS * LANES) == 0 else 1
    x_specs = [pl.BlockSpec((tm, k // n_parts), lambda i, j, c=c: (i, c), pipeline_mode=pl.Buffered(1))
               for c in range(n_parts)]
    if transposed:
        w_spec = pl.BlockSpec((None, tn, k), lambda i, j: (layer, c0 + j, 0))
    else:
        w_spec = pl.BlockSpec((None, k, tn), lambda i, j: (layer, 0, c0 + j))
    return pl.pallas_call(
        functools.partial(_mm_kernel, n_parts=n_parts, transposed=transposed),
        grid=(m // tm, nj),
        in_specs=x_specs + [w_spec, pl.BlockSpec((ms, k), lambda i, j: (0, 0))],
        out_specs=[pl.BlockSpec((tm, tn), lambda i, j: (i, j)),
                   pl.BlockSpec((ms, tn), lambda i, j: (0, jnp.where(i == 0, j, nj - 1)))],
        out_shape=[jax.ShapeDtypeStruct((m, ncols), F32), jax.ShapeDtypeStruct((ms, ncols), F32)],
        compiler_params=_params(2),
        name="proj_matmul",
    )(*([x] * n_parts), w, xs)


def _t5_bucket(dist):
    n = np.asarray(dist, dtype=np.int64)
    max_exact = NUM_BUCKETS // 2
    ratio = np.log(np.maximum(n, 1) / max_exact) / np.log(MAX_DISTANCE / max_exact)
    large = np.minimum(max_exact + (ratio * (NUM_BUCKETS - max_exact)).astype(np.int64), NUM_BUCKETS - 1)
    return np.where(n < max_exact, n, large).astype(np.int32)


def _attn_prompt_kernel(*refs, dilations, seq):
    ng = len(dilations)
    qkv = refs[:3 * ng]
    gate_ref, bias_ref, y_ref, o_s, lse_s = refs[3 * ng:]
    blk = A_BLOCK
    scale = A_HEAD_DIM ** -0.5
    ii = lax.broadcasted_iota(jnp.int32, (blk, blk), 0)
    jj = lax.broadcasted_iota(jnp.int32, (blk, blk), 1)
    cur_ok = jj <= ii
    prev_ok = jj >= ii

    for g, dil in enumerate(dilations):
        q_ref, k_ref, v_ref = qkv[3 * g:3 * g + 3]
        nb = seq // (dil * blk)
        band = pltpu.roll(jnp.broadcast_to(bias_ref[g:g + 1, :], (blk, 2 * blk)), 0, axis=1,
                          stride=1, stride_axis=0)
        b_prev = band[:, :blk]
        b_cur = band[:, blk:]

        def rows_at(start, dil=dil):
            if dil == 1:
                return pl.ds(pl.multiple_of(start, blk), blk)
            return pl.ds(start, blk, stride=dil)

        def scores(idx, dil=dil, nb=nb, q_ref=q_ref, k_ref=k_ref, b_prev=b_prev, b_cur=b_cur,
                   rows_at=rows_at):
            r = idx % dil
            bi = idx // dil
            start = r + bi * (blk * dil)
            rows = rows_at(start)
            q = q_ref[rows, :].astype(BF16)
            s = [jnp.where(cur_ok, _dot_nt(q, k_ref[rows, :].astype(BF16)) * scale + b_cur, NEG_INF)]
            prows = None
            if nb > 1:
                prows = rows_at(jnp.maximum(start - blk * dil, 0))
                s.append(jnp.where(prev_ok & (bi > 0),
                                   _dot_nt(q, k_ref[prows, :].astype(BF16)) * scale + b_prev, NEG_INF))
            return rows, prows, s

        def softmax(s):
            m = jnp.max(functools.reduce(jnp.maximum, s), axis=-1, keepdims=True)
            p = [jnp.exp(x - m) for x in s]
            den = jnp.sum(functools.reduce(jnp.add, p), axis=-1, keepdims=True)
            return m, den, [x.astype(BF16) for x in p]

        def values(rows, prows, p, v_ref=v_ref):
            num = _dot(p[0], v_ref[rows, :].astype(BF16))
            if prows is not None:
                num = num + _dot(p[1], v_ref[prows, :].astype(BF16))
            return num

        def body(it, carry, g=g, scores=scores, softmax=softmax, values=values):
            blocks = [scores(it * ATTN_UNROLL + u) for u in range(ATTN_UNROLL)]
            probs = [softmax(s) for _, _, s in blocks]
            nums = [values(rows, prows, p) for (rows, prows, _), (_, _, p) in zip(blocks, probs)]
            for (rows, _, _), (m, den, _), num in zip(blocks, probs, nums):
                o_s[g, rows, :] = num / den
                lse_s[g, rows, :] = jnp.broadcast_to(m + jnp.log(den), (blk, A_HEAD_DIM))
            return carry

        lax.fori_loop(0, seq // (blk * ATTN_UNROLL), body, 0)

    tr = 256

    def merge(c, carry):
        rows = pl.ds(pl.multiple_of(c * tr, tr), tr)
        lses = [lse_s[g, rows, :] for g in range(ng)]
        mx = functools.reduce(jnp.maximum, lses)
        num = jnp.zeros((tr, A_HEAD_DIM), F32)
        den = jnp.zeros((tr, A_HEAD_DIM), F32)
        for g in range(ng):
            w = jnp.exp(lses[g] - mx)
            num = num + w * o_s[g, rows, :]
            den = den + w
        y_ref[rows, :] = ((num / den) * _silu(gate_ref[rows, :])).astype(BF16)
        return carry

    lax.fori_loop(0, seq // tr, merge, 0)


def attn_prompt(qs, ks, vs, gate, bias_tab, n_heads):
    b, s, _ = gate[0].shape
    ng = len(A_GROUPS)
    dh = A_HEAD_DIM
    in_specs = []
    args = []
    units = [u for g in range(ng) for u in (qs[g], ks[g], vs[g])] + [gate]
    for arr, unit in units:
        in_specs.append(pl.BlockSpec((None, s, dh), lambda bi, hi, unit=unit: (bi, 0, unit * n_heads + hi)))
        args.append(arr)
    in_specs.append(pl.BlockSpec((None, ng, 2 * A_BLOCK), lambda bi, hi: (hi, 0, 0)))
    args.append(bias_tab)
    kern = functools.partial(_attn_prompt_kernel, dilations=tuple(d for _, d in A_GROUPS), seq=s)
    return pl.pallas_call(
        kern,
        grid=(b, n_heads),
        in_specs=in_specs,
        out_specs=pl.BlockSpec((None, s, dh), lambda bi, hi: (bi, 0, hi)),
        out_shape=jax.ShapeDtypeStruct((b, s, n_heads * dh), BF16),
        scratch_shapes=[pltpu.VMEM((ng, s, dh), F32)] * 2,
        compiler_params=_params(2),
        name="attn_prompt",
    )(*args)


def _attn_sample_kernel(*refs, n_heads):
    ng = len(A_GROUPS)
    q_ref, gate_ref = refs[0], refs[1]
    new = refs[2:2 + 2 * ng]
    caches = refs[2 + 2 * ng:2 + 4 * ng]
    bpast_ref, bself_ref, y_ref = refs[2 + 4 * ng:]
    dh = A_HEAD_DIM
    width = n_heads * dh
    scale = dh ** -0.5
    for h in range(n_heads):
        lanes = slice(h * dh, (h + 1) * dh)
        parts = []
        for g in range(ng):
            kc_ref, vc_ref = caches[2 * g], caches[2 * g + 1]
            q = q_ref[:, g * width + h * dh:g * width + (h + 1) * dh]
            kn = new[2 * g][:, lanes]
            vn = new[2 * g + 1][:, lanes]
            lp = jnp.sum(kc_ref[:, h, :] * q, axis=-1, keepdims=True) * scale + bpast_ref[g, :, h:h + 1]
            ls = jnp.sum(kn * q, axis=-1, keepdims=True) * scale + bself_ref[g, :, h:h + 1]
            m = jnp.maximum(jnp.max(lp, axis=0, keepdims=True), ls)
            p = jnp.exp(lp - m)
            ps = jnp.exp(ls - m)
            den = jnp.sum(p, axis=0, keepdims=True) + ps
            num = jnp.sum(p * vc_ref[:, h, :], axis=0, keepdims=True) + ps * vn
            parts.append((num, m, den))
        mx = functools.reduce(jnp.maximum, [p_[1] for p_ in parts])
        num = jnp.zeros((1, dh), F32)
        den = jnp.zeros((1, 1), F32)
        for num_g, m_g, den_g in parts:
            w = jnp.exp(m_g - mx)
            num = num + w * num_g
            den = den + w * den_g
        y_ref[:, lanes] = ((num / den) * _silu(gate_ref[:, lanes])).astype(BF16)


def attn_sample(qs, ks, vs, gate, k_caches, v_caches, bias_past, bias_self, n_heads):
    bs, width = gate.shape
    dh = A_HEAD_DIM
    row = lambda a: a.reshape(bs, 1, a.shape[-1])
    row_spec = lambda a: pl.BlockSpec((None, 1, a.shape[-1]), lambda bi: (bi, 0, 0))
    q = jnp.concatenate(qs, axis=1)
    args = [row(q), row(gate)]
    in_specs = [row_spec(q), row_spec(gate)]
    for g in range(len(A_GROUPS)):
        for a in (ks[g], vs[g]):
            args.append(row(a))
            in_specs.append(row_spec(a))
    for (window, dil), kc, vc in zip(A_GROUPS, k_caches, v_caches):
        n_keys = window // dil
        for c in (kc, vc):
            args.append(c.reshape(bs, n_keys, dil, n_heads, dh))
            in_specs.append(pl.BlockSpec((None, n_keys, None, n_heads, dh), lambda bi: (bi, 0, 0, 0, 0)))
    args += [bias_past, bias_self]
    in_specs += [pl.BlockSpec(bias_past.shape, lambda bi: (0, 0, 0)),
                 pl.BlockSpec(bias_self.shape, lambda bi: (0, 0, 0))]
    y = pl.pallas_call(
        functools.partial(_attn_sample_kernel, n_heads=n_heads),
        grid=(bs,),
        in_specs=in_specs,
        out_specs=pl.BlockSpec((None, 1, width), lambda bi: (bi, 0, 0)),
        out_shape=jax.ShapeDtypeStruct((bs, 1, width), BF16),
        compiler_params=_params(1),
        name="attn_sample",
    )(*args)
    return y.reshape(bs, width)


def _lru_gates(xc, wa_ref, ba_ref, wx_ref, bx_ref, lam_ref):
    xb = xc.astype(BF16)
    r = _sigmoid(_dot(xb, wa_ref[...].astype(BF16)) + ba_ref[...])
    ig = _sigmoid(_dot(xb, wx_ref[...].astype(BF16)) + bx_ref[...])
    log_a = -C_RG * r * _softplus(-lam_ref[...])
    a = jnp.exp(log_a)
    mult = jnp.sqrt(-jnp.tanh(log_a) * (a * a + 1.0))
    return a, mult, ig


def _lru_prompt_kernel(x_ref, gate_ref, cw_ref, cb_ref, wa_ref, ba_ref, wx_ref, bx_ref, lam_ref,
                       y_ref, hl_ref, hc_ref, xp_ref, *, tt, conv_w):
    t = pl.program_id(2)

    @pl.when(t == 0)
    def _():
        hc_ref[...] = jnp.zeros_like(hc_ref)
        xp_ref[:SUBLANES, :] = jnp.zeros((SUBLANES, xp_ref.shape[1]), F32)

    x = x_ref[...]
    c = x.shape[1]
    xp_ref[SUBLANES:, :] = x
    xc = cb_ref[...] + cw_ref[conv_w - 1:conv_w, :] * x
    for k in range(1, conv_w):
        xc = xc + cw_ref[conv_w - 1 - k:conv_w - k, :] * xp_ref[SUBLANES - k:SUBLANES - k + tt, :]
    xp_ref[:SUBLANES, :] = x[tt - SUBLANES:]

    row = lax.broadcasted_iota(jnp.int32, (tt, c), 0)
    a, mult, ig = _lru_gates(xc, wa_ref, ba_ref, wx_ref, bx_ref, lam_ref)
    mult = jnp.where(row + t * tt == 0, 1.0, mult)
    bx = mult * ig * xc
    ng = tt // SUBLANES
    a = a.reshape(ng, SUBLANES, c)
    bx = bx.reshape(ng, SUBLANES, c)
    sub = lax.broadcasted_iota(jnp.int32, (ng, SUBLANES, c), 1)
    d = 1
    while d < SUBLANES:
        keep = sub >= d
        a_sh = jnp.where(keep, pltpu.roll(a, d, axis=1), 1.0)
        b_sh = jnp.where(keep, pltpu.roll(bx, d, axis=1), 0.0)
        bx = a * b_sh + bx
        a = a * a_sh
        d *= 2
    h = hc_ref[...]
    groups = []
    for g in range(ng):
        groups.append(a[g] * h + bx[g])
        h = groups[-1][SUBLANES - 1:]
    y_ref[...] = (jnp.concatenate(groups, axis=0) * _silu(gate_ref[...])).astype(BF16)
    hc_ref[...] = h
    hl_ref[...] = h


def lru_prompt(proj, conv_w, conv_b, wa, ba, wx, bx, lam):
    b, s, two_br = proj.shape
    br = two_br // 2
    nblk, bs_, _ = wa.shape
    cw = conv_w.shape[0]
    tt = min(s, 256)
    vec = lambda a: a.reshape(1, br)
    vspec = pl.BlockSpec((1, bs_), lambda bi, ni, ti: (0, ni))
    wspec = pl.BlockSpec((None, bs_, bs_), lambda bi, ni, ti: (ni, 0, 0))
    y, hl = pl.pallas_call(
        functools.partial(_lru_prompt_kernel, tt=tt, conv_w=cw),
        grid=(b, nblk, s // tt),
        in_specs=[pl.BlockSpec((None, tt, bs_), lambda bi, ni, ti: (bi, ti, ni)),
                  pl.BlockSpec((None, tt, bs_), lambda bi, ni, ti: (bi, ti, nblk + ni)),
                  pl.BlockSpec((cw, bs_), lambda bi, ni, ti: (0, ni)),
                  vspec, wspec, vspec, wspec, vspec, vspec],
        out_specs=[pl.BlockSpec((None, tt, bs_), lambda bi, ni, ti: (bi, ti, ni)),
                   pl.BlockSpec((None, 1, bs_), lambda bi, ni, ti: (bi, 0, ni))],
        out_shape=[jax.ShapeDtypeStruct((b, s, br), BF16), jax.ShapeDtypeStruct((b, 1, br), F32)],
        scratch_shapes=[pltpu.VMEM((1, bs_), F32), pltpu.VMEM((SUBLANES + tt, bs_), F32)],
        compiler_params=_params(3),
        name="lru_prompt",
    )(proj, proj, conv_w, vec(conv_b), wa, vec(ba), wx, vec(bx), vec(lam))
    return y, hl.reshape(b, br)


def _lru_sample_kernel(x_ref, gate_ref, buf_ref, h0_ref, cw_ref, cb_ref, wa_ref, ba_ref, wx_ref, bx_ref,
                       lam_ref, y_ref, h_ref, *, conv_w):
    x = x_ref[...]
    xc = cb_ref[...] + cw_ref[conv_w - 1:conv_w, :] * x
    for w in range(conv_w - 1):
        xc = xc + cw_ref[w:w + 1, :] * buf_ref[:, w, :]
    a, mult, ig = _lru_gates(xc, wa_ref, ba_ref, wx_ref, bx_ref, lam_ref)
    h = a * h0_ref[...] + mult * ig * xc
    h_ref[...] = h
    y_ref[...] = (h * _silu(gate_ref[...])).astype(BF16)


def lru_sample(proj_s, conv_buf, h0, conv_w, conv_b, wa, ba, wx, bx, lam):
    bs, two_br = proj_s.shape
    br = two_br // 2
    nblk, bs_, _ = wa.shape
    cw = conv_w.shape[0]
    vec = lambda a: a.reshape(1, br)
    vspec = pl.BlockSpec((1, bs_), lambda ni: (0, ni))
    wspec = pl.BlockSpec((None, bs_, bs_), lambda ni: (ni, 0, 0))
    rspec = pl.BlockSpec((bs, bs_), lambda ni: (0, ni))
    return pl.pallas_call(
        functools.partial(_lru_sample_kernel, conv_w=cw),
        grid=(nblk,),
        in_specs=[rspec,
                  pl.BlockSpec((bs, bs_), lambda ni: (0, nblk + ni)),
                  pl.BlockSpec((bs, cw - 1, bs_), lambda ni: (0, 0, ni)),
                  rspec,
                  pl.BlockSpec((cw, bs_), lambda ni: (0, ni)),
                  vspec, wspec, vspec, wspec, vspec, vspec],
        out_specs=[rspec, rspec],
        out_shape=[jax.ShapeDtypeStruct((bs, br), BF16), jax.ShapeDtypeStruct((bs, br), F32)],
        compiler_params=_params(1),
        name="lru_sample",
    )(proj_s, proj_s, conv_buf, h0, conv_w, vec(conv_b), wa, vec(ba), wx, vec(bx), vec(lam))


def _rope(x, cos, sin):
    half = x.shape[-1] // 2
    x1, x2 = x[:, :half], x[:, half:]
    return jnp.concatenate([x1 * cos - x2 * sin, x1 * sin + x2 * cos], axis=-1)


def _groupnorm(o, gain):
    c = o - jnp.mean(o, axis=-1, keepdims=True)
    return c * lax.rsqrt(jnp.mean(c * c, axis=-1, keepdims=True) + EPS) * gain


def _ret_prompt_kernel(q_ref, k_ref, v_ref, gate_ref, cos_ref, sin_ref, lg_ref, gain_ref,
                       y_ref, s_ref, st_ref, *, chunk, n_chunks, group):
    lg = lg_ref[:, :1]
    idx = lax.broadcasted_iota(jnp.int32, (chunk, 1), 0).astype(F32)
    ii = lax.broadcasted_iota(jnp.int32, (chunk, chunk), 0)
    jj = lax.broadcasted_iota(jnp.int32, (chunk, chunk), 1)
    diff = (ii - jj).astype(F32)
    decay = jnp.where(diff >= 0, jnp.exp(diff * lg), 0.0)
    q_dec = jnp.exp((idx + 1.0) * lg)
    k_dec = jnp.exp((chunk - 1.0 - idx) * lg)
    chunk_dec = jnp.exp(chunk * lg)
    gain = gain_ref[...]
    st_ref[...] = jnp.zeros_like(st_ref)

    q_dec_rows = jnp.concatenate([q_dec] * group, axis=0)
    k_dec_rows = jnp.concatenate([k_dec] * group, axis=0)

    def body(it, carry):
        rows = pl.ds(pl.multiple_of(it * (group * chunk), group * chunk), group * chunk)
        cos = cos_ref[rows, :]
        sin = sin_ref[rows, :]
        qc = _rope(q_ref[rows, :], cos, sin)
        kc = _rope(k_ref[rows, :], cos, sin) * (RET_DK ** -0.5)
        vc = v_ref[rows, :].astype(BF16)
        qb = qc.astype(BF16)
        kb = kc.astype(BF16)
        qd = (qc * q_dec_rows).astype(BF16)
        kd = (kc * k_dec_rows).astype(BF16)
        sl = [slice(u * chunk, (u + 1) * chunk) for u in range(group)]
        intra, upd = [], []
        for u in range(group):
            scores = _dot_nt(qb[sl[u]], kb[sl[u]]) * decay
            intra.append(_dot(scores.astype(BF16), vc[sl[u]]))
            upd.append(_dot_tn(kd[sl[u]], vc[sl[u]]))
        st = st_ref[...]
        outs = []
        for u in range(group):
            outs.append(intra[u] + _dot(qd[sl[u]], st.astype(BF16)))
            st = chunk_dec * st + upd[u]
        st_ref[...] = st
        o = jnp.concatenate(outs, axis=0)
        y_ref[rows, :] = (_groupnorm(o, gain) * _silu(gate_ref[rows, :])).astype(BF16)
        return carry

    lax.fori_loop(0, n_chunks // group, body, 0)
    s_ref[...] = st_ref[...]


def _ret_log_gamma(n_heads):
    lg = np.log1p(-np.exp2(-5.0 - np.arange(n_heads, dtype=np.float32))).astype(np.float32)
    return jnp.asarray(np.broadcast_to(lg[:, None, None], (n_heads, 1, LANES)).copy())


def _rope_tables(pos):
    half = RET_DK // 2
    inv_freq = ROPE_BASE ** (-jnp.arange(half, dtype=F32) / half)
    ang = pos[:, None] * inv_freq[None, :]
    return jnp.cos(ang), jnp.sin(ang)


def ret_prompt(proj, gain, n_heads):
    b, s, _ = proj.shape
    dk, dv = RET_DK, RET_DV
    qk = n_heads * dk
    chunk = RET_CHUNK if s % RET_CHUNK == 0 else s
    cos, sin = _rope_tables(jnp.arange(s, dtype=F32))
    half = dk // 2
    koff = qk // dk
    voff = 2 * qk // dv
    goff = (2 * qk + n_heads * dv) // dv
    return pl.pallas_call(
        functools.partial(_ret_prompt_kernel, chunk=chunk, n_chunks=s // chunk,
                          group=_chunk_group(s // chunk, RET_GROUP)),
        grid=(b, n_heads),
        in_specs=[pl.BlockSpec((None, s, dk), lambda bi, hi: (bi, 0, hi)),
                  pl.BlockSpec((None, s, dk), lambda bi, hi: (bi, 0, koff + hi)),
                  pl.BlockSpec((None, s, dv), lambda bi, hi: (bi, 0, voff + hi)),
                  pl.BlockSpec((None, s, dv), lambda bi, hi: (bi, 0, goff + hi)),
                  pl.BlockSpec((s, half), lambda bi, hi: (0, 0)),
                  pl.BlockSpec((s, half), lambda bi, hi: (0, 0)),
                  pl.BlockSpec((None, 1, LANES), lambda bi, hi: (hi, 0, 0)),
                  pl.BlockSpec((1, dv), lambda bi, hi: (0, hi))],
        out_specs=[pl.BlockSpec((None, s, dv), lambda bi, hi: (bi, 0, hi)),
                   pl.BlockSpec((None, None, dk, dv), lambda bi, hi: (bi, hi, 0, 0))],
        out_shape=[jax.ShapeDtypeStruct((b, s, n_heads * dv), BF16),
                   jax.ShapeDtypeStruct((b, n_heads, dk, dv), F32)],
        scratch_shapes=[pltpu.VMEM((dk, dv), F32)],
        compiler_params=_params(2),
        name="ret_prompt",
    )(proj, proj, proj, proj, cos, sin, _ret_log_gamma(n_heads), gain.reshape(1, n_heads * dv))


def _ret_sample_kernel(q_ref, k_ref, v_ref, gate_ref, cos_ref, sin_ref, lg_ref, gain_ref, s0_ref,
                       y_ref, s_ref):
    dk, dv = RET_DK, RET_DV
    gamma = jnp.exp(lg_ref[:, :1])
    cos = cos_ref[...]
    sin = sin_ref[...]
    q = _rope(q_ref[...], cos, sin)
    k = _rope(k_ref[...], cos, sin) * (dk ** -0.5)
    v = v_ref[...]
    s0 = s0_ref[...]
    score = jnp.sum(q * k, axis=-1, keepdims=True)
    q8 = jnp.broadcast_to(q * gamma, (SUBLANES, dk)).astype(BF16)
    o = score * v + _dot(q8, s0.astype(BF16))[:1]
    kcol = _col_bcast(k, dk)
    for j in range(dv // dk):
        cols = slice(j * dk, (j + 1) * dk)
        s_ref[:, cols] = gamma * s0[:, cols] + kcol * v[:, cols]
    y_ref[...] = (_groupnorm(o, gain_ref[...]) * _silu(gate_ref[...])).astype(BF16)


def ret_sample(proj_s, gain, s0, n_heads):
    bs, ret_in = proj_s.shape
    dk, dv = RET_DK, RET_DV
    qk = n_heads * dk
    cos, sin = _rope_tables(jnp.full((1,), PAST_LEN, F32))
    half = dk // 2
    koff = qk // dk
    voff = 2 * qk // dv
    goff = (2 * qk + n_heads * dv) // dv
    p3 = proj_s.reshape(bs, 1, ret_in)
    y, s = pl.pallas_call(
        _ret_sample_kernel,
        grid=(bs, n_heads),
        in_specs=[pl.BlockSpec((None, 1, dk), lambda bi, hi: (bi, 0, hi)),
                  pl.BlockSpec((None, 1, dk), lambda bi, hi: (bi, 0, koff + hi)),
                  pl.BlockSpec((None, 1, dv), lambda bi, hi: (bi, 0, voff + hi)),
                  pl.BlockSpec((None, 1, dv), lambda bi, hi: (bi, 0, goff + hi)),
                  pl.BlockSpec((1, half), lambda bi, hi: (0, 0)),
                  pl.BlockSpec((1, half), lambda bi, hi: (0, 0)),
                  pl.BlockSpec((None, 1, LANES), lambda bi, hi: (hi, 0, 0)),
                  pl.BlockSpec((1, dv), lambda bi, hi: (0, hi)),
                  pl.BlockSpec((None, None, dk, dv), lambda bi, hi: (bi, hi, 0, 0))],
        out_specs=[pl.BlockSpec((None, 1, dv), lambda bi, hi: (bi, 0, hi)),
                   pl.BlockSpec((None, None, dk, dv), lambda bi, hi: (bi, hi, 0, 0))],
        out_shape=[jax.ShapeDtypeStruct((bs, 1, n_heads * dv), BF16),
                   jax.ShapeDtypeStruct((bs, n_heads, dk, dv), F32)],
        compiler_params=_params(2),
        name="ret_sample",
    )(p3, p3, p3, p3, cos, sin, _ret_log_gamma(n_heads), gain.reshape(1, n_heads * dv), s0)
    return y.reshape(bs, n_heads * dv), s


def _headnorm(o, gain):
    return o * lax.rsqrt(jnp.mean(o * o, axis=-1, keepdims=True) + EPS) * gain


def _gla_log_alpha(low_ref_val, gw_ref, gb_ref, rank):
    lane = lax.broadcasted_iota(jnp.int32, low_ref_val.shape, 1)
    low = jnp.where(lane < rank, low_ref_val, 0.0).astype(BF16)
    z = _dot(low, gw_ref[...].astype(BF16)) + gb_ref[...]
    return (jnp.minimum(z, 0.0) - jnp.log1p(jnp.exp(-jnp.abs(z)))) / GLA_TAU


def _gla_prompt_kernel(q_ref, k_ref, v_ref, gate_ref, low_ref, gw_ref, gb_ref, gain_ref,
                       y_ref, s_ref, st_ref, *, chunk, n_chunks, group, rank):
    dk, dv = GLA_DK, GLA_DV
    sub = lax.broadcasted_iota(jnp.int32, (group * chunk, dk), 0) % chunk
    ii = lax.broadcasted_iota(jnp.int32, (chunk, chunk), 0)
    jj = lax.broadcasted_iota(jnp.int32, (chunk, chunk), 1)
    causal = jj <= ii
    gain = gain_ref[...]
    st_ref[...] = jnp.zeros_like(st_ref)

    def body(it, carry):
        rows = pl.ds(pl.multiple_of(it * (group * chunk), group * chunk), group * chunk)
        bcum = _gla_log_alpha(low_ref[rows, :], gw_ref, gb_ref, rank)
        d = 1
        while d < chunk:
            bcum = bcum + jnp.where(sub >= d, pltpu.roll(bcum, d, axis=0), 0.0)
            d *= 2
        sl = [slice(u * chunk, (u + 1) * chunk) for u in range(group)]
        blast = [bcum[(u + 1) * chunk - 1:(u + 1) * chunk] for u in range(group)]
        blast_rows = jnp.concatenate([jnp.broadcast_to(b, (chunk, dk)) for b in blast], axis=0)
        kc = k_ref[rows, :]
        vc = v_ref[rows, :].astype(BF16)
        qe = (q_ref[rows, :] * (dk ** -0.5) * jnp.exp(bcum)).astype(BF16)
        ke = (kc * jnp.exp(-bcum)).astype(BF16)
        kd = (kc * jnp.exp(blast_rows - bcum)).astype(BF16)
        intra, upd, dec = [], [], []
        for u in range(group):
            scores = jnp.where(causal, _dot_nt(qe[sl[u]], ke[sl[u]]), 0.0)
            intra.append(_dot(scores.astype(BF16), vc[sl[u]]))
            upd.append(_dot_tn(kd[sl[u]], vc[sl[u]]))
            dec.append(_col_bcast(jnp.exp(blast[u]), dk))
        st = st_ref[...]
        outs = []
        for u in range(group):
            outs.append(intra[u] + _dot(qe[sl[u]], st.astype(BF16)))
            st = jnp.concatenate([dec[u] * st[:, j * dk:(j + 1) * dk] + upd[u][:, j * dk:(j + 1) * dk]
                                  for j in range(dv // dk)], axis=1)
        st_ref[...] = st
        o = jnp.concatenate(outs, axis=0)
        y_ref[rows, :] = (_headnorm(o, gain) * _silu(gate_ref[rows, :])).astype(BF16)
        return carry

    lax.fori_loop(0, n_chunks // group, body, 0)
    s_ref[...] = st_ref[...]


def _gla_offsets(n_heads):
    dk, dv = GLA_DK, GLA_DV
    qk = n_heads * dk
    koff = qk // dk
    voff = 2 * qk // dv
    goff = (2 * qk + n_heads * dv) // dv
    loff = (2 * qk + 2 * n_heads * dv) // LANES
    return koff, voff, goff, loff


def _pad_rank(gate_w):
    rank = gate_w.shape[0]
    return jnp.pad(gate_w, ((0, LANES - rank), (0, 0)))


def gla_prompt(proj, gate_w, gate_b, gain, n_heads):
    b, s, _ = proj.shape
    dk, dv = GLA_DK, GLA_DV
    rank = gate_w.shape[0]
    chunk = GLA_CHUNK if s % GLA_CHUNK == 0 else s
    koff, voff, goff, loff = _gla_offsets(n_heads)
    return pl.pallas_call(
        functools.partial(_gla_prompt_kernel, chunk=chunk, n_chunks=s // chunk,
                          group=_chunk_group(s // chunk, GLA_GROUP), rank=rank),
        grid=(b, n_heads),
        in_specs=[pl.BlockSpec((None, s, dk), lambda bi, hi: (bi, 0, hi)),
                  pl.BlockSpec((None, s, dk), lambda bi, hi: (bi, 0, koff + hi)),
                  pl.BlockSpec((None, s, dv), lambda bi, hi: (bi, 0, voff + hi)),
                  pl.BlockSpec((None, s, dv), lambda bi, hi: (bi, 0, goff + hi)),
                  pl.BlockSpec((None, s, LANES), lambda bi, hi: (bi, 0, loff)),
                  pl.BlockSpec((LANES, dk), lambda bi, hi: (0, hi)),
                  pl.BlockSpec((1, dk), lambda bi, hi: (0, hi)),
                  pl.BlockSpec((1, dv), lambda bi, hi: (0, hi))],
        out_specs=[pl.BlockSpec((None, s, dv), lambda bi, hi: (bi, 0, hi)),
                   pl.BlockSpec((None, None, dk, dv), lambda bi, hi: (bi, hi, 0, 0))],
        out_shape=[jax.ShapeDtypeStruct((b, s, n_heads * dv), BF16),
                   jax.ShapeDtypeStruct((b, n_heads, dk, dv), F32)],
        scratch_shapes=[pltpu.VMEM((dk, dv), F32)],
        compiler_params=_params(2),
        name="gla_prompt",
    )(proj, proj, proj, proj, proj, _pad_rank(gate_w), gate_b.reshape(1, -1), gain.reshape(1, -1))


def _gla_sample_kernel(q_ref, k_ref, v_ref, gate_ref, low_ref, gw_ref, gb_ref, gain_ref, s0_ref,
                       y_ref, s_ref, *, rank):
    dk, dv = GLA_DK, GLA_DV
    low8 = jnp.broadcast_to(low_ref[...], (SUBLANES, LANES))
    g = _gla_log_alpha(low8, gw_ref, gb_ref, rank)[:1]
    q = q_ref[...] * (dk ** -0.5)
    k = k_ref[...]
    v = v_ref[...]
    s0 = s0_ref[...]
    qe = q * jnp.exp(g)
    ke = k * jnp.exp(-g)
    score = jnp.sum(qe * ke, axis=-1, keepdims=True)
    qe8 = jnp.broadcast_to(qe, (SUBLANES, dk)).astype(BF16)
    o = score * v + _dot(qe8, s0.astype(BF16))[:1]
    dec = _col_bcast(jnp.exp(g), dk)
    kcol = _col_bcast(k * jnp.exp(g - g), dk)
    for j in range(dv // dk):
        cols = slice(j * dk, (j + 1) * dk)
        s_ref[:, cols] = dec * s0[:, cols] + kcol * v[:, cols]
    y_ref[...] = (_headnorm(o, gain_ref[...]) * _silu(gate_ref[...])).astype(BF16)


def gla_sample(proj_s, gate_w, gate_b, gain, s0, n_heads):
    bs, gla_in = proj_s.shape
    dk, dv = GLA_DK, GLA_DV
    rank = gate_w.shape[0]
    koff, voff, goff, loff = _gla_offsets(n_heads)
    p3 = proj_s.reshape(bs, 1, gla_in)
    y, s = pl.pallas_call(
        functools.partial(_gla_sample_kernel, rank=rank),
        grid=(bs, n_heads),
        in_specs=[pl.BlockSpec((None, 1, dk), lambda bi, hi: (bi, 0, hi)),
                  pl.BlockSpec((None, 1, dk), lambda bi, hi: (bi, 0, koff + hi)),
                  pl.BlockSpec((None, 1, dv), lambda bi, hi: (bi, 0, voff + hi)),
                  pl.BlockSpec((None, 1, dv), lambda bi, hi: (bi, 0, goff + hi)),
                  pl.BlockSpec((None, 1, LANES), lambda bi, hi: (bi, 0, loff)),
                  pl.BlockSpec((LANES, dk), lambda bi, hi: (0, hi)),
                  pl.BlockSpec((1, dk), lambda bi, hi: (0, hi)),
                  pl.BlockSpec((1, dv), lambda bi, hi: (0, hi)),
                  pl.BlockSpec((None, None, dk, dv), lambda bi, hi: (bi, hi, 0, 0))],
        out_specs=[pl.BlockSpec((None, 1, dv), lambda bi, hi: (bi, 0, hi)),
                   pl.BlockSpec((None, None, dk, dv), lambda bi, hi: (bi, hi, 0, 0))],
        out_shape=[jax.ShapeDtypeStruct((bs, 1, n_heads * dv), BF16),
                   jax.ShapeDtypeStruct((bs, n_heads, dk, dv), F32)],
        compiler_params=_params(2),
        name="gla_sample",
    )(p3, p3, p3, p3, p3, _pad_rank(gate_w), gate_b.reshape(1, -1), gain.reshape(1, -1), s0)
    return y.reshape(bs, n_heads * dv), s


def _attention_bias_tables(rel_bias, n_heads):
    blk = A_BLOCK
    ng = len(A_GROUPS)
    onehot = np.zeros((ng, 2 * blk, NUM_BUCKETS), np.float32)
    for g, (window, dil) in enumerate(A_GROUPS):
        n_keys = window // dil
        assert n_keys == blk, "one 128-key band per dilated stream is assumed"
        u = np.arange(n_keys + 1)
        onehot[g, u, _t5_bucket(dil * (n_keys - u))] = 1.0
    tab = jnp.einsum("gub,bgh->guh", jnp.asarray(onehot), rel_bias.astype(F32).reshape(NUM_BUCKETS, ng, n_heads),
                     precision=lax.Precision.HIGHEST)
    return tab.transpose(2, 0, 1), tab[:, :blk, :], tab[:, blk:blk + 1, :]


def kernel(x_prompt, x_sample, cache_k_w128, cache_v_w128, cache_k_w512, cache_v_w512, cache_k_w2048, cache_v_w2048, state_lru_h, state_lru_conv, state_ret, state_gla, norm_pre, norm_post, rel_bias, a_w_in, a_w_out, b_w_in, b_conv_w, b_conv_b, b_gate_a_w, b_gate_a_b, b_gate_x_w, b_gate_x_b, b_lambda, b_w_out, c_w_in, c_norm, c_w_out, d_w_in, d_gate_w, d_gate_b, d_norm, d_w_out):
    b, s, d = x_prompt.shape
    bs = x_sample.shape[0]
    assert x_sample.shape[1] == 1, "one new token per sequence"
    depth = norm_pre.shape[0]
    k_caches = (cache_k_w128, cache_k_w512, cache_k_w2048)
    v_caches = (cache_v_w128, cache_v_w512, cache_v_w2048)
    ng = len(A_GROUPS)
    for (window, dil), kc in zip(A_GROUPS, k_caches):
        assert kc.shape[2] == window and s % (dil * A_BLOCK) == 0

    xp = x_prompt.reshape(b * s, d)
    xs = x_sample.reshape(bs, d)
    hp = norm_cast(xp, norm_pre[0])
    hs = norm_cast(xs, norm_pre[0])

    kp_rows = [[] for _ in A_GROUPS]
    vp_rows = [[] for _ in A_GROUPS]
    ks_rows = [[] for _ in A_GROUPS]
    vs_rows = [[] for _ in A_GROUPS]
    lru_h_p, lru_h_s, lru_c_p, lru_c_s = [], [], [], []
    ret_p, ret_s, gla_p, gla_s = [], [], [], []

    for i in range(depth):
        kind, j = i % 4, i // 4
        if kind == 0:
            n_heads = a_w_out.shape[1] // A_HEAD_DIM
            width = n_heads * A_HEAD_DIM
            full = [min(window, s) == s for window, _ in A_GROUPS]
            cuts = {0, 3 * ng + 1}
            for which in (1, 2):
                for g in range(ng):
                    if full[g]:
                        cuts |= {which * ng + g, which * ng + g + 1}
            cuts = sorted(cuts)
            unit_p, unit_s = {}, {}
            for u0, u1 in zip(cuts[:-1], cuts[1:]):
                arr_p, arr_s = matmul(hp, hs, a_w_in, j, col0=u0 * width, ncols=(u1 - u0) * width)
                for u in range(u0, u1):
                    unit_p[u] = (arr_p.reshape(b, s, -1), u - u0)
                    unit_s[u] = arr_s[:, (u - u0) * width:(u - u0 + 1) * width]
            bias_tab, bias_past, bias_self = _attention_bias_tables(rel_bias, n_heads)
            yp = attn_prompt([unit_p[g] for g in range(ng)], [unit_p[ng + g] for g in range(ng)],
                             [unit_p[2 * ng + g] for g in range(ng)], unit_p[3 * ng],
                             bias_tab, n_heads).reshape(b * s, width)
            ys = attn_sample([unit_s[g] for g in range(ng)], [unit_s[ng + g] for g in range(ng)],
                             [unit_s[2 * ng + g] for g in range(ng)], unit_s[3 * ng],
                             [c[j] for c in k_caches], [c[j] for c in v_caches],
                             bias_past, bias_self, n_heads)
            for g, (window, _) in enumerate(A_GROUPS):
                keep = min(window, s)

                def tail(unit, keep=keep):
                    arr, off = unit
                    return arr[:, s - keep:, off * width:(off + 1) * width].reshape(b, keep, n_heads, A_HEAD_DIM)

                kp_rows[g].append(tail(unit_p[ng + g]))
                vp_rows[g].append(tail(unit_p[2 * ng + g]))
                ks_rows[g].append(unit_s[ng + g].reshape(bs, 1, n_heads, A_HEAD_DIM))
                vs_rows[g].append(unit_s[2 * ng + g].reshape(bs, 1, n_heads, A_HEAD_DIM))
            w_out = a_w_out
        elif kind == 1:
            br = b_w_out.shape[1]
            proj, proj_s = matmul(hp, hs, b_w_in, j)
            prm = (b_conv_w[j], b_conv_b[j], b_gate_a_w[j], b_gate_a_b[j], b_gate_x_w[j], b_gate_x_b[j], b_lambda[j])
            proj3 = proj.reshape(b, s, -1)
            yp, h_last = lru_prompt(proj3, *prm)
            yp = yp.reshape(b * s, br)
            ys, h_new = lru_sample(proj_s, state_lru_conv[j], state_lru_h[j], *prm)
            cw = b_conv_w.shape[1]
            lru_h_p.append(h_last)
            lru_c_p.append(proj3[:, s - (cw - 1):, :br])
            lru_h_s.append(h_new)
            lru_c_s.append(jnp.concatenate([state_lru_conv[j], proj_s[:, None, :br]], axis=1)[:, 1:])
            w_out = b_w_out
        elif kind == 2:
            br = c_w_out.shape[1]
            n_heads = br // RET_DV
            proj, proj_s = matmul(hp, hs, c_w_in, j)
            yp, st = ret_prompt(proj.reshape(b, s, -1), c_norm[j], n_heads)
            yp = yp.reshape(b * s, br)
            ys, st_s = ret_sample(proj_s, c_norm[j], state_ret[j], n_heads)
            ret_p.append(st)
            ret_s.append(st_s)
            w_out = c_w_out
        else:
            br = d_w_out.shape[1]
            n_heads = br // GLA_DV
            proj, proj_s = matmul(hp, hs, jnp.swapaxes(d_w_in, 1, 2), j, transposed=True)
            yp, st = gla_prompt(proj.reshape(b, s, -1), d_gate_w[j], d_gate_b[j], d_norm[j], n_heads)
            yp = yp.reshape(b * s, br)
            ys, st_s = gla_sample(proj_s, d_gate_w[j], d_gate_b[j], d_norm[j], state_gla[j], n_heads)
            gla_p.append(st)
            gla_s.append(st_s)
            w_out = d_w_out
        op, os_ = matmul(yp, ys, w_out, j)
        g_next = norm_pre[i + 1] if i + 1 < depth else None
        xp, hp = residual_norm(xp, op, norm_post[i], g_next)
        xs, hs = residual_norm(xs, os_, norm_post[i], g_next)

    return (xp.reshape(b, s, d), xs.reshape(bs, 1, d),
            jnp.stack(kp_rows[0]), jnp.stack(ks_rows[0]), jnp.stack(vp_rows[0]), jnp.stack(vs_rows[0]),
            jnp.stack(kp_rows[1]), jnp.stack(ks_rows[1]), jnp.stack(vp_rows[1]), jnp.stack(vs_rows[1]),
            jnp.stack(kp_rows[2]), jnp.stack(ks_rows[2]), jnp.stack(vp_rows[2]), jnp.stack(vs_rows[2]),
            jnp.stack(lru_h_p), jnp.stack(lru_h_s), jnp.stack(lru_c_p), jnp.stack(lru_c_s),
            jnp.stack(ret_p), jnp.stack(ret_s), jnp.stack(gla_p), jnp.stack(gla_s))
```

```python
import functools

import numpy as np
import jax
import jax.numpy as jnp
from jax import lax
from jax.experimental import pallas as pl
from jax.experimental.pallas import tpu as pltpu

F32 = jnp.float32
BF16 = jnp.bfloat16

PAST_LEN = 8192
EPS = 1e-6
NEG_INF = -1e30
A_GROUPS = ((128, 1), (512, 4), (2048, 16))
A_HEAD_DIM = 128
A_BLOCK = 128
ATTN_UNROLL = 16
NUM_BUCKETS = 32
MAX_DISTANCE = 2048
C_RG = 8.0
RET_DK = 256
RET_DV = 512
RET_CHUNK = 128
ROPE_BASE = 10000.0
GLA_DK = 256
GLA_DV = 512
GLA_TAU = 16.0
GLA_CHUNK = 64
GLA_GROUP = 8
RET_GROUP = 4

LANES = 128
SUBLANES = 8
V7X_VMEM_LIMIT_BYTES = 56 * 1024 * 1024

_ARB = pltpu.ARBITRARY


def _params(n_grid):
    return pltpu.CompilerParams(dimension_semantics=(_ARB,) * n_grid,
                                vmem_limit_bytes=V7X_VMEM_LIMIT_BYTES)


def _dot(a, b):
    return jnp.dot(a, b, preferred_element_type=F32)


def _dot_nt(a, b):
    return lax.dot_general(a, b, (((1,), (1,)), ((), ())), preferred_element_type=F32)


def _dot_tn(a, b):
    return lax.dot_general(a, b, (((0,), (0,)), ((), ())), preferred_element_type=F32)


def _sigmoid(x):
    return 0.5 * jnp.tanh(0.5 * x) + 0.5


def _silu(x):
    return x * _sigmoid(x)


def _softplus(x):
    return jnp.maximum(x, 0.0) + jnp.log1p(jnp.exp(-jnp.abs(x)))


def _rms(x, g):
    return x * lax.rsqrt(jnp.mean(x * x, axis=-1, keepdims=True) + EPS) * g


def _chunk_group(n_chunks, want):
    return want if n_chunks % want == 0 else 1


def _col_bcast(row, n):
    return jnp.broadcast_to(row, (n, n)).T


def _norm_kernel(x_ref, g_ref, h_ref):
    h_ref[...] = _rms(x_ref[...], g_ref[...]).astype(BF16)


def _resnorm_kernel(x_ref, y_ref, gpost_ref, gnext_ref, xo_ref, h_ref):
    xn = x_ref[...] + _rms(y_ref[...], gpost_ref[...])
    xo_ref[...] = xn
    h_ref[...] = _rms(xn, gnext_ref[...]).astype(BF16)


def _res_kernel(x_ref, y_ref, gpost_ref, xo_ref):
    xo_ref[...] = x_ref[...] + _rms(y_ref[...], gpost_ref[...])


def _row_tile(m):
    return min(m, 256)


def norm_cast(x, g):
    m, d = x.shape
    tm = _row_tile(m)
    return pl.pallas_call(
        _norm_kernel,
        grid=(m // tm,),
        in_specs=[pl.BlockSpec((tm, d), lambda i: (i, 0)),
                  pl.BlockSpec((1, d), lambda i: (0, 0))],
        out_specs=pl.BlockSpec((tm, d), lambda i: (i, 0)),
        out_shape=jax.ShapeDtypeStruct((m, d), BF16),
        compiler_params=_params(1),
        name="norm_cast",
    )(x, g.reshape(1, d))


def residual_norm(x, y, g_post, g_next):
    m, d = x.shape
    tm = _row_tile(m)
    row = pl.BlockSpec((tm, d), lambda i: (i, 0))
    vec = pl.BlockSpec((1, d), lambda i: (0, 0))
    if g_next is None:
        return pl.pallas_call(
            _res_kernel, grid=(m // tm,),
            in_specs=[row, row, vec], out_specs=row,
            out_shape=jax.ShapeDtypeStruct((m, d), F32),
            compiler_params=_params(1), name="residual",
        )(x, y, g_post.reshape(1, d)), None
    return pl.pallas_call(
        _resnorm_kernel, grid=(m // tm,),
        in_specs=[row, row, vec, vec], out_specs=[row, row],
        out_shape=[jax.ShapeDtypeStruct((m, d), F32), jax.ShapeDtypeStruct((m, d), BF16)],
        compiler_params=_params(1), name="residual_norm",
    )(x, y, g_post.reshape(1, d), g_next.reshape(1, d))


def _mm_kernel(*refs, n_parts, transposed):
    x_parts = refs[:n_parts]
    w_ref, xs_ref, o_ref, os_ref = refs[n_parts:]
    w = w_ref[...].astype(BF16)
    dot = _dot_nt if transposed else _dot
    x = jnp.concatenate([p[...] for p in x_parts], axis=1)
    o_ref[...] = dot(x, w)

    @pl.when(pl.program_id(0) == 0)
    def _():
        os_ref[...] = dot(xs_ref[...], w)


MM_COL_TILE = 512
MM_ROW_TILE = 2048
MM_ROW_TILE_LONG_K = 1024
MM_LONG_K = 4096
MM_X_PARTS = 4


def matmul(x, xs, w, layer, col0=0, ncols=None, transposed=False):
    m, k = x.shape
    ms = xs.shape[0]
    n = w.shape[1] if transposed else w.shape[2]
    ncols = n - col0 if ncols is None else ncols
    tn = MM_COL_TILE
    assert col0 % tn == 0
    tm = min(m, MM_ROW_TILE if k <= MM_LONG_K else MM_ROW_TILE_LONG_K)
    nj = pl.cdiv(ncols, tn)
    c0 = col0 // tn
    n_parts = MM_X_PARTS if k % (MM_X_PARTS * LANES) == 0 else 1
    x_specs = [pl.BlockSpec((tm, k // n_parts), lambda i, j, c=c: (i, c), pipeline_mode=pl.Buffered(1))
               for c in range(n_parts)]
    if transposed:
        w_spec = pl.BlockSpec((None, tn, k), lambda i, j: (layer, c0 + j, 0))
    else:
        w_spec = pl.BlockSpec((None, k, tn), lambda i, j: (layer, 0, c0 + j))
    return pl.pallas_call(
        functools.partial(_mm_kernel, n_parts=n_parts, transposed=transposed),
        grid=(m // tm, nj),
        in_specs=x_specs + [w_spec, pl.BlockSpec((ms, k), lambda i, j: (0, 0))],
        out_specs=[pl.BlockSpec((tm, tn), lambda i, j: (i, j)),
                   pl.BlockSpec((ms, tn), lambda i, j: (0, jnp.where(i == 0, j, nj - 1)))],
        out_shape=[jax.ShapeDtypeStruct((m, ncols), F32), jax.ShapeDtypeStruct((ms, ncols), F32)],
        compiler_params=_params(2),
        name="proj_matmul",
    )(*([x] * n_parts), w, xs)


def _t5_bucket(dist):
    n = np.asarray(dist, dtype=np.int64)
    max_exact = NUM_BUCKETS // 2
    ratio = np.log(np.maximum(n, 1) / max_exact) / np.log(MAX_DISTANCE / max_exact)
    large = np.minimum(max_exact + (ratio * (NUM_BUCKETS - max_exact)).astype(np.int64), NUM_BUCKETS - 1)
    return np.where(n < max_exact, n, large).astype(np.int32)


def _attn_prompt_kernel(*refs, dilations, seq):
    ng = len(dilations)
    qkv = refs[:3 * ng]
    gate_ref, bias_ref, y_ref, o_s, lse_s = refs[3 * ng:]
    blk = A_BLOCK
    scale = A_HEAD_DIM ** -0.5
    ii = lax.broadcasted_iota(jnp.int32, (blk, blk), 0)
    jj = lax.broadcasted_iota(jnp.int32, (blk, blk), 1)
    cur_ok = jj <= ii
    prev_ok = jj >= ii

    for g, dil in enumerate(dilations):
        q_ref, k_ref, v_ref = qkv[3 * g:3 * g + 3]
        nb = seq // (dil * blk)
        band = pltpu.roll(jnp.broadcast_to(bias_ref[g:g + 1, :], (blk, 2 * blk)), 0, axis=1,
                          stride=1, stride_axis=0)
        b_prev = band[:, :blk]
        b_cur = band[:, blk:]

        def rows_at(start, dil=dil):
            if dil == 1:
                return pl.ds(pl.multiple_of(start, blk), blk)
            return pl.ds(start, blk, stride=dil)

        def scores(idx, dil=dil, nb=nb, q_ref=q_ref, k_ref=k_ref, b_prev=b_prev, b_cur=b_cur,
                   rows_at=rows_at):
            r = idx % dil
            bi = idx // dil
            start = r + bi * (blk * dil)
            rows = rows_at(start)
            q = q_ref[rows, :].astype(BF16)
            s = [jnp.where(cur_ok, _dot_nt(q, k_ref[rows, :].astype(BF16)) * scale + b_cur, NEG_INF)]
            prows = None
            if nb > 1:
                prows = rows_at(jnp.maximum(start - blk * dil, 0))
                s.append(jnp.where(prev_ok & (bi > 0),
                                   _dot_nt(q, k_ref[prows, :].astype(BF16)) * scale + b_prev, NEG_INF))
            return rows, prows, s

        def softmax(s):
            m = jnp.max(functools.reduce(jnp.maximum, s), axis=-1, keepdims=True)
            p = [jnp.exp(x - m) for x in s]
            den = jnp.sum(functools.reduce(jnp.add, p), axis=-1, keepdims=True)
            return m, den, [x.astype(BF16) for x in p]

        def values(rows, prows, p, v_ref=v_ref):
            num = _dot(p[0], v_ref[rows, :].astype(BF16))
            if prows is not None:
                num = num + _dot(p[1], v_ref[prows, :].astype(BF16))
            return num

        def body(it, carry, g=g, scores=scores, softmax=softmax, values=values):
            blocks = [scores(it * ATTN_UNROLL + u) for u in range(ATTN_UNROLL)]
            probs = [softmax(s) for _, _, s in blocks]
            nums = [values(rows, prows, p) for (rows, prows, _), (_, _, p) in zip(blocks, probs)]
            for (rows, _, _), (m, den, _), num in zip(blocks, probs, nums):
                o_s[g, rows, :] = num / den
                lse_s[g, rows, :] = jnp.broadcast_to(m + jnp.log(den), (blk, A_HEAD_DIM))
            return carry

        lax.fori_loop(0, seq // (blk * ATTN_UNROLL), body, 0)

    tr = 256

    def merge(c, carry):
        rows = pl.ds(pl.multiple_of(c * tr, tr), tr)
        lses = [lse_s[g, rows, :] for g in range(ng)]
        mx = functools.reduce(jnp.maximum, lses)
        num = jnp.zeros((tr, A_HEAD_DIM), F32)
        den = jnp.zeros((tr, A_HEAD_DIM), F32)
        for g in range(ng):
            w = jnp.exp(lses[g] - mx)
            num = num + w * o_s[g, rows, :]
            den = den + w
        y_ref[rows, :] = ((num / den) * _silu(gate_ref[rows, :])).astype(BF16)
        return carry

    lax.fori_loop(0, seq // tr, merge, 0)


def attn_prompt(qs, ks, vs, gate, bias_tab, n_heads):
    b, s, _ = gate[0].shape
    ng = len(A_GROUPS)
    dh = A_HEAD_DIM
    in_specs = []
    args = []
    units = [u for g in range(ng) for u in (qs[g], ks[g], vs[g])] + [gate]
    for arr, unit in units:
        in_specs.append(pl.BlockSpec((None, s, dh), lambda bi, hi, unit=unit: (bi, 0, unit * n_heads + hi)))
        args.append(arr)
    in_specs.append(pl.BlockSpec((None, ng, 2 * A_BLOCK), lambda bi, hi: (hi, 0, 0)))
    args.append(bias_tab)
    kern = functools.partial(_attn_prompt_kernel, dilations=tuple(d for _, d in A_GROUPS), seq=s)
    return pl.pallas_call(
        kern,
        grid=(b, n_heads),
        in_specs=in_specs,
        out_specs=pl.BlockSpec((None, s, dh), lambda bi, hi: (bi, 0, hi)),
        out_shape=jax.ShapeDtypeStruct((b, s, n_heads * dh), BF16),
        scratch_shapes=[pltpu.VMEM((ng, s, dh), F32)] * 2,
        compiler_params=_params(2),
        name="attn_prompt",
    )(*args)


def _attn_sample_kernel(q_ref, kn_ref, vn_ref, gate_ref, *refs, n_heads, n_keys):
    ng = len(A_GROUPS)
    caches = refs[:2 * ng]
    bpast_ref, bself_ref, y_ref = refs[2 * ng:]
    dh = A_HEAD_DIM
    scale = dh ** -0.5
    lane = lax.broadcasted_iota(jnp.int32, (n_heads, n_keys), 1)
    parts = []
    for g in range(ng):
        kc_ref, vc_ref = caches[2 * g], caches[2 * g + 1]
        q = q_ref[g]
        vn = vn_ref[g]

        def logits(j, acc, kc_ref=kc_ref, q=q):
            col = jnp.sum(kc_ref[j] * q, axis=-1, keepdims=True)
            return jnp.where(lane == j, col, acc)

        lp = lax.fori_loop(0, n_keys, logits, jnp.zeros((n_heads, n_keys), F32), unroll=8)
        lp = lp * scale + bpast_ref[g]
        ls = jnp.sum(kn_ref[g] * q, axis=-1, keepdims=True) * scale + bself_ref[g]
        m = jnp.maximum(jnp.max(lp, axis=-1, keepdims=True), ls)
        p = jnp.exp(lp - m)
        ps = jnp.exp(ls - m)
        den = jnp.sum(p, axis=-1, keepdims=True) + ps

        def weighted(j, acc, vc_ref=vc_ref, p=p):
            pj = jnp.sum(jnp.where(lane == j, p, 0.0), axis=-1, keepdims=True)
            return acc + pj * vc_ref[j]

        num = lax.fori_loop(0, n_keys, weighted, ps * vn, unroll=8)
        parts.append((num, m, den))
    mx = functools.reduce(jnp.maximum, [p_[1] for p_ in parts])
    num = jnp.zeros((n_heads, dh), F32)
    den = jnp.zeros((n_heads, 1), F32)
    for num_g, m_g, den_g in parts:
        w = jnp.exp(m_g - mx)
        num = num + w * num_g
        den = den + w * den_g
    y_ref[...] = ((num / den) * _silu(gate_ref[...])).astype(BF16)


def attn_sample(qs, ks, vs, gate, k_caches, v_caches, bias_past, bias_self, n_heads):
    bs, width = gate.shape
    dh = A_HEAD_DIM
    ng = len(A_GROUPS)
    n_keys = A_BLOCK
    heads = lambda rows: jnp.stack(rows, axis=1).reshape(bs, ng, n_heads, dh)
    grp_spec = pl.BlockSpec((None, ng, n_heads, dh), lambda bi: (bi, 0, 0, 0))
    args = [heads(qs), heads(ks), heads(vs), gate.reshape(bs, n_heads, dh)]
    in_specs = [grp_spec, grp_spec, grp_spec, pl.BlockSpec((None, n_heads, dh), lambda bi: (bi, 0, 0))]
    for (window, dil), kc, vc in zip(A_GROUPS, k_caches, v_caches):
        for c in (kc, vc):
            args.append(c.reshape(bs, n_keys, dil, n_heads, dh))
            in_specs.append(pl.BlockSpec((None, n_keys, None, n_heads, dh), lambda bi: (bi, 0, 0, 0, 0)))
    args += [bias_past, bias_self]
    in_specs += [pl.BlockSpec(bias_past.shape, lambda bi: (0, 0, 0)),
                 pl.BlockSpec(bias_self.shape, lambda bi: (0, 0, 0))]
    y = pl.pallas_call(
        functools.partial(_attn_sample_kernel, n_heads=n_heads, n_keys=n_keys),
        grid=(bs,),
        in_specs=in_specs,
        out_specs=pl.BlockSpec((None, n_heads, dh), lambda bi: (bi, 0, 0)),
        out_shape=jax.ShapeDtypeStruct((bs, n_heads, dh), BF16),
        compiler_params=_params(1),
        name="attn_sample",
    )(*args)
    return y.reshape(bs, width)


def _lru_gates(xc, wa_ref, ba_ref, wx_ref, bx_ref, lam_ref):
    xb = xc.astype(BF16)
    r = _sigmoid(_dot(xb, wa_ref[...].astype(BF16)) + ba_ref[...])
    ig = _sigmoid(_dot(xb, wx_ref[...].astype(BF16)) + bx_ref[...])
    log_a = -C_RG * r * _softplus(-lam_ref[...])
    a = jnp.exp(log_a)
    mult = jnp.sqrt(-jnp.tanh(log_a) * (a * a + 1.0))
    return a, mult, ig


def _lru_prompt_kernel(x_ref, gate_ref, cw_ref, cb_ref, wa_ref, ba_ref, wx_ref, bx_ref, lam_ref,
                       y_ref, hl_ref, hc_ref, xp_ref, *, tt, conv_w):
    t = pl.program_id(2)

    @pl.when(t == 0)
    def _():
        hc_ref[...] = jnp.zeros_like(hc_ref)
        xp_ref[:SUBLANES, :] = jnp.zeros((SUBLANES, xp_ref.shape[1]), F32)

    x = x_ref[...]
    c = x.shape[1]
    xp_ref[SUBLANES:, :] = x
    xc = cb_ref[...] + cw_ref[conv_w - 1:conv_w, :] * x
    for k in range(1, conv_w):
        xc = xc + cw_ref[conv_w - 1 - k:conv_w - k, :] * xp_ref[SUBLANES - k:SUBLANES - k + tt, :]
    xp_ref[:SUBLANES, :] = x[tt - SUBLANES:]

    row = lax.broadcasted_iota(jnp.int32, (tt, c), 0)
    a, mult, ig = _lru_gates(xc, wa_ref, ba_ref, wx_ref, bx_ref, lam_ref)
    mult = jnp.where(row + t * tt == 0, 1.0, mult)
    bx = mult * ig * xc
    ng = tt // SUBLANES
    a = a.reshape(ng, SUBLANES, c)
    bx = bx.reshape(ng, SUBLANES, c)
    sub = lax.broadcasted_iota(jnp.int32, (ng, SUBLANES, c), 1)
    d = 1
    while d < SUBLANES:
        keep = sub >= d
        a_sh = jnp.where(keep, pltpu.roll(a, d, axis=1), 1.0)
        b_sh = jnp.where(keep, pltpu.roll(bx, d, axis=1), 0.0)
        bx = a * b_sh + bx
        a = a * a_sh
        d *= 2
    h = hc_ref[...]
    groups = []
    for g in range(ng):
        groups.append(a[g] * h + bx[g])
        h = groups[-1][SUBLANES - 1:]
    y_ref[...] = (jnp.concatenate(groups, axis=0) * _silu(gate_ref[...])).astype(BF16)
    hc_ref[...] = h
    hl_ref[...] = h


def lru_prompt(proj, conv_w, conv_b, wa, ba, wx, bx, lam):
    b, s, two_br = proj.shape
    br = two_br // 2
    nblk, bs_, _ = wa.shape
    cw = conv_w.shape[0]
    tt = min(s, 256)
    vec = lambda a: a.reshape(1, br)
    vspec = pl.BlockSpec((1, bs_), lambda bi, ni, ti: (0, ni))
    wspec = pl.BlockSpec((None, bs_, bs_), lambda bi, ni, ti: (ni, 0, 0))
    y, hl = pl.pallas_call(
        functools.partial(_lru_prompt_kernel, tt=tt, conv_w=cw),
        grid=(b, nblk, s // tt),
        in_specs=[pl.BlockSpec((None, tt, bs_), lambda bi, ni, ti: (bi, ti, ni)),
                  pl.BlockSpec((None, tt, bs_), lambda bi, ni, ti: (bi, ti, nblk + ni)),
                  pl.BlockSpec((cw, bs_), lambda bi, ni, ti: (0, ni)),
                  vspec, wspec, vspec, wspec, vspec, vspec],
        out_specs=[pl.BlockSpec((None, tt, bs_), lambda bi, ni, ti: (bi, ti, ni)),
                   pl.BlockSpec((None, 1, bs_), lambda bi, ni, ti: (bi, 0, ni))],
        out_shape=[jax.ShapeDtypeStruct((b, s, br), BF16), jax.ShapeDtypeStruct((b, 1, br), F32)],
        scratch_shapes=[pltpu.VMEM((1, bs_), F32), pltpu.VMEM((SUBLANES + tt, bs_), F32)],
        compiler_params=_params(3),
        name="lru_prompt",
    )(proj, proj, conv_w, vec(conv_b), wa, vec(ba), wx, vec(bx), vec(lam))
    return y, hl.reshape(b, br)


def _lru_sample_kernel(x_ref, gate_ref, buf_ref, h0_ref, cw_ref, cb_ref, wa_ref, ba_ref, wx_ref, bx_ref,
                       lam_ref, y_ref, h_ref, *, conv_w):
    x = x_ref[...]
    xc = cb_ref[...] + cw_ref[conv_w - 1:conv_w, :] * x
    for w in range(conv_w - 1):
        xc = xc + cw_ref[w:w + 1, :] * buf_ref[:, w, :]
    a, mult, ig = _lru_gates(xc, wa_ref, ba_ref, wx_ref, bx_ref, lam_ref)
    h = a * h0_ref[...] + mult * ig * xc
    h_ref[...] = h
    y_ref[...] = (h * _silu(gate_ref[...])).astype(BF16)


def lru_sample(proj_s, conv_buf, h0, conv_w, conv_b, wa, ba, wx, bx, lam):
    bs, two_br = proj_s.shape
    br = two_br // 2
    nblk, bs_, _ = wa.shape
    cw = conv_w.shape[0]
    vec = lambda a: a.reshape(1, br)
    vspec = pl.BlockSpec((1, bs_), lambda ni: (0, ni))
    wspec = pl.BlockSpec((None, bs_, bs_), lambda ni: (ni, 0, 0))
    rspec = pl.BlockSpec((bs, bs_), lambda ni: (0, ni))
    return pl.pallas_call(
        functools.partial(_lru_sample_kernel, conv_w=cw),
        grid=(nblk,),
        in_specs=[rspec,
                  pl.BlockSpec((bs, bs_), lambda ni: (0, nblk + ni)),
                  pl.BlockSpec((bs, cw - 1, bs_), lambda ni: (0, 0, ni)),
                  rspec,
                  pl.BlockSpec((cw, bs_), lambda ni: (0, ni)),
                  vspec, wspec, vspec, wspec, vspec, vspec],
        out_specs=[rspec, rspec],
        out_shape=[jax.ShapeDtypeStruct((bs, br), BF16), jax.ShapeDtypeStruct((bs, br), F32)],
        compiler_params=_params(1),
        name="lru_sample",
    )(proj_s, proj_s, conv_buf, h0, conv_w, vec(conv_b), wa, vec(ba), wx, vec(bx), vec(lam))


def _rope(x, cos, sin):
    half = x.shape[-1] // 2
    x1, x2 = x[:, :half], x[:, half:]
    return jnp.concatenate([x1 * cos - x2 * sin, x1 * sin + x2 * cos], axis=-1)


def _groupnorm(o, gain):
    c = o - jnp.mean(o, axis=-1, keepdims=True)
    return c * lax.rsqrt(jnp.mean(c * c, axis=-1, keepdims=True) + EPS) * gain


def _ret_prompt_kernel(q_ref, k_ref, v_ref, gate_ref, cos_ref, sin_ref, lg_ref, gain_ref,
                       y_ref, s_ref, st_ref, *, chunk, n_chunks, group):
    lg = lg_ref[:, :1]
    idx = lax.broadcasted_iota(jnp.int32, (chunk, 1), 0).astype(F32)
    ii = lax.broadcasted_iota(jnp.int32, (chunk, chunk), 0)
    jj = lax.broadcasted_iota(jnp.int32, (chunk, chunk), 1)
    diff = (ii - jj).astype(F32)
    decay = jnp.where(diff >= 0, jnp.exp(diff * lg), 0.0)
    q_dec = jnp.exp((idx + 1.0) * lg)
    k_dec = jnp.exp((chunk - 1.0 - idx) * lg)
    chunk_dec = jnp.exp(chunk * lg)
    gain = gain_ref[...]
    st_ref[...] = jnp.zeros_like(st_ref)

    q_dec_rows = jnp.concatenate([q_dec] * group, axis=0)
    k_dec_rows = jnp.concatenate([k_dec] * group, axis=0)

    def body(it, carry):
        rows = pl.ds(pl.multiple_of(it * (group * chunk), group * chunk), group * chunk)
        cos = cos_ref[rows, :]
        sin = sin_ref[rows, :]
        qc = _rope(q_ref[rows, :], cos, sin)
        kc = _rope(k_ref[rows, :], cos, sin) * (RET_DK ** -0.5)
        vc = v_ref[rows, :].astype(BF16)
        qb = qc.astype(BF16)
        kb = kc.astype(BF16)
        qd = (qc * q_dec_rows).astype(BF16)
        kd = (kc * k_dec_rows).astype(BF16)
        sl = [slice(u * chunk, (u + 1) * chunk) for u in range(group)]
        intra, upd = [], []
        for u in range(group):
            scores = _dot_nt(qb[sl[u]], kb[sl[u]]) * decay
            intra.append(_dot(scores.astype(BF16), vc[sl[u]]))
            upd.append(_dot_tn(kd[sl[u]], vc[sl[u]]))
        st = st_ref[...]
        outs = []
        for u in range(group):
            outs.append(intra[u] + _dot(qd[sl[u]], st.astype(BF16)))
            st = chunk_dec * st + upd[u]
        st_ref[...] = st
        o = jnp.concatenate(outs, axis=0)
        y_ref[rows, :] = (_groupnorm(o, gain) * _silu(gate_ref[rows, :])).astype(BF16)
        return carry

    lax.fori_loop(0, n_chunks // group, body, 0)
    s_ref[...] = st_ref[...]


def _ret_log_gamma(n_heads):
    lg = np.log1p(-np.exp2(-5.0 - np.arange(n_heads, dtype=np.float32))).astype(np.float32)
    return jnp.asarray(np.broadcast_to(lg[:, None, None], (n_heads, 1, LANES)).copy())


def _rope_tables(pos):
    half = RET_DK // 2
    inv_freq = ROPE_BASE ** (-jnp.arange(half, dtype=F32) / half)
    ang = pos[:, None] * inv_freq[None, :]
    return jnp.cos(ang), jnp.sin(ang)


def ret_prompt(proj, gain, n_heads):
    b, s, _ = proj.shape
    dk, dv = RET_DK, RET_DV
    qk = n_heads * dk
    chunk = RET_CHUNK if s % RET_CHUNK == 0 else s
    cos, sin = _rope_tables(jnp.arange(s, dtype=F32))
    half = dk // 2
    koff = qk // dk
    voff = 2 * qk // dv
    goff = (2 * qk + n_heads * dv) // dv
    return pl.pallas_call(
        functools.partial(_ret_prompt_kernel, chunk=chunk, n_chunks=s // chunk,
                          group=_chunk_group(s // chunk, RET_GROUP)),
        grid=(b, n_heads),
        in_specs=[pl.BlockSpec((None, s, dk), lambda bi, hi: (bi, 0, hi)),
                  pl.BlockSpec((None, s, dk), lambda bi, hi: (bi, 0, koff + hi)),
                  pl.BlockSpec((None, s, dv), lambda bi, hi: (bi, 0, voff + hi)),
                  pl.BlockSpec((None, s, dv), lambda bi, hi: (bi, 0, goff + hi)),
                  pl.BlockSpec((s, half), lambda bi, hi: (0, 0)),
                  pl.BlockSpec((s, half), lambda bi, hi: (0, 0)),
                  pl.BlockSpec((None, 1, LANES), lambda bi, hi: (hi, 0, 0)),
                  pl.BlockSpec((1, dv), lambda bi, hi: (0, hi))],
        out_specs=[pl.BlockSpec((None, s, dv), lambda bi, hi: (bi, 0, hi)),
                   pl.BlockSpec((None, None, dk, dv), lambda bi, hi: (bi, hi, 0, 0))],
        out_shape=[jax.ShapeDtypeStruct((b, s, n_heads * dv), BF16),
                   jax.ShapeDtypeStruct((b, n_heads, dk, dv), F32)],
        scratch_shapes=[pltpu.VMEM((dk, dv), F32)],
        compiler_params=_params(2),
        name="ret_prompt",
    )(proj, proj, proj, proj, cos, sin, _ret_log_gamma(n_heads), gain.reshape(1, n_heads * dv))


def _ret_sample_kernel(q_ref, k_ref, v_ref, gate_ref, cos_ref, sin_ref, lg_ref, gain_ref, s0_ref,
                       y_ref, s_ref):
    dk, dv = RET_DK, RET_DV
    gamma = jnp.exp(lg_ref[:, :1])
    cos = cos_ref[...]
    sin = sin_ref[...]
    q = _rope(q_ref[...], cos, sin)
    k = _rope(k_ref[...], cos, sin) * (dk ** -0.5)
    v = v_ref[...]
    s0 = s0_ref[...]
    score = jnp.sum(q * k, axis=-1, keepdims=True)
    q8 = jnp.broadcast_to(q * gamma, (SUBLANES, dk)).astype(BF16)
    o = score * v + _dot(q8, s0.astype(BF16))[:1]
    kcol = _col_bcast(k, dk)
    for j in range(dv // dk):
        cols = slice(j * dk, (j + 1) * dk)
        s_ref[:, cols] = gamma * s0[:, cols] + kcol * v[:, cols]
    y_ref[...] = (_groupnorm(o, gain_ref[...]) * _silu(gate_ref[...])).astype(BF16)


def ret_sample(proj_s, gain, s0, n_heads):
    bs, ret_in = proj_s.shape
    dk, dv = RET_DK, RET_DV
    qk = n_heads * dk
    cos, sin = _rope_tables(jnp.full((1,), PAST_LEN, F32))
    half = dk // 2
    koff = qk // dk
    voff = 2 * qk // dv
    goff = (2 * qk + n_heads * dv) // dv
    p3 = proj_s.reshape(bs, 1, ret_in)
    y, s = pl.pallas_call(
        _ret_sample_kernel,
        grid=(bs, n_heads),
        in_specs=[pl.BlockSpec((None, 1, dk), lambda bi, hi: (bi, 0, hi)),
                  pl.BlockSpec((None, 1, dk), lambda bi, hi: (bi, 0, koff + hi)),
                  pl.BlockSpec((None, 1, dv), lambda bi, hi: (bi, 0, voff + hi)),
                  pl.BlockSpec((None, 1, dv), lambda bi, hi: (bi, 0, goff + hi)),
                  pl.BlockSpec((1, half), lambda bi, hi: (0, 0)),
                  pl.BlockSpec((1, half), lambda bi, hi: (0, 0)),
                  pl.BlockSpec((None, 1, LANES), lambda bi, hi: (hi, 0, 0)),
                  pl.BlockSpec((1, dv), lambda bi, hi: (0, hi)),
                  pl.BlockSpec((None, None, dk, dv), lambda bi, hi: (bi, hi, 0, 0))],
        out_specs=[pl.BlockSpec((None, 1, dv), lambda bi, hi: (bi, 0, hi)),
                   pl.BlockSpec((None, None, dk, dv), lambda bi, hi: (bi, hi, 0, 0))],
        out_shape=[jax.ShapeDtypeStruct((bs, 1, n_heads * dv), BF16),
                   jax.ShapeDtypeStruct((bs, n_heads, dk, dv), F32)],
        compiler_params=_params(2),
        name="ret_sample",
    )(p3, p3, p3, p3, cos, sin, _ret_log_gamma(n_heads), gain.reshape(1, n_heads * dv), s0)
    return y.reshape(bs, n_heads * dv), s


def _headnorm(o, gain):
    return o * lax.rsqrt(jnp.mean(o * o, axis=-1, keepdims=True) + EPS) * gain


def _gla_log_alpha(low_ref_val, gw_ref, gb_ref, rank):
    lane = lax.broadcasted_iota(jnp.int32, low_ref_val.shape, 1)
    low = jnp.where(lane < rank, low_ref_val, 0.0).astype(BF16)
    z = _dot(low, gw_ref[...].astype(BF16)) + gb_ref[...]
    return (jnp.minimum(z, 0.0) - jnp.log1p(jnp.exp(-jnp.abs(z)))) / GLA_TAU


def _gla_prompt_kernel(q_ref, k_ref, v_ref, gate_ref, low_ref, gw_ref, gb_ref, gain_ref,
                       y_ref, s_ref, st_ref, *, chunk, n_chunks, group, rank):
    dk, dv = GLA_DK, GLA_DV
    sub = lax.broadcasted_iota(jnp.int32, (group * chunk, dk), 0) % chunk
    ii = lax.broadcasted_iota(jnp.int32, (chunk, chunk), 0)
    jj = lax.broadcasted_iota(jnp.int32, (chunk, chunk), 1)
    causal = jj <= ii
    gain = gain_ref[...]
    st_ref[...] = jnp.zeros_like(st_ref)

    def body(it, carry):
        rows = pl.ds(pl.multiple_of(it * (group * chunk), group * chunk), group * chunk)
        bcum = _gla_log_alpha(low_ref[rows, :], gw_ref, gb_ref, rank)
        d = 1
        while d < chunk:
            bcum = bcum + jnp.where(sub >= d, pltpu.roll(bcum, d, axis=0), 0.0)
            d *= 2
        sl = [slice(u * chunk, (u + 1) * chunk) for u in range(group)]
        blast = [bcum[(u + 1) * chunk - 1:(u + 1) * chunk] for u in range(group)]
        blast_rows = jnp.concatenate([jnp.broadcast_to(b, (chunk, dk)) for b in blast], axis=0)
        kc = k_ref[rows, :]
        vc = v_ref[rows, :].astype(BF16)
        qe = (q_ref[rows, :] * (dk ** -0.5) * jnp.exp(bcum)).astype(BF16)
        ke = (kc * jnp.exp(-bcum)).astype(BF16)
        kd = (kc * jnp.exp(blast_rows - bcum)).astype(BF16)
        intra, upd, dec = [], [], []
        for u in range(group):
            scores = jnp.where(causal, _dot_nt(qe[sl[u]], ke[sl[u]]), 0.0)
            intra.append(_dot(scores.astype(BF16), vc[sl[u]]))
            upd.append(_dot_tn(kd[sl[u]], vc[sl[u]]))
            dec.append(_col_bcast(jnp.exp(blast[u]), dk))
        st = st_ref[...]
        outs = []
        for u in range(group):
            outs.append(intra[u] + _dot(qe[sl[u]], st.astype(BF16)))
            st = jnp.concatenate([dec[u] * st[:, j * dk:(j + 1) * dk] + upd[u][:, j * dk:(j + 1) * dk]
                                  for j in range(dv // dk)], axis=1)
        st_ref[...] = st
        o = jnp.concatenate(outs, axis=0)
        y_ref[rows, :] = (_headnorm(o, gain) * _silu(gate_ref[rows, :])).astype(BF16)
        return carry

    lax.fori_loop(0, n_chunks // group, body, 0)
    s_ref[...] = st_ref[...]


def _gla_offsets(n_heads):
    dk, dv = GLA_DK, GLA_DV
    qk = n_heads * dk
    koff = qk // dk
    voff = 2 * qk // dv
    goff = (2 * qk + n_heads * dv) // dv
    loff = (2 * qk + 2 * n_heads * dv) // LANES
    return koff, voff, goff, loff


def _pad_rank(gate_w):
    rank = gate_w.shape[0]
    return jnp.pad(gate_w, ((0, LANES - rank), (0, 0)))


def gla_prompt(proj, gate_w, gate_b, gain, n_heads):
    b, s, _ = proj.shape
    dk, dv = GLA_DK, GLA_DV
    rank = gate_w.shape[0]
    chunk = GLA_CHUNK if s % GLA_CHUNK == 0 else s
    koff, voff, goff, loff = _gla_offsets(n_heads)
    return pl.pallas_call(
        functools.partial(_gla_prompt_kernel, chunk=chunk, n_chunks=s // chunk,
                          group=_chunk_group(s // chunk, GLA_GROUP), rank=rank),
        grid=(b, n_heads),
        in_specs=[pl.BlockSpec((None, s, dk), lambda bi, hi: (bi, 0, hi)),
                  pl.BlockSpec((None, s, dk), lambda bi, hi: (bi, 0, koff + hi)),
                  pl.BlockSpec((None, s, dv), lambda bi, hi: (bi, 0, voff + hi)),
                  pl.BlockSpec((None, s, dv), lambda bi, hi: (bi, 0, goff + hi)),
                  pl.BlockSpec((None, s, LANES), lambda bi, hi: (bi, 0, loff)),
                  pl.BlockSpec((LANES, dk), lambda bi, hi: (0, hi)),
                  pl.BlockSpec((1, dk), lambda bi, hi: (0, hi)),
                  pl.BlockSpec((1, dv), lambda bi, hi: (0, hi))],
        out_specs=[pl.BlockSpec((None, s, dv), lambda bi, hi: (bi, 0, hi)),
                   pl.BlockSpec((None, None, dk, dv), lambda bi, hi: (bi, hi, 0, 0))],
        out_shape=[jax.ShapeDtypeStruct((b, s, n_heads * dv), BF16),
                   jax.ShapeDtypeStruct((b, n_heads, dk, dv), F32)],
        scratch_shapes=[pltpu.VMEM((dk, dv), F32)],
        compiler_params=_params(2),
        name="gla_prompt",
    )(proj, proj, proj, proj, proj, _pad_rank(gate_w), gate_b.reshape(1, -1), gain.reshape(1, -1))


def _gla_sample_kernel(q_ref, k_ref, v_ref, gate_ref, low_ref, gw_ref, gb_ref, gain_ref, s0_ref,
                       y_ref, s_ref, *, rank):
    dk, dv = GLA_DK, GLA_DV
    low8 = jnp.broadcast_to(low_ref[...], (SUBLANES, LANES))
    g = _gla_log_alpha(low8, gw_ref, gb_ref, rank)[:1]
    q = q_ref[...] * (dk ** -0.5)
    k = k_ref[...]
    v = v_ref[...]
    s0 = s0_ref[...]
    qe = q * jnp.exp(g)
    ke = k * jnp.exp(-g)
    score = jnp.sum(qe * ke, axis=-1, keepdims=True)
    qe8 = jnp.broadcast_to(qe, (SUBLANES, dk)).astype(BF16)
    o = score * v + _dot(qe8, s0.astype(BF16))[:1]
    dec = _col_bcast(jnp.exp(g), dk)
    kcol = _col_bcast(k * jnp.exp(g - g), dk)
    for j in range(dv // dk):
        cols = slice(j * dk, (j + 1) * dk)
        s_ref[:, cols] = dec * s0[:, cols] + kcol * v[:, cols]
    y_ref[...] = (_headnorm(o, gain_ref[...]) * _silu(gate_ref[...])).astype(BF16)


def gla_sample(proj_s, gate_w, gate_b, gain, s0, n_heads):
    bs, gla_in = proj_s.shape
    dk, dv = GLA_DK, GLA_DV
    rank = gate_w.shape[0]
    koff, voff, goff, loff = _gla_offsets(n_heads)
    p3 = proj_s.reshape(bs, 1, gla_in)
    y, s = pl.pallas_call(
        functools.partial(_gla_sample_kernel, rank=rank),
        grid=(bs, n_heads),
        in_specs=[pl.BlockSpec((None, 1, dk), lambda bi, hi: (bi, 0, hi)),
                  pl.BlockSpec((None, 1, dk), lambda bi, hi: (bi, 0, koff + hi)),
                  pl.BlockSpec((None, 1, dv), lambda bi, hi: (bi, 0, voff + hi)),
                  pl.BlockSpec((None, 1, dv), lambda bi, hi: (bi, 0, goff + hi)),
                  pl.BlockSpec((None, 1, LANES), lambda bi, hi: (bi, 0, loff)),
                  pl.BlockSpec((LANES, dk), lambda bi, hi: (0, hi)),
                  pl.BlockSpec((1, dk), lambda bi, hi: (0, hi)),
                  pl.BlockSpec((1, dv), lambda bi, hi: (0, hi)),
                  pl.BlockSpec((None, None, dk, dv), lambda bi, hi: (bi, hi, 0, 0))],
        out_specs=[pl.BlockSpec((None, 1, dv), lambda bi, hi: (bi, 0, hi)),
                   pl.BlockSpec((None, None, dk, dv), lambda bi, hi: (bi, hi, 0, 0))],
        out_shape=[jax.ShapeDtypeStruct((bs, 1, n_heads * dv), BF16),
                   jax.ShapeDtypeStruct((bs, n_heads, dk, dv), F32)],
        compiler_params=_params(2),
        name="gla_sample",
    )(p3, p3, p3, p3, p3, _pad_rank(gate_w), gate_b.reshape(1, -1), gain.reshape(1, -1), s0)
    return y.reshape(bs, n_heads * dv), s


def _attention_bias_tables(rel_bias, n_heads):
    blk = A_BLOCK
    ng = len(A_GROUPS)
    onehot = np.zeros((ng, 2 * blk, NUM_BUCKETS), np.float32)
    for g, (window, dil) in enumerate(A_GROUPS):
        n_keys = window // dil
        assert n_keys == blk, "one 128-key band per dilated stream is assumed"
        u = np.arange(n_keys + 1)
        onehot[g, u, _t5_bucket(dil * (n_keys - u))] = 1.0
    tab = jnp.einsum("gub,bgh->guh", jnp.asarray(onehot), rel_bias.astype(F32).reshape(NUM_BUCKETS, ng, n_heads),
                     precision=lax.Precision.HIGHEST)
    tab_t = tab.transpose(0, 2, 1)
    return tab.transpose(2, 0, 1), tab_t[:, :, :blk], tab_t[:, :, blk:blk + 1]


def kernel(x_prompt, x_sample, cache_k_w128, cache_v_w128, cache_k_w512, cache_v_w512, cache_k_w2048, cache_v_w2048, state_lru_h, state_lru_conv, state_ret, state_gla, norm_pre, norm_post, rel_bias, a_w_in, a_w_out, b_w_in, b_conv_w, b_conv_b, b_gate_a_w, b_gate_a_b, b_gate_x_w, b_gate_x_b, b_lambda, b_w_out, c_w_in, c_norm, c_w_out, d_w_in, d_gate_w, d_gate_b, d_norm, d_w_out):
    b, s, d = x_prompt.shape
    bs = x_sample.shape[0]
    assert x_sample.shape[1] == 1, "one new token per sequence"
    depth = norm_pre.shape[0]
    k_caches = (cache_k_w128, cache_k_w512, cache_k_w2048)
    v_caches = (cache_v_w128, cache_v_w512, cache_v_w2048)
    ng = len(A_GROUPS)
    for (window, dil), kc in zip(A_GROUPS, k_caches):
        assert kc.shape[2] == window and s % (dil * A_BLOCK) == 0

    xp = x_prompt.reshape(b * s, d)
    xs = x_sample.reshape(bs, d)
    hp = norm_cast(xp, norm_pre[0])
    hs = norm_cast(xs, norm_pre[0])

    kp_rows = [[] for _ in A_GROUPS]
    vp_rows = [[] for _ in A_GROUPS]
    ks_rows = [[] for _ in A_GROUPS]
    vs_rows = [[] for _ in A_GROUPS]
    lru_h_p, lru_h_s, lru_c_p, lru_c_s = [], [], [], []
    ret_p, ret_s, gla_p, gla_s = [], [], [], []

    for i in range(depth):
        kind, j = i % 4, i // 4
        if kind == 0:
            n_heads = a_w_out.shape[1] // A_HEAD_DIM
            width = n_heads * A_HEAD_DIM
            full = [min(window, s) == s for window, _ in A_GROUPS]
            cuts = {0, 3 * ng + 1}
            for which in (1, 2):
                for g in range(ng):
                    if full[g]:
                        cuts |= {which * ng + g, which * ng + g + 1}
            cuts = sorted(cuts)
            unit_p, unit_s = {}, {}
            for u0, u1 in zip(cuts[:-1], cuts[1:]):
                arr_p, arr_s = matmul(hp, hs, a_w_in, j, col0=u0 * width, ncols=(u1 - u0) * width)
                for u in range(u0, u1):
                    unit_p[u] = (arr_p.reshape(b, s, -1), u - u0)
                    unit_s[u] = arr_s[:, (u - u0) * width:(u - u0 + 1) * width]
            bias_tab, bias_past, bias_self = _attention_bias_tables(rel_bias, n_heads)
            yp = attn_prompt([unit_p[g] for g in range(ng)], [unit_p[ng + g] for g in range(ng)],
                             [unit_p[2 * ng + g] for g in range(ng)], unit_p[3 * ng],
                             bias_tab, n_heads).reshape(b * s, width)
            ys = attn_sample([unit_s[g] for g in range(ng)], [unit_s[ng + g] for g in range(ng)],
                             [unit_s[2 * ng + g] for g in range(ng)], unit_s[3 * ng],
                             [c[j] for c in k_caches], [c[j] for c in v_caches],
                             bias_past, bias_self, n_heads)
            for g, (window, _) in enumerate(A_GROUPS):
                keep = min(window, s)

                def tail(unit, keep=keep):
                    arr, off = unit
                    return arr[:, s - keep:, off * width:(off + 1) * width].reshape(b, keep, n_heads, A_HEAD_DIM)

                kp_rows[g].append(tail(unit_p[ng + g]))
                vp_rows[g].append(tail(unit_p[2 * ng + g]))
                ks_rows[g].append(unit_s[ng + g].reshape(bs, 1, n_heads, A_HEAD_DIM))
                vs_rows[g].append(unit_s[2 * ng + g].reshape(bs, 1, n_heads, A_HEAD_DIM))
            w_out = a_w_out
        elif kind == 1:
            br = b_w_out.shape[1]
            proj, proj_s = matmul(hp, hs, b_w_in, j)
            prm = (b_conv_w[j], b_conv_b[j], b_gate_a_w[j], b_gate_a_b[j], b_gate_x_w[j], b_gate_x_b[j], b_lambda[j])
            proj3 = proj.reshape(b, s, -1)
            yp, h_last = lru_prompt(proj3, *prm)
            yp = yp.reshape(b * s, br)
            ys, h_new = lru_sample(proj_s, state_lru_conv[j], state_lru_h[j], *prm)
            cw = b_conv_w.shape[1]
            lru_h_p.append(h_last)
            lru_c_p.append(proj3[:, s - (cw - 1):, :br])
            lru_h_s.append(h_new)
            lru_c_s.append(jnp.concatenate([state_lru_conv[j], proj_s[:, None, :br]], axis=1)[:, 1:])
            w_out = b_w_out
        elif kind == 2:
            br = c_w_out.shape[1]
            n_heads = br // RET_DV
            proj, proj_s = matmul(hp, hs, c_w_in, j)
            yp, st = ret_prompt(proj.reshape(b, s, -1), c_norm[j], n_heads)
            yp = yp.reshape(b * s, br)
            ys, st_s = ret_sample(proj_s, c_norm[j], state_ret[j], n_heads)
            ret_p.append(st)
            ret_s.append(st_s)
            w_out = c_w_out
        else:
            br = d_w_out.shape[1]
            n_heads = br // GLA_DV
            proj, proj_s = matmul(hp, hs, jnp.swapaxes(d_w_in, 1, 2), j, transposed=True)
            yp, st = gla_prompt(proj.reshape(b, s, -1), d_gate_w[j], d_gate_b[j], d_norm[j], n_heads)
            yp = yp.reshape(b * s, br)
            ys, st_s = gla_sample(proj_s, d_gate_w[j], d_gate_b[j], d_norm[j], state_gla[j], n_heads)
            gla_p.append(st)
            gla_s.append(st_s)
            w_out = d_w_out
        op, os_ = matmul(yp, ys, w_out, j)
        g_next = norm_pre[i + 1] if i + 1 < depth else None
        xp, hp = residual_norm(xp, op, norm_post[i], g_next)
        xs, hs = residual_norm(xs, os_, norm_post[i], g_next)

    return (xp.reshape(b, s, d), xs.reshape(bs, 1, d),
            jnp.stack(kp_rows[0]), jnp.stack(ks_rows[0]), jnp.stack(vp_rows[0]), jnp.stack(vs_rows[0]),
            jnp.stack(kp_rows[1]), jnp.stack(ks_rows[1]), jnp.stack(vp_rows[1]), jnp.stack(vs_rows[1]),
            jnp.stack(kp_rows[2]), jnp.stack(ks_rows[2]), jnp.stack(vp_rows[2]), jnp.stack(vs_rows[2]),
            jnp.stack(lru_h_p), jnp.stack(lru_h_s), jnp.stack(lru_c_p), jnp.stack(lru_c_s),
            jnp.stack(ret_p), jnp.stack(ret_s), jnp.stack(gla_p), jnp.stack(gla_s))
```

```python
import functools

import numpy as np
import jax
import jax.numpy as jnp
from jax import lax
from jax.experimental import pallas as pl
from jax.experimental.pallas import tpu as pltpu

F32 = jnp.float32
BF16 = jnp.bfloat16

PAST_LEN = 8192
EPS = 1e-6
NEG_INF = -1e30
A_GROUPS = ((128, 1), (512, 4), (2048, 16))
A_HEAD_DIM = 128
A_BLOCK = 128
ATTN_UNROLL = 16
NUM_BUCKETS = 32
MAX_DISTANCE = 2048
C_RG = 8.0
LRU_TIME_TILE = 512
RET_DK = 256
RET_DV = 512
RET_CHUNK = 128
ROPE_BASE = 10000.0
GLA_DK = 256
GLA_DV = 512
GLA_TAU = 16.0
GLA_CHUNK = 64
GLA_GROUP = 8
RET_GROUP = 4

LANES = 128
SUBLANES = 8
V7X_VMEM_LIMIT_BYTES = 56 * 1024 * 1024

_ARB = pltpu.ARBITRARY


def _params(n_grid):
    return pltpu.CompilerParams(dimension_semantics=(_ARB,) * n_grid,
                                vmem_limit_bytes=V7X_VMEM_LIMIT_BYTES)


def _dot(a, b):
    return jnp.dot(a, b, preferred_element_type=F32)


def _dot_nt(a, b):
    return lax.dot_general(a, b, (((1,), (1,)), ((), ())), preferred_element_type=F32)


def _dot_tn(a, b):
    return lax.dot_general(a, b, (((0,), (0,)), ((), ())), preferred_element_type=F32)


def _sigmoid(x):
    return 0.5 * jnp.tanh(0.5 * x) + 0.5


def _silu(x):
    return x * _sigmoid(x)


def _softplus(x):
    return jnp.maximum(x, 0.0) + jnp.log1p(jnp.exp(-jnp.abs(x)))


def _rms(x, g):
    return x * lax.rsqrt(jnp.mean(x * x, axis=-1, keepdims=True) + EPS) * g


def _chunk_group(n_chunks, want):
    return want if n_chunks % want == 0 else 1


def _col_bcast(row, n):
    return jnp.broadcast_to(row, (n, n)).T


def _norm_kernel(x_ref, g_ref, h_ref):
    h_ref[...] = _rms(x_ref[...], g_ref[...]).astype(BF16)


def _resnorm_kernel(x_ref, y_ref, gpost_ref, gnext_ref, xo_ref, h_ref):
    xn = x_ref[...] + _rms(y_ref[...], gpost_ref[...])
    xo_ref[...] = xn
    h_ref[...] = _rms(xn, gnext_ref[...]).astype(BF16)


def _res_kernel(x_ref, y_ref, gpost_ref, xo_ref):
    xo_ref[...] = x_ref[...] + _rms(y_ref[...], gpost_ref[...])


def _row_tile(m):
    return min(m, 256)


def norm_cast(x, g):
    m, d = x.shape
    tm = _row_tile(m)
    return pl.pallas_call(
        _norm_kernel,
        grid=(m // tm,),
        in_specs=[pl.BlockSpec((tm, d), lambda i: (i, 0)),
                  pl.BlockSpec((1, d), lambda i: (0, 0))],
        out_specs=pl.BlockSpec((tm, d), lambda i: (i, 0)),
        out_shape=jax.ShapeDtypeStruct((m, d), BF16),
        compiler_params=_params(1),
        name="norm_cast",
    )(x, g.reshape(1, d))


def residual_norm(x, y, g_post, g_next):
    m, d = x.shape
    tm = _row_tile(m)
    row = pl.BlockSpec((tm, d), lambda i: (i, 0))
    vec = pl.BlockSpec((1, d), lambda i: (0, 0))
    if g_next is None:
        return pl.pallas_call(
            _res_kernel, grid=(m // tm,),
            in_specs=[row, row, vec], out_specs=row,
            out_shape=jax.ShapeDtypeStruct((m, d), F32),
            compiler_params=_params(1), name="residual",
        )(x, y, g_post.reshape(1, d)), None
    return pl.pallas_call(
        _resnorm_kernel, grid=(m // tm,),
        in_specs=[row, row, vec, vec], out_specs=[row, row],
        out_shape=[jax.ShapeDtypeStruct((m, d), F32), jax.ShapeDtypeStruct((m, d), BF16)],
        compiler_params=_params(1), name="residual_norm",
    )(x, y, g_post.reshape(1, d), g_next.reshape(1, d))


def _mm_kernel(*refs, n_parts, transposed):
    x_parts = refs[:n_parts]
    w_ref, xs_ref, o_ref, os_ref = refs[n_parts:]
    w = w_ref[...].astype(BF16)
    dot = _dot_nt if transposed else _dot
    x = jnp.concatenate([p[...] for p in x_parts], axis=1)
    o_ref[...] = dot(x, w)

    @pl.when(pl.program_id(0) == 0)
    def _():
        os_ref[...] = dot(xs_ref[...], w)


MM_COL_TILE = 512
MM_ROW_TILE = 2048
MM_ROW_TILE_LONG_K = 1024
MM_LONG_K = 4096
MM_X_PARTS = 4


def matmul(x, xs, w, layer, col0=0, ncols=None, transposed=False):
    m, k = x.shape
    ms = xs.shape[0]
    n = w.shape[1] if transposed else w.shape[2]
    ncols = n - col0 if ncols is None else ncols
    tn = MM_COL_TILE
    assert col0 % tn == 0
    tm = min(m, MM_ROW_TILE if k <= MM_LONG_K else MM_ROW_TILE_LONG_K)
    nj = pl.cdiv(ncols, tn)
    c0 = col0 // tn
    n_parts = MM_X_PARTS if k % (MM_X_PARTS * LANES) == 0 else 1
    x_specs = [pl.BlockSpec((tm, k // n_parts), lambda i, j, c=c: (i, c), pipeline_mode=pl.Buffered(1))
               for c in range(n_parts)]
    if transposed:
        w_spec = pl.BlockSpec((None, tn, k), lambda i, j: (layer, c0 + j, 0))
    else:
        w_spec = pl.BlockSpec((None, k, tn), lambda i, j: (layer, 0, c0 + j))
    return pl.pallas_call(
        functools.partial(_mm_kernel, n_parts=n_parts, transposed=transposed),
        grid=(m // tm, nj),
        in_specs=x_specs + [w_spec, pl.BlockSpec((ms, k), lambda i, j: (0, 0))],
        out_specs=[pl.BlockSpec((tm, tn), lambda i, j: (i, j)),
                   pl.BlockSpec((ms, tn), lambda i, j: (0, jnp.where(i == 0, j, nj - 1)))],
        out_shape=[jax.ShapeDtypeStruct((m, ncols), F32), jax.ShapeDtypeStruct((ms, ncols), F32)],
        compiler_params=_params(2),
        name="proj_matmul",
    )(*([x] * n_parts), w, xs)


def _t5_bucket(dist):
    n = np.asarray(dist, dtype=np.int64)
    max_exact = NUM_BUCKETS // 2
    ratio = np.log(np.maximum(n, 1) / max_exact) / np.log(MAX_DISTANCE / max_exact)
    large = np.minimum(max_exact + (ratio * (NUM_BUCKETS - max_exact)).astype(np.int64), NUM_BUCKETS - 1)
    return np.where(n < max_exact, n, large).astype(np.int32)


def _attn_prompt_kernel(*refs, dilations, seq):
    ng = len(dilations)
    qkv = refs[:3 * ng]
    gate_ref, bias_ref, y_ref, o_s, lse_s = refs[3 * ng:]
    blk = A_BLOCK
    scale = A_HEAD_DIM ** -0.5
    ii = lax.broadcasted_iota(jnp.int32, (blk, blk), 0)
    jj = lax.broadcasted_iota(jnp.int32, (blk, blk), 1)
    cur_ok = jj <= ii
    prev_ok = jj >= ii

    for g, dil in enumerate(dilations):
        q_ref, k_ref, v_ref = qkv[3 * g:3 * g + 3]
        nb = seq // (dil * blk)
        band = pltpu.roll(jnp.broadcast_to(bias_ref[g:g + 1, :], (blk, 2 * blk)), 0, axis=1,
                          stride=1, stride_axis=0)
        b_prev = band[:, :blk]
        b_cur = band[:, blk:]

        def rows_at(start, dil=dil):
            if dil > 1:
                return pl.ds(start, blk, stride=dil)
            return pl.ds(start if isinstance(start, int) else pl.multiple_of(start, blk), blk)

        def scores(idx, dil=dil, nb=nb, q_ref=q_ref, k_ref=k_ref, b_prev=b_prev, b_cur=b_cur,
                   rows_at=rows_at):
            r = idx % dil
            bi = idx // dil
            start = r + bi * (blk * dil)
            rows = rows_at(start)
            q = q_ref[rows, :].astype(BF16)
            s = [jnp.where(cur_ok, _dot_nt(q, k_ref[rows, :].astype(BF16)) * scale + b_cur, NEG_INF)]
            prows = None
            if nb > 1 and not (isinstance(bi, int) and bi == 0):
                if isinstance(bi, int):
                    prows = rows_at(start - blk * dil)
                    prev_mask = prev_ok
                else:
                    prows = rows_at(jnp.maximum(start - blk * dil, 0))
                    prev_mask = prev_ok & (bi > 0)
                s.append(jnp.where(prev_mask, _dot_nt(q, k_ref[prows, :].astype(BF16)) * scale + b_prev,
                                   NEG_INF))
            return rows, prows, s

        def softmax(s):
            m = jnp.max(functools.reduce(jnp.maximum, s), axis=-1, keepdims=True)
            p = [jnp.exp(x - m) for x in s]
            den = jnp.sum(functools.reduce(jnp.add, p), axis=-1, keepdims=True)
            return m, den, [x.astype(BF16) for x in p]

        def values(rows, prows, p, v_ref=v_ref):
            num = _dot(p[0], v_ref[rows, :].astype(BF16))
            if prows is not None:
                num = num + _dot(p[1], v_ref[prows, :].astype(BF16))
            return num

        def body(it, carry, g=g, scores=scores, softmax=softmax, values=values):
            blocks = [scores(it * ATTN_UNROLL + u) for u in range(ATTN_UNROLL)]
            probs = [softmax(s) for _, _, s in blocks]
            nums = [values(rows, prows, p) for (rows, prows, _), (_, _, p) in zip(blocks, probs)]
            for (rows, _, _), (m, den, _), num in zip(blocks, probs, nums):
                o_s[g, rows, :] = num / den
                lse_s[g, rows, :] = jnp.broadcast_to(m + jnp.log(den), (blk, A_HEAD_DIM))
            return carry

        if seq // blk == ATTN_UNROLL:
            body(0, 0)
        else:
            lax.fori_loop(0, seq // (blk * ATTN_UNROLL), body, 0)

    tr = 256

    def merge(c, carry):
        rows = pl.ds(pl.multiple_of(c * tr, tr), tr)
        lses = [lse_s[g, rows, :] for g in range(ng)]
        mx = functools.reduce(jnp.maximum, lses)
        num = jnp.zeros((tr, A_HEAD_DIM), F32)
        den = jnp.zeros((tr, A_HEAD_DIM), F32)
        for g in range(ng):
            w = jnp.exp(lses[g] - mx)
            num = num + w * o_s[g, rows, :]
            den = den + w
        y_ref[rows, :] = ((num / den) * _silu(gate_ref[rows, :])).astype(BF16)
        return carry

    lax.fori_loop(0, seq // tr, merge, 0)


def attn_prompt(qs, ks, vs, gate, bias_tab, n_heads):
    b, s, _ = gate[0].shape
    ng = len(A_GROUPS)
    dh = A_HEAD_DIM
    in_specs = []
    args = []
    units = [u for g in range(ng) for u in (qs[g], ks[g], vs[g])] + [gate]
    for arr, unit in units:
        in_specs.append(pl.BlockSpec((None, s, dh), lambda bi, hi, unit=unit: (bi, 0, unit * n_heads + hi)))
        args.append(arr)
    in_specs.append(pl.BlockSpec((None, ng, 2 * A_BLOCK), lambda bi, hi: (hi, 0, 0)))
    args.append(bias_tab)
    kern = functools.partial(_attn_prompt_kernel, dilations=tuple(d for _, d in A_GROUPS), seq=s)
    return pl.pallas_call(
        kern,
        grid=(b, n_heads),
        in_specs=in_specs,
        out_specs=pl.BlockSpec((None, s, dh), lambda bi, hi: (bi, 0, hi)),
        out_shape=jax.ShapeDtypeStruct((b, s, n_heads * dh), BF16),
        scratch_shapes=[pltpu.VMEM((ng, s, dh), F32)] * 2,
        compiler_params=_params(2),
        name="attn_prompt",
    )(*args)


def _attn_sample_kernel(q_ref, kn_ref, vn_ref, gate_ref, *refs, n_heads, n_keys):
    ng = len(A_GROUPS)
    caches = refs[:2 * ng]
    bpast_ref, bself_ref, y_ref = refs[2 * ng:]
    dh = A_HEAD_DIM
    scale = dh ** -0.5
    lane = lax.broadcasted_iota(jnp.int32, (n_heads, n_keys), 1)
    parts = []
    for g in range(ng):
        kc_ref, vc_ref = caches[2 * g], caches[2 * g + 1]
        q = q_ref[g]
        vn = vn_ref[g]

        def logits(j, acc, kc_ref=kc_ref, q=q):
            col = jnp.sum(kc_ref[j] * q, axis=-1, keepdims=True)
            return jnp.where(lane == j, col, acc)

        lp = lax.fori_loop(0, n_keys, logits, jnp.zeros((n_heads, n_keys), F32), unroll=8)
        lp = lp * scale + bpast_ref[g]
        ls = jnp.sum(kn_ref[g] * q, axis=-1, keepdims=True) * scale + bself_ref[g]
        m = jnp.maximum(jnp.max(lp, axis=-1, keepdims=True), ls)
        p = jnp.exp(lp - m)
        ps = jnp.exp(ls - m)
        den = jnp.sum(p, axis=-1, keepdims=True) + ps

        def weighted(j, acc, vc_ref=vc_ref, p=p):
            pj = jnp.sum(jnp.where(lane == j, p, 0.0), axis=-1, keepdims=True)
            return acc + pj * vc_ref[j]

        num = lax.fori_loop(0, n_keys, weighted, ps * vn, unroll=8)
        parts.append((num, m, den))
    mx = functools.reduce(jnp.maximum, [p_[1] for p_ in parts])
    num = jnp.zeros((n_heads, dh), F32)
    den = jnp.zeros((n_heads, 1), F32)
    for num_g, m_g, den_g in parts:
        w = jnp.exp(m_g - mx)
        num = num + w * num_g
        den = den + w * den_g
    y_ref[...] = ((num / den) * _silu(gate_ref[...])).astype(BF16)


def attn_sample(qs, ks, vs, gate, k_caches, v_caches, bias_past, bias_self, n_heads):
    bs, width = gate.shape
    dh = A_HEAD_DIM
    ng = len(A_GROUPS)
    n_keys = A_BLOCK
    heads = lambda rows: jnp.stack(rows, axis=1).reshape(bs, ng, n_heads, dh)
    grp_spec = pl.BlockSpec((None, ng, n_heads, dh), lambda bi: (bi, 0, 0, 0))
    args = [heads(qs), heads(ks), heads(vs), gate.reshape(bs, n_heads, dh)]
    in_specs = [grp_spec, grp_spec, grp_spec, pl.BlockSpec((None, n_heads, dh), lambda bi: (bi, 0, 0))]
    for (window, dil), kc, vc in zip(A_GROUPS, k_caches, v_caches):
        for c in (kc, vc):
            args.append(c.reshape(bs, n_keys, dil, n_heads, dh))
            in_specs.append(pl.BlockSpec((None, n_keys, None, n_heads, dh), lambda bi: (bi, 0, 0, 0, 0)))
    args += [bias_past, bias_self]
    in_specs += [pl.BlockSpec(bias_past.shape, lambda bi: (0, 0, 0)),
                 pl.BlockSpec(bias_self.shape, lambda bi: (0, 0, 0))]
    y = pl.pallas_call(
        functools.partial(_attn_sample_kernel, n_heads=n_heads, n_keys=n_keys),
        grid=(bs,),
        in_specs=in_specs,
        out_specs=pl.BlockSpec((None, n_heads, dh), lambda bi: (bi, 0, 0)),
        out_shape=jax.ShapeDtypeStruct((bs, n_heads, dh), BF16),
        compiler_params=_params(1),
        name="attn_sample",
    )(*args)
    return y.reshape(bs, width)


def _lru_gates(xc, wa_ref, ba_ref, wx_ref, bx_ref, lam_ref):
    xb = xc.astype(BF16)
    r = _sigmoid(_dot(xb, wa_ref[...].astype(BF16)) + ba_ref[...])
    ig = _sigmoid(_dot(xb, wx_ref[...].astype(BF16)) + bx_ref[...])
    log_a = -C_RG * r * _softplus(-lam_ref[...])
    a = jnp.exp(log_a)
    mult = jnp.sqrt(-jnp.tanh(log_a) * (a * a + 1.0))
    return a, mult, ig


def _lru_prompt_kernel(x_ref, gate_ref, cw_ref, cb_ref, wa_ref, ba_ref, wx_ref, bx_ref, lam_ref,
                       y_ref, hl_ref, hc_ref, xp_ref, *, tt, conv_w):
    t = pl.program_id(2)

    @pl.when(t == 0)
    def _():
        hc_ref[...] = jnp.zeros_like(hc_ref)
        xp_ref[:SUBLANES, :] = jnp.zeros((SUBLANES, xp_ref.shape[1]), F32)

    x = x_ref[...]
    c = x.shape[1]
    xp_ref[SUBLANES:, :] = x
    xc = cb_ref[...] + cw_ref[conv_w - 1:conv_w, :] * x
    for k in range(1, conv_w):
        xc = xc + cw_ref[conv_w - 1 - k:conv_w - k, :] * xp_ref[SUBLANES - k:SUBLANES - k + tt, :]
    xp_ref[:SUBLANES, :] = x[tt - SUBLANES:]

    row = lax.broadcasted_iota(jnp.int32, (tt, c), 0)
    a, mult, ig = _lru_gates(xc, wa_ref, ba_ref, wx_ref, bx_ref, lam_ref)
    mult = jnp.where(row + t * tt == 0, 1.0, mult)
    bx = mult * ig * xc
    ng = tt // SUBLANES
    a = a.reshape(ng, SUBLANES, c)
    bx = bx.reshape(ng, SUBLANES, c)
    sub = lax.broadcasted_iota(jnp.int32, (ng, SUBLANES, c), 1)
    d = 1
    while d < SUBLANES:
        keep = sub >= d
        a_sh = jnp.where(keep, pltpu.roll(a, d, axis=1), 1.0)
        b_sh = jnp.where(keep, pltpu.roll(bx, d, axis=1), 0.0)
        bx = a * b_sh + bx
        a = a * a_sh
        d *= 2
    h = hc_ref[...]
    groups = []
    for g in range(ng):
        groups.append(a[g] * h + bx[g])
        h = groups[-1][SUBLANES - 1:]
    y_ref[...] = (jnp.concatenate(groups, axis=0) * _silu(gate_ref[...])).astype(BF16)
    hc_ref[...] = h
    hl_ref[...] = h


def lru_prompt(proj, conv_w, conv_b, wa, ba, wx, bx, lam):
    b, s, two_br = proj.shape
    br = two_br // 2
    nblk, bs_, _ = wa.shape
    cw = conv_w.shape[0]
    tt = min(s, LRU_TIME_TILE)
    vec = lambda a: a.reshape(1, br)
    vspec = pl.BlockSpec((1, bs_), lambda bi, ni, ti: (0, ni))
    wspec = pl.BlockSpec((None, bs_, bs_), lambda bi, ni, ti: (ni, 0, 0))
    y, hl = pl.pallas_call(
        functools.partial(_lru_prompt_kernel, tt=tt, conv_w=cw),
        grid=(b, nblk, s // tt),
        in_specs=[pl.BlockSpec((None, tt, bs_), lambda bi, ni, ti: (bi, ti, ni)),
                  pl.BlockSpec((None, tt, bs_), lambda bi, ni, ti: (bi, ti, nblk + ni)),
                  pl.BlockSpec((cw, bs_), lambda bi, ni, ti: (0, ni)),
                  vspec, wspec, vspec, wspec, vspec, vspec],
        out_specs=[pl.BlockSpec((None, tt, bs_), lambda bi, ni, ti: (bi, ti, ni)),
                   pl.BlockSpec((None, 1, bs_), lambda bi, ni, ti: (bi, 0, ni))],
        out_shape=[jax.ShapeDtypeStruct((b, s, br), BF16), jax.ShapeDtypeStruct((b, 1, br), F32)],
        scratch_shapes=[pltpu.VMEM((1, bs_), F32), pltpu.VMEM((SUBLANES + tt, bs_), F32)],
        compiler_params=_params(3),
        name="lru_prompt",
    )(proj, proj, conv_w, vec(conv_b), wa, vec(ba), wx, vec(bx), vec(lam))
    return y, hl.reshape(b, br)


def _lru_sample_kernel(x_ref, gate_ref, buf_ref, h0_ref, cw_ref, cb_ref, wa_ref, ba_ref, wx_ref, bx_ref,
                       lam_ref, y_ref, h_ref, *, conv_w):
    x = x_ref[...]
    xc = cb_ref[...] + cw_ref[conv_w - 1:conv_w, :] * x
    for w in range(conv_w - 1):
        xc = xc + cw_ref[w:w + 1, :] * buf_ref[:, w, :]
    a, mult, ig = _lru_gates(xc, wa_ref, ba_ref, wx_ref, bx_ref, lam_ref)
    h = a * h0_ref[...] + mult * ig * xc
    h_ref[...] = h
    y_ref[...] = (h * _silu(gate_ref[...])).astype(BF16)


def lru_sample(proj_s, conv_buf, h0, conv_w, conv_b, wa, ba, wx, bx, lam):
    bs, two_br = proj_s.shape
    br = two_br // 2
    nblk, bs_, _ = wa.shape
    cw = conv_w.shape[0]
    vec = lambda a: a.reshape(1, br)
    vspec = pl.BlockSpec((1, bs_), lambda ni: (0, ni))
    wspec = pl.BlockSpec((None, bs_, bs_), lambda ni: (ni, 0, 0))
    rspec = pl.BlockSpec((bs, bs_), lambda ni: (0, ni))
    return pl.pallas_call(
        functools.partial(_lru_sample_kernel, conv_w=cw),
        grid=(nblk,),
        in_specs=[rspec,
                  pl.BlockSpec((bs, bs_), lambda ni: (0, nblk + ni)),
                  pl.BlockSpec((bs, cw - 1, bs_), lambda ni: (0, 0, ni)),
                  rspec,
                  pl.BlockSpec((cw, bs_), lambda ni: (0, ni)),
                  vspec, wspec, vspec, wspec, vspec, vspec],
        out_specs=[rspec, rspec],
        out_shape=[jax.ShapeDtypeStruct((bs, br), BF16), jax.ShapeDtypeStruct((bs, br), F32)],
        compiler_params=_params(1),
        name="lru_sample",
    )(proj_s, proj_s, conv_buf, h0, conv_w, vec(conv_b), wa, vec(ba), wx, vec(bx), vec(lam))


def _rope(x, cos, sin):
    half = x.shape[-1] // 2
    x1, x2 = x[:, :half], x[:, half:]
    return jnp.concatenate([x1 * cos - x2 * sin, x1 * sin + x2 * cos], axis=-1)


def _groupnorm(o, gain):
    c = o - jnp.mean(o, axis=-1, keepdims=True)
    return c * lax.rsqrt(jnp.mean(c * c, axis=-1, keepdims=True) + EPS) * gain


def _ret_prompt_kernel(q_ref, k_ref, v_ref, gate_ref, cos_ref, sin_ref, lg_ref, gain_ref,
                       y_ref, s_ref, st_ref, *, chunk, n_chunks, group):
    lg = lg_ref[:, :1]
    idx = lax.broadcasted_iota(jnp.int32, (chunk, 1), 0).astype(F32)
    ii = lax.broadcasted_iota(jnp.int32, (chunk, chunk), 0)
    jj = lax.broadcasted_iota(jnp.int32, (chunk, chunk), 1)
    diff = (ii - jj).astype(F32)
    decay = jnp.where(diff >= 0, jnp.exp(diff * lg), 0.0)
    q_dec = jnp.exp((idx + 1.0) * lg)
    k_dec = jnp.exp((chunk - 1.0 - idx) * lg)
    chunk_dec = jnp.exp(chunk * lg)
    gain = gain_ref[...]
    st_ref[...] = jnp.zeros_like(st_ref)

    q_dec_rows = jnp.concatenate([q_dec] * group, axis=0)
    k_dec_rows = jnp.concatenate([k_dec] * group, axis=0)

    def body(it, carry):
        rows = pl.ds(pl.multiple_of(it * (group * chunk), group * chunk), group * chunk)
        cos = cos_ref[rows, :]
        sin = sin_ref[rows, :]
        qc = _rope(q_ref[rows, :], cos, sin)
        kc = _rope(k_ref[rows, :], cos, sin) * (RET_DK ** -0.5)
        vc = v_ref[rows, :].astype(BF16)
        qb = qc.astype(BF16)
        kb = kc.astype(BF16)
        qd = (qc * q_dec_rows).astype(BF16)
        kd = (kc * k_dec_rows).astype(BF16)
        sl = [slice(u * chunk, (u + 1) * chunk) for u in range(group)]
        intra, upd = [], []
        for u in range(group):
            scores = _dot_nt(qb[sl[u]], kb[sl[u]]) * decay
            intra.append(_dot(scores.astype(BF16), vc[sl[u]]))
            upd.append(_dot_tn(kd[sl[u]], vc[sl[u]]))
        st = st_ref[...]
        outs = []
        for u in range(group):
            outs.append(intra[u] + _dot(qd[sl[u]], st.astype(BF16)))
            st = chunk_dec * st + upd[u]
        st_ref[...] = st
        o = jnp.concatenate(outs, axis=0)
        y_ref[rows, :] = (_groupnorm(o, gain) * _silu(gate_ref[rows, :])).astype(BF16)
        return carry

    lax.fori_loop(0, n_chunks // group, body, 0)
    s_ref[...] = st_ref[...]


def _ret_log_gamma(n_heads):
    lg = np.log1p(-np.exp2(-5.0 - np.arange(n_heads, dtype=np.float32))).astype(np.float32)
    return jnp.asarray(np.broadcast_to(lg[:, None, None], (n_heads, 1, LANES)).copy())


def _rope_tables(pos):
    half = RET_DK // 2
    inv_freq = ROPE_BASE ** (-jnp.arange(half, dtype=F32) / half)
    ang = pos[:, None] * inv_freq[None, :]
    return jnp.cos(ang), jnp.sin(ang)


def ret_prompt(proj, gain, n_heads):
    b, s, _ = proj.shape
    dk, dv = RET_DK, RET_DV
    qk = n_heads * dk
    chunk = RET_CHUNK if s % RET_CHUNK == 0 else s
    cos, sin = _rope_tables(jnp.arange(s, dtype=F32))
    half = dk // 2
    koff = qk // dk
    voff = 2 * qk // dv
    goff = (2 * qk + n_heads * dv) // dv
    return pl.pallas_call(
        functools.partial(_ret_prompt_kernel, chunk=chunk, n_chunks=s // chunk,
                          group=_chunk_group(s // chunk, RET_GROUP)),
        grid=(b, n_heads),
        in_specs=[pl.BlockSpec((None, s, dk), lambda bi, hi: (bi, 0, hi)),
                  pl.BlockSpec((None, s, dk), lambda bi, hi: (bi, 0, koff + hi)),
                  pl.BlockSpec((None, s, dv), lambda bi, hi: (bi, 0, voff + hi)),
                  pl.BlockSpec((None, s, dv), lambda bi, hi: (bi, 0, goff + hi)),
                  pl.BlockSpec((s, half), lambda bi, hi: (0, 0)),
                  pl.BlockSpec((s, half), lambda bi, hi: (0, 0)),
                  pl.BlockSpec((None, 1, LANES), lambda bi, hi: (hi, 0, 0)),
                  pl.BlockSpec((1, dv), lambda bi, hi: (0, hi))],
        out_specs=[pl.BlockSpec((None, s, dv), lambda bi, hi: (bi, 0, hi)),
                   pl.BlockSpec((None, None, dk, dv), lambda bi, hi: (bi, hi, 0, 0))],
        out_shape=[jax.ShapeDtypeStruct((b, s, n_heads * dv), BF16),
                   jax.ShapeDtypeStruct((b, n_heads, dk, dv), F32)],
        scratch_shapes=[pltpu.VMEM((dk, dv), F32)],
        compiler_params=_params(2),
        name="ret_prompt",
    )(proj, proj, proj, proj, cos, sin, _ret_log_gamma(n_heads), gain.reshape(1, n_heads * dv))


def _ret_sample_kernel(q_ref, k_ref, v_ref, gate_ref, cos_ref, sin_ref, lg_ref, gain_ref, s0_ref,
                       y_ref, s_ref):
    dk, dv = RET_DK, RET_DV
    bs = q_ref.shape[0]
    gamma = jnp.exp(lg_ref[:, :1])
    cos = cos_ref[...]
    sin = sin_ref[...]
    q = _rope(q_ref[...], cos, sin)
    k = _rope(k_ref[...], cos, sin) * (dk ** -0.5)
    v = v_ref[...]
    o = jnp.sum(q * k, axis=-1, keepdims=True) * v
    qg = (q * gamma).astype(BF16)
    row = lax.broadcasted_iota(jnp.int32, (bs, dv), 0)
    for b in range(bs):
        s0 = s0_ref[b]
        o = o + jnp.where(row == b, _dot(qg, s0.astype(BF16)), 0.0)
        kcol = _col_bcast(k[b:b + 1], dk)
        for j in range(dv // dk):
            cols = slice(j * dk, (j + 1) * dk)
            s_ref[b, :, cols] = gamma * s0[:, cols] + kcol * v[b:b + 1, cols]
    y_ref[...] = (_groupnorm(o, gain_ref[...]) * _silu(gate_ref[...])).astype(BF16)


def ret_sample(proj_s, gain, s0, n_heads):
    bs, ret_in = proj_s.shape
    dk, dv = RET_DK, RET_DV
    qk = n_heads * dk
    cos, sin = _rope_tables(jnp.full((1,), PAST_LEN, F32))
    half = dk // 2
    koff = qk // dk
    voff = 2 * qk // dv
    goff = (2 * qk + n_heads * dv) // dv
    state_spec = pl.BlockSpec((bs, None, dk, dv), lambda hi: (0, hi, 0, 0))
    return pl.pallas_call(
        _ret_sample_kernel,
        grid=(n_heads,),
        in_specs=[pl.BlockSpec((bs, dk), lambda hi: (0, hi)),
                  pl.BlockSpec((bs, dk), lambda hi: (0, koff + hi)),
                  pl.BlockSpec((bs, dv), lambda hi: (0, voff + hi)),
                  pl.BlockSpec((bs, dv), lambda hi: (0, goff + hi)),
                  pl.BlockSpec((1, half), lambda hi: (0, 0)),
                  pl.BlockSpec((1, half), lambda hi: (0, 0)),
                  pl.BlockSpec((None, 1, LANES), lambda hi: (hi, 0, 0)),
                  pl.BlockSpec((1, dv), lambda hi: (0, hi)),
                  state_spec],
        out_specs=[pl.BlockSpec((bs, dv), lambda hi: (0, hi)), state_spec],
        out_shape=[jax.ShapeDtypeStruct((bs, n_heads * dv), BF16),
                   jax.ShapeDtypeStruct((bs, n_heads, dk, dv), F32)],
        compiler_params=_params(1),
        name="ret_sample",
    )(proj_s, proj_s, proj_s, proj_s, cos, sin, _ret_log_gamma(n_heads), gain.reshape(1, n_heads * dv), s0)


def _headnorm(o, gain):
    return o * lax.rsqrt(jnp.mean(o * o, axis=-1, keepdims=True) + EPS) * gain


def _gla_log_alpha(low_ref_val, gw_ref, gb_ref, rank):
    lane = lax.broadcasted_iota(jnp.int32, low_ref_val.shape, 1)
    low = jnp.where(lane < rank, low_ref_val, 0.0).astype(BF16)
    z = _dot(low, gw_ref[...].astype(BF16)) + gb_ref[...]
    return (jnp.minimum(z, 0.0) - jnp.log1p(jnp.exp(-jnp.abs(z)))) / GLA_TAU


def _gla_prompt_kernel(q_ref, k_ref, v_ref, gate_ref, low_ref, gw_ref, gb_ref, gain_ref,
                       y_ref, s_ref, st_ref, *, chunk, n_chunks, group, rank):
    dk, dv = GLA_DK, GLA_DV
    sub = lax.broadcasted_iota(jnp.int32, (group * chunk, dk), 0) % chunk
    ii = lax.broadcasted_iota(jnp.int32, (chunk, chunk), 0)
    jj = lax.broadcasted_iota(jnp.int32, (chunk, chunk), 1)
    causal = jj <= ii
    gain = gain_ref[...]
    st_ref[...] = jnp.zeros_like(st_ref)

    def body(it, carry):
        rows = pl.ds(pl.multiple_of(it * (group * chunk), group * chunk), group * chunk)
        bcum = _gla_log_alpha(low_ref[rows, :], gw_ref, gb_ref, rank)
        d = 1
        while d < chunk:
            bcum = bcum + jnp.where(sub >= d, pltpu.roll(bcum, d, axis=0), 0.0)
            d *= 2
        sl = [slice(u * chunk, (u + 1) * chunk) for u in range(group)]
        blast = [bcum[(u + 1) * chunk - 1:(u + 1) * chunk] for u in range(group)]
        blast_rows = jnp.concatenate([jnp.broadcast_to(b, (chunk, dk)) for b in blast], axis=0)
        kc = k_ref[rows, :]
        vc = v_ref[rows, :].astype(BF16)
        qe = (q_ref[rows, :] * (dk ** -0.5) * jnp.exp(bcum)).astype(BF16)
        ke = (kc * jnp.exp(-bcum)).astype(BF16)
        kd = (kc * jnp.exp(blast_rows - bcum)).astype(BF16)
        intra, upd, dec = [], [], []
        for u in range(group):
            scores = jnp.where(causal, _dot_nt(qe[sl[u]], ke[sl[u]]), 0.0)
            intra.append(_dot(scores.astype(BF16), vc[sl[u]]))
            upd.append(_dot_tn(kd[sl[u]], vc[sl[u]]))
            dec.append(_col_bcast(jnp.exp(blast[u]), dk))
        st = st_ref[...]
        outs = []
        for u in range(group):
            outs.append(intra[u] + _dot(qe[sl[u]], st.astype(BF16)))
            st = jnp.concatenate([dec[u] * st[:, j * dk:(j + 1) * dk] + upd[u][:, j * dk:(j + 1) * dk]
                                  for j in range(dv // dk)], axis=1)
        st_ref[...] = st
        o = jnp.concatenate(outs, axis=0)
        y_ref[rows, :] = (_headnorm(o, gain) * _silu(gate_ref[rows, :])).astype(BF16)
        return carry

    lax.fori_loop(0, n_chunks // group, body, 0)
    s_ref[...] = st_ref[...]


def _gla_offsets(n_heads):
    dk, dv = GLA_DK, GLA_DV
    qk = n_heads * dk
    koff = qk // dk
    voff = 2 * qk // dv
    goff = (2 * qk + n_heads * dv) // dv
    loff = (2 * qk + 2 * n_heads * dv) // LANES
    return koff, voff, goff, loff


def _pad_rank(gate_w):
    rank = gate_w.shape[0]
    return jnp.pad(gate_w, ((0, LANES - rank), (0, 0)))


def gla_prompt(proj, gate_w, gate_b, gain, n_heads):
    b, s, _ = proj.shape
    dk, dv = GLA_DK, GLA_DV
    rank = gate_w.shape[0]
    chunk = GLA_CHUNK if s % GLA_CHUNK == 0 else s
    koff, voff, goff, loff = _gla_offsets(n_heads)
    return pl.pallas_call(
        functools.partial(_gla_prompt_kernel, chunk=chunk, n_chunks=s // chunk,
                          group=_chunk_group(s // chunk, GLA_GROUP), rank=rank),
        grid=(b, n_heads),
        in_specs=[pl.BlockSpec((None, s, dk), lambda bi, hi: (bi, 0, hi)),
                  pl.BlockSpec((None, s, dk), lambda bi, hi: (bi, 0, koff + hi)),
                  pl.BlockSpec((None, s, dv), lambda bi, hi: (bi, 0, voff + hi)),
                  pl.BlockSpec((None, s, dv), lambda bi, hi: (bi, 0, goff + hi)),
                  pl.BlockSpec((None, s, LANES), lambda bi, hi: (bi, 0, loff)),
                  pl.BlockSpec((LANES, dk), lambda bi, hi: (0, hi)),
                  pl.BlockSpec((1, dk), lambda bi, hi: (0, hi)),
                  pl.BlockSpec((1, dv), lambda bi, hi: (0, hi))],
        out_specs=[pl.BlockSpec((None, s, dv), lambda bi, hi: (bi, 0, hi)),
                   pl.BlockSpec((None, None, dk, dv), lambda bi, hi: (bi, hi, 0, 0))],
        out_shape=[jax.ShapeDtypeStruct((b, s, n_heads * dv), BF16),
                   jax.ShapeDtypeStruct((b, n_heads, dk, dv), F32)],
        scratch_shapes=[pltpu.VMEM((dk, dv), F32)],
        compiler_params=_params(2),
        name="gla_prompt",
    )(proj, proj, proj, proj, proj, _pad_rank(gate_w), gate_b.reshape(1, -1), gain.reshape(1, -1))


def _gla_sample_kernel(q_ref, k_ref, v_ref, gate_ref, low_ref, gw_ref, gb_ref, gain_ref, s0_ref,
                       y_ref, s_ref, *, rank):
    dk, dv = GLA_DK, GLA_DV
    bs = q_ref.shape[0]
    g = _gla_log_alpha(low_ref[...], gw_ref, gb_ref, rank)
    q = q_ref[...] * (dk ** -0.5)
    k = k_ref[...]
    v = v_ref[...]
    qe = q * jnp.exp(g)
    ke = k * jnp.exp(-g)
    kd = k * jnp.exp(g - g)
    eg = jnp.exp(g)
    o = jnp.sum(qe * ke, axis=-1, keepdims=True) * v
    qeb = qe.astype(BF16)
    row = lax.broadcasted_iota(jnp.int32, (bs, dv), 0)
    for b in range(bs):
        s0 = s0_ref[b]
        o = o + jnp.where(row == b, _dot(qeb, s0.astype(BF16)), 0.0)
        dec = _col_bcast(eg[b:b + 1], dk)
        kcol = _col_bcast(kd[b:b + 1], dk)
        for j in range(dv // dk):
            cols = slice(j * dk, (j + 1) * dk)
            s_ref[b, :, cols] = dec * s0[:, cols] + kcol * v[b:b + 1, cols]
    y_ref[...] = (_headnorm(o, gain_ref[...]) * _silu(gate_ref[...])).astype(BF16)


def gla_sample(proj_s, gate_w, gate_b, gain, s0, n_heads):
    bs, gla_in = proj_s.shape
    dk, dv = GLA_DK, GLA_DV
    rank = gate_w.shape[0]
    koff, voff, goff, loff = _gla_offsets(n_heads)
    state_spec = pl.BlockSpec((bs, None, dk, dv), lambda hi: (0, hi, 0, 0))
    return pl.pallas_call(
        functools.partial(_gla_sample_kernel, rank=rank),
        grid=(n_heads,),
        in_specs=[pl.BlockSpec((bs, dk), lambda hi: (0, hi)),
                  pl.BlockSpec((bs, dk), lambda hi: (0, koff + hi)),
                  pl.BlockSpec((bs, dv), lambda hi: (0, voff + hi)),
                  pl.BlockSpec((bs, dv), lambda hi: (0, goff + hi)),
                  pl.BlockSpec((bs, LANES), lambda hi: (0, loff)),
                  pl.BlockSpec((LANES, dk), lambda hi: (0, hi)),
                  pl.BlockSpec((1, dk), lambda hi: (0, hi)),
                  pl.BlockSpec((1, dv), lambda hi: (0, hi)),
                  state_spec],
        out_specs=[pl.BlockSpec((bs, dv), lambda hi: (0, hi)), state_spec],
        out_shape=[jax.ShapeDtypeStruct((bs, n_heads * dv), BF16),
                   jax.ShapeDtypeStruct((bs, n_heads, dk, dv), F32)],
        compiler_params=_params(1),
        name="gla_sample",
    )(proj_s, proj_s, proj_s, proj_s, proj_s, _pad_rank(gate_w), gate_b.reshape(1, -1),
      gain.reshape(1, -1), s0)


def _attention_bias_tables(rel_bias, n_heads):
    blk = A_BLOCK
    ng = len(A_GROUPS)
    onehot = np.zeros((ng, 2 * blk, NUM_BUCKETS), np.float32)
    for g, (window, dil) in enumerate(A_GROUPS):
        n_keys = window // dil
        assert n_keys == blk, "one 128-key band per dilated stream is assumed"
        u = np.arange(n_keys + 1)
        onehot[g, u, _t5_bucket(dil * (n_keys - u))] = 1.0
    tab = jnp.einsum("gub,bgh->guh", jnp.asarray(onehot), rel_bias.astype(F32).reshape(NUM_BUCKETS, ng, n_heads),
                     precision=lax.Precision.HIGHEST)
    tab_t = tab.transpose(0, 2, 1)
    return tab.transpose(2, 0, 1), tab_t[:, :, :blk], tab_t[:, :, blk:blk + 1]


def kernel(x_prompt, x_sample, cache_k_w128, cache_v_w128, cache_k_w512, cache_v_w512, cache_k_w2048, cache_v_w2048, state_lru_h, state_lru_conv, state_ret, state_gla, norm_pre, norm_post, rel_bias, a_w_in, a_w_out, b_w_in, b_conv_w, b_conv_b, b_gate_a_w, b_gate_a_b, b_gate_x_w, b_gate_x_b, b_lambda, b_w_out, c_w_in, c_norm, c_w_out, d_w_in, d_gate_w, d_gate_b, d_norm, d_w_out):
    b, s, d = x_prompt.shape
    bs = x_sample.shape[0]
    assert x_sample.shape[1] == 1, "one new token per sequence"
    depth = norm_pre.shape[0]
    k_caches = (cache_k_w128, cache_k_w512, cache_k_w2048)
    v_caches = (cache_v_w128, cache_v_w512, cache_v_w2048)
    ng = len(A_GROUPS)
    for (window, dil), kc in zip(A_GROUPS, k_caches):
        assert kc.shape[2] == window and s % (dil * A_BLOCK) == 0

    xp = x_prompt.reshape(b * s, d)
    xs = x_sample.reshape(bs, d)
    hp = norm_cast(xp, norm_pre[0])
    hs = norm_cast(xs, norm_pre[0])

    kp_rows = [[] for _ in A_GROUPS]
    vp_rows = [[] for _ in A_GROUPS]
    ks_rows = [[] for _ in A_GROUPS]
    vs_rows = [[] for _ in A_GROUPS]
    lru_h_p, lru_h_s, lru_c_p, lru_c_s = [], [], [], []
    ret_p, ret_s, gla_p, gla_s = [], [], [], []

    for i in range(depth):
        kind, j = i % 4, i // 4
        if kind == 0:
            n_heads = a_w_out.shape[1] // A_HEAD_DIM
            width = n_heads * A_HEAD_DIM
            full = [min(window, s) == s for window, _ in A_GROUPS]
            cuts = {0, 3 * ng + 1}
            for which in (1, 2):
                for g in range(ng):
                    if full[g]:
                        cuts |= {which * ng + g, which * ng + g + 1}
            cuts = sorted(cuts)
            unit_p, unit_s = {}, {}
            for u0, u1 in zip(cuts[:-1], cuts[1:]):
                arr_p, arr_s = matmul(hp, hs, a_w_in, j, col0=u0 * width, ncols=(u1 - u0) * width)
                for u in range(u0, u1):
                    unit_p[u] = (arr_p.reshape(b, s, -1), u - u0)
                    unit_s[u] = arr_s[:, (u - u0) * width:(u - u0 + 1) * width]
            bias_tab, bias_past, bias_self = _attention_bias_tables(rel_bias, n_heads)
            yp = attn_prompt([unit_p[g] for g in range(ng)], [unit_p[ng + g] for g in range(ng)],
                             [unit_p[2 * ng + g] for g in range(ng)], unit_p[3 * ng],
                             bias_tab, n_heads).reshape(b * s, width)
            ys = attn_sample([unit_s[g] for g in range(ng)], [unit_s[ng + g] for g in range(ng)],
                             [unit_s[2 * ng + g] for g in range(ng)], unit_s[3 * ng],
                             [c[j] for c in k_caches], [c[j] for c in v_caches],
                             bias_past, bias_self, n_heads)
            for g, (window, _) in enumerate(A_GROUPS):
                keep = min(window, s)

                def tail(unit, keep=keep):
                    arr, off = unit
                    return arr[:, s - keep:, off * width:(off + 1) * width].reshape(b, keep, n_heads, A_HEAD_DIM)

                kp_rows[g].append(tail(unit_p[ng + g]))
                vp_rows[g].append(tail(unit_p[2 * ng + g]))
                ks_rows[g].append(unit_s[ng + g].reshape(bs, 1, n_heads, A_HEAD_DIM))
                vs_rows[g].append(unit_s[2 * ng + g].reshape(bs, 1, n_heads, A_HEAD_DIM))
            w_out = a_w_out
        elif kind == 1:
            br = b_w_out.shape[1]
            proj, proj_s = matmul(hp, hs, b_w_in, j)
            prm = (b_conv_w[j], b_conv_b[j], b_gate_a_w[j], b_gate_a_b[j], b_gate_x_w[j], b_gate_x_b[j], b_lambda[j])
            proj3 = proj.reshape(b, s, -1)
            yp, h_last = lru_prompt(proj3, *prm)
            yp = yp.reshape(b * s, br)
            ys, h_new = lru_sample(proj_s, state_lru_conv[j], state_lru_h[j], *prm)
            cw = b_conv_w.shape[1]
            lru_h_p.append(h_last)
            lru_c_p.append(proj3[:, s - (cw - 1):, :br])
            lru_h_s.append(h_new)
            lru_c_s.append(jnp.concatenate([state_lru_conv[j], proj_s[:, None, :br]], axis=1)[:, 1:])
            w_out = b_w_out
        elif kind == 2:
            br = c_w_out.shape[1]
            n_heads = br // RET_DV
            proj, proj_s = matmul(hp, hs, c_w_in, j)
            yp, st = ret_prompt(proj.reshape(b, s, -1), c_norm[j], n_heads)
            yp = yp.reshape(b * s, br)
            ys, st_s = ret_sample(proj_s, c_norm[j], state_ret[j], n_heads)
            ret_p.append(st)
            ret_s.append(st_s)
            w_out = c_w_out
        else:
            br = d_w_out.shape[1]
            n_heads = br // GLA_DV
            proj, proj_s = matmul(hp, hs, jnp.swapaxes(d_w_in, 1, 2), j, transposed=True)
            yp, st = gla_prompt(proj.reshape(b, s, -1), d_gate_w[j], d_gate_b[j], d_norm[j], n_heads)
            yp = yp.reshape(b * s, br)
            ys, st_s = gla_sample(proj_s, d_gate_w[j], d_gate_b[j], d_norm[j], state_gla[j], n_heads)
            gla_p.append(st)
            gla_s.append(st_s)
            w_out = d_w_out
        op, os_ = matmul(yp, ys, w_out, j)
        g_next = norm_pre[i + 1] if i + 1 < depth else None
        xp, hp = residual_norm(xp, op, norm_post[i], g_next)
        xs, hs = residual_norm(xs, os_, norm_post[i], g_next)

    return (xp.reshape(b, s, d), xs.reshape(bs, 1, d),
            jnp.stack(kp_rows[0]), jnp.stack(ks_rows[0]), jnp.stack(vp_rows[0]), jnp.stack(vs_rows[0]),
            jnp.stack(kp_rows[1]), jnp.stack(ks_rows[1]), jnp.stack(vp_rows[1]), jnp.stack(vs_rows[1]),
            jnp.stack(kp_rows[2]), jnp.stack(ks_rows[2]), jnp.stack(vp_rows[2]), jnp.stack(vs_rows[2]),
            jnp.stack(lru_h_p), jnp.stack(lru_h_s), jnp.stack(lru_c_p), jnp.stack(lru_c_s),
            jnp.stack(ret_p), jnp.stack(ret_s), jnp.stack(gla_p), jnp.stack(gla_s))
```

```python
import functools

import numpy as np
import jax
import jax.numpy as jnp
from jax import lax
from jax.experimental import pallas as pl
from jax.experimental.pallas import tpu as pltpu

F32 = jnp.float32
BF16 = jnp.bfloat16

PAST_LEN = 8192
EPS = 1e-6
NEG_INF = -1e30
A_GROUPS = ((128, 1), (512, 4), (2048, 16))
A_HEAD_DIM = 128
A_BLOCK = 128
ATTN_UNROLL = 16
ATTN_MERGE_ROWS = 256
ATTN_SAMPLE_UNROLL = 16
NUM_BUCKETS = 32
MAX_DISTANCE = 2048
C_RG = 8.0
LRU_TIME_TILE = 512
RET_DK = 256
RET_DV = 512
RET_CHUNK = 128
ROPE_BASE = 10000.0
GLA_DK = 256
GLA_DV = 512
GLA_TAU = 16.0
GLA_CHUNK = 64
GLA_GROUP = 8
RET_GROUP = 4

LANES = 128
SUBLANES = 8
V7X_VMEM_LIMIT_BYTES = 56 * 1024 * 1024

_ARB = pltpu.ARBITRARY


def _params(n_grid):
    return pltpu.CompilerParams(dimension_semantics=(_ARB,) * n_grid,
                                vmem_limit_bytes=V7X_VMEM_LIMIT_BYTES)


def _dot(a, b):
    return jnp.dot(a, b, preferred_element_type=F32)


def _dot_nt(a, b):
    return lax.dot_general(a, b, (((1,), (1,)), ((), ())), preferred_element_type=F32)


def _dot_tn(a, b):
    return lax.dot_general(a, b, (((0,), (0,)), ((), ())), preferred_element_type=F32)


def _sigmoid(x):
    return 0.5 * jnp.tanh(0.5 * x) + 0.5


def _silu(x):
    return x * _sigmoid(x)


def _softplus(x):
    return jnp.maximum(x, 0.0) + jnp.log1p(jnp.exp(-jnp.abs(x)))


def _rms(x, g):
    return x * lax.rsqrt(jnp.mean(x * x, axis=-1, keepdims=True) + EPS) * g


def _chunk_group(n_chunks, want):
    return want if n_chunks % want == 0 else 1


def _col_bcast(row, n):
    return jnp.broadcast_to(row, (n, n)).T


def _norm_kernel(x_ref, g_ref, h_ref):
    h_ref[...] = _rms(x_ref[...], g_ref[...]).astype(BF16)


def _resnorm_kernel(x_ref, y_ref, gpost_ref, gnext_ref, xo_ref, h_ref):
    xn = x_ref[...] + _rms(y_ref[...], gpost_ref[...])
    xo_ref[...] = xn
    h_ref[...] = _rms(xn, gnext_ref[...]).astype(BF16)


def _res_kernel(x_ref, y_ref, gpost_ref, xo_ref):
    xo_ref[...] = x_ref[...] + _rms(y_ref[...], gpost_ref[...])


def _row_tile(m):
    return min(m, 256)


def norm_cast(x, g):
    m, d = x.shape
    tm = _row_tile(m)
    return pl.pallas_call(
        _norm_kernel,
        grid=(m // tm,),
        in_specs=[pl.BlockSpec((tm, d), lambda i: (i, 0)),
                  pl.BlockSpec((1, d), lambda i: (0, 0))],
        out_specs=pl.BlockSpec((tm, d), lambda i: (i, 0)),
        out_shape=jax.ShapeDtypeStruct((m, d), BF16),
        compiler_params=_params(1),
        name="norm_cast",
    )(x, g.reshape(1, d))


def residual_norm(x, y, g_post, g_next):
    m, d = x.shape
    tm = _row_tile(m)
    row = pl.BlockSpec((tm, d), lambda i: (i, 0))
    vec = pl.BlockSpec((1, d), lambda i: (0, 0))
    if g_next is None:
        return pl.pallas_call(
            _res_kernel, grid=(m // tm,),
            in_specs=[row, row, vec], out_specs=row,
            out_shape=jax.ShapeDtypeStruct((m, d), F32),
            compiler_params=_params(1), name="residual",
        )(x, y, g_post.reshape(1, d)), None
    return pl.pallas_call(
        _resnorm_kernel, grid=(m // tm,),
        in_specs=[row, row, vec, vec], out_specs=[row, row],
        out_shape=[jax.ShapeDtypeStruct((m, d), F32), jax.ShapeDtypeStruct((m, d), BF16)],
        compiler_params=_params(1), name="residual_norm",
    )(x, y, g_post.reshape(1, d), g_next.reshape(1, d))


def _mm_kernel(x_ref, w_ref, xs_ref, o_ref, os_ref, *, transposed):
    w = w_ref[...].astype(BF16)
    dot = _dot_nt if transposed else _dot
    o_ref[...] = dot(x_ref[...], w)

    @pl.when(pl.program_id(0) == 0)
    def _():
        os_ref[...] = dot(xs_ref[...], w)


MM_COL_TILE = 512
MM_ROW_TILE = 2048
MM_ROW_TILE_LONG_K = 1024
MM_LONG_K = 4096


def matmul(x, xs, w, layer, col0=0, ncols=None, transposed=False):
    m, k = x.shape
    ms = xs.shape[0]
    n = w.shape[1] if transposed else w.shape[2]
    ncols = n - col0 if ncols is None else ncols
    tn = MM_COL_TILE
    assert col0 % tn == 0
    tm = min(m, MM_ROW_TILE if k <= MM_LONG_K else MM_ROW_TILE_LONG_K)
    assert m % tm == 0
    nj = pl.cdiv(ncols, tn)
    c0 = col0 // tn
    if transposed:
        w_spec = pl.BlockSpec((None, tn, k), lambda i, j: (layer, c0 + j, 0))
    else:
        w_spec = pl.BlockSpec((None, k, tn), lambda i, j: (layer, 0, c0 + j))
    return pl.pallas_call(
        functools.partial(_mm_kernel, transposed=transposed),
        grid=(m // tm, nj),
        in_specs=[pl.BlockSpec((tm, k), lambda i, j: (i, 0), pipeline_mode=pl.Buffered(1)),
                  w_spec,
                  pl.BlockSpec((ms, k), lambda i, j: (0, 0))],
        out_specs=[pl.BlockSpec((tm, tn), lambda i, j: (i, j)),
                   pl.BlockSpec((ms, tn), lambda i, j: (0, jnp.where(i == 0, j, nj - 1)))],
        out_shape=[jax.ShapeDtypeStruct((m, ncols), F32), jax.ShapeDtypeStruct((ms, ncols), F32)],
        compiler_params=_params(2),
        name="proj_matmul",
    )(x, w, xs)


def _t5_bucket(dist):
    n = np.asarray(dist, dtype=np.int64)
    max_exact = NUM_BUCKETS // 2
    ratio = np.log(np.maximum(n, 1) / max_exact) / np.log(MAX_DISTANCE / max_exact)
    large = np.minimum(max_exact + (ratio * (NUM_BUCKETS - max_exact)).astype(np.int64), NUM_BUCKETS - 1)
    return np.where(n < max_exact, n, large).astype(np.int32)


def _attn_prompt_kernel(*refs, dilations, seq):
    ng = len(dilations)
    qkv = refs[:3 * ng]
    gate_ref, bias_ref, y_ref, o_s, lse_s = refs[3 * ng:]
    blk = A_BLOCK
    scale = A_HEAD_DIM ** -0.5
    ii = lax.broadcasted_iota(jnp.int32, (blk, blk), 0)
    jj = lax.broadcasted_iota(jnp.int32, (blk, blk), 1)
    cur_ok = jj <= ii
    prev_ok = jj >= ii

    for g, dil in enumerate(dilations):
        q_ref, k_ref, v_ref = qkv[3 * g:3 * g + 3]
        nb = seq // (dil * blk)
        band = pltpu.roll(jnp.broadcast_to(bias_ref[g:g + 1, :], (blk, 2 * blk)), 0, axis=1,
                          stride=1, stride_axis=0)
        b_prev = band[:, :blk]
        b_cur = band[:, blk:]

        def rows_at(start, dil=dil):
            if dil > 1:
                return pl.ds(start, blk, stride=dil)
            return pl.ds(start if isinstance(start, int) else pl.multiple_of(start, blk), blk)

        def scores(idx, dil=dil, nb=nb, q_ref=q_ref, k_ref=k_ref, b_prev=b_prev, b_cur=b_cur,
                   rows_at=rows_at):
            r = idx % dil
            bi = idx // dil
            start = r + bi * (blk * dil)
            rows = rows_at(start)
            q = q_ref[rows, :].astype(BF16)
            s = [jnp.where(cur_ok, _dot_nt(q, k_ref[rows, :].astype(BF16)) * scale + b_cur, NEG_INF)]
            prows = None
            if nb > 1 and not (isinstance(bi, int) and bi == 0):
                if isinstance(bi, int):
                    prows = rows_at(start - blk * dil)
                    prev_mask = prev_ok
                else:
                    prows = rows_at(jnp.maximum(start - blk * dil, 0))
                    prev_mask = prev_ok & (bi > 0)
                s.append(jnp.where(prev_mask, _dot_nt(q, k_ref[prows, :].astype(BF16)) * scale + b_prev,
                                   NEG_INF))
            return rows, prows, s

        def softmax(s):
            m = jnp.max(functools.reduce(jnp.maximum, s), axis=-1, keepdims=True)
            p = [jnp.exp(x - m) for x in s]
            den = jnp.sum(functools.reduce(jnp.add, p), axis=-1, keepdims=True)
            return m, den, [x.astype(BF16) for x in p]

        def values(rows, prows, p, v_ref=v_ref):
            num = _dot(p[0], v_ref[rows, :].astype(BF16))
            if prows is not None:
                num = num + _dot(p[1], v_ref[prows, :].astype(BF16))
            return num

        def body(it, carry, g=g, scores=scores, softmax=softmax, values=values):
            blocks = [scores(it * ATTN_UNROLL + u) for u in range(ATTN_UNROLL)]
            probs = [softmax(s) for _, _, s in blocks]
            nums = [values(rows, prows, p) for (rows, prows, _), (_, _, p) in zip(blocks, probs)]
            for (rows, _, _), (m, den, _), num in zip(blocks, probs, nums):
                o_s[g, rows, :] = num / den
                lse_s[g, rows, :] = jnp.broadcast_to(m + jnp.log(den), (blk, A_HEAD_DIM))
            return carry

        if seq // blk == ATTN_UNROLL:
            body(0, 0)
        else:
            lax.fori_loop(0, seq // (blk * ATTN_UNROLL), body, 0)

    tr = ATTN_MERGE_ROWS

    def merge(c, carry):
        rows = pl.ds(pl.multiple_of(c * tr, tr), tr)
        lses = [lse_s[g, rows, :] for g in range(ng)]
        mx = functools.reduce(jnp.maximum, lses)
        num = jnp.zeros((tr, A_HEAD_DIM), F32)
        den = jnp.zeros((tr, A_HEAD_DIM), F32)
        for g in range(ng):
            w = jnp.exp(lses[g] - mx)
            num = num + w * o_s[g, rows, :]
            den = den + w
        y_ref[rows, :] = ((num / den) * _silu(gate_ref[rows, :])).astype(BF16)
        return carry

    lax.fori_loop(0, seq // tr, merge, 0)


def attn_prompt(qs, ks, vs, gate, bias_tab, n_heads):
    b, s, _ = gate[0].shape
    ng = len(A_GROUPS)
    dh = A_HEAD_DIM
    in_specs = []
    args = []
    units = [u for g in range(ng) for u in (qs[g], ks[g], vs[g])] + [gate]
    for arr, unit in units:
        in_specs.append(pl.BlockSpec((None, s, dh), lambda bi, hi, unit=unit: (bi, 0, unit * n_heads + hi)))
        args.append(arr)
    in_specs.append(pl.BlockSpec((None, ng, 2 * A_BLOCK), lambda bi, hi: (hi, 0, 0)))
    args.append(bias_tab)
    kern = functools.partial(_attn_prompt_kernel, dilations=tuple(d for _, d in A_GROUPS), seq=s)
    return pl.pallas_call(
        kern,
        grid=(b, n_heads),
        in_specs=in_specs,
        out_specs=pl.BlockSpec((None, s, dh), lambda bi, hi: (bi, 0, hi)),
        out_shape=jax.ShapeDtypeStruct((b, s, n_heads * dh), BF16),
        scratch_shapes=[pltpu.VMEM((ng, s, dh), F32)] * 2,
        compiler_params=_params(2),
        name="attn_prompt",
    )(*args)


def _attn_sample_kernel(q_ref, kn_ref, vn_ref, gate_ref, *refs, n_heads, n_keys):
    ng = len(A_GROUPS)
    caches = refs[:2 * ng]
    bpast_ref, bself_ref, y_ref = refs[2 * ng:]
    dh = A_HEAD_DIM
    scale = dh ** -0.5
    lane = lax.broadcasted_iota(jnp.int32, (n_heads, n_keys), 1)
    parts = []
    for g in range(ng):
        kc_ref, vc_ref = caches[2 * g], caches[2 * g + 1]
        q = q_ref[g]
        vn = vn_ref[g]

        def logits(j, acc, kc_ref=kc_ref, q=q):
            col = jnp.sum(kc_ref[j] * q, axis=-1, keepdims=True)
            return jnp.where(lane == j, col, acc)

        lp = lax.fori_loop(0, n_keys, logits, jnp.zeros((n_heads, n_keys), F32), unroll=ATTN_SAMPLE_UNROLL)
        lp = lp * scale + bpast_ref[g]
        ls = jnp.sum(kn_ref[g] * q, axis=-1, keepdims=True) * scale + bself_ref[g]
        m = jnp.maximum(jnp.max(lp, axis=-1, keepdims=True), ls)
        p = jnp.exp(lp - m)
        ps = jnp.exp(ls - m)
        den = jnp.sum(p, axis=-1, keepdims=True) + ps

        def weighted(j, acc, vc_ref=vc_ref, p=p):
            pj = jnp.sum(jnp.where(lane == j, p, 0.0), axis=-1, keepdims=True)
            return acc + pj * vc_ref[j]

        num = lax.fori_loop(0, n_keys, weighted, ps * vn, unroll=ATTN_SAMPLE_UNROLL)
        parts.append((num, m, den))
    mx = functools.reduce(jnp.maximum, [p_[1] for p_ in parts])
    num = jnp.zeros((n_heads, dh), F32)
    den = jnp.zeros((n_heads, 1), F32)
    for num_g, m_g, den_g in parts:
        w = jnp.exp(m_g - mx)
        num = num + w * num_g
        den = den + w * den_g
    y_ref[...] = ((num / den) * _silu(gate_ref[...])).astype(BF16)


def attn_sample(qs, ks, vs, gate, k_caches, v_caches, bias_past, bias_self, n_heads):
    bs, width = gate.shape
    dh = A_HEAD_DIM
    ng = len(A_GROUPS)
    n_keys = A_BLOCK
    heads = lambda rows: jnp.stack(rows, axis=1).reshape(bs, ng, n_heads, dh)
    grp_spec = pl.BlockSpec((None, ng, n_heads, dh), lambda bi: (bi, 0, 0, 0))
    args = [heads(qs), heads(ks), heads(vs), gate.reshape(bs, n_heads, dh)]
    in_specs = [grp_spec, grp_spec, grp_spec, pl.BlockSpec((None, n_heads, dh), lambda bi: (bi, 0, 0))]
    for (_, dil), kc, vc in zip(A_GROUPS, k_caches, v_caches):
        for c in (kc, vc):
            args.append(c.reshape(bs, n_keys, dil, n_heads, dh))
            in_specs.append(pl.BlockSpec((None, n_keys, None, n_heads, dh), lambda bi: (bi, 0, 0, 0, 0)))
    args += [bias_past, bias_self]
    in_specs += [pl.BlockSpec(bias_past.shape, lambda bi: (0, 0, 0)),
                 pl.BlockSpec(bias_self.shape, lambda bi: (0, 0, 0))]
    y = pl.pallas_call(
        functools.partial(_attn_sample_kernel, n_heads=n_heads, n_keys=n_keys),
        grid=(bs,),
        in_specs=in_specs,
        out_specs=pl.BlockSpec((None, n_heads, dh), lambda bi: (bi, 0, 0)),
        out_shape=jax.ShapeDtypeStruct((bs, n_heads, dh), BF16),
        compiler_params=_params(1),
        name="attn_sample",
    )(*args)
    return y.reshape(bs, width)


def _lru_gates(xc, wa_ref, ba_ref, wx_ref, bx_ref, lam_ref):
    xb = xc.astype(BF16)
    r = _sigmoid(_dot(xb, wa_ref[...].astype(BF16)) + ba_ref[...])
    ig = _sigmoid(_dot(xb, wx_ref[...].astype(BF16)) + bx_ref[...])
    log_a = -C_RG * r * _softplus(-lam_ref[...])
    a = jnp.exp(log_a)
    mult = jnp.sqrt(-jnp.tanh(log_a) * (a * a + 1.0))
    return a, mult, ig


def _lru_prompt_kernel(x_ref, gate_ref, cw_ref, cb_ref, wa_ref, ba_ref, wx_ref, bx_ref, lam_ref,
                       y_ref, hl_ref, hc_ref, xp_ref, *, tt, conv_w):
    t = pl.program_id(2)

    @pl.when(t == 0)
    def _():
        hc_ref[...] = jnp.zeros_like(hc_ref)
        xp_ref[:SUBLANES, :] = jnp.zeros((SUBLANES, xp_ref.shape[1]), F32)

    x = x_ref[...]
    c = x.shape[1]
    xp_ref[SUBLANES:, :] = x
    xc = cb_ref[...] + cw_ref[conv_w - 1:conv_w, :] * x
    for k in range(1, conv_w):
        xc = xc + cw_ref[conv_w - 1 - k:conv_w - k, :] * xp_ref[SUBLANES - k:SUBLANES - k + tt, :]
    xp_ref[:SUBLANES, :] = x[tt - SUBLANES:]

    row = lax.broadcasted_iota(jnp.int32, (tt, c), 0)
    a, mult, ig = _lru_gates(xc, wa_ref, ba_ref, wx_ref, bx_ref, lam_ref)
    mult = jnp.where(row + t * tt == 0, 1.0, mult)
    bx = mult * ig * xc
    ng = tt // SUBLANES
    a = a.reshape(ng, SUBLANES, c)
    bx = bx.reshape(ng, SUBLANES, c)
    sub = lax.broadcasted_iota(jnp.int32, (ng, SUBLANES, c), 1)
    d = 1
    while d < SUBLANES:
        keep = sub >= d
        a_sh = jnp.where(keep, pltpu.roll(a, d, axis=1), 1.0)
        b_sh = jnp.where(keep, pltpu.roll(bx, d, axis=1), 0.0)
        bx = a * b_sh + bx
        a = a * a_sh
        d *= 2
    h = hc_ref[...]
    groups = []
    for g in range(ng):
        groups.append(a[g] * h + bx[g])
        h = groups[-1][SUBLANES - 1:]
    y_ref[...] = (jnp.concatenate(groups, axis=0) * _silu(gate_ref[...])).astype(BF16)
    hc_ref[...] = h
    hl_ref[...] = h


def lru_prompt(proj, conv_w, conv_b, wa, ba, wx, bx, lam):
    b, s, two_br = proj.shape
    br = two_br // 2
    nblk, bs_, _ = wa.shape
    cw = conv_w.shape[0]
    tt = min(s, LRU_TIME_TILE)
    vec = lambda a: a.reshape(1, br)
    vspec = pl.BlockSpec((1, bs_), lambda bi, ni, ti: (0, ni))
    wspec = pl.BlockSpec((None, bs_, bs_), lambda bi, ni, ti: (ni, 0, 0))
    y, hl = pl.pallas_call(
        functools.partial(_lru_prompt_kernel, tt=tt, conv_w=cw),
        grid=(b, nblk, s // tt),
        in_specs=[pl.BlockSpec((None, tt, bs_), lambda bi, ni, ti: (bi, ti, ni)),
                  pl.BlockSpec((None, tt, bs_), lambda bi, ni, ti: (bi, ti, nblk + ni)),
                  pl.BlockSpec((cw, bs_), lambda bi, ni, ti: (0, ni)),
                  vspec, wspec, vspec, wspec, vspec, vspec],
        out_specs=[pl.BlockSpec((None, tt, bs_), lambda bi, ni, ti: (bi, ti, ni)),
                   pl.BlockSpec((None, 1, bs_), lambda bi, ni, ti: (bi, 0, ni))],
        out_shape=[jax.ShapeDtypeStruct((b, s, br), BF16), jax.ShapeDtypeStruct((b, 1, br), F32)],
        scratch_shapes=[pltpu.VMEM((1, bs_), F32), pltpu.VMEM((SUBLANES + tt, bs_), F32)],
        compiler_params=_params(3),
        name="lru_prompt",
    )(proj, proj, conv_w, vec(conv_b), wa, vec(ba), wx, vec(bx), vec(lam))
    return y, hl.reshape(b, br)


def _lru_sample_kernel(x_ref, gate_ref, buf_ref, h0_ref, cw_ref, cb_ref, wa_ref, ba_ref, wx_ref, bx_ref,
                       lam_ref, y_ref, h_ref, *, conv_w):
    x = x_ref[...]
    xc = cb_ref[...] + cw_ref[conv_w - 1:conv_w, :] * x
    for w in range(conv_w - 1):
        xc = xc + cw_ref[w:w + 1, :] * buf_ref[:, w, :]
    a, mult, ig = _lru_gates(xc, wa_ref, ba_ref, wx_ref, bx_ref, lam_ref)
    h = a * h0_ref[...] + mult * ig * xc
    h_ref[...] = h
    y_ref[...] = (h * _silu(gate_ref[...])).astype(BF16)


def lru_sample(proj_s, conv_buf, h0, conv_w, conv_b, wa, ba, wx, bx, lam):
    bs, two_br = proj_s.shape
    br = two_br // 2
    nblk, bs_, _ = wa.shape
    cw = conv_w.shape[0]
    vec = lambda a: a.reshape(1, br)
    vspec = pl.BlockSpec((1, bs_), lambda ni: (0, ni))
    wspec = pl.BlockSpec((None, bs_, bs_), lambda ni: (ni, 0, 0))
    rspec = pl.BlockSpec((bs, bs_), lambda ni: (0, ni))
    return pl.pallas_call(
        functools.partial(_lru_sample_kernel, conv_w=cw),
        grid=(nblk,),
        in_specs=[rspec,
                  pl.BlockSpec((bs, bs_), lambda ni: (0, nblk + ni)),
                  pl.BlockSpec((bs, cw - 1, bs_), lambda ni: (0, 0, ni)),
                  rspec,
                  pl.BlockSpec((cw, bs_), lambda ni: (0, ni)),
                  vspec, wspec, vspec, wspec, vspec, vspec],
        out_specs=[rspec, rspec],
        out_shape=[jax.ShapeDtypeStruct((bs, br), BF16), jax.ShapeDtypeStruct((bs, br), F32)],
        compiler_params=_params(1),
        name="lru_sample",
    )(proj_s, proj_s, conv_buf, h0, conv_w, vec(conv_b), wa, vec(ba), wx, vec(bx), vec(lam))


def _rope(x, cos, sin):
    half = x.shape[-1] // 2
    x1, x2 = x[:, :half], x[:, half:]
    return jnp.concatenate([x1 * cos - x2 * sin, x1 * sin + x2 * cos], axis=-1)


def _groupnorm(o, gain):
    c = o - jnp.mean(o, axis=-1, keepdims=True)
    return c * lax.rsqrt(jnp.mean(c * c, axis=-1, keepdims=True) + EPS) * gain


def _ret_prompt_kernel(q_ref, k_ref, v_ref, gate_ref, cos_ref, sin_ref, lg_ref, gain_ref,
                       y_ref, s_ref, st_ref, *, chunk, n_chunks, group):
    lg = lg_ref[:, :1]
    idx = lax.broadcasted_iota(jnp.int32, (chunk, 1), 0).astype(F32)
    ii = lax.broadcasted_iota(jnp.int32, (chunk, chunk), 0)
    jj = lax.broadcasted_iota(jnp.int32, (chunk, chunk), 1)
    diff = (ii - jj).astype(F32)
    decay = jnp.where(diff >= 0, jnp.exp(diff * lg), 0.0)
    q_dec = jnp.exp((idx + 1.0) * lg)
    k_dec = jnp.exp((chunk - 1.0 - idx) * lg)
    chunk_dec = jnp.exp(chunk * lg)
    gain = gain_ref[...]
    st_ref[...] = jnp.zeros_like(st_ref)

    q_dec_rows = jnp.concatenate([q_dec] * group, axis=0)
    k_dec_rows = jnp.concatenate([k_dec] * group, axis=0)

    def body(it, carry):
        rows = pl.ds(pl.multiple_of(it * (group * chunk), group * chunk), group * chunk)
        cos = cos_ref[rows, :]
        sin = sin_ref[rows, :]
        qc = _rope(q_ref[rows, :], cos, sin)
        kc = _rope(k_ref[rows, :], cos, sin) * (RET_DK ** -0.5)
        vc = v_ref[rows, :].astype(BF16)
        qb = qc.astype(BF16)
        kb = kc.astype(BF16)
        qd = (qc * q_dec_rows).astype(BF16)
        kd = (kc * k_dec_rows).astype(BF16)
        sl = [slice(u * chunk, (u + 1) * chunk) for u in range(group)]
        intra, upd = [], []
        for u in range(group):
            scores = _dot_nt(qb[sl[u]], kb[sl[u]]) * decay
            intra.append(_dot(scores.astype(BF16), vc[sl[u]]))
            upd.append(_dot_tn(kd[sl[u]], vc[sl[u]]))
        st = st_ref[...]
        outs = []
        for u in range(group):
            outs.append(intra[u] + _dot(qd[sl[u]], st.astype(BF16)))
            st = chunk_dec * st + upd[u]
        st_ref[...] = st
        o = jnp.concatenate(outs, axis=0)
        y_ref[rows, :] = (_groupnorm(o, gain) * _silu(gate_ref[rows, :])).astype(BF16)
        return carry

    lax.fori_loop(0, n_chunks // group, body, 0)
    s_ref[...] = st_ref[...]


def _ret_log_gamma(n_heads):
    lg = np.log1p(-np.exp2(-5.0 - np.arange(n_heads, dtype=np.float32))).astype(np.float32)
    return jnp.asarray(np.broadcast_to(lg[:, None, None], (n_heads, 1, LANES)).copy())


def _rope_tables(pos):
    half = RET_DK // 2
    inv_freq = ROPE_BASE ** (-jnp.arange(half, dtype=F32) / half)
    ang = pos[:, None] * inv_freq[None, :]
    return jnp.cos(ang), jnp.sin(ang)


def ret_prompt(proj, gain, n_heads):
    b, s, _ = proj.shape
    dk, dv = RET_DK, RET_DV
    qk = n_heads * dk
    chunk = RET_CHUNK if s % RET_CHUNK == 0 else s
    cos, sin = _rope_tables(jnp.arange(s, dtype=F32))
    half = dk // 2
    koff = qk // dk
    voff = 2 * qk // dv
    goff = (2 * qk + n_heads * dv) // dv
    return pl.pallas_call(
        functools.partial(_ret_prompt_kernel, chunk=chunk, n_chunks=s // chunk,
                          group=_chunk_group(s // chunk, RET_GROUP)),
        grid=(b, n_heads),
        in_specs=[pl.BlockSpec((None, s, dk), lambda bi, hi: (bi, 0, hi)),
                  pl.BlockSpec((None, s, dk), lambda bi, hi: (bi, 0, koff + hi)),
                  pl.BlockSpec((None, s, dv), lambda bi, hi: (bi, 0, voff + hi)),
                  pl.BlockSpec((None, s, dv), lambda bi, hi: (bi, 0, goff + hi)),
                  pl.BlockSpec((s, half), lambda bi, hi: (0, 0)),
                  pl.BlockSpec((s, half), lambda bi, hi: (0, 0)),
                  pl.BlockSpec((None, 1, LANES), lambda bi, hi: (hi, 0, 0)),
                  pl.BlockSpec((1, dv), lambda bi, hi: (0, hi))],
        out_specs=[pl.BlockSpec((None, s, dv), lambda bi, hi: (bi, 0, hi)),
                   pl.BlockSpec((None, None, dk, dv), lambda bi, hi: (bi, hi, 0, 0))],
        out_shape=[jax.ShapeDtypeStruct((b, s, n_heads * dv), BF16),
                   jax.ShapeDtypeStruct((b, n_heads, dk, dv), F32)],
        scratch_shapes=[pltpu.VMEM((dk, dv), F32)],
        compiler_params=_params(2),
        name="ret_prompt",
    )(proj, proj, proj, proj, cos, sin, _ret_log_gamma(n_heads), gain.reshape(1, n_heads * dv))


def _ret_sample_kernel(q_ref, k_ref, v_ref, gate_ref, cos_ref, sin_ref, lg_ref, gain_ref, s0_ref,
                       y_ref, s_ref):
    dk, dv = RET_DK, RET_DV
    bs = q_ref.shape[0]
    gamma = jnp.exp(lg_ref[:, :1])
    cos = cos_ref[...]
    sin = sin_ref[...]
    q = _rope(q_ref[...], cos, sin)
    k = _rope(k_ref[...], cos, sin) * (dk ** -0.5)
    v = v_ref[...]
    o = jnp.sum(q * k, axis=-1, keepdims=True) * v
    qg = (q * gamma).astype(BF16)
    row = lax.broadcasted_iota(jnp.int32, (bs, dv), 0)
    for b in range(bs):
        s0 = s0_ref[b]
        o = o + jnp.where(row == b, _dot(qg, s0.astype(BF16)), 0.0)
        kcol = _col_bcast(k[b:b + 1], dk)
        for j in range(dv // dk):
            cols = slice(j * dk, (j + 1) * dk)
            s_ref[b, :, cols] = gamma * s0[:, cols] + kcol * v[b:b + 1, cols]
    y_ref[...] = (_groupnorm(o, gain_ref[...]) * _silu(gate_ref[...])).astype(BF16)


def ret_sample(proj_s, gain, s0, n_heads):
    bs = proj_s.shape[0]
    dk, dv = RET_DK, RET_DV
    qk = n_heads * dk
    cos, sin = _rope_tables(jnp.full((1,), PAST_LEN, F32))
    half = dk // 2
    koff = qk // dk
    voff = 2 * qk // dv
    goff = (2 * qk + n_heads * dv) // dv
    state_spec = pl.BlockSpec((bs, None, dk, dv), lambda hi: (0, hi, 0, 0))
    return pl.pallas_call(
        _ret_sample_kernel,
        grid=(n_heads,),
        in_specs=[pl.BlockSpec((bs, dk), lambda hi: (0, hi)),
                  pl.BlockSpec((bs, dk), lambda hi: (0, koff + hi)),
                  pl.BlockSpec((bs, dv), lambda hi: (0, voff + hi)),
                  pl.BlockSpec((bs, dv), lambda hi: (0, goff + hi)),
                  pl.BlockSpec((1, half), lambda hi: (0, 0)),
                  pl.BlockSpec((1, half), lambda hi: (0, 0)),
                  pl.BlockSpec((None, 1, LANES), lambda hi: (hi, 0, 0)),
                  pl.BlockSpec((1, dv), lambda hi: (0, hi)),
                  state_spec],
        out_specs=[pl.BlockSpec((bs, dv), lambda hi: (0, hi)), state_spec],
        out_shape=[jax.ShapeDtypeStruct((bs, n_heads * dv), BF16),
                   jax.ShapeDtypeStruct((bs, n_heads, dk, dv), F32)],
        compiler_params=_params(1),
        name="ret_sample",
    )(proj_s, proj_s, proj_s, proj_s, cos, sin, _ret_log_gamma(n_heads), gain.reshape(1, n_heads * dv), s0)


def _headnorm(o, gain):
    return o * lax.rsqrt(jnp.mean(o * o, axis=-1, keepdims=True) + EPS) * gain


def _gla_log_alpha(low_ref_val, gw_ref, gb_ref, rank):
    lane = lax.broadcasted_iota(jnp.int32, low_ref_val.shape, 1)
    low = jnp.where(lane < rank, low_ref_val, 0.0).astype(BF16)
    z = _dot(low, gw_ref[...].astype(BF16)) + gb_ref[...]
    return (jnp.minimum(z, 0.0) - jnp.log1p(jnp.exp(-jnp.abs(z)))) / GLA_TAU


def _gla_prompt_kernel(q_ref, k_ref, v_ref, gate_ref, low_ref, gw_ref, gb_ref, gain_ref,
                       y_ref, s_ref, st_ref, *, chunk, n_chunks, group, rank):
    dk, dv = GLA_DK, GLA_DV
    sub = lax.broadcasted_iota(jnp.int32, (group * chunk, dk), 0) % chunk
    ii = lax.broadcasted_iota(jnp.int32, (chunk, chunk), 0)
    jj = lax.broadcasted_iota(jnp.int32, (chunk, chunk), 1)
    causal = jj <= ii
    gain = gain_ref[...]
    st_ref[...] = jnp.zeros_like(st_ref)

    def body(it, carry):
        rows = pl.ds(pl.multiple_of(it * (group * chunk), group * chunk), group * chunk)
        bcum = _gla_log_alpha(low_ref[rows, :], gw_ref, gb_ref, rank)
        d = 1
        while d < chunk:
            bcum = bcum + jnp.where(sub >= d, pltpu.roll(bcum, d, axis=0), 0.0)
            d *= 2
        sl = [slice(u * chunk, (u + 1) * chunk) for u in range(group)]
        blast = [bcum[(u + 1) * chunk - 1:(u + 1) * chunk] for u in range(group)]
        blast_rows = jnp.concatenate([jnp.broadcast_to(b, (chunk, dk)) for b in blast], axis=0)
        kc = k_ref[rows, :]
        vc = v_ref[rows, :].astype(BF16)
        qe = (q_ref[rows, :] * (dk ** -0.5) * jnp.exp(bcum)).astype(BF16)
        ke = (kc * jnp.exp(-bcum)).astype(BF16)
        kd = (kc * jnp.exp(blast_rows - bcum)).astype(BF16)
        intra, upd, dec = [], [], []
        for u in range(group):
            scores = jnp.where(causal, _dot_nt(qe[sl[u]], ke[sl[u]]), 0.0)
            intra.append(_dot(scores.astype(BF16), vc[sl[u]]))
            upd.append(_dot_tn(kd[sl[u]], vc[sl[u]]))
            dec.append(_col_bcast(jnp.exp(blast[u]), dk))
        st = st_ref[...]
        outs = []
        for u in range(group):
            outs.append(intra[u] + _dot(qe[sl[u]], st.astype(BF16)))
            st = jnp.concatenate([dec[u] * st[:, j * dk:(j + 1) * dk] + upd[u][:, j * dk:(j + 1) * dk]
                                  for j in range(dv // dk)], axis=1)
        st_ref[...] = st
        o = jnp.concatenate(outs, axis=0)
        y_ref[rows, :] = (_headnorm(o, gain) * _silu(gate_ref[rows, :])).astype(BF16)
        return carry

    lax.fori_loop(0, n_chunks // group, body, 0)
    s_ref[...] = st_ref[...]


def _gla_offsets(n_heads):
    dk, dv = GLA_DK, GLA_DV
    qk = n_heads * dk
    koff = qk // dk
    voff = 2 * qk // dv
    goff = (2 * qk + n_heads * dv) // dv
    loff = (2 * qk + 2 * n_heads * dv) // LANES
    return koff, voff, goff, loff


def _pad_rank(gate_w):
    rank = gate_w.shape[0]
    return jnp.pad(gate_w, ((0, LANES - rank), (0, 0)))


def gla_prompt(proj, gate_w, gate_b, gain, n_heads):
    b, s, _ = proj.shape
    dk, dv = GLA_DK, GLA_DV
    rank = gate_w.shape[0]
    chunk = GLA_CHUNK if s % GLA_CHUNK == 0 else s
    koff, voff, goff, loff = _gla_offsets(n_heads)
    return pl.pallas_call(
        functools.partial(_gla_prompt_kernel, chunk=chunk, n_chunks=s // chunk,
                          group=_chunk_group(s // chunk, GLA_GROUP), rank=rank),
        grid=(b, n_heads),
        in_specs=[pl.BlockSpec((None, s, dk), lambda bi, hi: (bi, 0, hi)),
                  pl.BlockSpec((None, s, dk), lambda bi, hi: (bi, 0, koff + hi)),
                  pl.BlockSpec((None, s, dv), lambda bi, hi: (bi, 0, voff + hi)),
                  pl.BlockSpec((None, s, dv), lambda bi, hi: (bi, 0, goff + hi)),
                  pl.BlockSpec((None, s, LANES), lambda bi, hi: (bi, 0, loff)),
                  pl.BlockSpec((LANES, dk), lambda bi, hi: (0, hi)),
                  pl.BlockSpec((1, dk), lambda bi, hi: (0, hi)),
                  pl.BlockSpec((1, dv), lambda bi, hi: (0, hi))],
        out_specs=[pl.BlockSpec((None, s, dv), lambda bi, hi: (bi, 0, hi)),
                   pl.BlockSpec((None, None, dk, dv), lambda bi, hi: (bi, hi, 0, 0))],
        out_shape=[jax.ShapeDtypeStruct((b, s, n_heads * dv), BF16),
                   jax.ShapeDtypeStruct((b, n_heads, dk, dv), F32)],
        scratch_shapes=[pltpu.VMEM((dk, dv), F32)],
        compiler_params=_params(2),
        name="gla_prompt",
    )(proj, proj, proj, proj, proj, _pad_rank(gate_w), gate_b.reshape(1, -1), gain.reshape(1, -1))


def _gla_sample_kernel(q_ref, k_ref, v_ref, gate_ref, low_ref, gw_ref, gb_ref, gain_ref, s0_ref,
                       y_ref, s_ref, *, rank):
    dk, dv = GLA_DK, GLA_DV
    bs = q_ref.shape[0]
    g = _gla_log_alpha(low_ref[...], gw_ref, gb_ref, rank)
    q = q_ref[...] * (dk ** -0.5)
    k = k_ref[...]
    v = v_ref[...]
    qe = q * jnp.exp(g)
    ke = k * jnp.exp(-g)
    kd = k * jnp.exp(g - g)
    eg = jnp.exp(g)
    o = jnp.sum(qe * ke, axis=-1, keepdims=True) * v
    qeb = qe.astype(BF16)
    row = lax.broadcasted_iota(jnp.int32, (bs, dv), 0)
    for b in range(bs):
        s0 = s0_ref[b]
        o = o + jnp.where(row == b, _dot(qeb, s0.astype(BF16)), 0.0)
        dec = _col_bcast(eg[b:b + 1], dk)
        kcol = _col_bcast(kd[b:b + 1], dk)
        for j in range(dv // dk):
            cols = slice(j * dk, (j + 1) * dk)
            s_ref[b, :, cols] = dec * s0[:, cols] + kcol * v[b:b + 1, cols]
    y_ref[...] = (_headnorm(o, gain_ref[...]) * _silu(gate_ref[...])).astype(BF16)


def gla_sample(proj_s, gate_w, gate_b, gain, s0, n_heads):
    bs = proj_s.shape[0]
    dk, dv = GLA_DK, GLA_DV
    rank = gate_w.shape[0]
    koff, voff, goff, loff = _gla_offsets(n_heads)
    state_spec = pl.BlockSpec((bs, None, dk, dv), lambda hi: (0, hi, 0, 0))
    return pl.pallas_call(
        functools.partial(_gla_sample_kernel, rank=rank),
        grid=(n_heads,),
        in_specs=[pl.BlockSpec((bs, dk), lambda hi: (0, hi)),
                  pl.BlockSpec((bs, dk), lambda hi: (0, koff + hi)),
                  pl.BlockSpec((bs, dv), lambda hi: (0, voff + hi)),
                  pl.BlockSpec((bs, dv), lambda hi: (0, goff + hi)),
                  pl.BlockSpec((bs, LANES), lambda hi: (0, loff)),
                  pl.BlockSpec((LANES, dk), lambda hi: (0, hi)),
                  pl.BlockSpec((1, dk), lambda hi: (0, hi)),
                  pl.BlockSpec((1, dv), lambda hi: (0, hi)),
                  state_spec],
        out_specs=[pl.BlockSpec((bs, dv), lambda hi: (0, hi)), state_spec],
        out_shape=[jax.ShapeDtypeStruct((bs, n_heads * dv), BF16),
                   jax.ShapeDtypeStruct((bs, n_heads, dk, dv), F32)],
        compiler_params=_params(1),
        name="gla_sample",
    )(proj_s, proj_s, proj_s, proj_s, proj_s, _pad_rank(gate_w), gate_b.reshape(1, -1),
      gain.reshape(1, -1), s0)


def _attention_bias_tables(rel_bias, n_heads):
    blk = A_BLOCK
    ng = len(A_GROUPS)
    onehot = np.zeros((ng, 2 * blk, NUM_BUCKETS), np.float32)
    for g, (window, dil) in enumerate(A_GROUPS):
        n_keys = window // dil
        assert n_keys == blk, "one 128-key band per dilated stream is assumed"
        u = np.arange(n_keys + 1)
        onehot[g, u, _t5_bucket(dil * (n_keys - u))] = 1.0
    tab = jnp.einsum("gub,bgh->guh", jnp.asarray(onehot), rel_bias.astype(F32).reshape(NUM_BUCKETS, ng, n_heads),
                     precision=lax.Precision.HIGHEST)
    tab_t = tab.transpose(0, 2, 1)
    return tab.transpose(2, 0, 1), tab_t[:, :, :blk], tab_t[:, :, blk:blk + 1]


def kernel(x_prompt, x_sample, cache_k_w128, cache_v_w128, cache_k_w512, cache_v_w512, cache_k_w2048, cache_v_w2048, state_lru_h, state_lru_conv, state_ret, state_gla, norm_pre, norm_post, rel_bias, a_w_in, a_w_out, b_w_in, b_conv_w, b_conv_b, b_gate_a_w, b_gate_a_b, b_gate_x_w, b_gate_x_b, b_lambda, b_w_out, c_w_in, c_norm, c_w_out, d_w_in, d_gate_w, d_gate_b, d_norm, d_w_out):
    b, s, d = x_prompt.shape
    bs = x_sample.shape[0]
    assert x_sample.shape[1] == 1, "one new token per sequence"
    depth = norm_pre.shape[0]
    k_caches = (cache_k_w128, cache_k_w512, cache_k_w2048)
    v_caches = (cache_v_w128, cache_v_w512, cache_v_w2048)
    ng = len(A_GROUPS)
    for (window, dil), kc in zip(A_GROUPS, k_caches):
        assert kc.shape[2] == window and s % (dil * A_BLOCK) == 0

    xp = x_prompt.reshape(b * s, d)
    xs = x_sample.reshape(bs, d)
    hp = norm_cast(xp, norm_pre[0])
    hs = norm_cast(xs, norm_pre[0])

    kp_rows = [[] for _ in A_GROUPS]
    vp_rows = [[] for _ in A_GROUPS]
    ks_rows = [[] for _ in A_GROUPS]
    vs_rows = [[] for _ in A_GROUPS]
    lru_h_p, lru_h_s, lru_c_p, lru_c_s = [], [], [], []
    ret_p, ret_s, gla_p, gla_s = [], [], [], []

    for i in range(depth):
        kind, j = i % 4, i // 4
        if kind == 0:
            n_heads = a_w_out.shape[1] // A_HEAD_DIM
            width = n_heads * A_HEAD_DIM
            full = [min(window, s) == s for window, _ in A_GROUPS]
            cuts = {0, 3 * ng + 1}
            for which in (1, 2):
                for g in range(ng):
                    if full[g]:
                        cuts |= {which * ng + g, which * ng + g + 1}
            cuts = sorted(cuts)
            unit_p, unit_s = {}, {}
            for u0, u1 in zip(cuts[:-1], cuts[1:]):
                arr_p, arr_s = matmul(hp, hs, a_w_in, j, col0=u0 * width, ncols=(u1 - u0) * width)
                for u in range(u0, u1):
                    unit_p[u] = (arr_p.reshape(b, s, -1), u - u0)
                    unit_s[u] = arr_s[:, (u - u0) * width:(u - u0 + 1) * width]
            bias_tab, bias_past, bias_self = _attention_bias_tables(rel_bias, n_heads)
            yp = attn_prompt([unit_p[g] for g in range(ng)], [unit_p[ng + g] for g in range(ng)],
                             [unit_p[2 * ng + g] for g in range(ng)], unit_p[3 * ng],
                             bias_tab, n_heads).reshape(b * s, width)
            ys = attn_sample([unit_s[g] for g in range(ng)], [unit_s[ng + g] for g in range(ng)],
                             [unit_s[2 * ng + g] for g in range(ng)], unit_s[3 * ng],
                             [c[j] for c in k_caches], [c[j] for c in v_caches],
                             bias_past, bias_self, n_heads)
            for g, (window, _) in enumerate(A_GROUPS):
                keep = min(window, s)

                def tail(unit, keep=keep):
                    arr, off = unit
                    return arr[:, s - keep:, off * width:(off + 1) * width].reshape(b, keep, n_heads, A_HEAD_DIM)

                kp_rows[g].append(tail(unit_p[ng + g]))
                vp_rows[g].append(tail(unit_p[2 * ng + g]))
                ks_rows[g].append(unit_s[ng + g].reshape(bs, 1, n_heads, A_HEAD_DIM))
                vs_rows[g].append(unit_s[2 * ng + g].reshape(bs, 1, n_heads, A_HEAD_DIM))
            w_out = a_w_out
        elif kind == 1:
            br = b_w_out.shape[1]
            proj, proj_s = matmul(hp, hs, b_w_in, j)
            prm = (b_conv_w[j], b_conv_b[j], b_gate_a_w[j], b_gate_a_b[j], b_gate_x_w[j], b_gate_x_b[j], b_lambda[j])
            proj3 = proj.reshape(b, s, -1)
            yp, h_last = lru_prompt(proj3, *prm)
            yp = yp.reshape(b * s, br)
            ys, h_new = lru_sample(proj_s, state_lru_conv[j], state_lru_h[j], *prm)
            cw = b_conv_w.shape[1]
            lru_h_p.append(h_last)
            lru_c_p.append(proj3[:, s - (cw - 1):, :br])
            lru_h_s.append(h_new)
            lru_c_s.append(jnp.concatenate([state_lru_conv[j], proj_s[:, None, :br]], axis=1)[:, 1:])
            w_out = b_w_out
        elif kind == 2:
            br = c_w_out.shape[1]
            n_heads = br // RET_DV
            proj, proj_s = matmul(hp, hs, c_w_in, j)
            yp, st = ret_prompt(proj.reshape(b, s, -1), c_norm[j], n_heads)
            yp = yp.reshape(b * s, br)
            ys, st_s = ret_sample(proj_s, c_norm[j], state_ret[j], n_heads)
            ret_p.append(st)
            ret_s.append(st_s)
            w_out = c_w_out
        else:
            br = d_w_out.shape[1]
            n_heads = br // GLA_DV
            proj, proj_s = matmul(hp, hs, jnp.swapaxes(d_w_in, 1, 2), j, transposed=True)
            yp, st = gla_prompt(proj.reshape(b, s, -1), d_gate_w[j], d_gate_b[j], d_norm[j], n_heads)
            yp = yp.reshape(b * s, br)
            ys, st_s = gla_sample(proj_s, d_gate_w[j], d_gate_b[j], d_norm[j], state_gla[j], n_heads)
            gla_p.append(st)
            gla_s.append(st_s)
            w_out = d_w_out
        op, os_ = matmul(yp, ys, w_out, j)
        g_next = norm_pre[i + 1] if i + 1 < depth else None
        xp, hp = residual_norm(xp, op, norm_post[i], g_next)
        xs, hs = residual_norm(xs, os_, norm_post[i], g_next)

    return (xp.reshape(b, s, d), xs.reshape(bs, 1, d),
            jnp.stack(kp_rows[0]), jnp.stack(ks_rows[0]), jnp.stack(vp_rows[0]), jnp.stack(vs_rows[0]),
            jnp.stack(kp_rows[1]), jnp.stack(ks_rows[1]), jnp.stack(vp_rows[1]), jnp.stack(vs_rows[1]),
            jnp.stack(kp_rows[2]), jnp.stack(ks_rows[2]), jnp.stack(vp_rows[2]), jnp.stack(vs_rows[2]),
            jnp.stack(lru_h_p), jnp.stack(lru_h_s), jnp.stack(lru_c_p), jnp.stack(lru_c_s),
            jnp.stack(ret_p), jnp.stack(ret_s), jnp.stack(gla_p), jnp.stack(gla_s))
```

```python
import functools

import numpy as np
import jax
import jax.numpy as jnp
from jax import lax
from jax.experimental import pallas as pl
from jax.experimental.pallas import tpu as pltpu

F32 = jnp.float32
BF16 = jnp.bfloat16

PAST_LEN = 8192
EPS = 1e-6
NEG_INF = -1e30
A_GROUPS = ((128, 1), (512, 4), (2048, 16))
A_HEAD_DIM = 128
A_BLOCK = 128
ATTN_UNROLL = 16
ATTN_MERGE_ROWS = 256
ATTN_SAMPLE_UNROLL = 16
NUM_BUCKETS = 32
MAX_DISTANCE = 2048
C_RG = 8.0
LRU_TIME_TILE = 512
RET_DK = 256
RET_DV = 512
RET_CHUNK = 128
ROPE_BASE = 10000.0
GLA_DK = 256
GLA_DV = 512
GLA_TAU = 16.0
GLA_CHUNK = 64
GLA_GROUP = 16
RET_GROUP = 8

LANES = 128
SUBLANES = 8
V7X_VMEM_LIMIT_BYTES = 56 * 1024 * 1024

_ARB = pltpu.ARBITRARY


def _params(n_grid):
    return pltpu.CompilerParams(dimension_semantics=(_ARB,) * n_grid,
                                vmem_limit_bytes=V7X_VMEM_LIMIT_BYTES)


def _dot(a, b):
    return jnp.dot(a, b, preferred_element_type=F32)


def _dot_nt(a, b):
    return lax.dot_general(a, b, (((1,), (1,)), ((), ())), preferred_element_type=F32)


def _dot_tn(a, b):
    return lax.dot_general(a, b, (((0,), (0,)), ((), ())), preferred_element_type=F32)


def _sigmoid(x):
    return 0.5 * jnp.tanh(0.5 * x) + 0.5


def _silu(x):
    return x * _sigmoid(x)


def _softplus(x):
    return jnp.maximum(x, 0.0) + jnp.log1p(jnp.exp(-jnp.abs(x)))


def _rms(x, g):
    return x * lax.rsqrt(jnp.mean(x * x, axis=-1, keepdims=True) + EPS) * g


def _chunk_group(n_chunks, want):
    return want if n_chunks % want == 0 else 1


def _col_bcast(row, n):
    return jnp.broadcast_to(row, (n, n)).T


def _norm_kernel(x_ref, g_ref, h_ref):
    h_ref[...] = _rms(x_ref[...], g_ref[...]).astype(BF16)


def _resnorm_kernel(x_ref, y_ref, gpost_ref, gnext_ref, xo_ref, h_ref):
    xn = x_ref[...] + _rms(y_ref[...], gpost_ref[...])
    xo_ref[...] = xn
    h_ref[...] = _rms(xn, gnext_ref[...]).astype(BF16)


def _res_kernel(x_ref, y_ref, gpost_ref, xo_ref):
    xo_ref[...] = x_ref[...] + _rms(y_ref[...], gpost_ref[...])


def _row_tile(m):
    return min(m, 256)


def norm_cast(x, g):
    m, d = x.shape
    tm = _row_tile(m)
    return pl.pallas_call(
        _norm_kernel,
        grid=(m // tm,),
        in_specs=[pl.BlockSpec((tm, d), lambda i: (i, 0)),
                  pl.BlockSpec((1, d), lambda i: (0, 0))],
        out_specs=pl.BlockSpec((tm, d), lambda i: (i, 0)),
        out_shape=jax.ShapeDtypeStruct((m, d), BF16),
        compiler_params=_params(1),
        name="norm_cast",
    )(x, g.reshape(1, d))


def residual_norm(x, y, g_post, g_next):
    m, d = x.shape
    tm = _row_tile(m)
    row = pl.BlockSpec((tm, d), lambda i: (i, 0))
    vec = pl.BlockSpec((1, d), lambda i: (0, 0))
    if g_next is None:
        return pl.pallas_call(
            _res_kernel, grid=(m // tm,),
            in_specs=[row, row, vec], out_specs=row,
            out_shape=jax.ShapeDtypeStruct((m, d), F32),
            compiler_params=_params(1), name="residual",
        )(x, y, g_post.reshape(1, d)), None
    return pl.pallas_call(
        _resnorm_kernel, grid=(m // tm,),
        in_specs=[row, row, vec, vec], out_specs=[row, row],
        out_shape=[jax.ShapeDtypeStruct((m, d), F32), jax.ShapeDtypeStruct((m, d), BF16)],
        compiler_params=_params(1), name="residual_norm",
    )(x, y, g_post.reshape(1, d), g_next.reshape(1, d))


def _mm_kernel(x_ref, w_ref, xs_ref, o_ref, os_ref, *, transposed, tn):
    w = w_ref[...].astype(BF16)
    dot = _dot_nt if transposed else _dot
    o_ref[...] = dot(x_ref[...], w)

    @pl.when(pl.program_id(0) == 0)
    def _():
        cols = pl.ds(pl.multiple_of(pl.program_id(1) * tn, tn), tn)
        os_ref[:, cols] = dot(xs_ref[...], w)


MM_COL_TILE = 512
MM_ROW_TILE = 2048
MM_ROW_TILE_LONG_K = 1024
MM_LONG_K = 4096


def matmul(x, xs, w, layer, col0=0, ncols=None, transposed=False):
    m, k = x.shape
    ms = xs.shape[0]
    n = w.shape[1] if transposed else w.shape[2]
    ncols = n - col0 if ncols is None else ncols
    tn = MM_COL_TILE
    assert col0 % tn == 0
    tm = min(m, MM_ROW_TILE if k <= MM_LONG_K else MM_ROW_TILE_LONG_K)
    assert m % tm == 0
    nj = pl.cdiv(ncols, tn)
    c0 = col0 // tn
    if transposed:
        w_spec = pl.BlockSpec((None, tn, k), lambda i, j: (layer, c0 + j, 0))
    else:
        w_spec = pl.BlockSpec((None, k, tn), lambda i, j: (layer, 0, c0 + j))
    out, out_s = pl.pallas_call(
        functools.partial(_mm_kernel, transposed=transposed, tn=tn),
        grid=(m // tm, nj),
        in_specs=[pl.BlockSpec((tm, k), lambda i, j: (i, 0), pipeline_mode=pl.Buffered(1)),
                  w_spec,
                  pl.BlockSpec((ms, k), lambda i, j: (0, 0))],
        out_specs=[pl.BlockSpec((tm, tn), lambda i, j: (i, j)),
                   pl.BlockSpec((ms, nj * tn), lambda i, j: (0, 0))],
        out_shape=[jax.ShapeDtypeStruct((m, ncols), F32), jax.ShapeDtypeStruct((ms, nj * tn), F32)],
        compiler_params=_params(2),
        name="proj_matmul",
    )(x, w, xs)
    return out, (out_s if nj * tn == ncols else out_s[:, :ncols])


def _t5_bucket(dist):
    n = np.asarray(dist, dtype=np.int64)
    max_exact = NUM_BUCKETS // 2
    ratio = np.log(np.maximum(n, 1) / max_exact) / np.log(MAX_DISTANCE / max_exact)
    large = np.minimum(max_exact + (ratio * (NUM_BUCKETS - max_exact)).astype(np.int64), NUM_BUCKETS - 1)
    return np.where(n < max_exact, n, large).astype(np.int32)


def _attn_prompt_kernel(*refs, dilations, seq, tails):
    ng = len(dilations)
    qkv = refs[:3 * ng]
    gate_ref, bias_ref, y_ref = refs[3 * ng:3 * ng + 3]
    tail_refs = refs[3 * ng + 3:-2]
    o_s, lse_s = refs[-2:]
    blk = A_BLOCK
    for n, (g, keep) in enumerate(tails):
        tail_refs[2 * n][...] = qkv[3 * g + 1][seq - keep:, :]
        tail_refs[2 * n + 1][...] = qkv[3 * g + 2][seq - keep:, :]
    scale = A_HEAD_DIM ** -0.5
    ii = lax.broadcasted_iota(jnp.int32, (blk, blk), 0)
    jj = lax.broadcasted_iota(jnp.int32, (blk, blk), 1)
    cur_ok = jj <= ii
    prev_ok = jj >= ii

    for g, dil in enumerate(dilations):
        q_ref, k_ref, v_ref = qkv[3 * g:3 * g + 3]
        nb = seq // (dil * blk)
        band = pltpu.roll(jnp.broadcast_to(bias_ref[g:g + 1, :], (blk, 2 * blk)), 0, axis=1,
                          stride=1, stride_axis=0)
        b_prev = band[:, :blk]
        b_cur = band[:, blk:]

        def rows_at(start, dil=dil):
            if dil > 1:
                return pl.ds(start, blk, stride=dil)
            return pl.ds(start if isinstance(start, int) else pl.multiple_of(start, blk), blk)

        def scores(idx, dil=dil, nb=nb, q_ref=q_ref, k_ref=k_ref, b_prev=b_prev, b_cur=b_cur,
                   rows_at=rows_at):
            r = idx % dil
            bi = idx // dil
            start = r + bi * (blk * dil)
            rows = rows_at(start)
            q = q_ref[rows, :].astype(BF16)
            s = [jnp.where(cur_ok, _dot_nt(q, k_ref[rows, :].astype(BF16)) * scale + b_cur, NEG_INF)]
            prows = None
            if nb > 1 and not (isinstance(bi, int) and bi == 0):
                if isinstance(bi, int):
                    prows = rows_at(start - blk * dil)
                    prev_mask = prev_ok
                else:
                    prows = rows_at(jnp.maximum(start - blk * dil, 0))
                    prev_mask = prev_ok & (bi > 0)
                s.append(jnp.where(prev_mask, _dot_nt(q, k_ref[prows, :].astype(BF16)) * scale + b_prev,
                                   NEG_INF))
            return rows, prows, s

        def softmax(s):
            m = jnp.max(functools.reduce(jnp.maximum, s), axis=-1, keepdims=True)
            p = [jnp.exp(x - m) for x in s]
            den = jnp.sum(functools.reduce(jnp.add, p), axis=-1, keepdims=True)
            return m, den, [x.astype(BF16) for x in p]

        def values(rows, prows, p, v_ref=v_ref):
            num = _dot(p[0], v_ref[rows, :].astype(BF16))
            if prows is not None:
                num = num + _dot(p[1], v_ref[prows, :].astype(BF16))
            return num

        def body(it, carry, g=g, scores=scores, softmax=softmax, values=values):
            blocks = [scores(it * ATTN_UNROLL + u) for u in range(ATTN_UNROLL)]
            probs = [softmax(s) for _, _, s in blocks]
            nums = [values(rows, prows, p) for (rows, prows, _), (_, _, p) in zip(blocks, probs)]
            for (rows, _, _), (m, den, _), num in zip(blocks, probs, nums):
                o_s[g, rows, :] = num / den
                lse_s[g, rows, :] = jnp.broadcast_to(m + jnp.log(den), (blk, A_HEAD_DIM))
            return carry

        if seq // blk == ATTN_UNROLL:
            body(0, 0)
        else:
            lax.fori_loop(0, seq // (blk * ATTN_UNROLL), body, 0)

    tr = ATTN_MERGE_ROWS

    def merge(c, carry):
        rows = pl.ds(pl.multiple_of(c * tr, tr), tr)
        lses = [lse_s[g, rows, :] for g in range(ng)]
        mx = functools.reduce(jnp.maximum, lses)
        num = jnp.zeros((tr, A_HEAD_DIM), F32)
        den = jnp.zeros((tr, A_HEAD_DIM), F32)
        for g in range(ng):
            w = jnp.exp(lses[g] - mx)
            num = num + w * o_s[g, rows, :]
            den = den + w
        y_ref[rows, :] = ((num / den) * _silu(gate_ref[rows, :])).astype(BF16)
        return carry

    lax.fori_loop(0, seq // tr, merge, 0)


def attn_prompt(qs, ks, vs, gate, bias_tab, n_heads):
    b, s, _ = gate[0].shape
    ng = len(A_GROUPS)
    dh = A_HEAD_DIM
    in_specs = []
    args = []
    units = [u for g in range(ng) for u in (qs[g], ks[g], vs[g])] + [gate]
    for arr, unit in units:
        in_specs.append(pl.BlockSpec((None, s, dh), lambda bi, hi, unit=unit: (bi, 0, unit * n_heads + hi)))
        args.append(arr)
    in_specs.append(pl.BlockSpec((None, ng, 2 * A_BLOCK), lambda bi, hi: (hi, 0, 0)))
    args.append(bias_tab)
    tails = tuple((g, window) for g, (window, _) in enumerate(A_GROUPS) if window < s)
    kern = functools.partial(_attn_prompt_kernel, dilations=tuple(d for _, d in A_GROUPS), seq=s,
                             tails=tails)
    out_specs = [pl.BlockSpec((None, s, dh), lambda bi, hi: (bi, 0, hi))]
    out_shape = [jax.ShapeDtypeStruct((b, s, n_heads * dh), BF16)]
    for _, keep in tails:
        out_specs += [pl.BlockSpec((None, keep, dh), lambda bi, hi: (bi, 0, hi))] * 2
        out_shape += [jax.ShapeDtypeStruct((b, keep, n_heads * dh), F32)] * 2
    return pl.pallas_call(
        kern,
        grid=(b, n_heads),
        in_specs=in_specs,
        out_specs=out_specs,
        out_shape=out_shape,
        scratch_shapes=[pltpu.VMEM((ng, s, dh), F32)] * 2,
        compiler_params=_params(2),
        name="attn_prompt",
    )(*args)


def _attn_sample_kernel(q_ref, kn_ref, vn_ref, gate_ref, *refs, n_heads, n_keys):
    ng = len(A_GROUPS)
    caches = refs[:2 * ng]
    bpast_ref, bself_ref, y_ref = refs[2 * ng:]
    dh = A_HEAD_DIM
    scale = dh ** -0.5
    lane = lax.broadcasted_iota(jnp.int32, (n_heads, n_keys), 1)
    parts = []
    for g in range(ng):
        kc_ref, vc_ref = caches[2 * g], caches[2 * g + 1]
        q = q_ref[g]
        vn = vn_ref[g]

        def logits(j, acc, kc_ref=kc_ref, q=q):
            col = jnp.sum(kc_ref[j] * q, axis=-1, keepdims=True)
            return jnp.where(lane == j, col, acc)

        lp = lax.fori_loop(0, n_keys, logits, jnp.zeros((n_heads, n_keys), F32), unroll=ATTN_SAMPLE_UNROLL)
        lp = lp * scale + bpast_ref[g]
        ls = jnp.sum(kn_ref[g] * q, axis=-1, keepdims=True) * scale + bself_ref[g]
        m = jnp.maximum(jnp.max(lp, axis=-1, keepdims=True), ls)
        p = jnp.exp(lp - m)
        ps = jnp.exp(ls - m)
        den = jnp.sum(p, axis=-1, keepdims=True) + ps

        def weighted(j, acc, vc_ref=vc_ref, p=p):
            pj = jnp.sum(jnp.where(lane == j, p, 0.0), axis=-1, keepdims=True)
            return acc + pj * vc_ref[j]

        num = lax.fori_loop(0, n_keys, weighted, ps * vn, unroll=ATTN_SAMPLE_UNROLL)
        parts.append((num, m, den))
    mx = functools.reduce(jnp.maximum, [p_[1] for p_ in parts])
    num = jnp.zeros((n_heads, dh), F32)
    den = jnp.zeros((n_heads, 1), F32)
    for num_g, m_g, den_g in parts:
        w = jnp.exp(m_g - mx)
        num = num + w * num_g
        den = den + w * den_g
    y_ref[...] = ((num / den) * _silu(gate_ref[...])).astype(BF16)


def attn_sample(qs, ks, vs, gate, k_caches, v_caches, bias_past, bias_self, n_heads):
    bs, width = gate.shape
    dh = A_HEAD_DIM
    ng = len(A_GROUPS)
    n_keys = A_BLOCK
    heads = lambda rows: jnp.stack(rows, axis=1).reshape(bs, ng, n_heads, dh)
    grp_spec = pl.BlockSpec((None, ng, n_heads, dh), lambda bi: (bi, 0, 0, 0))
    args = [heads(qs), heads(ks), heads(vs), gate.reshape(bs, n_heads, dh)]
    in_specs = [grp_spec, grp_spec, grp_spec, pl.BlockSpec((None, n_heads, dh), lambda bi: (bi, 0, 0))]
    for (_, dil), kc, vc in zip(A_GROUPS, k_caches, v_caches):
        for c in (kc, vc):
            args.append(c.reshape(bs, n_keys, dil, n_heads, dh))
            in_specs.append(pl.BlockSpec((None, n_keys, None, n_heads, dh), lambda bi: (bi, 0, 0, 0, 0)))
    args += [bias_past, bias_self]
    in_specs += [pl.BlockSpec(bias_past.shape, lambda bi: (0, 0, 0)),
                 pl.BlockSpec(bias_self.shape, lambda bi: (0, 0, 0))]
    y = pl.pallas_call(
        functools.partial(_attn_sample_kernel, n_heads=n_heads, n_keys=n_keys),
        grid=(bs,),
        in_specs=in_specs,
        out_specs=pl.BlockSpec((None, n_heads, dh), lambda bi: (bi, 0, 0)),
        out_shape=jax.ShapeDtypeStruct((bs, n_heads, dh), BF16),
        compiler_params=_params(1),
        name="attn_sample",
    )(*args)
    return y.reshape(bs, width)


def _lru_gates(xc, wa_ref, ba_ref, wx_ref, bx_ref, lam_ref):
    xb = xc.astype(BF16)
    r = _sigmoid(_dot(xb, wa_ref[...].astype(BF16)) + ba_ref[...])
    ig = _sigmoid(_dot(xb, wx_ref[...].astype(BF16)) + bx_ref[...])
    log_a = -C_RG * r * _softplus(-lam_ref[...])
    a = jnp.exp(log_a)
    mult = jnp.sqrt(-jnp.tanh(log_a) * (a * a + 1.0))
    return a, mult, ig


def _lru_prompt_kernel(x_ref, gate_ref, cw_ref, cb_ref, wa_ref, ba_ref, wx_ref, bx_ref, lam_ref,
                       y_ref, hl_ref, hc_ref, xp_ref, *, tt, conv_w):
    t = pl.program_id(2)

    @pl.when(t == 0)
    def _():
        hc_ref[...] = jnp.zeros_like(hc_ref)
        xp_ref[:SUBLANES, :] = jnp.zeros((SUBLANES, xp_ref.shape[1]), F32)

    x = x_ref[...]
    c = x.shape[1]
    xp_ref[SUBLANES:, :] = x
    xc = cb_ref[...] + cw_ref[conv_w - 1:conv_w, :] * x
    for k in range(1, conv_w):
        xc = xc + cw_ref[conv_w - 1 - k:conv_w - k, :] * xp_ref[SUBLANES - k:SUBLANES - k + tt, :]
    xp_ref[:SUBLANES, :] = x[tt - SUBLANES:]

    row = lax.broadcasted_iota(jnp.int32, (tt, c), 0)
    a, mult, ig = _lru_gates(xc, wa_ref, ba_ref, wx_ref, bx_ref, lam_ref)
    mult = jnp.where(row + t * tt == 0, 1.0, mult)
    bx = mult * ig * xc
    ng = tt // SUBLANES
    a = a.reshape(ng, SUBLANES, c)
    bx = bx.reshape(ng, SUBLANES, c)
    sub = lax.broadcasted_iota(jnp.int32, (ng, SUBLANES, c), 1)
    d = 1
    while d < SUBLANES:
        keep = sub >= d
        a_sh = jnp.where(keep, pltpu.roll(a, d, axis=1), 1.0)
        b_sh = jnp.where(keep, pltpu.roll(bx, d, axis=1), 0.0)
        bx = a * b_sh + bx
        a = a * a_sh
        d *= 2
    h = hc_ref[...]
    groups = []
    for g in range(ng):
        groups.append(a[g] * h + bx[g])
        h = groups[-1][SUBLANES - 1:]
    y_ref[...] = (jnp.concatenate(groups, axis=0) * _silu(gate_ref[...])).astype(BF16)
    hc_ref[...] = h
    hl_ref[...] = h


def lru_prompt(proj, conv_w, conv_b, wa, ba, wx, bx, lam):
    b, s, two_br = proj.shape
    br = two_br // 2
    nblk, bs_, _ = wa.shape
    cw = conv_w.shape[0]
    tt = min(s, LRU_TIME_TILE)
    vec = lambda a: a.reshape(1, br)
    vspec = pl.BlockSpec((1, bs_), lambda bi, ni, ti: (0, ni))
    wspec = pl.BlockSpec((None, bs_, bs_), lambda bi, ni, ti: (ni, 0, 0))
    y, hl = pl.pallas_call(
        functools.partial(_lru_prompt_kernel, tt=tt, conv_w=cw),
        grid=(b, nblk, s // tt),
        in_specs=[pl.BlockSpec((None, tt, bs_), lambda bi, ni, ti: (bi, ti, ni)),
                  pl.BlockSpec((None, tt, bs_), lambda bi, ni, ti: (bi, ti, nblk + ni)),
                  pl.BlockSpec((cw, bs_), lambda bi, ni, ti: (0, ni)),
                  vspec, wspec, vspec, wspec, vspec, vspec],
        out_specs=[pl.BlockSpec((None, tt, bs_), lambda bi, ni, ti: (bi, ti, ni)),
                   pl.BlockSpec((None, 1, bs_), lambda bi, ni, ti: (bi, 0, ni))],
        out_shape=[jax.ShapeDtypeStruct((b, s, br), BF16), jax.ShapeDtypeStruct((b, 1, br), F32)],
        scratch_shapes=[pltpu.VMEM((1, bs_), F32), pltpu.VMEM((SUBLANES + tt, bs_), F32)],
        compiler_params=_params(3),
        name="lru_prompt",
    )(proj, proj, conv_w, vec(conv_b), wa, vec(ba), wx, vec(bx), vec(lam))
    return y, hl.reshape(b, br)


def _lru_sample_kernel(x_ref, gate_ref, buf_ref, h0_ref, cw_ref, cb_ref, wa_ref, ba_ref, wx_ref, bx_ref,
                       lam_ref, y_ref, h_ref, *, conv_w):
    x = x_ref[...]
    xc = cb_ref[...] + cw_ref[conv_w - 1:conv_w, :] * x
    for w in range(conv_w - 1):
        xc = xc + cw_ref[w:w + 1, :] * buf_ref[:, w, :]
    a, mult, ig = _lru_gates(xc, wa_ref, ba_ref, wx_ref, bx_ref, lam_ref)
    h = a * h0_ref[...] + mult * ig * xc
    h_ref[...] = h
    y_ref[...] = (h * _silu(gate_ref[...])).astype(BF16)


def lru_sample(proj_s, conv_buf, h0, conv_w, conv_b, wa, ba, wx, bx, lam):
    bs, two_br = proj_s.shape
    br = two_br // 2
    nblk, bs_, _ = wa.shape
    cw = conv_w.shape[0]
    vec = lambda a: a.reshape(1, br)
    vspec = pl.BlockSpec((1, bs_), lambda ni: (0, ni))
    wspec = pl.BlockSpec((None, bs_, bs_), lambda ni: (ni, 0, 0))
    rspec = pl.BlockSpec((bs, bs_), lambda ni: (0, ni))
    return pl.pallas_call(
        functools.partial(_lru_sample_kernel, conv_w=cw),
        grid=(nblk,),
        in_specs=[rspec,
                  pl.BlockSpec((bs, bs_), lambda ni: (0, nblk + ni)),
                  pl.BlockSpec((bs, cw - 1, bs_), lambda ni: (0, 0, ni)),
                  rspec,
                  pl.BlockSpec((cw, bs_), lambda ni: (0, ni)),
                  vspec, wspec, vspec, wspec, vspec, vspec],
        out_specs=[rspec, rspec],
        out_shape=[jax.ShapeDtypeStruct((bs, br), BF16), jax.ShapeDtypeStruct((bs, br), F32)],
        compiler_params=_params(1),
        name="lru_sample",
    )(proj_s, proj_s, conv_buf, h0, conv_w, vec(conv_b), wa, vec(ba), wx, vec(bx), vec(lam))


def _rope(x, cos, sin):
    half = x.shape[-1] // 2
    x1, x2 = x[:, :half], x[:, half:]
    return jnp.concatenate([x1 * cos - x2 * sin, x1 * sin + x2 * cos], axis=-1)


def _groupnorm(o, gain):
    c = o - jnp.mean(o, axis=-1, keepdims=True)
    return c * lax.rsqrt(jnp.mean(c * c, axis=-1, keepdims=True) + EPS) * gain


def _ret_prompt_kernel(q_ref, k_ref, v_ref, gate_ref, cos_ref, sin_ref, lg_ref, gain_ref,
                       y_ref, s_ref, st_ref, *, chunk, n_chunks, group):
    lg = lg_ref[:, :1]
    idx = lax.broadcasted_iota(jnp.int32, (chunk, 1), 0).astype(F32)
    ii = lax.broadcasted_iota(jnp.int32, (chunk, chunk), 0)
    jj = lax.broadcasted_iota(jnp.int32, (chunk, chunk), 1)
    diff = (ii - jj).astype(F32)
    decay = jnp.where(diff >= 0, jnp.exp(diff * lg), 0.0)
    q_dec = jnp.exp((idx + 1.0) * lg)
    k_dec = jnp.exp((chunk - 1.0 - idx) * lg)
    chunk_dec = jnp.exp(chunk * lg)
    gain = gain_ref[...]
    st_ref[...] = jnp.zeros_like(st_ref)

    q_dec_rows = jnp.concatenate([q_dec] * group, axis=0)
    k_dec_rows = jnp.concatenate([k_dec] * group, axis=0)

    def body(it, carry):
        rows = pl.ds(pl.multiple_of(it * (group * chunk), group * chunk), group * chunk)
        cos = cos_ref[rows, :]
        sin = sin_ref[rows, :]
        qc = _rope(q_ref[rows, :], cos, sin)
        kc = _rope(k_ref[rows, :], cos, sin) * (RET_DK ** -0.5)
        vc = v_ref[rows, :].astype(BF16)
        qb = qc.astype(BF16)
        kb = kc.astype(BF16)
        qd = (qc * q_dec_rows).astype(BF16)
        kd = (kc * k_dec_rows).astype(BF16)
        sl = [slice(u * chunk, (u + 1) * chunk) for u in range(group)]
        intra, upd = [], []
        for u in range(group):
            scores = _dot_nt(qb[sl[u]], kb[sl[u]]) * decay
            intra.append(_dot(scores.astype(BF16), vc[sl[u]]))
            upd.append(_dot_tn(kd[sl[u]], vc[sl[u]]))
        st = st_ref[...]
        outs = []
        for u in range(group):
            outs.append(intra[u] + _dot(qd[sl[u]], st.astype(BF16)))
            st = chunk_dec * st + upd[u]
        st_ref[...] = st
        o = jnp.concatenate(outs, axis=0)
        y_ref[rows, :] = (_groupnorm(o, gain) * _silu(gate_ref[rows, :])).astype(BF16)
        return carry

    lax.fori_loop(0, n_chunks // group, body, 0)
    s_ref[...] = st_ref[...]


def _ret_log_gamma(n_heads):
    lg = np.log1p(-np.exp2(-5.0 - np.arange(n_heads, dtype=np.float32))).astype(np.float32)
    return jnp.asarray(np.broadcast_to(lg[:, None, None], (n_heads, 1, LANES)).copy())


def _rope_tables(pos):
    half = RET_DK // 2
    inv_freq = ROPE_BASE ** (-jnp.arange(half, dtype=F32) / half)
    ang = pos[:, None] * inv_freq[None, :]
    return jnp.cos(ang), jnp.sin(ang)


def ret_prompt(proj, gain, n_heads):
    b, s, _ = proj.shape
    dk, dv = RET_DK, RET_DV
    qk = n_heads * dk
    chunk = RET_CHUNK if s % RET_CHUNK == 0 else s
    cos, sin = _rope_tables(jnp.arange(s, dtype=F32))
    half = dk // 2
    koff = qk // dk
    voff = 2 * qk // dv
    goff = (2 * qk + n_heads * dv) // dv
    return pl.pallas_call(
        functools.partial(_ret_prompt_kernel, chunk=chunk, n_chunks=s // chunk,
                          group=_chunk_group(s // chunk, RET_GROUP)),
        grid=(b, n_heads),
        in_specs=[pl.BlockSpec((None, s, dk), lambda bi, hi: (bi, 0, hi)),
                  pl.BlockSpec((None, s, dk), lambda bi, hi: (bi, 0, koff + hi)),
                  pl.BlockSpec((None, s, dv), lambda bi, hi: (bi, 0, voff + hi)),
                  pl.BlockSpec((None, s, dv), lambda bi, hi: (bi, 0, goff + hi)),
                  pl.BlockSpec((s, half), lambda bi, hi: (0, 0)),
                  pl.BlockSpec((s, half), lambda bi, hi: (0, 0)),
                  pl.BlockSpec((None, 1, LANES), lambda bi, hi: (hi, 0, 0)),
                  pl.BlockSpec((1, dv), lambda bi, hi: (0, hi))],
        out_specs=[pl.BlockSpec((None, s, dv), lambda bi, hi: (bi, 0, hi)),
                   pl.BlockSpec((None, None, dk, dv), lambda bi, hi: (bi, hi, 0, 0))],
        out_shape=[jax.ShapeDtypeStruct((b, s, n_heads * dv), BF16),
                   jax.ShapeDtypeStruct((b, n_heads, dk, dv), F32)],
        scratch_shapes=[pltpu.VMEM((dk, dv), F32)],
        compiler_params=_params(2),
        name="ret_prompt",
    )(proj, proj, proj, proj, cos, sin, _ret_log_gamma(n_heads), gain.reshape(1, n_heads * dv))


def _ret_sample_kernel(q_ref, k_ref, v_ref, gate_ref, cos_ref, sin_ref, lg_ref, gain_ref, s0_ref,
                       y_ref, s_ref):
    dk, dv = RET_DK, RET_DV
    bs = q_ref.shape[0]
    gamma = jnp.exp(lg_ref[:, :1])
    cos = cos_ref[...]
    sin = sin_ref[...]
    q = _rope(q_ref[...], cos, sin)
    k = _rope(k_ref[...], cos, sin) * (dk ** -0.5)
    v = v_ref[...]
    o = jnp.sum(q * k, axis=-1, keepdims=True) * v
    qg = (q * gamma).astype(BF16)
    row = lax.broadcasted_iota(jnp.int32, (bs, dv), 0)
    for b in range(bs):
        s0 = s0_ref[b]
        o = o + jnp.where(row == b, _dot(qg, s0.astype(BF16)), 0.0)
        kcol = _col_bcast(k[b:b + 1], dk)
        for j in range(dv // dk):
            cols = slice(j * dk, (j + 1) * dk)
            s_ref[b, :, cols] = gamma * s0[:, cols] + kcol * v[b:b + 1, cols]
    y_ref[...] = (_groupnorm(o, gain_ref[...]) * _silu(gate_ref[...])).astype(BF16)


def ret_sample(proj_s, gain, s0, n_heads):
    bs = proj_s.shape[0]
    dk, dv = RET_DK, RET_DV
    qk = n_heads * dk
    cos, sin = _rope_tables(jnp.full((1,), PAST_LEN, F32))
    half = dk // 2
    koff = qk // dk
    voff = 2 * qk // dv
    goff = (2 * qk + n_heads * dv) // dv
    state_spec = pl.BlockSpec((bs, None, dk, dv), lambda hi: (0, hi, 0, 0))
    return pl.pallas_call(
        _ret_sample_kernel,
        grid=(n_heads,),
        in_specs=[pl.BlockSpec((bs, dk), lambda hi: (0, hi)),
                  pl.BlockSpec((bs, dk), lambda hi: (0, koff + hi)),
                  pl.BlockSpec((bs, dv), lambda hi: (0, voff + hi)),
                  pl.BlockSpec((bs, dv), lambda hi: (0, goff + hi)),
                  pl.BlockSpec((1, half), lambda hi: (0, 0)),
                  pl.BlockSpec((1, half), lambda hi: (0, 0)),
                  pl.BlockSpec((None, 1, LANES), lambda hi: (hi, 0, 0)),
                  pl.BlockSpec((1, dv), lambda hi: (0, hi)),
                  state_spec],
        out_specs=[pl.BlockSpec((bs, dv), lambda hi: (0, hi)), state_spec],
        out_shape=[jax.ShapeDtypeStruct((bs, n_heads * dv), BF16),
                   jax.ShapeDtypeStruct((bs, n_heads, dk, dv), F32)],
        compiler_params=_params(1),
        name="ret_sample",
    )(proj_s, proj_s, proj_s, proj_s, cos, sin, _ret_log_gamma(n_heads), gain.reshape(1, n_heads * dv), s0)


def _headnorm(o, gain):
    return o * lax.rsqrt(jnp.mean(o * o, axis=-1, keepdims=True) + EPS) * gain


def _gla_log_alpha(low_ref_val, gw_ref, gb_ref, rank):
    lane = lax.broadcasted_iota(jnp.int32, low_ref_val.shape, 1)
    low = jnp.where(lane < rank, low_ref_val, 0.0).astype(BF16)
    z = _dot(low, gw_ref[...].astype(BF16)) + gb_ref[...]
    return (jnp.minimum(z, 0.0) - jnp.log1p(jnp.exp(-jnp.abs(z)))) / GLA_TAU


def _gla_prompt_kernel(q_ref, k_ref, v_ref, gate_ref, low_ref, gw_ref, gb_ref, gain_ref,
                       y_ref, s_ref, st_ref, *, chunk, n_chunks, group, rank):
    dk, dv = GLA_DK, GLA_DV
    sub = lax.broadcasted_iota(jnp.int32, (group * chunk, dk), 0) % chunk
    ii = lax.broadcasted_iota(jnp.int32, (chunk, chunk), 0)
    jj = lax.broadcasted_iota(jnp.int32, (chunk, chunk), 1)
    causal = jj <= ii
    gain = gain_ref[...]
    st_ref[...] = jnp.zeros_like(st_ref)

    def body(it, carry):
        rows = pl.ds(pl.multiple_of(it * (group * chunk), group * chunk), group * chunk)
        bcum = _gla_log_alpha(low_ref[rows, :], gw_ref, gb_ref, rank)
        d = 1
        while d < chunk:
            bcum = bcum + jnp.where(sub >= d, pltpu.roll(bcum, d, axis=0), 0.0)
            d *= 2
        sl = [slice(u * chunk, (u + 1) * chunk) for u in range(group)]
        blast = [bcum[(u + 1) * chunk - 1:(u + 1) * chunk] for u in range(group)]
        blast_rows = jnp.concatenate([jnp.broadcast_to(b, (chunk, dk)) for b in blast], axis=0)
        kc = k_ref[rows, :]
        vc = v_ref[rows, :].astype(BF16)
        qe = (q_ref[rows, :] * (dk ** -0.5) * jnp.exp(bcum)).astype(BF16)
        ke = (kc * jnp.exp(-bcum)).astype(BF16)
        kd = (kc * jnp.exp(blast_rows - bcum)).astype(BF16)
        intra, upd, dec = [], [], []
        for u in range(group):
            scores = jnp.where(causal, _dot_nt(qe[sl[u]], ke[sl[u]]), 0.0)
            intra.append(_dot(scores.astype(BF16), vc[sl[u]]))
            upd.append(_dot_tn(kd[sl[u]], vc[sl[u]]))
            dec.append(_col_bcast(jnp.exp(blast[u]), dk))
        st = st_ref[...]
        outs = []
        for u in range(group):
            outs.append(intra[u] + _dot(qe[sl[u]], st.astype(BF16)))
            st = jnp.concatenate([dec[u] * st[:, j * dk:(j + 1) * dk] + upd[u][:, j * dk:(j + 1) * dk]
                                  for j in range(dv // dk)], axis=1)
        st_ref[...] = st
        o = jnp.concatenate(outs, axis=0)
        y_ref[rows, :] = (_headnorm(o, gain) * _silu(gate_ref[rows, :])).astype(BF16)
        return carry

    lax.fori_loop(0, n_chunks // group, body, 0)
    s_ref[...] = st_ref[...]


def _gla_offsets(n_heads):
    dk, dv = GLA_DK, GLA_DV
    qk = n_heads * dk
    koff = qk // dk
    voff = 2 * qk // dv
    goff = (2 * qk + n_heads * dv) // dv
    loff = (2 * qk + 2 * n_heads * dv) // LANES
    return koff, voff, goff, loff


def _pad_rank(gate_w):
    rank = gate_w.shape[0]
    return jnp.pad(gate_w, ((0, LANES - rank), (0, 0)))


def gla_prompt(proj, gate_w, gate_b, gain, n_heads):
    b, s, _ = proj.shape
    dk, dv = GLA_DK, GLA_DV
    rank = gate_w.shape[0]
    chunk = GLA_CHUNK if s % GLA_CHUNK == 0 else s
    koff, voff, goff, loff = _gla_offsets(n_heads)
    return pl.pallas_call(
        functools.partial(_gla_prompt_kernel, chunk=chunk, n_chunks=s // chunk,
                          group=_chunk_group(s // chunk, GLA_GROUP), rank=rank),
        grid=(b, n_heads),
        in_specs=[pl.BlockSpec((None, s, dk), lambda bi, hi: (bi, 0, hi)),
                  pl.BlockSpec((None, s, dk), lambda bi, hi: (bi, 0, koff + hi)),
                  pl.BlockSpec((None, s, dv), lambda bi, hi: (bi, 0, voff + hi)),
                  pl.BlockSpec((None, s, dv), lambda bi, hi: (bi, 0, goff + hi)),
                  pl.BlockSpec((None, s, LANES), lambda bi, hi: (bi, 0, loff)),
                  pl.BlockSpec((LANES, dk), lambda bi, hi: (0, hi)),
                  pl.BlockSpec((1, dk), lambda bi, hi: (0, hi)),
                  pl.BlockSpec((1, dv), lambda bi, hi: (0, hi))],
        out_specs=[pl.BlockSpec((None, s, dv), lambda bi, hi: (bi, 0, hi)),
                   pl.BlockSpec((None, None, dk, dv), lambda bi, hi: (bi, hi, 0, 0))],
        out_shape=[jax.ShapeDtypeStruct((b, s, n_heads * dv), BF16),
                   jax.ShapeDtypeStruct((b, n_heads, dk, dv), F32)],
        scratch_shapes=[pltpu.VMEM((dk, dv), F32)],
        compiler_params=_params(2),
        name="gla_prompt",
    )(proj, proj, proj, proj, proj, _pad_rank(gate_w), gate_b.reshape(1, -1), gain.reshape(1, -1))


def _gla_sample_kernel(q_ref, k_ref, v_ref, gate_ref, low_ref, gw_ref, gb_ref, gain_ref, s0_ref,
                       y_ref, s_ref, *, rank):
    dk, dv = GLA_DK, GLA_DV
    bs = q_ref.shape[0]
    g = _gla_log_alpha(low_ref[...], gw_ref, gb_ref, rank)
    q = q_ref[...] * (dk ** -0.5)
    k = k_ref[...]
    v = v_ref[...]
    qe = q * jnp.exp(g)
    ke = k * jnp.exp(-g)
    kd = k * jnp.exp(g - g)
    eg = jnp.exp(g)
    o = jnp.sum(qe * ke, axis=-1, keepdims=True) * v
    qeb = qe.astype(BF16)
    row = lax.broadcasted_iota(jnp.int32, (bs, dv), 0)
    for b in range(bs):
        s0 = s0_ref[b]
        o = o + jnp.where(row == b, _dot(qeb, s0.astype(BF16)), 0.0)
        dec = _col_bcast(eg[b:b + 1], dk)
        kcol = _col_bcast(kd[b:b + 1], dk)
        for j in range(dv // dk):
            cols = slice(j * dk, (j + 1) * dk)
            s_ref[b, :, cols] = dec * s0[:, cols] + kcol * v[b:b + 1, cols]
    y_ref[...] = (_headnorm(o, gain_ref[...]) * _silu(gate_ref[...])).astype(BF16)


def gla_sample(proj_s, gate_w, gate_b, gain, s0, n_heads):
    bs = proj_s.shape[0]
    dk, dv = GLA_DK, GLA_DV
    rank = gate_w.shape[0]
    koff, voff, goff, loff = _gla_offsets(n_heads)
    state_spec = pl.BlockSpec((bs, None, dk, dv), lambda hi: (0, hi, 0, 0))
    return pl.pallas_call(
        functools.partial(_gla_sample_kernel, rank=rank),
        grid=(n_heads,),
        in_specs=[pl.BlockSpec((bs, dk), lambda hi: (0, hi)),
                  pl.BlockSpec((bs, dk), lambda hi: (0, koff + hi)),
                  pl.BlockSpec((bs, dv), lambda hi: (0, voff + hi)),
                  pl.BlockSpec((bs, dv), lambda hi: (0, goff + hi)),
                  pl.BlockSpec((bs, LANES), lambda hi: (0, loff)),
                  pl.BlockSpec((LANES, dk), lambda hi: (0, hi)),
                  pl.BlockSpec((1, dk), lambda hi: (0, hi)),
                  pl.BlockSpec((1, dv), lambda hi: (0, hi)),
                  state_spec],
        out_specs=[pl.BlockSpec((bs, dv), lambda hi: (0, hi)), state_spec],
        out_shape=[jax.ShapeDtypeStruct((bs, n_heads * dv), BF16),
                   jax.ShapeDtypeStruct((bs, n_heads, dk, dv), F32)],
        compiler_params=_params(1),
        name="gla_sample",
    )(proj_s, proj_s, proj_s, proj_s, proj_s, _pad_rank(gate_w), gate_b.reshape(1, -1),
      gain.reshape(1, -1), s0)


def _attention_bias_tables(rel_bias, n_heads):
    blk = A_BLOCK
    ng = len(A_GROUPS)
    onehot = np.zeros((ng, 2 * blk, NUM_BUCKETS), np.float32)
    for g, (window, dil) in enumerate(A_GROUPS):
        n_keys = window // dil
        assert n_keys == blk, "one 128-key band per dilated stream is assumed"
        u = np.arange(n_keys + 1)
        onehot[g, u, _t5_bucket(dil * (n_keys - u))] = 1.0
    tab = jnp.einsum("gub,bgh->guh", jnp.asarray(onehot), rel_bias.astype(F32).reshape(NUM_BUCKETS, ng, n_heads),
                     precision=lax.Precision.HIGHEST)
    tab_t = tab.transpose(0, 2, 1)
    return tab.transpose(2, 0, 1), tab_t[:, :, :blk], tab_t[:, :, blk:blk + 1]


def kernel(x_prompt, x_sample, cache_k_w128, cache_v_w128, cache_k_w512, cache_v_w512, cache_k_w2048, cache_v_w2048, state_lru_h, state_lru_conv, state_ret, state_gla, norm_pre, norm_post, rel_bias, a_w_in, a_w_out, b_w_in, b_conv_w, b_conv_b, b_gate_a_w, b_gate_a_b, b_gate_x_w, b_gate_x_b, b_lambda, b_w_out, c_w_in, c_norm, c_w_out, d_w_in, d_gate_w, d_gate_b, d_norm, d_w_out):
    b, s, d = x_prompt.shape
    bs = x_sample.shape[0]
    assert x_sample.shape[1] == 1, "one new token per sequence"
    depth = norm_pre.shape[0]
    k_caches = (cache_k_w128, cache_k_w512, cache_k_w2048)
    v_caches = (cache_v_w128, cache_v_w512, cache_v_w2048)
    ng = len(A_GROUPS)
    for (window, dil), kc in zip(A_GROUPS, k_caches):
        assert kc.shape[2] == window and s % (dil * A_BLOCK) == 0

    xp = x_prompt.reshape(b * s, d)
    xs = x_sample.reshape(bs, d)
    hp = norm_cast(xp, norm_pre[0])
    hs = norm_cast(xs, norm_pre[0])

    kp_rows = [[] for _ in A_GROUPS]
    vp_rows = [[] for _ in A_GROUPS]
    ks_rows = [[] for _ in A_GROUPS]
    vs_rows = [[] for _ in A_GROUPS]
    lru_h_p, lru_h_s, lru_c_p, lru_c_s = [], [], [], []
    ret_p, ret_s, gla_p, gla_s = [], [], [], []

    for i in range(depth):
        kind, j = i % 4, i // 4
        if kind == 0:
            n_heads = a_w_out.shape[1] // A_HEAD_DIM
            width = n_heads * A_HEAD_DIM
            full = [min(window, s) == s for window, _ in A_GROUPS]
            cuts = {0, 3 * ng + 1}
            for which in (1, 2):
                for g in range(ng):
                    if full[g]:
                        cuts |= {which * ng + g, which * ng + g + 1}
            cuts = sorted(cuts)
            unit_p, unit_s = {}, {}
            for u0, u1 in zip(cuts[:-1], cuts[1:]):
                arr_p, arr_s = matmul(hp, hs, a_w_in, j, col0=u0 * width, ncols=(u1 - u0) * width)
                for u in range(u0, u1):
                    unit_p[u] = (arr_p.reshape(b, s, -1), u - u0)
                    unit_s[u] = arr_s[:, (u - u0) * width:(u - u0 + 1) * width]
            bias_tab, bias_past, bias_self = _attention_bias_tables(rel_bias, n_heads)
            yp, *tails = attn_prompt([unit_p[g] for g in range(ng)], [unit_p[ng + g] for g in range(ng)],
                                     [unit_p[2 * ng + g] for g in range(ng)], unit_p[3 * ng],
                                     bias_tab, n_heads)
            yp = yp.reshape(b * s, width)
            ys = attn_sample([unit_s[g] for g in range(ng)], [unit_s[ng + g] for g in range(ng)],
                             [unit_s[2 * ng + g] for g in range(ng)], unit_s[3 * ng],
                             [c[j] for c in k_caches], [c[j] for c in v_caches],
                             bias_past, bias_self, n_heads)
            for g in range(ng):
                if full[g]:
                    k_rows, v_rows = unit_p[ng + g][0], unit_p[2 * ng + g][0]
                else:
                    k_rows, v_rows = tails.pop(0), tails.pop(0)
                kp_rows[g].append(k_rows.reshape(b, -1, n_heads, A_HEAD_DIM))
                vp_rows[g].append(v_rows.reshape(b, -1, n_heads, A_HEAD_DIM))
                ks_rows[g].append(unit_s[ng + g].reshape(bs, 1, n_heads, A_HEAD_DIM))
                vs_rows[g].append(unit_s[2 * ng + g].reshape(bs, 1, n_heads, A_HEAD_DIM))
            w_out = a_w_out
        elif kind == 1:
            br = b_w_out.shape[1]
            proj, proj_s = matmul(hp, hs, b_w_in, j)
            prm = (b_conv_w[j], b_conv_b[j], b_gate_a_w[j], b_gate_a_b[j], b_gate_x_w[j], b_gate_x_b[j], b_lambda[j])
            proj3 = proj.reshape(b, s, -1)
            yp, h_last = lru_prompt(proj3, *prm)
            yp = yp.reshape(b * s, br)
            ys, h_new = lru_sample(proj_s, state_lru_conv[j], state_lru_h[j], *prm)
            cw = b_conv_w.shape[1]
            lru_h_p.append(h_last)
            lru_c_p.append(proj3[:, s - (cw - 1):, :br])
            lru_h_s.append(h_new)
            lru_c_s.append(jnp.concatenate([state_lru_conv[j], proj_s[:, None, :br]], axis=1)[:, 1:])
            w_out = b_w_out
        elif kind == 2:
            br = c_w_out.shape[1]
            n_heads = br // RET_DV
            proj, proj_s = matmul(hp, hs, c_w_in, j)
            yp, st = ret_prompt(proj.reshape(b, s, -1), c_norm[j], n_heads)
            yp = yp.reshape(b * s, br)
            ys, st_s = ret_sample(proj_s, c_norm[j], state_ret[j], n_heads)
            ret_p.append(st)
            ret_s.append(st_s)
            w_out = c_w_out
        else:
            br = d_w_out.shape[1]
            n_heads = br // GLA_DV
            proj, proj_s = matmul(hp, hs, jnp.swapaxes(d_w_in, 1, 2), j, transposed=True)
            yp, st = gla_prompt(proj.reshape(b, s, -1), d_gate_w[j], d_gate_b[j], d_norm[j], n_heads)
            yp = yp.reshape(b * s, br)
            ys, st_s = gla_sample(proj_s, d_gate_w[j], d_gate_b[j], d_norm[j], state_gla[j], n_heads)
            gla_p.append(st)
            gla_s.append(st_s)
            w_out = d_w_out
        op, os_ = matmul(yp, ys, w_out, j)
        g_next = norm_pre[i + 1] if i + 1 < depth else None
        xp, hp = residual_norm(xp, op, norm_post[i], g_next)
        xs, hs = residual_norm(xs, os_, norm_post[i], g_next)

    return (xp.reshape(b, s, d), xs.reshape(bs, 1, d),
            jnp.stack(kp_rows[0]), jnp.stack(ks_rows[0]), jnp.stack(vp_rows[0]), jnp.stack(vs_rows[0]),
            jnp.stack(kp_rows[1]), jnp.stack(ks_rows[1]), jnp.stack(vp_rows[1]), jnp.stack(vs_rows[1]),
            jnp.stack(kp_rows[2]), jnp.stack(ks_rows[2]), jnp.stack(vp_rows[2]), jnp.stack(vs_rows[2]),
            jnp.stack(lru_h_p), jnp.stack(lru_h_s), jnp.stack(lru_c_p), jnp.stack(lru_c_s),
            jnp.stack(ret_p), jnp.stack(ret_s), jnp.stack(gla_p), jnp.stack(gla_s))
```

```python
import functools

import numpy as np
import jax
import jax.numpy as jnp
from jax import lax
from jax.experimental import pallas as pl
from jax.experimental.pallas import tpu as pltpu

F32 = jnp.float32
BF16 = jnp.bfloat16

PAST_LEN = 8192
EPS = 1e-6
NEG_INF = -1e30
A_GROUPS = ((128, 1), (512, 4), (2048, 16))
A_HEAD_DIM = 128
A_BLOCK = 128
ATTN_UNROLL = 16
ATTN_MERGE_ROWS = 256
ATTN_SAMPLE_UNROLL = 16
NUM_BUCKETS = 32
MAX_DISTANCE = 2048
C_RG = 8.0
LRU_TIME_TILE = 512
RET_DK = 256
RET_DV = 512
RET_CHUNK = 128
ROPE_BASE = 10000.0
GLA_DK = 256
GLA_DV = 512
GLA_TAU = 16.0
GLA_CHUNK = 64
GLA_GROUP = 16
RET_GROUP = 8

LANES = 128
SUBLANES = 8
V7X_VMEM_LIMIT_BYTES = 56 * 1024 * 1024

_ARB = pltpu.ARBITRARY


def _params(n_grid):
    return pltpu.CompilerParams(dimension_semantics=(_ARB,) * n_grid,
                                vmem_limit_bytes=V7X_VMEM_LIMIT_BYTES)


def _dot(a, b):
    return jnp.dot(a, b, preferred_element_type=F32)


def _dot_nt(a, b):
    return lax.dot_general(a, b, (((1,), (1,)), ((), ())), preferred_element_type=F32)


def _dot_tn(a, b):
    return lax.dot_general(a, b, (((0,), (0,)), ((), ())), preferred_element_type=F32)


def _sigmoid(x):
    return 0.5 * jnp.tanh(0.5 * x) + 0.5


def _silu(x):
    return x * _sigmoid(x)


def _softplus(x):
    return jnp.maximum(x, 0.0) + jnp.log1p(jnp.exp(-jnp.abs(x)))


def _rms(x, g):
    return x * lax.rsqrt(jnp.mean(x * x, axis=-1, keepdims=True) + EPS) * g


def _chunk_group(n_chunks, want):
    return want if n_chunks % want == 0 else 1


def _col_bcast(row, n):
    return jnp.broadcast_to(row, (n, n)).T


def _norm_kernel(x_ref, g_ref, h_ref):
    h_ref[...] = _rms(x_ref[...], g_ref[...]).astype(BF16)


def _resnorm_kernel(x_ref, y_ref, gpost_ref, gnext_ref, xo_ref, h_ref):
    xn = x_ref[...] + _rms(y_ref[...], gpost_ref[...])
    xo_ref[...] = xn
    h_ref[...] = _rms(xn, gnext_ref[...]).astype(BF16)


def _res_kernel(x_ref, y_ref, gpost_ref, xo_ref):
    xo_ref[...] = x_ref[...] + _rms(y_ref[...], gpost_ref[...])


def _row_tile(m):
    return min(m, 256)


def norm_cast(x, g):
    m, d = x.shape
    tm = _row_tile(m)
    return pl.pallas_call(
        _norm_kernel,
        grid=(m // tm,),
        in_specs=[pl.BlockSpec((tm, d), lambda i: (i, 0)),
                  pl.BlockSpec((1, d), lambda i: (0, 0))],
        out_specs=pl.BlockSpec((tm, d), lambda i: (i, 0)),
        out_shape=jax.ShapeDtypeStruct((m, d), BF16),
        compiler_params=_params(1),
        name="norm_cast",
    )(x, g.reshape(1, d))


def residual_norm(x, y, g_post, g_next):
    m, d = x.shape
    tm = _row_tile(m)
    row = pl.BlockSpec((tm, d), lambda i: (i, 0))
    vec = pl.BlockSpec((1, d), lambda i: (0, 0))
    if g_next is None:
        return pl.pallas_call(
            _res_kernel, grid=(m // tm,),
            in_specs=[row, row, vec], out_specs=row,
            out_shape=jax.ShapeDtypeStruct((m, d), F32),
            compiler_params=_params(1), name="residual",
        )(x, y, g_post.reshape(1, d)), None
    return pl.pallas_call(
        _resnorm_kernel, grid=(m // tm,),
        in_specs=[row, row, vec, vec], out_specs=[row, row],
        out_shape=[jax.ShapeDtypeStruct((m, d), F32), jax.ShapeDtypeStruct((m, d), BF16)],
        compiler_params=_params(1), name="residual_norm",
    )(x, y, g_post.reshape(1, d), g_next.reshape(1, d))


def _mm_kernel(x_ref, w_ref, xs_ref, o_ref, os_ref, *, transposed, tn):
    w = w_ref[...].astype(BF16)
    dot = _dot_nt if transposed else _dot
    o_ref[...] = dot(x_ref[...], w)

    @pl.when(pl.program_id(0) == 0)
    def _():
        cols = pl.ds(pl.multiple_of(pl.program_id(1) * tn, tn), tn)
        os_ref[:, cols] = dot(xs_ref[...], w)


MM_COL_TILE = 512
MM_ROW_TILE = 2048
MM_ROW_TILE_LONG_K = 1024
MM_LONG_K = 4096


def matmul(x, xs, w, layer, transposed=False, unit=None):
    m, k = x.shape
    ms = xs.shape[0]
    n = w.shape[1] if transposed else w.shape[2]
    tn = MM_COL_TILE
    tm = min(m, MM_ROW_TILE if k <= MM_LONG_K else MM_ROW_TILE_LONG_K)
    assert m % tm == 0
    nj = pl.cdiv(n, tn)
    if transposed:
        w_spec = pl.BlockSpec((None, tn, k), lambda i, j: (layer, j, 0))
    else:
        w_spec = pl.BlockSpec((None, k, tn), lambda i, j: (layer, 0, j))
    if unit is None:
        o_spec = pl.BlockSpec((tm, tn), lambda i, j: (i, j))
        o_shape = jax.ShapeDtypeStruct((m, n), F32)
    else:
        assert n % unit == 0 and unit % tn == 0
        per = unit // tn
        o_spec = pl.BlockSpec((None, tm, tn), lambda i, j: (j // per, i, j % per))
        o_shape = jax.ShapeDtypeStruct((n // unit, m, unit), F32)
    out, out_s = pl.pallas_call(
        functools.partial(_mm_kernel, transposed=transposed, tn=tn),
        grid=(m // tm, nj),
        in_specs=[pl.BlockSpec((tm, k), lambda i, j: (i, 0), pipeline_mode=pl.Buffered(1)),
                  w_spec,
                  pl.BlockSpec((ms, k), lambda i, j: (0, 0))],
        out_specs=[o_spec, pl.BlockSpec((ms, nj * tn), lambda i, j: (0, 0))],
        out_shape=[o_shape, jax.ShapeDtypeStruct((ms, nj * tn), F32)],
        compiler_params=_params(2),
        name="proj_matmul",
    )(x, w, xs)
    return out, (out_s if nj * tn == n else out_s[:, :n])


def _t5_bucket(dist):
    n = np.asarray(dist, dtype=np.int64)
    max_exact = NUM_BUCKETS // 2
    ratio = np.log(np.maximum(n, 1) / max_exact) / np.log(MAX_DISTANCE / max_exact)
    large = np.minimum(max_exact + (ratio * (NUM_BUCKETS - max_exact)).astype(np.int64), NUM_BUCKETS - 1)
    return np.where(n < max_exact, n, large).astype(np.int32)


def _attn_prompt_kernel(*refs, dilations, seq, tails):
    ng = len(dilations)
    qkv = refs[:3 * ng]
    gate_ref, bias_ref, y_ref = refs[3 * ng:3 * ng + 3]
    tail_refs = refs[3 * ng + 3:-2]
    o_s, lse_s = refs[-2:]
    blk = A_BLOCK
    for n, (g, keep) in enumerate(tails):
        tail_refs[2 * n][...] = qkv[3 * g + 1][seq - keep:, :]
        tail_refs[2 * n + 1][...] = qkv[3 * g + 2][seq - keep:, :]
    scale = A_HEAD_DIM ** -0.5
    ii = lax.broadcasted_iota(jnp.int32, (blk, blk), 0)
    jj = lax.broadcasted_iota(jnp.int32, (blk, blk), 1)
    cur_ok = jj <= ii
    prev_ok = jj >= ii

    for g, dil in enumerate(dilations):
        q_ref, k_ref, v_ref = qkv[3 * g:3 * g + 3]
        nb = seq // (dil * blk)
        band = pltpu.roll(jnp.broadcast_to(bias_ref[g:g + 1, :], (blk, 2 * blk)), 0, axis=1,
                          stride=1, stride_axis=0)
        b_prev = band[:, :blk]
        b_cur = band[:, blk:]

        def rows_at(start, dil=dil):
            if dil > 1:
                return pl.ds(start, blk, stride=dil)
            return pl.ds(start if isinstance(start, int) else pl.multiple_of(start, blk), blk)

        def scores(idx, dil=dil, nb=nb, q_ref=q_ref, k_ref=k_ref, b_prev=b_prev, b_cur=b_cur,
                   rows_at=rows_at):
            r = idx % dil
            bi = idx // dil
            start = r + bi * (blk * dil)
            rows = rows_at(start)
            q = q_ref[rows, :].astype(BF16)
            s = [jnp.where(cur_ok, _dot_nt(q, k_ref[rows, :].astype(BF16)) * scale + b_cur, NEG_INF)]
            prows = None
            if nb > 1 and not (isinstance(bi, int) and bi == 0):
                if isinstance(bi, int):
                    prows = rows_at(start - blk * dil)
                    prev_mask = prev_ok
                else:
                    prows = rows_at(jnp.maximum(start - blk * dil, 0))
                    prev_mask = prev_ok & (bi > 0)
                s.append(jnp.where(prev_mask, _dot_nt(q, k_ref[prows, :].astype(BF16)) * scale + b_prev,
                                   NEG_INF))
            return rows, prows, s

        def softmax(s):
            m = jnp.max(functools.reduce(jnp.maximum, s), axis=-1, keepdims=True)
            p = [jnp.exp(x - m) for x in s]
            den = jnp.sum(functools.reduce(jnp.add, p), axis=-1, keepdims=True)
            return m, den, [x.astype(BF16) for x in p]

        def values(rows, prows, p, v_ref=v_ref):
            num = _dot(p[0], v_ref[rows, :].astype(BF16))
            if prows is not None:
                num = num + _dot(p[1], v_ref[prows, :].astype(BF16))
            return num

        def body(it, carry, g=g, scores=scores, softmax=softmax, values=values):
            blocks = [scores(it * ATTN_UNROLL + u) for u in range(ATTN_UNROLL)]
            probs = [softmax(s) for _, _, s in blocks]
            nums = [values(rows, prows, p) for (rows, prows, _), (_, _, p) in zip(blocks, probs)]
            for (rows, _, _), (m, den, _), num in zip(blocks, probs, nums):
                o_s[g, rows, :] = num / den
                lse_s[g, rows, :] = jnp.broadcast_to(m + jnp.log(den), (blk, A_HEAD_DIM))
            return carry

        if seq // blk == ATTN_UNROLL:
            body(0, 0)
        else:
            lax.fori_loop(0, seq // (blk * ATTN_UNROLL), body, 0)

    tr = ATTN_MERGE_ROWS

    def merge(c, carry):
        rows = pl.ds(pl.multiple_of(c * tr, tr), tr)
        lses = [lse_s[g, rows, :] for g in range(ng)]
        mx = functools.reduce(jnp.maximum, lses)
        num = jnp.zeros((tr, A_HEAD_DIM), F32)
        den = jnp.zeros((tr, A_HEAD_DIM), F32)
        for g in range(ng):
            w = jnp.exp(lses[g] - mx)
            num = num + w * o_s[g, rows, :]
            den = den + w
        y_ref[rows, :] = ((num / den) * _silu(gate_ref[rows, :])).astype(BF16)
        return carry

    lax.fori_loop(0, seq // tr, merge, 0)


def attn_prompt(units, bias_tab, n_heads):
    _, b, s, _ = units.shape
    ng = len(A_GROUPS)
    dh = A_HEAD_DIM
    in_specs = []
    args = []
    order = [which * ng + g for g in range(ng) for which in range(3)] + [3 * ng]
    for unit in order:
        in_specs.append(pl.BlockSpec((None, None, s, dh), lambda bi, hi, unit=unit: (unit, bi, 0, hi)))
        args.append(units)
    in_specs.append(pl.BlockSpec((None, ng, 2 * A_BLOCK), lambda bi, hi: (hi, 0, 0)))
    args.append(bias_tab)
    tails = tuple((g, window) for g, (window, _) in enumerate(A_GROUPS) if window < s)
    kern = functools.partial(_attn_prompt_kernel, dilations=tuple(d for _, d in A_GROUPS), seq=s,
                             tails=tails)
    out_specs = [pl.BlockSpec((None, s, dh), lambda bi, hi: (bi, 0, hi))]
    out_shape = [jax.ShapeDtypeStruct((b, s, n_heads * dh), BF16)]
    for _, keep in tails:
        out_specs += [pl.BlockSpec((None, keep, dh), lambda bi, hi: (bi, 0, hi))] * 2
        out_shape += [jax.ShapeDtypeStruct((b, keep, n_heads * dh), F32)] * 2
    return pl.pallas_call(
        kern,
        grid=(b, n_heads),
        in_specs=in_specs,
        out_specs=out_specs,
        out_shape=out_shape,
        scratch_shapes=[pltpu.VMEM((ng, s, dh), F32)] * 2,
        compiler_params=_params(2),
        name="attn_prompt",
    )(*args)


def _attn_sample_kernel(q_ref, kn_ref, vn_ref, gate_ref, *refs, n_heads, n_keys):
    ng = len(A_GROUPS)
    caches = refs[:2 * ng]
    bpast_ref, bself_ref, y_ref = refs[2 * ng:]
    dh = A_HEAD_DIM
    scale = dh ** -0.5
    lane = lax.broadcasted_iota(jnp.int32, (n_heads, n_keys), 1)
    parts = []
    for g in range(ng):
        kc_ref, vc_ref = caches[2 * g], caches[2 * g + 1]
        q = q_ref[g]
        vn = vn_ref[g]

        def logits(j, acc, kc_ref=kc_ref, q=q):
            col = jnp.sum(kc_ref[j] * q, axis=-1, keepdims=True)
            return jnp.where(lane == j, col, acc)

        lp = lax.fori_loop(0, n_keys, logits, jnp.zeros((n_heads, n_keys), F32), unroll=ATTN_SAMPLE_UNROLL)
        lp = lp * scale + bpast_ref[g]
        ls = jnp.sum(kn_ref[g] * q, axis=-1, keepdims=True) * scale + bself_ref[g]
        m = jnp.maximum(jnp.max(lp, axis=-1, keepdims=True), ls)
        p = jnp.exp(lp - m)
        ps = jnp.exp(ls - m)
        den = jnp.sum(p, axis=-1, keepdims=True) + ps

        def weighted(j, acc, vc_ref=vc_ref, p=p):
            pj = jnp.sum(jnp.where(lane == j, p, 0.0), axis=-1, keepdims=True)
            return acc + pj * vc_ref[j]

        num = lax.fori_loop(0, n_keys, weighted, ps * vn, unroll=ATTN_SAMPLE_UNROLL)
        parts.append((num, m, den))
    mx = functools.reduce(jnp.maximum, [p_[1] for p_ in parts])
    num = jnp.zeros((n_heads, dh), F32)
    den = jnp.zeros((n_heads, 1), F32)
    for num_g, m_g, den_g in parts:
        w = jnp.exp(m_g - mx)
        num = num + w * num_g
        den = den + w * den_g
    y_ref[...] = ((num / den) * _silu(gate_ref[...])).astype(BF16)


def attn_sample(qs, ks, vs, gate, k_caches, v_caches, bias_past, bias_self, n_heads):
    bs, width = gate.shape
    dh = A_HEAD_DIM
    ng = len(A_GROUPS)
    n_keys = A_BLOCK
    heads = lambda rows: jnp.stack(rows, axis=1).reshape(bs, ng, n_heads, dh)
    grp_spec = pl.BlockSpec((None, ng, n_heads, dh), lambda bi: (bi, 0, 0, 0))
    args = [heads(qs), heads(ks), heads(vs), gate.reshape(bs, n_heads, dh)]
    in_specs = [grp_spec, grp_spec, grp_spec, pl.BlockSpec((None, n_heads, dh), lambda bi: (bi, 0, 0))]
    for (_, dil), kc, vc in zip(A_GROUPS, k_caches, v_caches):
        for c in (kc, vc):
            args.append(c.reshape(bs, n_keys, dil, n_heads, dh))
            in_specs.append(pl.BlockSpec((None, n_keys, None, n_heads, dh), lambda bi: (bi, 0, 0, 0, 0)))
    args += [bias_past, bias_self]
    in_specs += [pl.BlockSpec(bias_past.shape, lambda bi: (0, 0, 0)),
                 pl.BlockSpec(bias_self.shape, lambda bi: (0, 0, 0))]
    y = pl.pallas_call(
        functools.partial(_attn_sample_kernel, n_heads=n_heads, n_keys=n_keys),
        grid=(bs,),
        in_specs=in_specs,
        out_specs=pl.BlockSpec((None, n_heads, dh), lambda bi: (bi, 0, 0)),
        out_shape=jax.ShapeDtypeStruct((bs, n_heads, dh), BF16),
        compiler_params=_params(1),
        name="attn_sample",
    )(*args)
    return y.reshape(bs, width)


def _lru_gates(xc, wa_ref, ba_ref, wx_ref, bx_ref, lam_ref):
    xb = xc.astype(BF16)
    r = _sigmoid(_dot(xb, wa_ref[...].astype(BF16)) + ba_ref[...])
    ig = _sigmoid(_dot(xb, wx_ref[...].astype(BF16)) + bx_ref[...])
    log_a = -C_RG * r * _softplus(-lam_ref[...])
    a = jnp.exp(log_a)
    mult = jnp.sqrt(-jnp.tanh(log_a) * (a * a + 1.0))
    return a, mult, ig


def _lru_prompt_kernel(x_ref, gate_ref, cw_ref, cb_ref, wa_ref, ba_ref, wx_ref, bx_ref, lam_ref,
                       y_ref, hl_ref, hc_ref, xp_ref, *, tt, conv_w):
    t = pl.program_id(2)

    @pl.when(t == 0)
    def _():
        hc_ref[...] = jnp.zeros_like(hc_ref)
        xp_ref[:SUBLANES, :] = jnp.zeros((SUBLANES, xp_ref.shape[1]), F32)

    x = x_ref[...]
    c = x.shape[1]
    xp_ref[SUBLANES:, :] = x
    xc = cb_ref[...] + cw_ref[conv_w - 1:conv_w, :] * x
    for k in range(1, conv_w):
        xc = xc + cw_ref[conv_w - 1 - k:conv_w - k, :] * xp_ref[SUBLANES - k:SUBLANES - k + tt, :]
    xp_ref[:SUBLANES, :] = x[tt - SUBLANES:]

    row = lax.broadcasted_iota(jnp.int32, (tt, c), 0)
    a, mult, ig = _lru_gates(xc, wa_ref, ba_ref, wx_ref, bx_ref, lam_ref)
    mult = jnp.where(row + t * tt == 0, 1.0, mult)
    bx = mult * ig * xc
    ng = tt // SUBLANES
    a = a.reshape(ng, SUBLANES, c)
    bx = bx.reshape(ng, SUBLANES, c)
    sub = lax.broadcasted_iota(jnp.int32, (ng, SUBLANES, c), 1)
    d = 1
    while d < SUBLANES:
        keep = sub >= d
        a_sh = jnp.where(keep, pltpu.roll(a, d, axis=1), 1.0)
        b_sh = jnp.where(keep, pltpu.roll(bx, d, axis=1), 0.0)
        bx = a * b_sh + bx
        a = a * a_sh
        d *= 2
    h = hc_ref[...]
    groups = []
    for g in range(ng):
        groups.append(a[g] * h + bx[g])
        h = groups[-1][SUBLANES - 1:]
    y_ref[...] = (jnp.concatenate(groups, axis=0) * _silu(gate_ref[...])).astype(BF16)
    hc_ref[...] = h
    hl_ref[...] = h


def lru_prompt(proj, conv_w, conv_b, wa, ba, wx, bx, lam):
    b, s, two_br = proj.shape
    br = two_br // 2
    nblk, bs_, _ = wa.shape
    cw = conv_w.shape[0]
    tt = min(s, LRU_TIME_TILE)
    vec = lambda a: a.reshape(1, br)
    vspec = pl.BlockSpec((1, bs_), lambda bi, ni, ti: (0, ni))
    wspec = pl.BlockSpec((None, bs_, bs_), lambda bi, ni, ti: (ni, 0, 0))
    y, hl = pl.pallas_call(
        functools.partial(_lru_prompt_kernel, tt=tt, conv_w=cw),
        grid=(b, nblk, s // tt),
        in_specs=[pl.BlockSpec((None, tt, bs_), lambda bi, ni, ti: (bi, ti, ni)),
                  pl.BlockSpec((None, tt, bs_), lambda bi, ni, ti: (bi, ti, nblk + ni)),
                  pl.BlockSpec((cw, bs_), lambda bi, ni, ti: (0, ni)),
                  vspec, wspec, vspec, wspec, vspec, vspec],
        out_specs=[pl.BlockSpec((None, tt, bs_), lambda bi, ni, ti: (bi, ti, ni)),
                   pl.BlockSpec((None, 1, bs_), lambda bi, ni, ti: (bi, 0, ni))],
        out_shape=[jax.ShapeDtypeStruct((b, s, br), BF16), jax.ShapeDtypeStruct((b, 1, br), F32)],
        scratch_shapes=[pltpu.VMEM((1, bs_), F32), pltpu.VMEM((SUBLANES + tt, bs_), F32)],
        compiler_params=_params(3),
        name="lru_prompt",
    )(proj, proj, conv_w, vec(conv_b), wa, vec(ba), wx, vec(bx), vec(lam))
    return y, hl.reshape(b, br)


def _lru_sample_kernel(x_ref, gate_ref, buf_ref, h0_ref, cw_ref, cb_ref, wa_ref, ba_ref, wx_ref, bx_ref,
                       lam_ref, y_ref, h_ref, *, conv_w):
    x = x_ref[...]
    xc = cb_ref[...] + cw_ref[conv_w - 1:conv_w, :] * x
    for w in range(conv_w - 1):
        xc = xc + cw_ref[w:w + 1, :] * buf_ref[:, w, :]
    a, mult, ig = _lru_gates(xc, wa_ref, ba_ref, wx_ref, bx_ref, lam_ref)
    h = a * h0_ref[...] + mult * ig * xc
    h_ref[...] = h
    y_ref[...] = (h * _silu(gate_ref[...])).astype(BF16)


def lru_sample(proj_s, conv_buf, h0, conv_w, conv_b, wa, ba, wx, bx, lam):
    bs, two_br = proj_s.shape
    br = two_br // 2
    nblk, bs_, _ = wa.shape
    cw = conv_w.shape[0]
    vec = lambda a: a.reshape(1, br)
    vspec = pl.BlockSpec((1, bs_), lambda ni: (0, ni))
    wspec = pl.BlockSpec((None, bs_, bs_), lambda ni: (ni, 0, 0))
    rspec = pl.BlockSpec((bs, bs_), lambda ni: (0, ni))
    return pl.pallas_call(
        functools.partial(_lru_sample_kernel, conv_w=cw),
        grid=(nblk,),
        in_specs=[rspec,
                  pl.BlockSpec((bs, bs_), lambda ni: (0, nblk + ni)),
                  pl.BlockSpec((bs, cw - 1, bs_), lambda ni: (0, 0, ni)),
                  rspec,
                  pl.BlockSpec((cw, bs_), lambda ni: (0, ni)),
                  vspec, wspec, vspec, wspec, vspec, vspec],
        out_specs=[rspec, rspec],
        out_shape=[jax.ShapeDtypeStruct((bs, br), BF16), jax.ShapeDtypeStruct((bs, br), F32)],
        compiler_params=_params(1),
        name="lru_sample",
    )(proj_s, proj_s, conv_buf, h0, conv_w, vec(conv_b), wa, vec(ba), wx, vec(bx), vec(lam))


def _rope(x, cos, sin):
    half = x.shape[-1] // 2
    x1, x2 = x[:, :half], x[:, half:]
    return jnp.concatenate([x1 * cos - x2 * sin, x1 * sin + x2 * cos], axis=-1)


def _groupnorm(o, gain):
    c = o - jnp.mean(o, axis=-1, keepdims=True)
    return c * lax.rsqrt(jnp.mean(c * c, axis=-1, keepdims=True) + EPS) * gain


def _ret_prompt_kernel(q_ref, k_ref, v_ref, gate_ref, cos_ref, sin_ref, lg_ref, gain_ref,
                       y_ref, s_ref, st_ref, *, chunk, n_chunks, group):
    lg = lg_ref[:, :1]
    idx = lax.broadcasted_iota(jnp.int32, (chunk, 1), 0).astype(F32)
    ii = lax.broadcasted_iota(jnp.int32, (chunk, chunk), 0)
    jj = lax.broadcasted_iota(jnp.int32, (chunk, chunk), 1)
    diff = (ii - jj).astype(F32)
    decay = jnp.where(diff >= 0, jnp.exp(diff * lg), 0.0)
    q_dec = jnp.exp((idx + 1.0) * lg)
    k_dec = jnp.exp((chunk - 1.0 - idx) * lg)
    chunk_dec = jnp.exp(chunk * lg)
    gain = gain_ref[...]
    st_ref[...] = jnp.zeros_like(st_ref)

    q_dec_rows = jnp.concatenate([q_dec] * group, axis=0)
    k_dec_rows = jnp.concatenate([k_dec] * group, axis=0)

    def body(it, carry):
        rows = pl.ds(pl.multiple_of(it * (group * chunk), group * chunk), group * chunk)
        cos = cos_ref[rows, :]
        sin = sin_ref[rows, :]
        qc = _rope(q_ref[rows, :], cos, sin)
        kc = _rope(k_ref[rows, :], cos, sin) * (RET_DK ** -0.5)
        vc = v_ref[rows, :].astype(BF16)
        qb = qc.astype(BF16)
        kb = kc.astype(BF16)
        qd = (qc * q_dec_rows).astype(BF16)
        kd = (kc * k_dec_rows).astype(BF16)
        sl = [slice(u * chunk, (u + 1) * chunk) for u in range(group)]
        intra, upd = [], []
        for u in range(group):
            scores = _dot_nt(qb[sl[u]], kb[sl[u]]) * decay
            intra.append(_dot(scores.astype(BF16), vc[sl[u]]))
            upd.append(_dot_tn(kd[sl[u]], vc[sl[u]]))
        st = st_ref[...]
        outs = []
        for u in range(group):
            outs.append(intra[u] + _dot(qd[sl[u]], st.astype(BF16)))
            st = chunk_dec * st + upd[u]
        st_ref[...] = st
        o = jnp.concatenate(outs, axis=0)
        y_ref[rows, :] = (_groupnorm(o, gain) * _silu(gate_ref[rows, :])).astype(BF16)
        return carry

    lax.fori_loop(0, n_chunks // group, body, 0)
    s_ref[...] = st_ref[...]


def _ret_log_gamma(n_heads):
    lg = np.log1p(-np.exp2(-5.0 - np.arange(n_heads, dtype=np.float32))).astype(np.float32)
    return jnp.asarray(np.broadcast_to(lg[:, None, None], (n_heads, 1, LANES)).copy())


def _rope_tables(pos):
    half = RET_DK // 2
    inv_freq = ROPE_BASE ** (-jnp.arange(half, dtype=F32) / half)
    ang = pos[:, None] * inv_freq[None, :]
    return jnp.cos(ang), jnp.sin(ang)


def ret_prompt(proj, gain, n_heads):
    b, s, _ = proj.shape
    dk, dv = RET_DK, RET_DV
    qk = n_heads * dk
    chunk = RET_CHUNK if s % RET_CHUNK == 0 else s
    cos, sin = _rope_tables(jnp.arange(s, dtype=F32))
    half = dk // 2
    koff = qk // dk
    voff = 2 * qk // dv
    goff = (2 * qk + n_heads * dv) // dv
    return pl.pallas_call(
        functools.partial(_ret_prompt_kernel, chunk=chunk, n_chunks=s // chunk,
                          group=_chunk_group(s // chunk, RET_GROUP)),
        grid=(b, n_heads),
        in_specs=[pl.BlockSpec((None, s, dk), lambda bi, hi: (bi, 0, hi)),
                  pl.BlockSpec((None, s, dk), lambda bi, hi: (bi, 0, koff + hi)),
                  pl.BlockSpec((None, s, dv), lambda bi, hi: (bi, 0, voff + hi)),
                  pl.BlockSpec((None, s, dv), lambda bi, hi: (bi, 0, goff + hi)),
                  pl.BlockSpec((s, half), lambda bi, hi: (0, 0)),
                  pl.BlockSpec((s, half), lambda bi, hi: (0, 0)),
                  pl.BlockSpec((None, 1, LANES), lambda bi, hi: (hi, 0, 0)),
                  pl.BlockSpec((1, dv), lambda bi, hi: (0, hi))],
        out_specs=[pl.BlockSpec((None, s, dv), lambda bi, hi: (bi, 0, hi)),
                   pl.BlockSpec((None, None, dk, dv), lambda bi, hi: (bi, hi, 0, 0))],
        out_shape=[jax.ShapeDtypeStruct((b, s, n_heads * dv), BF16),
                   jax.ShapeDtypeStruct((b, n_heads, dk, dv), F32)],
        scratch_shapes=[pltpu.VMEM((dk, dv), F32)],
        compiler_params=_params(2),
        name="ret_prompt",
    )(proj, proj, proj, proj, cos, sin, _ret_log_gamma(n_heads), gain.reshape(1, n_heads * dv))


def _ret_sample_kernel(q_ref, k_ref, v_ref, gate_ref, cos_ref, sin_ref, lg_ref, gain_ref, s0_ref,
                       y_ref, s_ref):
    dk, dv = RET_DK, RET_DV
    bs = q_ref.shape[0]
    gamma = jnp.exp(lg_ref[:, :1])
    cos = cos_ref[...]
    sin = sin_ref[...]
    q = _rope(q_ref[...], cos, sin)
    k = _rope(k_ref[...], cos, sin) * (dk ** -0.5)
    v = v_ref[...]
    o = jnp.sum(q * k, axis=-1, keepdims=True) * v
    qg = (q * gamma).astype(BF16)
    row = lax.broadcasted_iota(jnp.int32, (bs, dv), 0)
    for b in range(bs):
        s0 = s0_ref[b]
        o = o + jnp.where(row == b, _dot(qg, s0.astype(BF16)), 0.0)
        kcol = _col_bcast(k[b:b + 1], dk)
        for j in range(dv // dk):
            cols = slice(j * dk, (j + 1) * dk)
            s_ref[b, :, cols] = gamma * s0[:, cols] + kcol * v[b:b + 1, cols]
    y_ref[...] = (_groupnorm(o, gain_ref[...]) * _silu(gate_ref[...])).astype(BF16)


def ret_sample(proj_s, gain, s0, n_heads):
    bs = proj_s.shape[0]
    dk, dv = RET_DK, RET_DV
    qk = n_heads * dk
    cos, sin = _rope_tables(jnp.full((1,), PAST_LEN, F32))
    half = dk // 2
    koff = qk // dk
    voff = 2 * qk // dv
    goff = (2 * qk + n_heads * dv) // dv
    state_spec = pl.BlockSpec((bs, None, dk, dv), lambda hi: (0, hi, 0, 0))
    return pl.pallas_call(
        _ret_sample_kernel,
        grid=(n_heads,),
        in_specs=[pl.BlockSpec((bs, dk), lambda hi: (0, hi)),
                  pl.BlockSpec((bs, dk), lambda hi: (0, koff + hi)),
                  pl.BlockSpec((bs, dv), lambda hi: (0, voff + hi)),
                  pl.BlockSpec((bs, dv), lambda hi: (0, goff + hi)),
                  pl.BlockSpec((1, half), lambda hi: (0, 0)),
                  pl.BlockSpec((1, half), lambda hi: (0, 0)),
                  pl.BlockSpec((None, 1, LANES), lambda hi: (hi, 0, 0)),
                  pl.BlockSpec((1, dv), lambda hi: (0, hi)),
                  state_spec],
        out_specs=[pl.BlockSpec((bs, dv), lambda hi: (0, hi)), state_spec],
        out_shape=[jax.ShapeDtypeStruct((bs, n_heads * dv), BF16),
                   jax.ShapeDtypeStruct((bs, n_heads, dk, dv), F32)],
        compiler_params=_params(1),
        name="ret_sample",
    )(proj_s, proj_s, proj_s, proj_s, cos, sin, _ret_log_gamma(n_heads), gain.reshape(1, n_heads * dv), s0)


def _headnorm(o, gain):
    return o * lax.rsqrt(jnp.mean(o * o, axis=-1, keepdims=True) + EPS) * gain


def _gla_log_alpha(low_ref_val, gw_ref, gb_ref, rank):
    lane = lax.broadcasted_iota(jnp.int32, low_ref_val.shape, 1)
    low = jnp.where(lane < rank, low_ref_val, 0.0).astype(BF16)
    z = _dot(low, gw_ref[...].astype(BF16)) + gb_ref[...]
    return (jnp.minimum(z, 0.0) - jnp.log1p(jnp.exp(-jnp.abs(z)))) / GLA_TAU


def _gla_prompt_kernel(q_ref, k_ref, v_ref, gate_ref, low_ref, gw_ref, gb_ref, gain_ref,
                       y_ref, s_ref, st_ref, *, chunk, n_chunks, group, rank):
    dk, dv = GLA_DK, GLA_DV
    sub = lax.broadcasted_iota(jnp.int32, (group * chunk, dk), 0) % chunk
    ii = lax.broadcasted_iota(jnp.int32, (chunk, chunk), 0)
    jj = lax.broadcasted_iota(jnp.int32, (chunk, chunk), 1)
    causal = jj <= ii
    gain = gain_ref[...]
    st_ref[...] = jnp.zeros_like(st_ref)

    def body(it, carry):
        rows = pl.ds(pl.multiple_of(it * (group * chunk), group * chunk), group * chunk)
        bcum = _gla_log_alpha(low_ref[rows, :], gw_ref, gb_ref, rank)
        d = 1
        while d < chunk:
            bcum = bcum + jnp.where(sub >= d, pltpu.roll(bcum, d, axis=0), 0.0)
            d *= 2
        sl = [slice(u * chunk, (u + 1) * chunk) for u in range(group)]
        blast = [bcum[(u + 1) * chunk - 1:(u + 1) * chunk] for u in range(group)]
        blast_rows = jnp.concatenate([jnp.broadcast_to(b, (chunk, dk)) for b in blast], axis=0)
        kc = k_ref[rows, :]
        vc = v_ref[rows, :].astype(BF16)
        qe = (q_ref[rows, :] * (dk ** -0.5) * jnp.exp(bcum)).astype(BF16)
        ke = (kc * jnp.exp(-bcum)).astype(BF16)
        kd = (kc * jnp.exp(blast_rows - bcum)).astype(BF16)
        intra, upd, dec = [], [], []
        for u in range(group):
            scores = jnp.where(causal, _dot_nt(qe[sl[u]], ke[sl[u]]), 0.0)
            intra.append(_dot(scores.astype(BF16), vc[sl[u]]))
            upd.append(_dot_tn(kd[sl[u]], vc[sl[u]]))
            dec.append(_col_bcast(jnp.exp(blast[u]), dk))
        st = st_ref[...]
        outs = []
        for u in range(group):
            outs.append(intra[u] + _dot(qe[sl[u]], st.astype(BF16)))
            st = jnp.concatenate([dec[u] * st[:, j * dk:(j + 1) * dk] + upd[u][:, j * dk:(j + 1) * dk]
                                  for j in range(dv // dk)], axis=1)
        st_ref[...] = st
        o = jnp.concatenate(outs, axis=0)
        y_ref[rows, :] = (_headnorm(o, gain) * _silu(gate_ref[rows, :])).astype(BF16)
        return carry

    lax.fori_loop(0, n_chunks // group, body, 0)
    s_ref[...] = st_ref[...]


def _gla_offsets(n_heads):
    dk, dv = GLA_DK, GLA_DV
    qk = n_heads * dk
    koff = qk // dk
    voff = 2 * qk // dv
    goff = (2 * qk + n_heads * dv) // dv
    loff = (2 * qk + 2 * n_heads * dv) // LANES
    return koff, voff, goff, loff


def _pad_rank(gate_w):
    rank = gate_w.shape[0]
    return jnp.pad(gate_w, ((0, LANES - rank), (0, 0)))


def gla_prompt(proj, gate_w, gate_b, gain, n_heads):
    b, s, _ = proj.shape
    dk, dv = GLA_DK, GLA_DV
    rank = gate_w.shape[0]
    chunk = GLA_CHUNK if s % GLA_CHUNK == 0 else s
    koff, voff, goff, loff = _gla_offsets(n_heads)
    return pl.pallas_call(
        functools.partial(_gla_prompt_kernel, chunk=chunk, n_chunks=s // chunk,
                          group=_chunk_group(s // chunk, GLA_GROUP), rank=rank),
        grid=(b, n_heads),
        in_specs=[pl.BlockSpec((None, s, dk), lambda bi, hi: (bi, 0, hi)),
                  pl.BlockSpec((None, s, dk), lambda bi, hi: (bi, 0, koff + hi)),
                  pl.BlockSpec((None, s, dv), lambda bi, hi: (bi, 0, voff + hi)),
                  pl.BlockSpec((None, s, dv), lambda bi, hi: (bi, 0, goff + hi)),
                  pl.BlockSpec((None, s, LANES), lambda bi, hi: (bi, 0, loff)),
                  pl.BlockSpec((LANES, dk), lambda bi, hi: (0, hi)),
                  pl.BlockSpec((1, dk), lambda bi, hi: (0, hi)),
                  pl.BlockSpec((1, dv), lambda bi, hi: (0, hi))],
        out_specs=[pl.BlockSpec((None, s, dv), lambda bi, hi: (bi, 0, hi)),
                   pl.BlockSpec((None, None, dk, dv), lambda bi, hi: (bi, hi, 0, 0))],
        out_shape=[jax.ShapeDtypeStruct((b, s, n_heads * dv), BF16),
                   jax.ShapeDtypeStruct((b, n_heads, dk, dv), F32)],
        scratch_shapes=[pltpu.VMEM((dk, dv), F32)],
        compiler_params=_params(2),
        name="gla_prompt",
    )(proj, proj, proj, proj, proj, _pad_rank(gate_w), gate_b.reshape(1, -1), gain.reshape(1, -1))


def _gla_sample_kernel(q_ref, k_ref, v_ref, gate_ref, low_ref, gw_ref, gb_ref, gain_ref, s0_ref,
                       y_ref, s_ref, *, rank):
    dk, dv = GLA_DK, GLA_DV
    bs = q_ref.shape[0]
    g = _gla_log_alpha(low_ref[...], gw_ref, gb_ref, rank)
    q = q_ref[...] * (dk ** -0.5)
    k = k_ref[...]
    v = v_ref[...]
    qe = q * jnp.exp(g)
    ke = k * jnp.exp(-g)
    kd = k * jnp.exp(g - g)
    eg = jnp.exp(g)
    o = jnp.sum(qe * ke, axis=-1, keepdims=True) * v
    qeb = qe.astype(BF16)
    row = lax.broadcasted_iota(jnp.int32, (bs, dv), 0)
    for b in range(bs):
        s0 = s0_ref[b]
        o = o + jnp.where(row == b, _dot(qeb, s0.astype(BF16)), 0.0)
        dec = _col_bcast(eg[b:b + 1], dk)
        kcol = _col_bcast(kd[b:b + 1], dk)
        for j in range(dv // dk):
            cols = slice(j * dk, (j + 1) * dk)
            s_ref[b, :, cols] = dec * s0[:, cols] + kcol * v[b:b + 1, cols]
    y_ref[...] = (_headnorm(o, gain_ref[...]) * _silu(gate_ref[...])).astype(BF16)


def gla_sample(proj_s, gate_w, gate_b, gain, s0, n_heads):
    bs = proj_s.shape[0]
    dk, dv = GLA_DK, GLA_DV
    rank = gate_w.shape[0]
    koff, voff, goff, loff = _gla_offsets(n_heads)
    state_spec = pl.BlockSpec((bs, None, dk, dv), lambda hi: (0, hi, 0, 0))
    return pl.pallas_call(
        functools.partial(_gla_sample_kernel, rank=rank),
        grid=(n_heads,),
        in_specs=[pl.BlockSpec((bs, dk), lambda hi: (0, hi)),
                  pl.BlockSpec((bs, dk), lambda hi: (0, koff + hi)),
                  pl.BlockSpec((bs, dv), lambda hi: (0, voff + hi)),
                  pl.BlockSpec((bs, dv), lambda hi: (0, goff + hi)),
                  pl.BlockSpec((bs, LANES), lambda hi: (0, loff)),
                  pl.BlockSpec((LANES, dk), lambda hi: (0, hi)),
                  pl.BlockSpec((1, dk), lambda hi: (0, hi)),
                  pl.BlockSpec((1, dv), lambda hi: (0, hi)),
                  state_spec],
        out_specs=[pl.BlockSpec((bs, dv), lambda hi: (0, hi)), state_spec],
        out_shape=[jax.ShapeDtypeStruct((bs, n_heads * dv), BF16),
                   jax.ShapeDtypeStruct((bs, n_heads, dk, dv), F32)],
        compiler_params=_params(1),
        name="gla_sample",
    )(proj_s, proj_s, proj_s, proj_s, proj_s, _pad_rank(gate_w), gate_b.reshape(1, -1),
      gain.reshape(1, -1), s0)


def _attention_bias_tables(rel_bias, n_heads):
    blk = A_BLOCK
    ng = len(A_GROUPS)
    onehot = np.zeros((ng, 2 * blk, NUM_BUCKETS), np.float32)
    for g, (window, dil) in enumerate(A_GROUPS):
        n_keys = window // dil
        assert n_keys == blk, "one 128-key band per dilated stream is assumed"
        u = np.arange(n_keys + 1)
        onehot[g, u, _t5_bucket(dil * (n_keys - u))] = 1.0
    tab = jnp.einsum("gub,bgh->guh", jnp.asarray(onehot), rel_bias.astype(F32).reshape(NUM_BUCKETS, ng, n_heads),
                     precision=lax.Precision.HIGHEST)
    tab_t = tab.transpose(0, 2, 1)
    return tab.transpose(2, 0, 1), tab_t[:, :, :blk], tab_t[:, :, blk:blk + 1]


def kernel(x_prompt, x_sample, cache_k_w128, cache_v_w128, cache_k_w512, cache_v_w512, cache_k_w2048, cache_v_w2048, state_lru_h, state_lru_conv, state_ret, state_gla, norm_pre, norm_post, rel_bias, a_w_in, a_w_out, b_w_in, b_conv_w, b_conv_b, b_gate_a_w, b_gate_a_b, b_gate_x_w, b_gate_x_b, b_lambda, b_w_out, c_w_in, c_norm, c_w_out, d_w_in, d_gate_w, d_gate_b, d_norm, d_w_out):
    b, s, d = x_prompt.shape
    bs = x_sample.shape[0]
    assert x_sample.shape[1] == 1, "one new token per sequence"
    depth = norm_pre.shape[0]
    k_caches = (cache_k_w128, cache_k_w512, cache_k_w2048)
    v_caches = (cache_v_w128, cache_v_w512, cache_v_w2048)
    ng = len(A_GROUPS)
    for (window, dil), kc in zip(A_GROUPS, k_caches):
        assert kc.shape[2] == window and s % (dil * A_BLOCK) == 0

    xp = x_prompt.reshape(b * s, d)
    xs = x_sample.reshape(bs, d)
    hp = norm_cast(xp, norm_pre[0])
    hs = norm_cast(xs, norm_pre[0])

    kp_rows = [[] for _ in A_GROUPS]
    vp_rows = [[] for _ in A_GROUPS]
    ks_rows = [[] for _ in A_GROUPS]
    vs_rows = [[] for _ in A_GROUPS]
    lru_h_p, lru_h_s, lru_c_p, lru_c_s = [], [], [], []
    ret_p, ret_s, gla_p, gla_s = [], [], [], []

    for i in range(depth):
        kind, j = i % 4, i // 4
        if kind == 0:
            n_heads = a_w_out.shape[1] // A_HEAD_DIM
            width = n_heads * A_HEAD_DIM
            units, proj_s = matmul(hp, hs, a_w_in, j, unit=width)
            units = units.reshape(-1, b, s, width)
            unit_s = [proj_s[:, u * width:(u + 1) * width] for u in range(3 * ng + 1)]
            bias_tab, bias_past, bias_self = _attention_bias_tables(rel_bias, n_heads)
            yp, *tails = attn_prompt(units, bias_tab, n_heads)
            yp = yp.reshape(b * s, width)
            ys = attn_sample(unit_s[:ng], unit_s[ng:2 * ng], unit_s[2 * ng:3 * ng], unit_s[3 * ng],
                             [c[j] for c in k_caches], [c[j] for c in v_caches],
                             bias_past, bias_self, n_heads)
            for g, (window, _) in enumerate(A_GROUPS):
                if window >= s:
                    k_rows, v_rows = units[ng + g], units[2 * ng + g]
                else:
                    k_rows, v_rows = tails.pop(0), tails.pop(0)
                kp_rows[g].append(k_rows.reshape(b, -1, n_heads, A_HEAD_DIM))
                vp_rows[g].append(v_rows.reshape(b, -1, n_heads, A_HEAD_DIM))
                ks_rows[g].append(unit_s[ng + g].reshape(bs, 1, n_heads, A_HEAD_DIM))
                vs_rows[g].append(unit_s[2 * ng + g].reshape(bs, 1, n_heads, A_HEAD_DIM))
            w_out = a_w_out
        elif kind == 1:
            br = b_w_out.shape[1]
            proj, proj_s = matmul(hp, hs, b_w_in, j)
            prm = (b_conv_w[j], b_conv_b[j], b_gate_a_w[j], b_gate_a_b[j], b_gate_x_w[j], b_gate_x_b[j], b_lambda[j])
            proj3 = proj.reshape(b, s, -1)
            yp, h_last = lru_prompt(proj3, *prm)
            yp = yp.reshape(b * s, br)
            ys, h_new = lru_sample(proj_s, state_lru_conv[j], state_lru_h[j], *prm)
            cw = b_conv_w.shape[1]
            lru_h_p.append(h_last)
            lru_c_p.append(proj3[:, s - (cw - 1):, :br])
            lru_h_s.append(h_new)
            lru_c_s.append(jnp.concatenate([state_lru_conv[j], proj_s[:, None, :br]], axis=1)[:, 1:])
            w_out = b_w_out
        elif kind == 2:
            br = c_w_out.shape[1]
            n_heads = br // RET_DV
            proj, proj_s = matmul(hp, hs, c_w_in, j)
            yp, st = ret_prompt(proj.reshape(b, s, -1), c_norm[j], n_heads)
            yp = yp.reshape(b * s, br)
            ys, st_s = ret_sample(proj_s, c_norm[j], state_ret[j], n_heads)
            ret_p.append(st)
            ret_s.append(st_s)
            w_out = c_w_out
        else:
            br = d_w_out.shape[1]
            n_heads = br // GLA_DV
            proj, proj_s = matmul(hp, hs, jnp.swapaxes(d_w_in, 1, 2), j, transposed=True)
            yp, st = gla_prompt(proj.reshape(b, s, -1), d_gate_w[j], d_gate_b[j], d_norm[j], n_heads)
            yp = yp.reshape(b * s, br)
            ys, st_s = gla_sample(proj_s, d_gate_w[j], d_gate_b[j], d_norm[j], state_gla[j], n_heads)
            gla_p.append(st)
            gla_s.append(st_s)
            w_out = d_w_out
        op, os_ = matmul(yp, ys, w_out, j)
        g_next = norm_pre[i + 1] if i + 1 < depth else None
        xp, hp = residual_norm(xp, op, norm_post[i], g_next)
        xs, hs = residual_norm(xs, os_, norm_post[i], g_next)

    return (xp.reshape(b, s, d), xs.reshape(bs, 1, d),
            jnp.stack(kp_rows[0]), jnp.stack(ks_rows[0]), jnp.stack(vp_rows[0]), jnp.stack(vs_rows[0]),
            jnp.stack(kp_rows[1]), jnp.stack(ks_rows[1]), jnp.stack(vp_rows[1]), jnp.stack(vs_rows[1]),
            jnp.stack(kp_rows[2]), jnp.stack(ks_rows[2]), jnp.stack(vp_rows[2]), jnp.stack(vs_rows[2]),
            jnp.stack(lru_h_p), jnp.stack(lru_h_s), jnp.stack(lru_c_p), jnp.stack(lru_c_s),
            jnp.stack(ret_p), jnp.stack(ret_s), jnp.stack(gla_p), jnp.stack(gla_s))
```

```python
import functools

import numpy as np
import jax
import jax.numpy as jnp
from jax import lax
from jax.experimental import pallas as pl
from jax.experimental.pallas import tpu as pltpu

F32 = jnp.float32
BF16 = jnp.bfloat16

PAST_LEN = 8192
EPS = 1e-6
NEG_INF = -1e30
A_GROUPS = ((128, 1), (512, 4), (2048, 16))
A_HEAD_DIM = 128
A_BLOCK = 128
ATTN_UNROLL = 16
ATTN_MERGE_ROWS = 256
ATTN_SAMPLE_UNROLL = 16
NUM_BUCKETS = 32
MAX_DISTANCE = 2048
C_RG = 8.0
LRU_TIME_TILE = 512
RET_DK = 256
RET_DV = 512
RET_CHUNK = 128
ROPE_BASE = 10000.0
GLA_DK = 256
GLA_DV = 512
GLA_TAU = 16.0
GLA_CHUNK = 64
GLA_GROUP = 16
RET_GROUP = 8

LANES = 128
SUBLANES = 8
V7X_VMEM_LIMIT_BYTES = 56 * 1024 * 1024
V7X_VMEM_LIMIT_LONG_K_BYTES = 60 * 1024 * 1024

_ARB = pltpu.ARBITRARY


def _params(n_grid, vmem_limit_bytes=V7X_VMEM_LIMIT_BYTES):
    return pltpu.CompilerParams(dimension_semantics=(_ARB,) * n_grid,
                                vmem_limit_bytes=vmem_limit_bytes)


def _dot(a, b):
    return jnp.dot(a, b, preferred_element_type=F32)


def _dot_nt(a, b):
    return lax.dot_general(a, b, (((1,), (1,)), ((), ())), preferred_element_type=F32)


def _dot_tn(a, b):
    return lax.dot_general(a, b, (((0,), (0,)), ((), ())), preferred_element_type=F32)


def _sigmoid(x):
    return 0.5 * jnp.tanh(0.5 * x) + 0.5


def _silu(x):
    return x * _sigmoid(x)


def _softplus(x):
    return jnp.maximum(x, 0.0) + jnp.log1p(jnp.exp(-jnp.abs(x)))


def _rms(x, g):
    return x * lax.rsqrt(jnp.mean(x * x, axis=-1, keepdims=True) + EPS) * g


def _chunk_group(n_chunks, want):
    return want if n_chunks % want == 0 else 1


def _col_bcast(row, n):
    return jnp.broadcast_to(row, (n, n)).T


def _norm_kernel(x_ref, g_ref, h_ref):
    h_ref[...] = _rms(x_ref[...], g_ref[...]).astype(BF16)


def _resnorm_kernel(x_ref, y_ref, gpost_ref, gnext_ref, xo_ref, h_ref):
    xn = x_ref[...] + _rms(y_ref[...], gpost_ref[...])
    xo_ref[...] = xn
    h_ref[...] = _rms(xn, gnext_ref[...]).astype(BF16)


def _res_kernel(x_ref, y_ref, gpost_ref, xo_ref):
    xo_ref[...] = x_ref[...] + _rms(y_ref[...], gpost_ref[...])


def _row_tile(m):
    return min(m, 256)


def norm_cast(x, g):
    m, d = x.shape
    tm = _row_tile(m)
    return pl.pallas_call(
        _norm_kernel,
        grid=(m // tm,),
        in_specs=[pl.BlockSpec((tm, d), lambda i: (i, 0)),
                  pl.BlockSpec((1, d), lambda i: (0, 0))],
        out_specs=pl.BlockSpec((tm, d), lambda i: (i, 0)),
        out_shape=jax.ShapeDtypeStruct((m, d), BF16),
        compiler_params=_params(1),
        name="norm_cast",
    )(x, g.reshape(1, d))


def residual_norm(x, y, g_post, g_next):
    m, d = x.shape
    tm = _row_tile(m)
    row = pl.BlockSpec((tm, d), lambda i: (i, 0))
    vec = pl.BlockSpec((1, d), lambda i: (0, 0))
    if g_next is None:
        return pl.pallas_call(
            _res_kernel, grid=(m // tm,),
            in_specs=[row, row, vec], out_specs=row,
            out_shape=jax.ShapeDtypeStruct((m, d), F32),
            compiler_params=_params(1), name="residual",
        )(x, y, g_post.reshape(1, d)), None
    return pl.pallas_call(
        _resnorm_kernel, grid=(m // tm,),
        in_specs=[row, row, vec, vec], out_specs=[row, row],
        out_shape=[jax.ShapeDtypeStruct((m, d), F32), jax.ShapeDtypeStruct((m, d), BF16)],
        compiler_params=_params(1), name="residual_norm",
    )(x, y, g_post.reshape(1, d), g_next.reshape(1, d))


def _mm_kernel(x_ref, w_ref, xs_ref, o_ref, os_ref, *, transposed, tn):
    w = w_ref[...].astype(BF16)
    dot = _dot_nt if transposed else _dot
    o_ref[...] = dot(x_ref[...], w)

    @pl.when(pl.program_id(0) == 0)
    def _():
        cols = pl.ds(pl.multiple_of(pl.program_id(1) * tn, tn), tn)
        os_ref[:, cols] = dot(xs_ref[...], w)


MM_COL_TILE = 512
MM_ROW_TILE = 2048
MM_ROW_TILE_LONG_K = 1024
MM_LONG_K = 4096


def matmul(x, xs, w, layer, col0=0, ncols=None, transposed=False):
    m, k = x.shape
    ms = xs.shape[0]
    n = w.shape[1] if transposed else w.shape[2]
    ncols = n - col0 if ncols is None else ncols
    tn = MM_COL_TILE
    assert col0 % tn == 0
    tm = min(m, MM_ROW_TILE if k <= MM_LONG_K else MM_ROW_TILE_LONG_K)
    assert m % tm == 0
    nj = pl.cdiv(ncols, tn)
    c0 = col0 // tn
    if transposed:
        w_spec = pl.BlockSpec((None, tn, k), lambda i, j: (layer, c0 + j, 0))
    else:
        w_spec = pl.BlockSpec((None, k, tn), lambda i, j: (layer, 0, c0 + j))
    out, out_s = pl.pallas_call(
        functools.partial(_mm_kernel, transposed=transposed, tn=tn),
        grid=(m // tm, nj),
        in_specs=[pl.BlockSpec((tm, k), lambda i, j: (i, 0),
                               pipeline_mode=pl.Buffered(1 if k <= MM_LONG_K else 2)),
                  w_spec,
                  pl.BlockSpec((ms, k), lambda i, j: (0, 0))],
        out_specs=[pl.BlockSpec((tm, tn), lambda i, j: (i, j)),
                   pl.BlockSpec((ms, nj * tn), lambda i, j: (0, 0))],
        out_shape=[jax.ShapeDtypeStruct((m, ncols), F32), jax.ShapeDtypeStruct((ms, nj * tn), F32)],
        compiler_params=_params(2, V7X_VMEM_LIMIT_BYTES if k <= MM_LONG_K else V7X_VMEM_LIMIT_LONG_K_BYTES),
        name="proj_matmul",
    )(x, w, xs)
    return out, (out_s if nj * tn == ncols else out_s[:, :ncols])


def _t5_bucket(dist):
    n = np.asarray(dist, dtype=np.int64)
    max_exact = NUM_BUCKETS // 2
    ratio = np.log(np.maximum(n, 1) / max_exact) / np.log(MAX_DISTANCE / max_exact)
    large = np.minimum(max_exact + (ratio * (NUM_BUCKETS - max_exact)).astype(np.int64), NUM_BUCKETS - 1)
    return np.where(n < max_exact, n, large).astype(np.int32)


def _attn_prompt_kernel(*refs, dilations, seq, tails):
    ng = len(dilations)
    qkv = refs[:3 * ng]
    gate_ref, bias_ref, y_ref = refs[3 * ng:3 * ng + 3]
    tail_refs = refs[3 * ng + 3:-2]
    o_s, lse_s = refs[-2:]
    blk = A_BLOCK
    for n, (g, keep) in enumerate(tails):
        tail_refs[2 * n][...] = qkv[3 * g + 1][seq - keep:, :]
        tail_refs[2 * n + 1][...] = qkv[3 * g + 2][seq - keep:, :]
    scale = A_HEAD_DIM ** -0.5
    ii = lax.broadcasted_iota(jnp.int32, (blk, blk), 0)
    jj = lax.broadcasted_iota(jnp.int32, (blk, blk), 1)
    cur_ok = jj <= ii
    prev_ok = jj >= ii

    for g, dil in enumerate(dilations):
        q_ref, k_ref, v_ref = qkv[3 * g:3 * g + 3]
        nb = seq // (dil * blk)
        band = pltpu.roll(jnp.broadcast_to(bias_ref[g:g + 1, :], (blk, 2 * blk)), 0, axis=1,
                          stride=1, stride_axis=0)
        b_prev = band[:, :blk]
        b_cur = band[:, blk:]

        def rows_at(start, dil=dil):
            if dil > 1:
                return pl.ds(start, blk, stride=dil)
            return pl.ds(start if isinstance(start, int) else pl.multiple_of(start, blk), blk)

        def scores(idx, dil=dil, nb=nb, q_ref=q_ref, k_ref=k_ref, b_prev=b_prev, b_cur=b_cur,
                   rows_at=rows_at):
            r = idx % dil
            bi = idx // dil
            start = r + bi * (blk * dil)
            rows = rows_at(start)
            q = q_ref[rows, :].astype(BF16)
            s = [jnp.where(cur_ok, _dot_nt(q, k_ref[rows, :].astype(BF16)) * scale + b_cur, NEG_INF)]
            prows = None
            if nb > 1 and not (isinstance(bi, int) and bi == 0):
                if isinstance(bi, int):
                    prows = rows_at(start - blk * dil)
                    prev_mask = prev_ok
                else:
                    prows = rows_at(jnp.maximum(start - blk * dil, 0))
                    prev_mask = prev_ok & (bi > 0)
                s.append(jnp.where(prev_mask, _dot_nt(q, k_ref[prows, :].astype(BF16)) * scale + b_prev,
                                   NEG_INF))
            return rows, prows, s

        def softmax(s):
            m = jnp.max(functools.reduce(jnp.maximum, s), axis=-1, keepdims=True)
            p = [jnp.exp(x - m) for x in s]
            den = jnp.sum(functools.reduce(jnp.add, p), axis=-1, keepdims=True)
            return m, den, [x.astype(BF16) for x in p]

        def values(rows, prows, p, v_ref=v_ref):
            num = _dot(p[0], v_ref[rows, :].astype(BF16))
            if prows is not None:
                num = num + _dot(p[1], v_ref[prows, :].astype(BF16))
            return num

        def body(it, carry, g=g, scores=scores, softmax=softmax, values=values):
            blocks = [scores(it * ATTN_UNROLL + u) for u in range(ATTN_UNROLL)]
            probs = [softmax(s) for _, _, s in blocks]
            nums = [values(rows, prows, p) for (rows, prows, _), (_, _, p) in zip(blocks, probs)]
            for (rows, _, _), (m, den, _), num in zip(blocks, probs, nums):
                o_s[g, rows, :] = num / den
                lse_s[g, rows, :] = jnp.broadcast_to(m + jnp.log(den), (blk, A_HEAD_DIM))
            return carry

        if seq // blk == ATTN_UNROLL:
            body(0, 0)
        else:
            lax.fori_loop(0, seq // (blk * ATTN_UNROLL), body, 0)

    tr = ATTN_MERGE_ROWS

    def merge(c, carry):
        rows = pl.ds(pl.multiple_of(c * tr, tr), tr)
        lses = [lse_s[g, rows, :] for g in range(ng)]
        mx = functools.reduce(jnp.maximum, lses)
        num = jnp.zeros((tr, A_HEAD_DIM), F32)
        den = jnp.zeros((tr, A_HEAD_DIM), F32)
        for g in range(ng):
            w = jnp.exp(lses[g] - mx)
            num = num + w * o_s[g, rows, :]
            den = den + w
        y_ref[rows, :] = ((num / den) * _silu(gate_ref[rows, :])).astype(BF16)
        return carry

    lax.fori_loop(0, seq // tr, merge, 0)


def attn_prompt(qs, ks, vs, gate, bias_tab, n_heads):
    b, s, _ = gate[0].shape
    ng = len(A_GROUPS)
    dh = A_HEAD_DIM
    in_specs = []
    args = []
    units = [u for g in range(ng) for u in (qs[g], ks[g], vs[g])] + [gate]
    for arr, unit in units:
        in_specs.append(pl.BlockSpec((None, s, dh), lambda bi, hi, unit=unit: (bi, 0, unit * n_heads + hi)))
        args.append(arr)
    in_specs.append(pl.BlockSpec((None, ng, 2 * A_BLOCK), lambda bi, hi: (hi, 0, 0)))
    args.append(bias_tab)
    tails = tuple((g, window) for g, (window, _) in enumerate(A_GROUPS) if window < s)
    kern = functools.partial(_attn_prompt_kernel, dilations=tuple(d for _, d in A_GROUPS), seq=s,
                             tails=tails)
    out_specs = [pl.BlockSpec((None, s, dh), lambda bi, hi: (bi, 0, hi))]
    out_shape = [jax.ShapeDtypeStruct((b, s, n_heads * dh), BF16)]
    for _, keep in tails:
        out_specs += [pl.BlockSpec((None, keep, dh), lambda bi, hi: (bi, 0, hi))] * 2
        out_shape += [jax.ShapeDtypeStruct((b, keep, n_heads * dh), F32)] * 2
    return pl.pallas_call(
        kern,
        grid=(b, n_heads),
        in_specs=in_specs,
        out_specs=out_specs,
        out_shape=out_shape,
        scratch_shapes=[pltpu.VMEM((ng, s, dh), F32)] * 2,
        compiler_params=_params(2),
        name="attn_prompt",
    )(*args)


def _attn_sample_kernel(q_ref, kn_ref, vn_ref, gate_ref, *refs, n_heads, n_keys):
    ng = len(A_GROUPS)
    caches = refs[:2 * ng]
    bpast_ref, bself_ref, y_ref = refs[2 * ng:]
    dh = A_HEAD_DIM
    scale = dh ** -0.5
    lane = lax.broadcasted_iota(jnp.int32, (n_heads, n_keys), 1)
    parts = []
    for g in range(ng):
        kc_ref, vc_ref = caches[2 * g], caches[2 * g + 1]
        q = q_ref[g]
        vn = vn_ref[g]

        def logits(j, acc, kc_ref=kc_ref, q=q):
            col = jnp.sum(kc_ref[j] * q, axis=-1, keepdims=True)
            return jnp.where(lane == j, col, acc)

        lp = lax.fori_loop(0, n_keys, logits, jnp.zeros((n_heads, n_keys), F32), unroll=ATTN_SAMPLE_UNROLL)
        lp = lp * scale + bpast_ref[g]
        ls = jnp.sum(kn_ref[g] * q, axis=-1, keepdims=True) * scale + bself_ref[g]
        m = jnp.maximum(jnp.max(lp, axis=-1, keepdims=True), ls)
        p = jnp.exp(lp - m)
        ps = jnp.exp(ls - m)
        den = jnp.sum(p, axis=-1, keepdims=True) + ps

        def weighted(j, acc, vc_ref=vc_ref, p=p):
            pj = jnp.sum(jnp.where(lane == j, p, 0.0), axis=-1, keepdims=True)
            return acc + pj * vc_ref[j]

        num = lax.fori_loop(0, n_keys, weighted, ps * vn, unroll=ATTN_SAMPLE_UNROLL)
        parts.append((num, m, den))
    mx = functools.reduce(jnp.maximum, [p_[1] for p_ in parts])
    num = jnp.zeros((n_heads, dh), F32)
    den = jnp.zeros((n_heads, 1), F32)
    for num_g, m_g, den_g in parts:
        w = jnp.exp(m_g - mx)
        num = num + w * num_g
        den = den + w * den_g
    y_ref[...] = ((num / den) * _silu(gate_ref[...])).astype(BF16)


def attn_sample(qs, ks, vs, gate, k_caches, v_caches, bias_past, bias_self, n_heads):
    bs, width = gate.shape
    dh = A_HEAD_DIM
    ng = len(A_GROUPS)
    n_keys = A_BLOCK
    heads = lambda rows: jnp.stack(rows, axis=1).reshape(bs, ng, n_heads, dh)
    grp_spec = pl.BlockSpec((None, ng, n_heads, dh), lambda bi: (bi, 0, 0, 0))
    args = [heads(qs), heads(ks), heads(vs), gate.reshape(bs, n_heads, dh)]
    in_specs = [grp_spec, grp_spec, grp_spec, pl.BlockSpec((None, n_heads, dh), lambda bi: (bi, 0, 0))]
    for (_, dil), kc, vc in zip(A_GROUPS, k_caches, v_caches):
        for c in (kc, vc):
            args.append(c.reshape(bs, n_keys, dil, n_heads, dh))
            in_specs.append(pl.BlockSpec((None, n_keys, None, n_heads, dh), lambda bi: (bi, 0, 0, 0, 0)))
    args += [bias_past, bias_self]
    in_specs += [pl.BlockSpec(bias_past.shape, lambda bi: (0, 0, 0)),
                 pl.BlockSpec(bias_self.shape, lambda bi: (0, 0, 0))]
    y = pl.pallas_call(
        functools.partial(_attn_sample_kernel, n_heads=n_heads, n_keys=n_keys),
        grid=(bs,),
        in_specs=in_specs,
        out_specs=pl.BlockSpec((None, n_heads, dh), lambda bi: (bi, 0, 0)),
        out_shape=jax.ShapeDtypeStruct((bs, n_heads, dh), BF16),
        compiler_params=_params(1),
        name="attn_sample",
    )(*args)
    return y.reshape(bs, width)


def _lru_gates(xc, wa_ref, ba_ref, wx_ref, bx_ref, lam_ref):
    xb = xc.astype(BF16)
    r = _sigmoid(_dot(xb, wa_ref[...].astype(BF16)) + ba_ref[...])
    ig = _sigmoid(_dot(xb, wx_ref[...].astype(BF16)) + bx_ref[...])
    log_a = -C_RG * r * _softplus(-lam_ref[...])
    a = jnp.exp(log_a)
    mult = jnp.sqrt(-jnp.tanh(log_a) * (a * a + 1.0))
    return a, mult, ig


def _lru_prompt_kernel(x_ref, gate_ref, cw_ref, cb_ref, wa_ref, ba_ref, wx_ref, bx_ref, lam_ref,
                       y_ref, hl_ref, hc_ref, xp_ref, *, tt, conv_w):
    t = pl.program_id(2)

    @pl.when(t == 0)
    def _():
        hc_ref[...] = jnp.zeros_like(hc_ref)
        xp_ref[:SUBLANES, :] = jnp.zeros((SUBLANES, xp_ref.shape[1]), F32)

    x = x_ref[...]
    c = x.shape[1]
    xp_ref[SUBLANES:, :] = x
    xc = cb_ref[...] + cw_ref[conv_w - 1:conv_w, :] * x
    for k in range(1, conv_w):
        xc = xc + cw_ref[conv_w - 1 - k:conv_w - k, :] * xp_ref[SUBLANES - k:SUBLANES - k + tt, :]
    xp_ref[:SUBLANES, :] = x[tt - SUBLANES:]

    row = lax.broadcasted_iota(jnp.int32, (tt, c), 0)
    a, mult, ig = _lru_gates(xc, wa_ref, ba_ref, wx_ref, bx_ref, lam_ref)
    mult = jnp.where(row + t * tt == 0, 1.0, mult)
    bx = mult * ig * xc
    ng = tt // SUBLANES
    a = a.reshape(ng, SUBLANES, c)
    bx = bx.reshape(ng, SUBLANES, c)
    sub = lax.broadcasted_iota(jnp.int32, (ng, SUBLANES, c), 1)
    d = 1
    while d < SUBLANES:
        keep = sub >= d
        a_sh = jnp.where(keep, pltpu.roll(a, d, axis=1), 1.0)
        b_sh = jnp.where(keep, pltpu.roll(bx, d, axis=1), 0.0)
        bx = a * b_sh + bx
        a = a * a_sh
        d *= 2
    h = hc_ref[...]
    groups = []
    for g in range(ng):
        groups.append(a[g] * h + bx[g])
        h = groups[-1][SUBLANES - 1:]
    y_ref[...] = (jnp.concatenate(groups, axis=0) * _silu(gate_ref[...])).astype(BF16)
    hc_ref[...] = h
    hl_ref[...] = h


def lru_prompt(proj, conv_w, conv_b, wa, ba, wx, bx, lam):
    b, s, two_br = proj.shape
    br = two_br // 2
    nblk, bs_, _ = wa.shape
    cw = conv_w.shape[0]
    tt = min(s, LRU_TIME_TILE)
    vec = lambda a: a.reshape(1, br)
    vspec = pl.BlockSpec((1, bs_), lambda bi, ni, ti: (0, ni))
    wspec = pl.BlockSpec((None, bs_, bs_), lambda bi, ni, ti: (ni, 0, 0))
    y, hl = pl.pallas_call(
        functools.partial(_lru_prompt_kernel, tt=tt, conv_w=cw),
        grid=(b, nblk, s // tt),
        in_specs=[pl.BlockSpec((None, tt, bs_), lambda bi, ni, ti: (bi, ti, ni)),
                  pl.BlockSpec((None, tt, bs_), lambda bi, ni, ti: (bi, ti, nblk + ni)),
                  pl.BlockSpec((cw, bs_), lambda bi, ni, ti: (0, ni)),
                  vspec, wspec, vspec, wspec, vspec, vspec],
        out_specs=[pl.BlockSpec((None, tt, bs_), lambda bi, ni, ti: (bi, ti, ni)),
                   pl.BlockSpec((None, 1, bs_), lambda bi, ni, ti: (bi, 0, ni))],
        out_shape=[jax.ShapeDtypeStruct((b, s, br), BF16), jax.ShapeDtypeStruct((b, 1, br), F32)],
        scratch_shapes=[pltpu.VMEM((1, bs_), F32), pltpu.VMEM((SUBLANES + tt, bs_), F32)],
        compiler_params=_params(3),
        name="lru_prompt",
    )(proj, proj, conv_w, vec(conv_b), wa, vec(ba), wx, vec(bx), vec(lam))
    return y, hl.reshape(b, br)


def _lru_sample_kernel(x_ref, gate_ref, buf_ref, h0_ref, cw_ref, cb_ref, wa_ref, ba_ref, wx_ref, bx_ref,
                       lam_ref, y_ref, h_ref, *, conv_w):
    x = x_ref[...]
    xc = cb_ref[...] + cw_ref[conv_w - 1:conv_w, :] * x
    for w in range(conv_w - 1):
        xc = xc + cw_ref[w:w + 1, :] * buf_ref[:, w, :]
    a, mult, ig = _lru_gates(xc, wa_ref, ba_ref, wx_ref, bx_ref, lam_ref)
    h = a * h0_ref[...] + mult * ig * xc
    h_ref[...] = h
    y_ref[...] = (h * _silu(gate_ref[...])).astype(BF16)


def lru_sample(proj_s, conv_buf, h0, conv_w, conv_b, wa, ba, wx, bx, lam):
    bs, two_br = proj_s.shape
    br = two_br // 2
    nblk, bs_, _ = wa.shape
    cw = conv_w.shape[0]
    vec = lambda a: a.reshape(1, br)
    vspec = pl.BlockSpec((1, bs_), lambda ni: (0, ni))
    wspec = pl.BlockSpec((None, bs_, bs_), lambda ni: (ni, 0, 0))
    rspec = pl.BlockSpec((bs, bs_), lambda ni: (0, ni))
    return pl.pallas_call(
        functools.partial(_lru_sample_kernel, conv_w=cw),
        grid=(nblk,),
        in_specs=[rspec,
                  pl.BlockSpec((bs, bs_), lambda ni: (0, nblk + ni)),
                  pl.BlockSpec((bs, cw - 1, bs_), lambda ni: (0, 0, ni)),
                  rspec,
                  pl.BlockSpec((cw, bs_), lambda ni: (0, ni)),
                  vspec, wspec, vspec, wspec, vspec, vspec],
        out_specs=[rspec, rspec],
        out_shape=[jax.ShapeDtypeStruct((bs, br), BF16), jax.ShapeDtypeStruct((bs, br), F32)],
        compiler_params=_params(1),
        name="lru_sample",
    )(proj_s, proj_s, conv_buf, h0, conv_w, vec(conv_b), wa, vec(ba), wx, vec(bx), vec(lam))


def _rope(x, cos, sin):
    half = x.shape[-1] // 2
    x1, x2 = x[:, :half], x[:, half:]
    return jnp.concatenate([x1 * cos - x2 * sin, x1 * sin + x2 * cos], axis=-1)


def _groupnorm(o, gain):
    c = o - jnp.mean(o, axis=-1, keepdims=True)
    return c * lax.rsqrt(jnp.mean(c * c, axis=-1, keepdims=True) + EPS) * gain


def _ret_prompt_kernel(q_ref, k_ref, v_ref, gate_ref, cos_ref, sin_ref, lg_ref, gain_ref,
                       y_ref, s_ref, st_ref, *, chunk, n_chunks, group):
    lg = lg_ref[:, :1]
    idx = lax.broadcasted_iota(jnp.int32, (chunk, 1), 0).astype(F32)
    ii = lax.broadcasted_iota(jnp.int32, (chunk, chunk), 0)
    jj = lax.broadcasted_iota(jnp.int32, (chunk, chunk), 1)
    diff = (ii - jj).astype(F32)
    decay = jnp.where(diff >= 0, jnp.exp(diff * lg), 0.0)
    q_dec = jnp.exp((idx + 1.0) * lg)
    k_dec = jnp.exp((chunk - 1.0 - idx) * lg)
    chunk_dec = jnp.exp(chunk * lg)
    gain = gain_ref[...]
    st_ref[...] = jnp.zeros_like(st_ref)

    q_dec_rows = jnp.concatenate([q_dec] * group, axis=0)
    k_dec_rows = jnp.concatenate([k_dec] * group, axis=0)

    def body(it, carry):
        rows = pl.ds(pl.multiple_of(it * (group * chunk), group * chunk), group * chunk)
        cos = cos_ref[rows, :]
        sin = sin_ref[rows, :]
        qc = _rope(q_ref[rows, :], cos, sin)
        kc = _rope(k_ref[rows, :], cos, sin) * (RET_DK ** -0.5)
        vc = v_ref[rows, :].astype(BF16)
        qb = qc.astype(BF16)
        kb = kc.astype(BF16)
        qd = (qc * q_dec_rows).astype(BF16)
        kd = (kc * k_dec_rows).astype(BF16)
        sl = [slice(u * chunk, (u + 1) * chunk) for u in range(group)]
        intra, upd = [], []
        for u in range(group):
            scores = _dot_nt(qb[sl[u]], kb[sl[u]]) * decay
            intra.append(_dot(scores.astype(BF16), vc[sl[u]]))
            upd.append(_dot_tn(kd[sl[u]], vc[sl[u]]))
        st = st_ref[...]
        outs = []
        for u in range(group):
            outs.append(intra[u] + _dot(qd[sl[u]], st.astype(BF16)))
            st = chunk_dec * st + upd[u]
        st_ref[...] = st
        o = jnp.concatenate(outs, axis=0)
        y_ref[rows, :] = (_groupnorm(o, gain) * _silu(gate_ref[rows, :])).astype(BF16)
        return carry

    lax.fori_loop(0, n_chunks // group, body, 0)
    s_ref[...] = st_ref[...]


def _ret_log_gamma(n_heads):
    lg = np.log1p(-np.exp2(-5.0 - np.arange(n_heads, dtype=np.float32))).astype(np.float32)
    return jnp.asarray(np.broadcast_to(lg[:, None, None], (n_heads, 1, LANES)).copy())


def _rope_tables(pos):
    half = RET_DK // 2
    inv_freq = ROPE_BASE ** (-jnp.arange(half, dtype=F32) / half)
    ang = pos[:, None] * inv_freq[None, :]
    return jnp.cos(ang), jnp.sin(ang)


def ret_prompt(proj, gain, n_heads):
    b, s, _ = proj.shape
    dk, dv = RET_DK, RET_DV
    qk = n_heads * dk
    chunk = RET_CHUNK if s % RET_CHUNK == 0 else s
    cos, sin = _rope_tables(jnp.arange(s, dtype=F32))
    half = dk // 2
    koff = qk // dk
    voff = 2 * qk // dv
    goff = (2 * qk + n_heads * dv) // dv
    return pl.pallas_call(
        functools.partial(_ret_prompt_kernel, chunk=chunk, n_chunks=s // chunk,
                          group=_chunk_group(s // chunk, RET_GROUP)),
        grid=(b, n_heads),
        in_specs=[pl.BlockSpec((None, s, dk), lambda bi, hi: (bi, 0, hi)),
                  pl.BlockSpec((None, s, dk), lambda bi, hi: (bi, 0, koff + hi)),
                  pl.BlockSpec((None, s, dv), lambda bi, hi: (bi, 0, voff + hi)),
                  pl.BlockSpec((None, s, dv), lambda bi, hi: (bi, 0, goff + hi)),
                  pl.BlockSpec((s, half), lambda bi, hi: (0, 0)),
                  pl.BlockSpec((s, half), lambda bi, hi: (0, 0)),
                  pl.BlockSpec((None, 1, LANES), lambda bi, hi: (hi, 0, 0)),
                  pl.BlockSpec((1, dv), lambda bi, hi: (0, hi))],
        out_specs=[pl.BlockSpec((None, s, dv), lambda bi, hi: (bi, 0, hi)),
                   pl.BlockSpec((None, None, dk, dv), lambda bi, hi: (bi, hi, 0, 0))],
        out_shape=[jax.ShapeDtypeStruct((b, s, n_heads * dv), BF16),
                   jax.ShapeDtypeStruct((b, n_heads, dk, dv), F32)],
        scratch_shapes=[pltpu.VMEM((dk, dv), F32)],
        compiler_params=_params(2),
        name="ret_prompt",
    )(proj, proj, proj, proj, cos, sin, _ret_log_gamma(n_heads), gain.reshape(1, n_heads * dv))


def _ret_sample_kernel(q_ref, k_ref, v_ref, gate_ref, cos_ref, sin_ref, lg_ref, gain_ref, s0_ref,
                       y_ref, s_ref):
    dk, dv = RET_DK, RET_DV
    bs = q_ref.shape[0]
    gamma = jnp.exp(lg_ref[:, :1])
    cos = cos_ref[...]
    sin = sin_ref[...]
    q = _rope(q_ref[...], cos, sin)
    k = _rope(k_ref[...], cos, sin) * (dk ** -0.5)
    v = v_ref[...]
    o = jnp.sum(q * k, axis=-1, keepdims=True) * v
    qg = (q * gamma).astype(BF16)
    row = lax.broadcasted_iota(jnp.int32, (bs, dv), 0)
    for b in range(bs):
        s0 = s0_ref[b]
        o = o + jnp.where(row == b, _dot(qg, s0.astype(BF16)), 0.0)
        kcol = _col_bcast(k[b:b + 1], dk)
        for j in range(dv // dk):
            cols = slice(j * dk, (j + 1) * dk)
            s_ref[b, :, cols] = gamma * s0[:, cols] + kcol * v[b:b + 1, cols]
    y_ref[...] = (_groupnorm(o, gain_ref[...]) * _silu(gate_ref[...])).astype(BF16)


def ret_sample(proj_s, gain, s0, n_heads):
    bs = proj_s.shape[0]
    dk, dv = RET_DK, RET_DV
    qk = n_heads * dk
    cos, sin = _rope_tables(jnp.full((1,), PAST_LEN, F32))
    half = dk // 2
    koff = qk // dk
    voff = 2 * qk // dv
    goff = (2 * qk + n_heads * dv) // dv
    state_spec = pl.BlockSpec((bs, None, dk, dv), lambda hi: (0, hi, 0, 0))
    return pl.pallas_call(
        _ret_sample_kernel,
        grid=(n_heads,),
        in_specs=[pl.BlockSpec((bs, dk), lambda hi: (0, hi)),
                  pl.BlockSpec((bs, dk), lambda hi: (0, koff + hi)),
                  pl.BlockSpec((bs, dv), lambda hi: (0, voff + hi)),
                  pl.BlockSpec((bs, dv), lambda hi: (0, goff + hi)),
                  pl.BlockSpec((1, half), lambda hi: (0, 0)),
                  pl.BlockSpec((1, half), lambda hi: (0, 0)),
                  pl.BlockSpec((None, 1, LANES), lambda hi: (hi, 0, 0)),
                  pl.BlockSpec((1, dv), lambda hi: (0, hi)),
                  state_spec],
        out_specs=[pl.BlockSpec((bs, dv), lambda hi: (0, hi)), state_spec],
        out_shape=[jax.ShapeDtypeStruct((bs, n_heads * dv), BF16),
                   jax.ShapeDtypeStruct((bs, n_heads, dk, dv), F32)],
        compiler_params=_params(1),
        name="ret_sample",
    )(proj_s, proj_s, proj_s, proj_s, cos, sin, _ret_log_gamma(n_heads), gain.reshape(1, n_heads * dv), s0)


def _headnorm(o, gain):
    return o * lax.rsqrt(jnp.mean(o * o, axis=-1, keepdims=True) + EPS) * gain


def _gla_log_alpha(low_ref_val, gw_ref, gb_ref, rank):
    lane = lax.broadcasted_iota(jnp.int32, low_ref_val.shape, 1)
    low = jnp.where(lane < rank, low_ref_val, 0.0).astype(BF16)
    z = _dot(low, gw_ref[...].astype(BF16)) + gb_ref[...]
    return (jnp.minimum(z, 0.0) - jnp.log1p(jnp.exp(-jnp.abs(z)))) / GLA_TAU


def _gla_prompt_kernel(q_ref, k_ref, v_ref, gate_ref, low_ref, gw_ref, gb_ref, gain_ref,
                       y_ref, s_ref, st_ref, *, chunk, n_chunks, group, rank):
    dk, dv = GLA_DK, GLA_DV
    sub = lax.broadcasted_iota(jnp.int32, (group * chunk, dk), 0) % chunk
    ii = lax.broadcasted_iota(jnp.int32, (chunk, chunk), 0)
    jj = lax.broadcasted_iota(jnp.int32, (chunk, chunk), 1)
    causal = jj <= ii
    gain = gain_ref[...]
    st_ref[...] = jnp.zeros_like(st_ref)

    def body(it, carry):
        rows = pl.ds(pl.multiple_of(it * (group * chunk), group * chunk), group * chunk)
        bcum = _gla_log_alpha(low_ref[rows, :], gw_ref, gb_ref, rank)
        d = 1
        while d < chunk:
            bcum = bcum + jnp.where(sub >= d, pltpu.roll(bcum, d, axis=0), 0.0)
            d *= 2
        sl = [slice(u * chunk, (u + 1) * chunk) for u in range(group)]
        blast = [bcum[(u + 1) * chunk - 1:(u + 1) * chunk] for u in range(group)]
        blast_rows = jnp.concatenate([jnp.broadcast_to(b, (chunk, dk)) for b in blast], axis=0)
        kc = k_ref[rows, :]
        vc = v_ref[rows, :].astype(BF16)
        qe = (q_ref[rows, :] * (dk ** -0.5) * jnp.exp(bcum)).astype(BF16)
        ke = (kc * jnp.exp(-bcum)).astype(BF16)
        kd = (kc * jnp.exp(blast_rows - bcum)).astype(BF16)
        intra, upd, dec = [], [], []
        for u in range(group):
            scores = jnp.where(causal, _dot_nt(qe[sl[u]], ke[sl[u]]), 0.0)
            intra.append(_dot(scores.astype(BF16), vc[sl[u]]))
            upd.append(_dot_tn(kd[sl[u]], vc[sl[u]]))
            dec.append(_col_bcast(jnp.exp(blast[u]), dk))
        st = st_ref[...]
        outs = []
        for u in range(group):
            outs.append(intra[u] + _dot(qe[sl[u]], st.astype(BF16)))
            st = jnp.concatenate([dec[u] * st[:, j * dk:(j + 1) * dk] + upd[u][:, j * dk:(j + 1) * dk]
                                  for j in range(dv // dk)], axis=1)
        st_ref[...] = st
        o = jnp.concatenate(outs, axis=0)
        y_ref[rows, :] = (_headnorm(o, gain) * _silu(gate_ref[rows, :])).astype(BF16)
        return carry

    lax.fori_loop(0, n_chunks // group, body, 0)
    s_ref[...] = st_ref[...]


def _gla_offsets(n_heads):
    dk, dv = GLA_DK, GLA_DV
    qk = n_heads * dk
    koff = qk // dk
    voff = 2 * qk // dv
    goff = (2 * qk + n_heads * dv) // dv
    loff = (2 * qk + 2 * n_heads * dv) // LANES
    return koff, voff, goff, loff


def _pad_rank(gate_w):
    rank = gate_w.shape[0]
    return jnp.pad(gate_w, ((0, LANES - rank), (0, 0)))


def gla_prompt(proj, gate_w, gate_b, gain, n_heads):
    b, s, _ = proj.shape
    dk, dv = GLA_DK, GLA_DV
    rank = gate_w.shape[0]
    chunk = GLA_CHUNK if s % GLA_CHUNK == 0 else s
    koff, voff, goff, loff = _gla_offsets(n_heads)
    return pl.pallas_call(
        functools.partial(_gla_prompt_kernel, chunk=chunk, n_chunks=s // chunk,
                          group=_chunk_group(s // chunk, GLA_GROUP), rank=rank),
        grid=(b, n_heads),
        in_specs=[pl.BlockSpec((None, s, dk), lambda bi, hi: (bi, 0, hi)),
                  pl.BlockSpec((None, s, dk), lambda bi, hi: (bi, 0, koff + hi)),
                  pl.BlockSpec((None, s, dv), lambda bi, hi: (bi, 0, voff + hi)),
                  pl.BlockSpec((None, s, dv), lambda bi, hi: (bi, 0, goff + hi)),
                  pl.BlockSpec((None, s, LANES), lambda bi, hi: (bi, 0, loff)),
                  pl.BlockSpec((LANES, dk), lambda bi, hi: (0, hi)),
                  pl.BlockSpec((1, dk), lambda bi, hi: (0, hi)),
                  pl.BlockSpec((1, dv), lambda bi, hi: (0, hi))],
        out_specs=[pl.BlockSpec((None, s, dv), lambda bi, hi: (bi, 0, hi)),
                   pl.BlockSpec((None, None, dk, dv), lambda bi, hi: (bi, hi, 0, 0))],
        out_shape=[jax.ShapeDtypeStruct((b, s, n_heads * dv), BF16),
                   jax.ShapeDtypeStruct((b, n_heads, dk, dv), F32)],
        scratch_shapes=[pltpu.VMEM((dk, dv), F32)],
        compiler_params=_params(2),
        name="gla_prompt",
    )(proj, proj, proj, proj, proj, _pad_rank(gate_w), gate_b.reshape(1, -1), gain.reshape(1, -1))


def _gla_sample_kernel(q_ref, k_ref, v_ref, gate_ref, low_ref, gw_ref, gb_ref, gain_ref, s0_ref,
                       y_ref, s_ref, *, rank):
    dk, dv = GLA_DK, GLA_DV
    bs = q_ref.shape[0]
    g = _gla_log_alpha(low_ref[...], gw_ref, gb_ref, rank)
    q = q_ref[...] * (dk ** -0.5)
    k = k_ref[...]
    v = v_ref[...]
    qe = q * jnp.exp(g)
    ke = k * jnp.exp(-g)
    kd = k * jnp.exp(g - g)
    eg = jnp.exp(g)
    o = jnp.sum(qe * ke, axis=-1, keepdims=True) * v
    qeb = qe.astype(BF16)
    row = lax.broadcasted_iota(jnp.int32, (bs, dv), 0)
    for b in range(bs):
        s0 = s0_ref[b]
        o = o + jnp.where(row == b, _dot(qeb, s0.astype(BF16)), 0.0)
        dec = _col_bcast(eg[b:b + 1], dk)
        kcol = _col_bcast(kd[b:b + 1], dk)
        for j in range(dv // dk):
            cols = slice(j * dk, (j + 1) * dk)
            s_ref[b, :, cols] = dec * s0[:, cols] + kcol * v[b:b + 1, cols]
    y_ref[...] = (_headnorm(o, gain_ref[...]) * _silu(gate_ref[...])).astype(BF16)


def gla_sample(proj_s, gate_w, gate_b, gain, s0, n_heads):
    bs = proj_s.shape[0]
    dk, dv = GLA_DK, GLA_DV
    rank = gate_w.shape[0]
    koff, voff, goff, loff = _gla_offsets(n_heads)
    state_spec = pl.BlockSpec((bs, None, dk, dv), lambda hi: (0, hi, 0, 0))
    return pl.pallas_call(
        functools.partial(_gla_sample_kernel, rank=rank),
        grid=(n_heads,),
        in_specs=[pl.BlockSpec((bs, dk), lambda hi: (0, hi)),
                  pl.BlockSpec((bs, dk), lambda hi: (0, koff + hi)),
                  pl.BlockSpec((bs, dv), lambda hi: (0, voff + hi)),
                  pl.BlockSpec((bs, dv), lambda hi: (0, goff + hi)),
                  pl.BlockSpec((bs, LANES), lambda hi: (0, loff)),
                  pl.BlockSpec((LANES, dk), lambda hi: (0, hi)),
                  pl.BlockSpec((1, dk), lambda hi: (0, hi)),
                  pl.BlockSpec((1, dv), lambda hi: (0, hi)),
                  state_spec],
        out_specs=[pl.BlockSpec((bs, dv), lambda hi: (0, hi)), state_spec],
        out_shape=[jax.ShapeDtypeStruct((bs, n_heads * dv), BF16),
                   jax.ShapeDtypeStruct((bs, n_heads, dk, dv), F32)],
        compiler_params=_params(1),
        name="gla_sample",
    )(proj_s, proj_s, proj_s, proj_s, proj_s, _pad_rank(gate_w), gate_b.reshape(1, -1),
      gain.reshape(1, -1), s0)


def _attention_bias_tables(rel_bias, n_heads):
    blk = A_BLOCK
    ng = len(A_GROUPS)
    onehot = np.zeros((ng, 2 * blk, NUM_BUCKETS), np.float32)
    for g, (window, dil) in enumerate(A_GROUPS):
        n_keys = window // dil
        assert n_keys == blk, "one 128-key band per dilated stream is assumed"
        u = np.arange(n_keys + 1)
        onehot[g, u, _t5_bucket(dil * (n_keys - u))] = 1.0
    tab = jnp.einsum("gub,bgh->guh", jnp.asarray(onehot), rel_bias.astype(F32).reshape(NUM_BUCKETS, ng, n_heads),
                     precision=lax.Precision.HIGHEST)
    tab_t = tab.transpose(0, 2, 1)
    return tab.transpose(2, 0, 1), tab_t[:, :, :blk], tab_t[:, :, blk:blk + 1]


def kernel(x_prompt, x_sample, cache_k_w128, cache_v_w128, cache_k_w512, cache_v_w512, cache_k_w2048, cache_v_w2048, state_lru_h, state_lru_conv, state_ret, state_gla, norm_pre, norm_post, rel_bias, a_w_in, a_w_out, b_w_in, b_conv_w, b_conv_b, b_gate_a_w, b_gate_a_b, b_gate_x_w, b_gate_x_b, b_lambda, b_w_out, c_w_in, c_norm, c_w_out, d_w_in, d_gate_w, d_gate_b, d_norm, d_w_out):
    b, s, d = x_prompt.shape
    bs = x_sample.shape[0]
    assert x_sample.shape[1] == 1, "one new token per sequence"
    depth = norm_pre.shape[0]
    k_caches = (cache_k_w128, cache_k_w512, cache_k_w2048)
    v_caches = (cache_v_w128, cache_v_w512, cache_v_w2048)
    ng = len(A_GROUPS)
    for (window, dil), kc in zip(A_GROUPS, k_caches):
        assert kc.shape[2] == window and s % (dil * A_BLOCK) == 0

    xp = x_prompt.reshape(b * s, d)
    xs = x_sample.reshape(bs, d)
    hp = norm_cast(xp, norm_pre[0])
    hs = norm_cast(xs, norm_pre[0])

    kp_rows = [[] for _ in A_GROUPS]
    vp_rows = [[] for _ in A_GROUPS]
    ks_rows = [[] for _ in A_GROUPS]
    vs_rows = [[] for _ in A_GROUPS]
    lru_h_p, lru_h_s, lru_c_p, lru_c_s = [], [], [], []
    ret_p, ret_s, gla_p, gla_s = [], [], [], []

    for i in range(depth):
        kind, j = i % 4, i // 4
        if kind == 0:
            n_heads = a_w_out.shape[1] // A_HEAD_DIM
            width = n_heads * A_HEAD_DIM
            full = [min(window, s) == s for window, _ in A_GROUPS]
            cuts = {0, 3 * ng + 1}
            for which in (1, 2):
                for g in range(ng):
                    if full[g]:
                        cuts |= {which * ng + g, which * ng + g + 1}
            cuts = sorted(cuts)
            unit_p, unit_s = {}, {}
            for u0, u1 in zip(cuts[:-1], cuts[1:]):
                arr_p, arr_s = matmul(hp, hs, a_w_in, j, col0=u0 * width, ncols=(u1 - u0) * width)
                for u in range(u0, u1):
                    unit_p[u] = (arr_p.reshape(b, s, -1), u - u0)
                    unit_s[u] = arr_s[:, (u - u0) * width:(u - u0 + 1) * width]
            bias_tab, bias_past, bias_self = _attention_bias_tables(rel_bias, n_heads)
            yp, *tails = attn_prompt([unit_p[g] for g in range(ng)], [unit_p[ng + g] for g in range(ng)],
                                     [unit_p[2 * ng + g] for g in range(ng)], unit_p[3 * ng],
                                     bias_tab, n_heads)
            yp = yp.reshape(b * s, width)
            ys = attn_sample([unit_s[g] for g in range(ng)], [unit_s[ng + g] for g in range(ng)],
                             [unit_s[2 * ng + g] for g in range(ng)], unit_s[3 * ng],
                             [c[j] for c in k_caches], [c[j] for c in v_caches],
                             bias_past, bias_self, n_heads)
            for g in range(ng):
                if full[g]:
                    k_rows, v_rows = unit_p[ng + g][0], unit_p[2 * ng + g][0]
                else:
                    k_rows, v_rows = tails.pop(0), tails.pop(0)
                kp_rows[g].append(k_rows.reshape(b, -1, n_heads, A_HEAD_DIM))
                vp_rows[g].append(v_rows.reshape(b, -1, n_heads, A_HEAD_DIM))
                ks_rows[g].append(unit_s[ng + g].reshape(bs, 1, n_heads, A_HEAD_DIM))
                vs_rows[g].append(unit_s[2 * ng + g].reshape(bs, 1, n_heads, A_HEAD_DIM))
            w_out = a_w_out
        elif kind == 1:
            br = b_w_out.shape[1]
            proj, proj_s = matmul(hp, hs, b_w_in, j)
            prm = (b_conv_w[j], b_conv_b[j], b_gate_a_w[j], b_gate_a_b[j], b_gate_x_w[j], b_gate_x_b[j], b_lambda[j])
            proj3 = proj.reshape(b, s, -1)
            yp, h_last = lru_prompt(proj3, *prm)
            yp = yp.reshape(b * s, br)
            ys, h_new = lru_sample(proj_s, state_lru_conv[j], state_lru_h[j], *prm)
            cw = b_conv_w.shape[1]
            lru_h_p.append(h_last)
            lru_c_p.append(proj3[:, s - (cw - 1):, :br])
            lru_h_s.append(h_new)
            lru_c_s.append(jnp.concatenate([state_lru_conv[j], proj_s[:, None, :br]], axis=1)[:, 1:])
            w_out = b_w_out
        elif kind == 2:
            br = c_w_out.shape[1]
            n_heads = br // RET_DV
            proj, proj_s = matmul(hp, hs, c_w_in, j)
            yp, st = ret_prompt(proj.reshape(b, s, -1), c_norm[j], n_heads)
            yp = yp.reshape(b * s, br)
            ys, st_s = ret_sample(proj_s, c_norm[j], state_ret[j], n_heads)
            ret_p.append(st)
            ret_s.append(st_s)
            w_out = c_w_out
        else:
            br = d_w_out.shape[1]
            n_heads = br // GLA_DV
            proj, proj_s = matmul(hp, hs, jnp.swapaxes(d_w_in, 1, 2), j, transposed=True)
            yp, st = gla_prompt(proj.reshape(b, s, -1), d_gate_w[j], d_gate_b[j], d_norm[j], n_heads)
            yp = yp.reshape(b * s, br)
            ys, st_s = gla_sample(proj_s, d_gate_w[j], d_gate_b[j], d_norm[j], state_gla[j], n_heads)
            gla_p.append(st)
            gla_s.append(st_s)
            w_out = d_w_out
        op, os_ = matmul(yp, ys, w_out, j)
        g_next = norm_pre[i + 1] if i + 1 < depth else None
        xp, hp = residual_norm(xp, op, norm_post[i], g_next)
        xs, hs = residual_norm(xs, os_, norm_post[i], g_next)

    return (xp.reshape(b, s, d), xs.reshape(bs, 1, d),
            jnp.stack(kp_rows[0]), jnp.stack(ks_rows[0]), jnp.stack(vp_rows[0]), jnp.stack(vs_rows[0]),
            jnp.stack(kp_rows[1]), jnp.stack(ks_rows[1]), jnp.stack(vp_rows[1]), jnp.stack(vs_rows[1]),
            jnp.stack(kp_rows[2]), jnp.stack(ks_rows[2]), jnp.stack(vp_rows[2]), jnp.stack(vs_rows[2]),
            jnp.stack(lru_h_p), jnp.stack(lru_h_s), jnp.stack(lru_c_p), jnp.stack(lru_c_s),
            jnp.stack(ret_p), jnp.stack(ret_s), jnp.stack(gla_p), jnp.stack(gla_s))
```

```python
import functools

import numpy as np
import jax
import jax.numpy as jnp
from jax import lax
from jax.experimental import pallas as pl
from jax.experimental.pallas import tpu as pltpu

F32 = jnp.float32
BF16 = jnp.bfloat16

PAST_LEN = 8192
EPS = 1e-6
NEG_INF = -1e30
A_GROUPS = ((128, 1), (512, 4), (2048, 16))
A_HEAD_DIM = 128
A_BLOCK = 128
ATTN_UNROLL = 16
ATTN_MERGE_ROWS = 256
ATTN_SAMPLE_UNROLL = 16
NUM_BUCKETS = 32
MAX_DISTANCE = 2048
C_RG = 8.0
LRU_TIME_TILE = 512
RET_DK = 256
RET_DV = 512
RET_CHUNK = 128
ROPE_BASE = 10000.0
GLA_DK = 256
GLA_DV = 512
GLA_TAU = 16.0
GLA_CHUNK = 64
GLA_GROUP = 16
RET_GROUP = 8

LANES = 128
SUBLANES = 8
V7X_VMEM_LIMIT_BYTES = 56 * 1024 * 1024
V7X_VMEM_LIMIT_MATMUL_BYTES = 62 * 1024 * 1024

_ARB = pltpu.ARBITRARY


def _params(n_grid, vmem_limit_bytes=V7X_VMEM_LIMIT_BYTES):
    return pltpu.CompilerParams(dimension_semantics=(_ARB,) * n_grid,
                                vmem_limit_bytes=vmem_limit_bytes)


def _dot(a, b):
    return jnp.dot(a, b, preferred_element_type=F32)


def _dot_nt(a, b):
    return lax.dot_general(a, b, (((1,), (1,)), ((), ())), preferred_element_type=F32)


def _dot_tn(a, b):
    return lax.dot_general(a, b, (((0,), (0,)), ((), ())), preferred_element_type=F32)


def _sigmoid(x):
    return 0.5 * jnp.tanh(0.5 * x) + 0.5


def _silu(x):
    return x * _sigmoid(x)


def _softplus(x):
    return jnp.maximum(x, 0.0) + jnp.log1p(jnp.exp(-jnp.abs(x)))


def _rms(x, g):
    return x * lax.rsqrt(jnp.mean(x * x, axis=-1, keepdims=True) + EPS) * g


def _chunk_group(n_chunks, want):
    return want if n_chunks % want == 0 else 1


def _col_bcast(row, n):
    return jnp.broadcast_to(row, (n, n)).T


def _norm_kernel(x_ref, g_ref, h_ref):
    h_ref[...] = _rms(x_ref[...], g_ref[...]).astype(BF16)


def _resnorm_kernel(x_ref, y_ref, gpost_ref, gnext_ref, xo_ref, h_ref):
    xn = x_ref[...] + _rms(y_ref[...], gpost_ref[...])
    xo_ref[...] = xn
    h_ref[...] = _rms(xn, gnext_ref[...]).astype(BF16)


def _res_kernel(x_ref, y_ref, gpost_ref, xo_ref):
    xo_ref[...] = x_ref[...] + _rms(y_ref[...], gpost_ref[...])


def _row_tile(m):
    return min(m, 256)


def norm_cast(x, g):
    m, d = x.shape
    tm = _row_tile(m)
    return pl.pallas_call(
        _norm_kernel,
        grid=(m // tm,),
        in_specs=[pl.BlockSpec((tm, d), lambda i: (i, 0)),
                  pl.BlockSpec((1, d), lambda i: (0, 0))],
        out_specs=pl.BlockSpec((tm, d), lambda i: (i, 0)),
        out_shape=jax.ShapeDtypeStruct((m, d), BF16),
        compiler_params=_params(1),
        name="norm_cast",
    )(x, g.reshape(1, d))


def residual_norm(x, y, g_post, g_next):
    m, d = x.shape
    tm = _row_tile(m)
    row = pl.BlockSpec((tm, d), lambda i: (i, 0))
    vec = pl.BlockSpec((1, d), lambda i: (0, 0))
    if g_next is None:
        return pl.pallas_call(
            _res_kernel, grid=(m // tm,),
            in_specs=[row, row, vec], out_specs=row,
            out_shape=jax.ShapeDtypeStruct((m, d), F32),
            compiler_params=_params(1), name="residual",
        )(x, y, g_post.reshape(1, d)), None
    return pl.pallas_call(
        _resnorm_kernel, grid=(m // tm,),
        in_specs=[row, row, vec, vec], out_specs=[row, row],
        out_shape=[jax.ShapeDtypeStruct((m, d), F32), jax.ShapeDtypeStruct((m, d), BF16)],
        compiler_params=_params(1), name="residual_norm",
    )(x, y, g_post.reshape(1, d), g_next.reshape(1, d))


def _mm_kernel(x_ref, w_ref, xs_ref, o_ref, os_ref, *, transposed, tn):
    dot = _dot_nt if transposed else _dot
    k = x_ref.shape[1]
    kc = k // MM_K_CHUNKS

    def product(lhs_ref):
        acc = None
        for c in range(MM_K_CHUNKS):
            ks = slice(c * kc, (c + 1) * kc)
            w = (w_ref[:, ks] if transposed else w_ref[ks, :]).astype(BF16)
            part = dot(lhs_ref[:, ks], w)
            acc = part if acc is None else acc + part
        return acc

    o_ref[...] = product(x_ref)

    @pl.when(pl.program_id(0) == 0)
    def _():
        cols = pl.ds(pl.multiple_of(pl.program_id(1) * tn, tn), tn)
        os_ref[:, cols] = product(xs_ref)


MM_COL_TILE = 512
MM_K_CHUNKS = 4
MM_ROW_TILE = 2048
MM_ROW_TILE_LONG_K = 1024
MM_LONG_K = 4096


def matmul(x, xs, w, layer, col0=0, ncols=None, transposed=False):
    m, k = x.shape
    ms = xs.shape[0]
    n = w.shape[1] if transposed else w.shape[2]
    ncols = n - col0 if ncols is None else ncols
    tn = MM_COL_TILE
    assert col0 % tn == 0
    tm = min(m, MM_ROW_TILE if k <= MM_LONG_K else MM_ROW_TILE_LONG_K)
    assert m % tm == 0
    nj = pl.cdiv(ncols, tn)
    c0 = col0 // tn
    if transposed:
        w_spec = pl.BlockSpec((None, tn, k), lambda i, j: (layer, c0 + j, 0))
    else:
        w_spec = pl.BlockSpec((None, k, tn), lambda i, j: (layer, 0, c0 + j))
    out, out_s = pl.pallas_call(
        functools.partial(_mm_kernel, transposed=transposed, tn=tn),
        grid=(m // tm, nj),
        in_specs=[pl.BlockSpec((tm, k), lambda i, j: (i, 0)),
                  w_spec,
                  pl.BlockSpec((ms, k), lambda i, j: (0, 0))],
        out_specs=[pl.BlockSpec((tm, tn), lambda i, j: (i, j)),
                   pl.BlockSpec((ms, nj * tn), lambda i, j: (0, 0))],
        out_shape=[jax.ShapeDtypeStruct((m, ncols), F32), jax.ShapeDtypeStruct((ms, nj * tn), F32)],
        compiler_params=_params(2, V7X_VMEM_LIMIT_MATMUL_BYTES),
        name="proj_matmul",
    )(x, w, xs)
    return out, (out_s if nj * tn == ncols else out_s[:, :ncols])


def _t5_bucket(dist):
    n = np.asarray(dist, dtype=np.int64)
    max_exact = NUM_BUCKETS // 2
    ratio = np.log(np.maximum(n, 1) / max_exact) / np.log(MAX_DISTANCE / max_exact)
    large = np.minimum(max_exact + (ratio * (NUM_BUCKETS - max_exact)).astype(np.int64), NUM_BUCKETS - 1)
    return np.where(n < max_exact, n, large).astype(np.int32)


def _attn_prompt_kernel(*refs, dilations, seq, tails):
    ng = len(dilations)
    qkv = refs[:3 * ng]
    gate_ref, bias_ref, y_ref = refs[3 * ng:3 * ng + 3]
    tail_refs = refs[3 * ng + 3:-2]
    o_s, lse_s = refs[-2:]
    blk = A_BLOCK
    for n, (g, keep) in enumerate(tails):
        tail_refs[2 * n][...] = qkv[3 * g + 1][seq - keep:, :]
        tail_refs[2 * n + 1][...] = qkv[3 * g + 2][seq - keep:, :]
    scale = A_HEAD_DIM ** -0.5
    ii = lax.broadcasted_iota(jnp.int32, (blk, blk), 0)
    jj = lax.broadcasted_iota(jnp.int32, (blk, blk), 1)
    cur_ok = jj <= ii
    prev_ok = jj >= ii

    for g, dil in enumerate(dilations):
        q_ref, k_ref, v_ref = qkv[3 * g:3 * g + 3]
        nb = seq // (dil * blk)
        band = pltpu.roll(jnp.broadcast_to(bias_ref[g:g + 1, :], (blk, 2 * blk)), 0, axis=1,
                          stride=1, stride_axis=0)
        b_prev = band[:, :blk]
        b_cur = band[:, blk:]

        def rows_at(start, dil=dil):
            if dil > 1:
                return pl.ds(start, blk, stride=dil)
            return pl.ds(start if isinstance(start, int) else pl.multiple_of(start, blk), blk)

        def scores(idx, dil=dil, nb=nb, q_ref=q_ref, k_ref=k_ref, b_prev=b_prev, b_cur=b_cur,
                   rows_at=rows_at):
            r = idx % dil
            bi = idx // dil
            start = r + bi * (blk * dil)
            rows = rows_at(start)
            q = q_ref[rows, :].astype(BF16)
            s = [jnp.where(cur_ok, _dot_nt(q, k_ref[rows, :].astype(BF16)) * scale + b_cur, NEG_INF)]
            prows = None
            if nb > 1 and not (isinstance(bi, int) and bi == 0):
                if isinstance(bi, int):
                    prows = rows_at(start - blk * dil)
                    prev_mask = prev_ok
                else:
                    prows = rows_at(jnp.maximum(start - blk * dil, 0))
                    prev_mask = prev_ok & (bi > 0)
                s.append(jnp.where(prev_mask, _dot_nt(q, k_ref[prows, :].astype(BF16)) * scale + b_prev,
                                   NEG_INF))
            return rows, prows, s

        def softmax(s):
            m = jnp.max(functools.reduce(jnp.maximum, s), axis=-1, keepdims=True)
            p = [jnp.exp(x - m) for x in s]
            den = jnp.sum(functools.reduce(jnp.add, p), axis=-1, keepdims=True)
            return m, den, [x.astype(BF16) for x in p]

        def values(rows, prows, p, v_ref=v_ref):
            num = _dot(p[0], v_ref[rows, :].astype(BF16))
            if prows is not None:
                num = num + _dot(p[1], v_ref[prows, :].astype(BF16))
            return num

        def body(it, carry, g=g, scores=scores, softmax=softmax, values=values):
            blocks = [scores(it * ATTN_UNROLL + u) for u in range(ATTN_UNROLL)]
            probs = [softmax(s) for _, _, s in blocks]
            nums = [values(rows, prows, p) for (rows, prows, _), (_, _, p) in zip(blocks, probs)]
            for (rows, _, _), (m, den, _), num in zip(blocks, probs, nums):
                o_s[g, rows, :] = num / den
                lse_s[g, rows, :] = jnp.broadcast_to(m + jnp.log(den), (blk, A_HEAD_DIM))
            return carry

        if seq // blk == ATTN_UNROLL:
            body(0, 0)
        else:
            lax.fori_loop(0, seq // (blk * ATTN_UNROLL), body, 0)

    tr = ATTN_MERGE_ROWS

    def merge(c, carry):
        rows = pl.ds(pl.multiple_of(c * tr, tr), tr)
        lses = [lse_s[g, rows, :] for g in range(ng)]
        mx = functools.reduce(jnp.maximum, lses)
        num = jnp.zeros((tr, A_HEAD_DIM), F32)
        den = jnp.zeros((tr, A_HEAD_DIM), F32)
        for g in range(ng):
            w = jnp.exp(lses[g] - mx)
            num = num + w * o_s[g, rows, :]
            den = den + w
        y_ref[rows, :] = ((num / den) * _silu(gate_ref[rows, :])).astype(BF16)
        return carry

    lax.fori_loop(0, seq // tr, merge, 0)


def attn_prompt(qs, ks, vs, gate, bias_tab, n_heads):
    b, s, _ = gate[0].shape
    ng = len(A_GROUPS)
    dh = A_HEAD_DIM
    in_specs = []
    args = []
    units = [u for g in range(ng) for u in (qs[g], ks[g], vs[g])] + [gate]
    for arr, unit in units:
        in_specs.append(pl.BlockSpec((None, s, dh), lambda bi, hi, unit=unit: (bi, 0, unit * n_heads + hi)))
        args.append(arr)
    in_specs.append(pl.BlockSpec((None, ng, 2 * A_BLOCK), lambda bi, hi: (hi, 0, 0)))
    args.append(bias_tab)
    tails = tuple((g, window) for g, (window, _) in enumerate(A_GROUPS) if window < s)
    kern = functools.partial(_attn_prompt_kernel, dilations=tuple(d for _, d in A_GROUPS), seq=s,
                             tails=tails)
    out_specs = [pl.BlockSpec((None, s, dh), lambda bi, hi: (bi, 0, hi))]
    out_shape = [jax.ShapeDtypeStruct((b, s, n_heads * dh), BF16)]
    for _, keep in tails:
        out_specs += [pl.BlockSpec((None, keep, dh), lambda bi, hi: (bi, 0, hi))] * 2
        out_shape += [jax.ShapeDtypeStruct((b, keep, n_heads * dh), F32)] * 2
    return pl.pallas_call(
        kern,
        grid=(b, n_heads),
        in_specs=in_specs,
        out_specs=out_specs,
        out_shape=out_shape,
        scratch_shapes=[pltpu.VMEM((ng, s, dh), F32)] * 2,
        compiler_params=_params(2),
        name="attn_prompt",
    )(*args)


def _attn_sample_kernel(q_ref, kn_ref, vn_ref, gate_ref, *refs, n_heads, n_keys):
    ng = len(A_GROUPS)
    caches = refs[:2 * ng]
    bpast_ref, bself_ref, y_ref = refs[2 * ng:]
    dh = A_HEAD_DIM
    scale = dh ** -0.5
    lane = lax.broadcasted_iota(jnp.int32, (n_heads, n_keys), 1)
    parts = []
    for g in range(ng):
        kc_ref, vc_ref = caches[2 * g], caches[2 * g + 1]
        q = q_ref[g]
        vn = vn_ref[g]

        def logits(j, acc, kc_ref=kc_ref, q=q):
            col = jnp.sum(kc_ref[j] * q, axis=-1, keepdims=True)
            return jnp.where(lane == j, col, acc)

        lp = lax.fori_loop(0, n_keys, logits, jnp.zeros((n_heads, n_keys), F32), unroll=ATTN_SAMPLE_UNROLL)
        lp = lp * scale + bpast_ref[g]
        ls = jnp.sum(kn_ref[g] * q, axis=-1, keepdims=True) * scale + bself_ref[g]
        m = jnp.maximum(jnp.max(lp, axis=-1, keepdims=True), ls)
        p = jnp.exp(lp - m)
        ps = jnp.exp(ls - m)
        den = jnp.sum(p, axis=-1, keepdims=True) + ps

        def weighted(j, acc, vc_ref=vc_ref, p=p):
            pj = jnp.sum(jnp.where(lane == j, p, 0.0), axis=-1, keepdims=True)
            return acc + pj * vc_ref[j]

        num = lax.fori_loop(0, n_keys, weighted, ps * vn, unroll=ATTN_SAMPLE_UNROLL)
        parts.append((num, m, den))
    mx = functools.reduce(jnp.maximum, [p_[1] for p_ in parts])
    num = jnp.zeros((n_heads, dh), F32)
    den = jnp.zeros((n_heads, 1), F32)
    for num_g, m_g, den_g in parts:
        w = jnp.exp(m_g - mx)
        num = num + w * num_g
        den = den + w * den_g
    y_ref[...] = ((num / den) * _silu(gate_ref[...])).astype(BF16)


def attn_sample(qs, ks, vs, gate, k_caches, v_caches, bias_past, bias_self, n_heads):
    bs, width = gate.shape
    dh = A_HEAD_DIM
    ng = len(A_GROUPS)
    n_keys = A_BLOCK
    heads = lambda rows: jnp.stack(rows, axis=1).reshape(bs, ng, n_heads, dh)
    grp_spec = pl.BlockSpec((None, ng, n_heads, dh), lambda bi: (bi, 0, 0, 0))
    args = [heads(qs), heads(ks), heads(vs), gate.reshape(bs, n_heads, dh)]
    in_specs = [grp_spec, grp_spec, grp_spec, pl.BlockSpec((None, n_heads, dh), lambda bi: (bi, 0, 0))]
    for (_, dil), kc, vc in zip(A_GROUPS, k_caches, v_caches):
        for c in (kc, vc):
            args.append(c.reshape(bs, n_keys, dil, n_heads, dh))
            in_specs.append(pl.BlockSpec((None, n_keys, None, n_heads, dh), lambda bi: (bi, 0, 0, 0, 0)))
    args += [bias_past, bias_self]
    in_specs += [pl.BlockSpec(bias_past.shape, lambda bi: (0, 0, 0)),
                 pl.BlockSpec(bias_self.shape, lambda bi: (0, 0, 0))]
    y = pl.pallas_call(
        functools.partial(_attn_sample_kernel, n_heads=n_heads, n_keys=n_keys),
        grid=(bs,),
        in_specs=in_specs,
        out_specs=pl.BlockSpec((None, n_heads, dh), lambda bi: (bi, 0, 0)),
        out_shape=jax.ShapeDtypeStruct((bs, n_heads, dh), BF16),
        compiler_params=_params(1),
        name="attn_sample",
    )(*args)
    return y.reshape(bs, width)


def _lru_gates(xc, wa_ref, ba_ref, wx_ref, bx_ref, lam_ref):
    xb = xc.astype(BF16)
    r = _sigmoid(_dot(xb, wa_ref[...].astype(BF16)) + ba_ref[...])
    ig = _sigmoid(_dot(xb, wx_ref[...].astype(BF16)) + bx_ref[...])
    log_a = -C_RG * r * _softplus(-lam_ref[...])
    a = jnp.exp(log_a)
    mult = jnp.sqrt(-jnp.tanh(log_a) * (a * a + 1.0))
    return a, mult, ig


def _lru_prompt_kernel(x_ref, gate_ref, cw_ref, cb_ref, wa_ref, ba_ref, wx_ref, bx_ref, lam_ref,
                       y_ref, hl_ref, hc_ref, xp_ref, *, tt, conv_w):
    t = pl.program_id(2)

    @pl.when(t == 0)
    def _():
        hc_ref[...] = jnp.zeros_like(hc_ref)
        xp_ref[:SUBLANES, :] = jnp.zeros((SUBLANES, xp_ref.shape[1]), F32)

    x = x_ref[...]
    c = x.shape[1]
    xp_ref[SUBLANES:, :] = x
    xc = cb_ref[...] + cw_ref[conv_w - 1:conv_w, :] * x
    for k in range(1, conv_w):
        xc = xc + cw_ref[conv_w - 1 - k:conv_w - k, :] * xp_ref[SUBLANES - k:SUBLANES - k + tt, :]
    xp_ref[:SUBLANES, :] = x[tt - SUBLANES:]

    row = lax.broadcasted_iota(jnp.int32, (tt, c), 0)
    a, mult, ig = _lru_gates(xc, wa_ref, ba_ref, wx_ref, bx_ref, lam_ref)
    mult = jnp.where(row + t * tt == 0, 1.0, mult)
    bx = mult * ig * xc
    ng = tt // SUBLANES
    a = a.reshape(ng, SUBLANES, c)
    bx = bx.reshape(ng, SUBLANES, c)
    sub = lax.broadcasted_iota(jnp.int32, (ng, SUBLANES, c), 1)
    d = 1
    while d < SUBLANES:
        keep = sub >= d
        a_sh = jnp.where(keep, pltpu.roll(a, d, axis=1), 1.0)
        b_sh = jnp.where(keep, pltpu.roll(bx, d, axis=1), 0.0)
        bx = a * b_sh + bx
        a = a * a_sh
        d *= 2
    h = hc_ref[...]
    groups = []
    for g in range(ng):
        groups.append(a[g] * h + bx[g])
        h = groups[-1][SUBLANES - 1:]
    y_ref[...] = (jnp.concatenate(groups, axis=0) * _silu(gate_ref[...])).astype(BF16)
    hc_ref[...] = h
    hl_ref[...] = h


def lru_prompt(proj, conv_w, conv_b, wa, ba, wx, bx, lam):
    b, s, two_br = proj.shape
    br = two_br // 2
    nblk, bs_, _ = wa.shape
    cw = conv_w.shape[0]
    tt = min(s, LRU_TIME_TILE)
    vec = lambda a: a.reshape(1, br)
    vspec = pl.BlockSpec((1, bs_), lambda bi, ni, ti: (0, ni))
    wspec = pl.BlockSpec((None, bs_, bs_), lambda bi, ni, ti: (ni, 0, 0))
    y, hl = pl.pallas_call(
        functools.partial(_lru_prompt_kernel, tt=tt, conv_w=cw),
        grid=(b, nblk, s // tt),
        in_specs=[pl.BlockSpec((None, tt, bs_), lambda bi, ni, ti: (bi, ti, ni)),
                  pl.BlockSpec((None, tt, bs_), lambda bi, ni, ti: (bi, ti, nblk + ni)),
                  pl.BlockSpec((cw, bs_), lambda bi, ni, ti: (0, ni)),
                  vspec, wspec, vspec, wspec, vspec, vspec],
        out_specs=[pl.BlockSpec((None, tt, bs_), lambda bi, ni, ti: (bi, ti, ni)),
                   pl.BlockSpec((None, 1, bs_), lambda bi, ni, ti: (bi, 0, ni))],
        out_shape=[jax.ShapeDtypeStruct((b, s, br), BF16), jax.ShapeDtypeStruct((b, 1, br), F32)],
        scratch_shapes=[pltpu.VMEM((1, bs_), F32), pltpu.VMEM((SUBLANES + tt, bs_), F32)],
        compiler_params=_params(3),
        name="lru_prompt",
    )(proj, proj, conv_w, vec(conv_b), wa, vec(ba), wx, vec(bx), vec(lam))
    return y, hl.reshape(b, br)


def _lru_sample_kernel(x_ref, gate_ref, buf_ref, h0_ref, cw_ref, cb_ref, wa_ref, ba_ref, wx_ref, bx_ref,
                       lam_ref, y_ref, h_ref, *, conv_w):
    x = x_ref[...]
    xc = cb_ref[...] + cw_ref[conv_w - 1:conv_w, :] * x
    for w in range(conv_w - 1):
        xc = xc + cw_ref[w:w + 1, :] * buf_ref[:, w, :]
    a, mult, ig = _lru_gates(xc, wa_ref, ba_ref, wx_ref, bx_ref, lam_ref)
    h = a * h0_ref[...] + mult * ig * xc
    h_ref[...] = h
    y_ref[...] = (h * _silu(gate_ref[...])).astype(BF16)


def lru_sample(proj_s, conv_buf, h0, conv_w, conv_b, wa, ba, wx, bx, lam):
    bs, two_br = proj_s.shape
    br = two_br // 2
    nblk, bs_, _ = wa.shape
    cw = conv_w.shape[0]
    vec = lambda a: a.reshape(1, br)
    vspec = pl.BlockSpec((1, bs_), lambda ni: (0, ni))
    wspec = pl.BlockSpec((None, bs_, bs_), lambda ni: (ni, 0, 0))
    rspec = pl.BlockSpec((bs, bs_), lambda ni: (0, ni))
    return pl.pallas_call(
        functools.partial(_lru_sample_kernel, conv_w=cw),
        grid=(nblk,),
        in_specs=[rspec,
                  pl.BlockSpec((bs, bs_), lambda ni: (0, nblk + ni)),
                  pl.BlockSpec((bs, cw - 1, bs_), lambda ni: (0, 0, ni)),
                  rspec,
                  pl.BlockSpec((cw, bs_), lambda ni: (0, ni)),
                  vspec, wspec, vspec, wspec, vspec, vspec],
        out_specs=[rspec, rspec],
        out_shape=[jax.ShapeDtypeStruct((bs, br), BF16), jax.ShapeDtypeStruct((bs, br), F32)],
        compiler_params=_params(1),
        name="lru_sample",
    )(proj_s, proj_s, conv_buf, h0, conv_w, vec(conv_b), wa, vec(ba), wx, vec(bx), vec(lam))


def _rope(x, cos, sin):
    half = x.shape[-1] // 2
    x1, x2 = x[:, :half], x[:, half:]
    return jnp.concatenate([x1 * cos - x2 * sin, x1 * sin + x2 * cos], axis=-1)


def _groupnorm(o, gain):
    c = o - jnp.mean(o, axis=-1, keepdims=True)
    return c * lax.rsqrt(jnp.mean(c * c, axis=-1, keepdims=True) + EPS) * gain


def _ret_prompt_kernel(q_ref, k_ref, v_ref, gate_ref, cos_ref, sin_ref, lg_ref, gain_ref,
                       y_ref, s_ref, st_ref, *, chunk, n_chunks, group):
    lg = lg_ref[:, :1]
    idx = lax.broadcasted_iota(jnp.int32, (chunk, 1), 0).astype(F32)
    ii = lax.broadcasted_iota(jnp.int32, (chunk, chunk), 0)
    jj = lax.broadcasted_iota(jnp.int32, (chunk, chunk), 1)
    diff = (ii - jj).astype(F32)
    decay = jnp.where(diff >= 0, jnp.exp(diff * lg), 0.0)
    q_dec = jnp.exp((idx + 1.0) * lg)
    k_dec = jnp.exp((chunk - 1.0 - idx) * lg)
    chunk_dec = jnp.exp(chunk * lg)
    gain = gain_ref[...]
    st_ref[...] = jnp.zeros_like(st_ref)

    q_dec_rows = jnp.concatenate([q_dec] * group, axis=0)
    k_dec_rows = jnp.concatenate([k_dec] * group, axis=0)

    def body(it, carry):
        rows = pl.ds(pl.multiple_of(it * (group * chunk), group * chunk), group * chunk)
        cos = cos_ref[rows, :]
        sin = sin_ref[rows, :]
        qc = _rope(q_ref[rows, :], cos, sin)
        kc = _rope(k_ref[rows, :], cos, sin) * (RET_DK ** -0.5)
        vc = v_ref[rows, :].astype(BF16)
        qb = qc.astype(BF16)
        kb = kc.astype(BF16)
        qd = (qc * q_dec_rows).astype(BF16)
        kd = (kc * k_dec_rows).astype(BF16)
        sl = [slice(u * chunk, (u + 1) * chunk) for u in range(group)]
        intra, upd = [], []
        for u in range(group):
            scores = _dot_nt(qb[sl[u]], kb[sl[u]]) * decay
            intra.append(_dot(scores.astype(BF16), vc[sl[u]]))
            upd.append(_dot_tn(kd[sl[u]], vc[sl[u]]))
        st = st_ref[...]
        outs = []
        for u in range(group):
            outs.append(intra[u] + _dot(qd[sl[u]], st.astype(BF16)))
            st = chunk_dec * st + upd[u]
        st_ref[...] = st
        o = jnp.concatenate(outs, axis=0)
        y_ref[rows, :] = (_groupnorm(o, gain) * _silu(gate_ref[rows, :])).astype(BF16)
        return carry

    lax.fori_loop(0, n_chunks // group, body, 0)
    s_ref[...] = st_ref[...]


def _ret_log_gamma(n_heads):
    lg = np.log1p(-np.exp2(-5.0 - np.arange(n_heads, dtype=np.float32))).astype(np.float32)
    return jnp.asarray(np.broadcast_to(lg[:, None, None], (n_heads, 1, LANES)).copy())


def _rope_tables(pos):
    half = RET_DK // 2
    inv_freq = ROPE_BASE ** (-jnp.arange(half, dtype=F32) / half)
    ang = pos[:, None] * inv_freq[None, :]
    return jnp.cos(ang), jnp.sin(ang)


def ret_prompt(proj, gain, n_heads):
    b, s, _ = proj.shape
    dk, dv = RET_DK, RET_DV
    qk = n_heads * dk
    chunk = RET_CHUNK if s % RET_CHUNK == 0 else s
    cos, sin = _rope_tables(jnp.arange(s, dtype=F32))
    half = dk // 2
    koff = qk // dk
    voff = 2 * qk // dv
    goff = (2 * qk + n_heads * dv) // dv
    return pl.pallas_call(
        functools.partial(_ret_prompt_kernel, chunk=chunk, n_chunks=s // chunk,
                          group=_chunk_group(s // chunk, RET_GROUP)),
        grid=(b, n_heads),
        in_specs=[pl.BlockSpec((None, s, dk), lambda bi, hi: (bi, 0, hi)),
                  pl.BlockSpec((None, s, dk), lambda bi, hi: (bi, 0, koff + hi)),
                  pl.BlockSpec((None, s, dv), lambda bi, hi: (bi, 0, voff + hi)),
                  pl.BlockSpec((None, s, dv), lambda bi, hi: (bi, 0, goff + hi)),
                  pl.BlockSpec((s, half), lambda bi, hi: (0, 0)),
                  pl.BlockSpec((s, half), lambda bi, hi: (0, 0)),
                  pl.BlockSpec((None, 1, LANES), lambda bi, hi: (hi, 0, 0)),
                  pl.BlockSpec((1, dv), lambda bi, hi: (0, hi))],
        out_specs=[pl.BlockSpec((None, s, dv), lambda bi, hi: (bi, 0, hi)),
                   pl.BlockSpec((None, None, dk, dv), lambda bi, hi: (bi, hi, 0, 0))],
        out_shape=[jax.ShapeDtypeStruct((b, s, n_heads * dv), BF16),
                   jax.ShapeDtypeStruct((b, n_heads, dk, dv), F32)],
        scratch_shapes=[pltpu.VMEM((dk, dv), F32)],
        compiler_params=_params(2),
        name="ret_prompt",
    )(proj, proj, proj, proj, cos, sin, _ret_log_gamma(n_heads), gain.reshape(1, n_heads * dv))


def _ret_sample_kernel(q_ref, k_ref, v_ref, gate_ref, cos_ref, sin_ref, lg_ref, gain_ref, s0_ref,
                       y_ref, s_ref):
    dk, dv = RET_DK, RET_DV
    bs = q_ref.shape[0]
    gamma = jnp.exp(lg_ref[:, :1])
    cos = cos_ref[...]
    sin = sin_ref[...]
    q = _rope(q_ref[...], cos, sin)
    k = _rope(k_ref[...], cos, sin) * (dk ** -0.5)
    v = v_ref[...]
    o = jnp.sum(q * k, axis=-1, keepdims=True) * v
    qg = (q * gamma).astype(BF16)
    row = lax.broadcasted_iota(jnp.int32, (bs, dv), 0)
    for b in range(bs):
        s0 = s0_ref[b]
        o = o + jnp.where(row == b, _dot(qg, s0.astype(BF16)), 0.0)
        kcol = _col_bcast(k[b:b + 1], dk)
        for j in range(dv // dk):
            cols = slice(j * dk, (j + 1) * dk)
            s_ref[b, :, cols] = gamma * s0[:, cols] + kcol * v[b:b + 1, cols]
    y_ref[...] = (_groupnorm(o, gain_ref[...]) * _silu(gate_ref[...])).astype(BF16)


def ret_sample(proj_s, gain, s0, n_heads):
    bs = proj_s.shape[0]
    dk, dv = RET_DK, RET_DV
    qk = n_heads * dk
    cos, sin = _rope_tables(jnp.full((1,), PAST_LEN, F32))
    half = dk // 2
    koff = qk // dk
    voff = 2 * qk // dv
    goff = (2 * qk + n_heads * dv) // dv
    state_spec = pl.BlockSpec((bs, None, dk, dv), lambda hi: (0, hi, 0, 0))
    return pl.pallas_call(
        _ret_sample_kernel,
        grid=(n_heads,),
        in_specs=[pl.BlockSpec((bs, dk), lambda hi: (0, hi)),
                  pl.BlockSpec((bs, dk), lambda hi: (0, koff + hi)),
                  pl.BlockSpec((bs, dv), lambda hi: (0, voff + hi)),
                  pl.BlockSpec((bs, dv), lambda hi: (0, goff + hi)),
                  pl.BlockSpec((1, half), lambda hi: (0, 0)),
                  pl.BlockSpec((1, half), lambda hi: (0, 0)),
                  pl.BlockSpec((None, 1, LANES), lambda hi: (hi, 0, 0)),
                  pl.BlockSpec((1, dv), lambda hi: (0, hi)),
                  state_spec],
        out_specs=[pl.BlockSpec((bs, dv), lambda hi: (0, hi)), state_spec],
        out_shape=[jax.ShapeDtypeStruct((bs, n_heads * dv), BF16),
                   jax.ShapeDtypeStruct((bs, n_heads, dk, dv), F32)],
        compiler_params=_params(1),
        name="ret_sample",
    )(proj_s, proj_s, proj_s, proj_s, cos, sin, _ret_log_gamma(n_heads), gain.reshape(1, n_heads * dv), s0)


def _headnorm(o, gain):
    return o * lax.rsqrt(jnp.mean(o * o, axis=-1, keepdims=True) + EPS) * gain


def _gla_log_alpha(low_ref_val, gw_ref, gb_ref, rank):
    lane = lax.broadcasted_iota(jnp.int32, low_ref_val.shape, 1)
    low = jnp.where(lane < rank, low_ref_val, 0.0).astype(BF16)
    z = _dot(low, gw_ref[...].astype(BF16)) + gb_ref[...]
    return (jnp.minimum(z, 0.0) - jnp.log1p(jnp.exp(-jnp.abs(z)))) / GLA_TAU


def _gla_prompt_kernel(q_ref, k_ref, v_ref, gate_ref, low_ref, gw_ref, gb_ref, gain_ref,
                       y_ref, s_ref, st_ref, *, chunk, n_chunks, group, rank):
    dk, dv = GLA_DK, GLA_DV
    sub = lax.broadcasted_iota(jnp.int32, (group * chunk, dk), 0) % chunk
    ii = lax.broadcasted_iota(jnp.int32, (chunk, chunk), 0)
    jj = lax.broadcasted_iota(jnp.int32, (chunk, chunk), 1)
    causal = jj <= ii
    gain = gain_ref[...]
    st_ref[...] = jnp.zeros_like(st_ref)

    def body(it, carry):
        rows = pl.ds(pl.multiple_of(it * (group * chunk), group * chunk), group * chunk)
        bcum = _gla_log_alpha(low_ref[rows, :], gw_ref, gb_ref, rank)
        d = 1
        while d < chunk:
            bcum = bcum + jnp.where(sub >= d, pltpu.roll(bcum, d, axis=0), 0.0)
            d *= 2
        sl = [slice(u * chunk, (u + 1) * chunk) for u in range(group)]
        blast = [bcum[(u + 1) * chunk - 1:(u + 1) * chunk] for u in range(group)]
        blast_rows = jnp.concatenate([jnp.broadcast_to(b, (chunk, dk)) for b in blast], axis=0)
        kc = k_ref[rows, :]
        vc = v_ref[rows, :].astype(BF16)
        qe = (q_ref[rows, :] * (dk ** -0.5) * jnp.exp(bcum)).astype(BF16)
        ke = (kc * jnp.exp(-bcum)).astype(BF16)
        kd = (kc * jnp.exp(blast_rows - bcum)).astype(BF16)
        intra, upd, dec = [], [], []
        for u in range(group):
            scores = jnp.where(causal, _dot_nt(qe[sl[u]], ke[sl[u]]), 0.0)
            intra.append(_dot(scores.astype(BF16), vc[sl[u]]))
            upd.append(_dot_tn(kd[sl[u]], vc[sl[u]]))
            dec.append(_col_bcast(jnp.exp(blast[u]), dk))
        st = st_ref[...]
        outs = []
        for u in range(group):
            outs.append(intra[u] + _dot(qe[sl[u]], st.astype(BF16)))
            st = jnp.concatenate([dec[u] * st[:, j * dk:(j + 1) * dk] + upd[u][:, j * dk:(j + 1) * dk]
                                  for j in range(dv // dk)], axis=1)
        st_ref[...] = st
        o = jnp.concatenate(outs, axis=0)
        y_ref[rows, :] = (_headnorm(o, gain) * _silu(gate_ref[rows, :])).astype(BF16)
        return carry

    lax.fori_loop(0, n_chunks // group, body, 0)
    s_ref[...] = st_ref[...]


def _gla_offsets(n_heads):
    dk, dv = GLA_DK, GLA_DV
    qk = n_heads * dk
    koff = qk // dk
    voff = 2 * qk // dv
    goff = (2 * qk + n_heads * dv) // dv
    loff = (2 * qk + 2 * n_heads * dv) // LANES
    return koff, voff, goff, loff


def _pad_rank(gate_w):
    rank = gate_w.shape[0]
    return jnp.pad(gate_w, ((0, LANES - rank), (0, 0)))


def gla_prompt(proj, gate_w, gate_b, gain, n_heads):
    b, s, _ = proj.shape
    dk, dv = GLA_DK, GLA_DV
    rank = gate_w.shape[0]
    chunk = GLA_CHUNK if s % GLA_CHUNK == 0 else s
    koff, voff, goff, loff = _gla_offsets(n_heads)
    return pl.pallas_call(
        functools.partial(_gla_prompt_kernel, chunk=chunk, n_chunks=s // chunk,
                          group=_chunk_group(s // chunk, GLA_GROUP), rank=rank),
        grid=(b, n_heads),
        in_specs=[pl.BlockSpec((None, s, dk), lambda bi, hi: (bi, 0, hi)),
                  pl.BlockSpec((None, s, dk), lambda bi, hi: (bi, 0, koff + hi)),
                  pl.BlockSpec((None, s, dv), lambda bi, hi: (bi, 0, voff + hi)),
                  pl.BlockSpec((None, s, dv), lambda bi, hi: (bi, 0, goff + hi)),
                  pl.BlockSpec((None, s, LANES), lambda bi, hi: (bi, 0, loff)),
                  pl.BlockSpec((LANES, dk), lambda bi, hi: (0, hi)),
                  pl.BlockSpec((1, dk), lambda bi, hi: (0, hi)),
                  pl.BlockSpec((1, dv), lambda bi, hi: (0, hi))],
        out_specs=[pl.BlockSpec((None, s, dv), lambda bi, hi: (bi, 0, hi)),
                   pl.BlockSpec((None, None, dk, dv), lambda bi, hi: (bi, hi, 0, 0))],
        out_shape=[jax.ShapeDtypeStruct((b, s, n_heads * dv), BF16),
                   jax.ShapeDtypeStruct((b, n_heads, dk, dv), F32)],
        scratch_shapes=[pltpu.VMEM((dk, dv), F32)],
        compiler_params=_params(2),
        name="gla_prompt",
    )(proj, proj, proj, proj, proj, _pad_rank(gate_w), gate_b.reshape(1, -1), gain.reshape(1, -1))


def _gla_sample_kernel(q_ref, k_ref, v_ref, gate_ref, low_ref, gw_ref, gb_ref, gain_ref, s0_ref,
                       y_ref, s_ref, *, rank):
    dk, dv = GLA_DK, GLA_DV
    bs = q_ref.shape[0]
    g = _gla_log_alpha(low_ref[...], gw_ref, gb_ref, rank)
    q = q_ref[...] * (dk ** -0.5)
    k = k_ref[...]
    v = v_ref[...]
    qe = q * jnp.exp(g)
    ke = k * jnp.exp(-g)
    kd = k * jnp.exp(g - g)
    eg = jnp.exp(g)
    o = jnp.sum(qe * ke, axis=-1, keepdims=True) * v
    qeb = qe.astype(BF16)
    row = lax.broadcasted_iota(jnp.int32, (bs, dv), 0)
    for b in range(bs):
        s0 = s0_ref[b]
        o = o + jnp.where(row == b, _dot(qeb, s0.astype(BF16)), 0.0)
        dec = _col_bcast(eg[b:b + 1], dk)
        kcol = _col_bcast(kd[b:b + 1], dk)
        for j in range(dv // dk):
            cols = slice(j * dk, (j + 1) * dk)
            s_ref[b, :, cols] = dec * s0[:, cols] + kcol * v[b:b + 1, cols]
    y_ref[...] = (_headnorm(o, gain_ref[...]) * _silu(gate_ref[...])).astype(BF16)


def gla_sample(proj_s, gate_w, gate_b, gain, s0, n_heads):
    bs = proj_s.shape[0]
    dk, dv = GLA_DK, GLA_DV
    rank = gate_w.shape[0]
    koff, voff, goff, loff = _gla_offsets(n_heads)
    state_spec = pl.BlockSpec((bs, None, dk, dv), lambda hi: (0, hi, 0, 0))
    return pl.pallas_call(
        functools.partial(_gla_sample_kernel, rank=rank),
        grid=(n_heads,),
        in_specs=[pl.BlockSpec((bs, dk), lambda hi: (0, hi)),
                  pl.BlockSpec((bs, dk), lambda hi: (0, koff + hi)),
                  pl.BlockSpec((bs, dv), lambda hi: (0, voff + hi)),
                  pl.BlockSpec((bs, dv), lambda hi: (0, goff + hi)),
                  pl.BlockSpec((bs, LANES), lambda hi: (0, loff)),
                  pl.BlockSpec((LANES, dk), lambda hi: (0, hi)),
                  pl.BlockSpec((1, dk), lambda hi: (0, hi)),
                  pl.BlockSpec((1, dv), lambda hi: (0, hi)),
                  state_spec],
        out_specs=[pl.BlockSpec((bs, dv), lambda hi: (0, hi)), state_spec],
        out_shape=[jax.ShapeDtypeStruct((bs, n_heads * dv), BF16),
                   jax.ShapeDtypeStruct((bs, n_heads, dk, dv), F32)],
        compiler_params=_params(1),
        name="gla_sample",
    )(proj_s, proj_s, proj_s, proj_s, proj_s, _pad_rank(gate_w), gate_b.reshape(1, -1),
      gain.reshape(1, -1), s0)


def _attention_bias_tables(rel_bias, n_heads):
    blk = A_BLOCK
    ng = len(A_GROUPS)
    onehot = np.zeros((ng, 2 * blk, NUM_BUCKETS), np.float32)
    for g, (window, dil) in enumerate(A_GROUPS):
        n_keys = window // dil
        assert n_keys == blk, "one 128-key band per dilated stream is assumed"
        u = np.arange(n_keys + 1)
        onehot[g, u, _t5_bucket(dil * (n_keys - u))] = 1.0
    tab = jnp.einsum("gub,bgh->guh", jnp.asarray(onehot), rel_bias.astype(F32).reshape(NUM_BUCKETS, ng, n_heads),
                     precision=lax.Precision.HIGHEST)
    tab_t = tab.transpose(0, 2, 1)
    return tab.transpose(2, 0, 1), tab_t[:, :, :blk], tab_t[:, :, blk:blk + 1]


def kernel(x_prompt, x_sample, cache_k_w128, cache_v_w128, cache_k_w512, cache_v_w512, cache_k_w2048, cache_v_w2048, state_lru_h, state_lru_conv, state_ret, state_gla, norm_pre, norm_post, rel_bias, a_w_in, a_w_out, b_w_in, b_conv_w, b_conv_b, b_gate_a_w, b_gate_a_b, b_gate_x_w, b_gate_x_b, b_lambda, b_w_out, c_w_in, c_norm, c_w_out, d_w_in, d_gate_w, d_gate_b, d_norm, d_w_out):
    b, s, d = x_prompt.shape
    bs = x_sample.shape[0]
    assert x_sample.shape[1] == 1, "one new token per sequence"
    depth = norm_pre.shape[0]
    k_caches = (cache_k_w128, cache_k_w512, cache_k_w2048)
    v_caches = (cache_v_w128, cache_v_w512, cache_v_w2048)
    ng = len(A_GROUPS)
    for (window, dil), kc in zip(A_GROUPS, k_caches):
        assert kc.shape[2] == window and s % (dil * A_BLOCK) == 0

    xp = x_prompt.reshape(b * s, d)
    xs = x_sample.reshape(bs, d)
    hp = norm_cast(xp, norm_pre[0])
    hs = norm_cast(xs, norm_pre[0])

    kp_rows = [[] for _ in A_GROUPS]
    vp_rows = [[] for _ in A_GROUPS]
    ks_rows = [[] for _ in A_GROUPS]
    vs_rows = [[] for _ in A_GROUPS]
    lru_h_p, lru_h_s, lru_c_p, lru_c_s = [], [], [], []
    ret_p, ret_s, gla_p, gla_s = [], [], [], []

    for i in range(depth):
        kind, j = i % 4, i // 4
        if kind == 0:
            n_heads = a_w_out.shape[1] // A_HEAD_DIM
            width = n_heads * A_HEAD_DIM
            full = [min(window, s) == s for window, _ in A_GROUPS]
            cuts = {0, 3 * ng + 1}
            for which in (1, 2):
                for g in range(ng):
                    if full[g]:
                        cuts |= {which * ng + g, which * ng + g + 1}
            cuts = sorted(cuts)
            unit_p, unit_s = {}, {}
            for u0, u1 in zip(cuts[:-1], cuts[1:]):
                arr_p, arr_s = matmul(hp, hs, a_w_in, j, col0=u0 * width, ncols=(u1 - u0) * width)
                for u in range(u0, u1):
                    unit_p[u] = (arr_p.reshape(b, s, -1), u - u0)
                    unit_s[u] = arr_s[:, (u - u0) * width:(u - u0 + 1) * width]
            bias_tab, bias_past, bias_self = _attention_bias_tables(rel_bias, n_heads)
            yp, *tails = attn_prompt([unit_p[g] for g in range(ng)], [unit_p[ng + g] for g in range(ng)],
                                     [unit_p[2 * ng + g] for g in range(ng)], unit_p[3 * ng],
                                     bias_tab, n_heads)
            yp = yp.reshape(b * s, width)
            ys = attn_sample([unit_s[g] for g in range(ng)], [unit_s[ng + g] for g in range(ng)],
                             [unit_s[2 * ng + g] for g in range(ng)], unit_s[3 * ng],
                             [c[j] for c in k_caches], [c[j] for c in v_caches],
                             bias_past, bias_self, n_heads)
            for g in range(ng):
                if full[g]:
                    k_rows, v_rows = unit_p[ng + g][0], unit_p[2 * ng + g][0]
                else:
                    k_rows, v_rows = tails.pop(0), tails.pop(0)
                kp_rows[g].append(k_rows.reshape(b, -1, n_heads, A_HEAD_DIM))
                vp_rows[g].append(v_rows.reshape(b, -1, n_heads, A_HEAD_DIM))
                ks_rows[g].append(unit_s[ng + g].reshape(bs, 1, n_heads, A_HEAD_DIM))
                vs_rows[g].append(unit_s[2 * ng + g].reshape(bs, 1, n_heads, A_HEAD_DIM))
            w_out = a_w_out
        elif kind == 1:
            br = b_w_out.shape[1]
            proj, proj_s = matmul(hp, hs, b_w_in, j)
            prm = (b_conv_w[j], b_conv_b[j], b_gate_a_w[j], b_gate_a_b[j], b_gate_x_w[j], b_gate_x_b[j], b_lambda[j])
            proj3 = proj.reshape(b, s, -1)
            yp, h_last = lru_prompt(proj3, *prm)
            yp = yp.reshape(b * s, br)
            ys, h_new = lru_sample(proj_s, state_lru_conv[j], state_lru_h[j], *prm)
            cw = b_conv_w.shape[1]
            lru_h_p.append(h_last)
            lru_c_p.append(proj3[:, s - (cw - 1):, :br])
            lru_h_s.append(h_new)
            lru_c_s.append(jnp.concatenate([state_lru_conv[j], proj_s[:, None, :br]], axis=1)[:, 1:])
            w_out = b_w_out
        elif kind == 2:
            br = c_w_out.shape[1]
            n_heads = br // RET_DV
            proj, proj_s = matmul(hp, hs, c_w_in, j)
            yp, st = ret_prompt(proj.reshape(b, s, -1), c_norm[j], n_heads)
            yp = yp.reshape(b * s, br)
            ys, st_s = ret_sample(proj_s, c_norm[j], state_ret[j], n_heads)
            ret_p.append(st)
            ret_s.append(st_s)
            w_out = c_w_out
        else:
            br = d_w_out.shape[1]
            n_heads = br // GLA_DV
            proj, proj_s = matmul(hp, hs, jnp.swapaxes(d_w_in, 1, 2), j, transposed=True)
            yp, st = gla_prompt(proj.reshape(b, s, -1), d_gate_w[j], d_gate_b[j], d_norm[j], n_heads)
            yp = yp.reshape(b * s, br)
            ys, st_s = gla_sample(proj_s, d_gate_w[j], d_gate_b[j], d_norm[j], state_gla[j], n_heads)
            gla_p.append(st)
            gla_s.append(st_s)
            w_out = d_w_out
        op, os_ = matmul(yp, ys, w_out, j)
        g_next = norm_pre[i + 1] if i + 1 < depth else None
        xp, hp = residual_norm(xp, op, norm_post[i], g_next)
        xs, hs = residual_norm(xs, os_, norm_post[i], g_next)

    return (xp.reshape(b, s, d), xs.reshape(bs, 1, d),
            jnp.stack(kp_rows[0]), jnp.stack(ks_rows[0]), jnp.stack(vp_rows[0]), jnp.stack(vs_rows[0]),
            jnp.stack(kp_rows[1]), jnp.stack(ks_rows[1]), jnp.stack(vp_rows[1]), jnp.stack(vs_rows[1]),
            jnp.stack(kp_rows[2]), jnp.stack(ks_rows[2]), jnp.stack(vp_rows[2]), jnp.stack(vs_rows[2]),
            jnp.stack(lru_h_p), jnp.stack(lru_h_s), jnp.stack(lru_c_p), jnp.stack(lru_c_s),
            jnp.stack(ret_p), jnp.stack(ret_s), jnp.stack(gla_p), jnp.stack(gla_s))
```

```python
import functools

import numpy as np
import jax
import jax.numpy as jnp
from jax import lax
from jax.experimental import pallas as pl
from jax.experimental.pallas import tpu as pltpu

F32 = jnp.float32
BF16 = jnp.bfloat16

PAST_LEN = 8192
EPS = 1e-6
NEG_INF = -1e30
A_GROUPS = ((128, 1), (512, 4), (2048, 16))
A_HEAD_DIM = 128
A_BLOCK = 128
ATTN_UNROLL = 16
ATTN_MERGE_ROWS = 256
ATTN_SAMPLE_UNROLL = 16
NUM_BUCKETS = 32
MAX_DISTANCE = 2048
C_RG = 8.0
LRU_TIME_TILE = 512
RET_DK = 256
RET_DV = 512
RET_CHUNK = 128
ROPE_BASE = 10000.0
GLA_DK = 256
GLA_DV = 512
GLA_TAU = 16.0
GLA_CHUNK = 64
GLA_GROUP = 16
RET_GROUP = 8

LANES = 128
SUBLANES = 8
V7X_VMEM_LIMIT_BYTES = 56 * 1024 * 1024
V7X_VMEM_LIMIT_MATMUL_BYTES = 62 * 1024 * 1024

_ARB = pltpu.ARBITRARY


def _params(n_grid, vmem_limit_bytes=V7X_VMEM_LIMIT_BYTES):
    return pltpu.CompilerParams(dimension_semantics=(_ARB,) * n_grid,
                                vmem_limit_bytes=vmem_limit_bytes)


def _dot(a, b):
    return jnp.dot(a, b, preferred_element_type=F32)


def _dot_nt(a, b):
    return lax.dot_general(a, b, (((1,), (1,)), ((), ())), preferred_element_type=F32)


def _dot_tn(a, b):
    return lax.dot_general(a, b, (((0,), (0,)), ((), ())), preferred_element_type=F32)


def _sigmoid(x):
    return 0.5 * jnp.tanh(0.5 * x) + 0.5


def _silu(x):
    return x * _sigmoid(x)


def _softplus(x):
    return jnp.maximum(x, 0.0) + jnp.log1p(jnp.exp(-jnp.abs(x)))


def _rms(x, g):
    return x * lax.rsqrt(jnp.mean(x * x, axis=-1, keepdims=True) + EPS) * g


def _chunk_group(n_chunks, want):
    return want if n_chunks % want == 0 else 1


def _col_bcast(row, n):
    return jnp.broadcast_to(row, (n, n)).T


def _norm_kernel(x_ref, g_ref, h_ref):
    h_ref[...] = _rms(x_ref[...], g_ref[...]).astype(BF16)


def _resnorm_kernel(x_ref, y_ref, gpost_ref, gnext_ref, xo_ref, h_ref):
    xn = x_ref[...] + _rms(y_ref[...], gpost_ref[...])
    xo_ref[...] = xn
    h_ref[...] = _rms(xn, gnext_ref[...]).astype(BF16)


def _res_kernel(x_ref, y_ref, gpost_ref, xo_ref):
    xo_ref[...] = x_ref[...] + _rms(y_ref[...], gpost_ref[...])


def _row_tile(m):
    return min(m, 256)


def norm_cast(x, g):
    m, d = x.shape
    tm = _row_tile(m)
    return pl.pallas_call(
        _norm_kernel,
        grid=(m // tm,),
        in_specs=[pl.BlockSpec((tm, d), lambda i: (i, 0)),
                  pl.BlockSpec((1, d), lambda i: (0, 0))],
        out_specs=pl.BlockSpec((tm, d), lambda i: (i, 0)),
        out_shape=jax.ShapeDtypeStruct((m, d), BF16),
        compiler_params=_params(1),
        name="norm_cast",
    )(x, g.reshape(1, d))


def residual_norm(x, y, g_post, g_next):
    m, d = x.shape
    tm = _row_tile(m)
    row = pl.BlockSpec((tm, d), lambda i: (i, 0))
    vec = pl.BlockSpec((1, d), lambda i: (0, 0))
    if g_next is None:
        return pl.pallas_call(
            _res_kernel, grid=(m // tm,),
            in_specs=[row, row, vec], out_specs=row,
            out_shape=jax.ShapeDtypeStruct((m, d), F32),
            compiler_params=_params(1), name="residual",
        )(x, y, g_post.reshape(1, d)), None
    return pl.pallas_call(
        _resnorm_kernel, grid=(m // tm,),
        in_specs=[row, row, vec, vec], out_specs=[row, row],
        out_shape=[jax.ShapeDtypeStruct((m, d), F32), jax.ShapeDtypeStruct((m, d), BF16)],
        compiler_params=_params(1), name="residual_norm",
    )(x, y, g_post.reshape(1, d), g_next.reshape(1, d))


def _mm_kernel(x_ref, w_ref, xs_ref, o_ref, os_ref, *, transposed, tn):
    dot = _dot_nt if transposed else _dot
    k = x_ref.shape[1]
    kc = k // MM_K_CHUNKS

    def product(lhs_ref):
        acc = None
        for c in range(MM_K_CHUNKS):
            ks = slice(c * kc, (c + 1) * kc)
            w = (w_ref[:, ks] if transposed else w_ref[ks, :]).astype(BF16)
            part = dot(lhs_ref[:, ks], w)
            acc = part if acc is None else acc + part
        return acc

    o_ref[...] = product(x_ref)

    @pl.when(pl.program_id(0) == 0)
    def _():
        cols = pl.ds(pl.multiple_of(pl.program_id(1) * tn, tn), tn)
        os_ref[:, cols] = product(xs_ref)


MM_COL_TILE = 512
MM_K_CHUNKS = 4
MM_ROW_TILE = 2048
MM_ROW_TILE_LONG_K = 1024
MM_LONG_K = 4096


def matmul(x, xs, w, layer, units=None, unit=None, transposed=False):
    m, k = x.shape
    ms = xs.shape[0]
    n = w.shape[1] if transposed else w.shape[2]
    tn = MM_COL_TILE
    tm = min(m, MM_ROW_TILE if k <= MM_LONG_K else MM_ROW_TILE_LONG_K)
    assert m % tm == 0
    if units is None:
        ncols = n
        col_tile = lambda j: j
    else:
        assert unit % tn == 0
        per = unit // tn
        ncols = len(units) * unit

        def col_tile(j):
            pos = j // per
            src = units[0]
            for idx in range(1, len(units)):
                src = jnp.where(pos >= idx, units[idx], src)
            return src * per + j % per

    nj = pl.cdiv(ncols, tn)
    if transposed:
        w_spec = pl.BlockSpec((None, tn, k), lambda i, j: (layer, col_tile(j), 0))
    else:
        w_spec = pl.BlockSpec((None, k, tn), lambda i, j: (layer, 0, col_tile(j)))
    out, out_s = pl.pallas_call(
        functools.partial(_mm_kernel, transposed=transposed, tn=tn),
        grid=(m // tm, nj),
        in_specs=[pl.BlockSpec((tm, k), lambda i, j: (i, 0)),
                  w_spec,
                  pl.BlockSpec((ms, k), lambda i, j: (0, 0))],
        out_specs=[pl.BlockSpec((tm, tn), lambda i, j: (i, j)),
                   pl.BlockSpec((ms, nj * tn), lambda i, j: (0, 0))],
        out_shape=[jax.ShapeDtypeStruct((m, ncols), F32), jax.ShapeDtypeStruct((ms, nj * tn), F32)],
        compiler_params=_params(2, V7X_VMEM_LIMIT_MATMUL_BYTES),
        name="proj_matmul",
    )(x, w, xs)
    return out, (out_s if nj * tn == ncols else out_s[:, :ncols])


def _t5_bucket(dist):
    n = np.asarray(dist, dtype=np.int64)
    max_exact = NUM_BUCKETS // 2
    ratio = np.log(np.maximum(n, 1) / max_exact) / np.log(MAX_DISTANCE / max_exact)
    large = np.minimum(max_exact + (ratio * (NUM_BUCKETS - max_exact)).astype(np.int64), NUM_BUCKETS - 1)
    return np.where(n < max_exact, n, large).astype(np.int32)


def _attn_prompt_kernel(*refs, dilations, seq, tails):
    ng = len(dilations)
    qkv = refs[:3 * ng]
    gate_ref, bias_ref, y_ref = refs[3 * ng:3 * ng + 3]
    tail_refs = refs[3 * ng + 3:-2]
    o_s, lse_s = refs[-2:]
    blk = A_BLOCK
    for n, (g, keep) in enumerate(tails):
        tail_refs[2 * n][...] = qkv[3 * g + 1][seq - keep:, :]
        tail_refs[2 * n + 1][...] = qkv[3 * g + 2][seq - keep:, :]
    scale = A_HEAD_DIM ** -0.5
    ii = lax.broadcasted_iota(jnp.int32, (blk, blk), 0)
    jj = lax.broadcasted_iota(jnp.int32, (blk, blk), 1)
    cur_ok = jj <= ii
    prev_ok = jj >= ii

    for g, dil in enumerate(dilations):
        q_ref, k_ref, v_ref = qkv[3 * g:3 * g + 3]
        nb = seq // (dil * blk)
        band = pltpu.roll(jnp.broadcast_to(bias_ref[g:g + 1, :], (blk, 2 * blk)), 0, axis=1,
                          stride=1, stride_axis=0)
        b_prev = band[:, :blk]
        b_cur = band[:, blk:]

        def rows_at(start, dil=dil):
            if dil > 1:
                return pl.ds(start, blk, stride=dil)
            return pl.ds(start if isinstance(start, int) else pl.multiple_of(start, blk), blk)

        def scores(idx, dil=dil, nb=nb, q_ref=q_ref, k_ref=k_ref, b_prev=b_prev, b_cur=b_cur,
                   rows_at=rows_at):
            r = idx % dil
            bi = idx // dil
            start = r + bi * (blk * dil)
            rows = rows_at(start)
            q = q_ref[rows, :].astype(BF16)
            s = [jnp.where(cur_ok, _dot_nt(q, k_ref[rows, :].astype(BF16)) * scale + b_cur, NEG_INF)]
            prows = None
            if nb > 1 and not (isinstance(bi, int) and bi == 0):
                if isinstance(bi, int):
                    prows = rows_at(start - blk * dil)
                    prev_mask = prev_ok
                else:
                    prows = rows_at(jnp.maximum(start - blk * dil, 0))
                    prev_mask = prev_ok & (bi > 0)
                s.append(jnp.where(prev_mask, _dot_nt(q, k_ref[prows, :].astype(BF16)) * scale + b_prev,
                                   NEG_INF))
            return rows, prows, s

        def softmax(s):
            m = jnp.max(functools.reduce(jnp.maximum, s), axis=-1, keepdims=True)
            p = [jnp.exp(x - m) for x in s]
            den = jnp.sum(functools.reduce(jnp.add, p), axis=-1, keepdims=True)
            return m, den, [x.astype(BF16) for x in p]

        def values(rows, prows, p, v_ref=v_ref):
            num = _dot(p[0], v_ref[rows, :].astype(BF16))
            if prows is not None:
                num = num + _dot(p[1], v_ref[prows, :].astype(BF16))
            return num

        def body(it, carry, g=g, scores=scores, softmax=softmax, values=values):
            blocks = [scores(it * ATTN_UNROLL + u) for u in range(ATTN_UNROLL)]
            probs = [softmax(s) for _, _, s in blocks]
            nums = [values(rows, prows, p) for (rows, prows, _), (_, _, p) in zip(blocks, probs)]
            for (rows, _, _), (m, den, _), num in zip(blocks, probs, nums):
                o_s[g, rows, :] = num / den
                lse_s[g, rows, :] = jnp.broadcast_to(m + jnp.log(den), (blk, A_HEAD_DIM))
            return carry

        if seq // blk == ATTN_UNROLL:
            body(0, 0)
        else:
            lax.fori_loop(0, seq // (blk * ATTN_UNROLL), body, 0)

    tr = ATTN_MERGE_ROWS

    def merge(c, carry):
        rows = pl.ds(pl.multiple_of(c * tr, tr), tr)
        lses = [lse_s[g, rows, :] for g in range(ng)]
        mx = functools.reduce(jnp.maximum, lses)
        num = jnp.zeros((tr, A_HEAD_DIM), F32)
        den = jnp.zeros((tr, A_HEAD_DIM), F32)
        for g in range(ng):
            w = jnp.exp(lses[g] - mx)
            num = num + w * o_s[g, rows, :]
            den = den + w
        y_ref[rows, :] = ((num / den) * _silu(gate_ref[rows, :])).astype(BF16)
        return carry

    lax.fori_loop(0, seq // tr, merge, 0)


def attn_prompt(qs, ks, vs, gate, bias_tab, n_heads):
    b, s, _ = gate[0].shape
    ng = len(A_GROUPS)
    dh = A_HEAD_DIM
    in_specs = []
    args = []
    units = [u for g in range(ng) for u in (qs[g], ks[g], vs[g])] + [gate]
    for arr, unit in units:
        in_specs.append(pl.BlockSpec((None, s, dh), lambda bi, hi, unit=unit: (bi, 0, unit * n_heads + hi)))
        args.append(arr)
    in_specs.append(pl.BlockSpec((None, ng, 2 * A_BLOCK), lambda bi, hi: (hi, 0, 0)))
    args.append(bias_tab)
    tails = tuple((g, window) for g, (window, _) in enumerate(A_GROUPS) if window < s)
    kern = functools.partial(_attn_prompt_kernel, dilations=tuple(d for _, d in A_GROUPS), seq=s,
                             tails=tails)
    out_specs = [pl.BlockSpec((None, s, dh), lambda bi, hi: (bi, 0, hi))]
    out_shape = [jax.ShapeDtypeStruct((b, s, n_heads * dh), BF16)]
    for _, keep in tails:
        out_specs += [pl.BlockSpec((None, keep, dh), lambda bi, hi: (bi, 0, hi))] * 2
        out_shape += [jax.ShapeDtypeStruct((b, keep, n_heads * dh), F32)] * 2
    return pl.pallas_call(
        kern,
        grid=(b, n_heads),
        in_specs=in_specs,
        out_specs=out_specs,
        out_shape=out_shape,
        scratch_shapes=[pltpu.VMEM((ng, s, dh), F32)] * 2,
        compiler_params=_params(2),
        name="attn_prompt",
    )(*args)


def _attn_sample_kernel(q_ref, kn_ref, vn_ref, gate_ref, *refs, n_heads, n_keys):
    ng = len(A_GROUPS)
    caches = refs[:2 * ng]
    bpast_ref, bself_ref, y_ref = refs[2 * ng:]
    dh = A_HEAD_DIM
    scale = dh ** -0.5
    lane = lax.broadcasted_iota(jnp.int32, (n_heads, n_keys), 1)
    parts = []
    for g in range(ng):
        kc_ref, vc_ref = caches[2 * g], caches[2 * g + 1]
        q = q_ref[g]
        vn = vn_ref[g]

        def logits(j, acc, kc_ref=kc_ref, q=q):
            col = jnp.sum(kc_ref[j] * q, axis=-1, keepdims=True)
            return jnp.where(lane == j, col, acc)

        lp = lax.fori_loop(0, n_keys, logits, jnp.zeros((n_heads, n_keys), F32), unroll=ATTN_SAMPLE_UNROLL)
        lp = lp * scale + bpast_ref[g]
        ls = jnp.sum(kn_ref[g] * q, axis=-1, keepdims=True) * scale + bself_ref[g]
        m = jnp.maximum(jnp.max(lp, axis=-1, keepdims=True), ls)
        p = jnp.exp(lp - m)
        ps = jnp.exp(ls - m)
        den = jnp.sum(p, axis=-1, keepdims=True) + ps

        def weighted(j, acc, vc_ref=vc_ref, p=p):
            pj = jnp.sum(jnp.where(lane == j, p, 0.0), axis=-1, keepdims=True)
            return acc + pj * vc_ref[j]

        num = lax.fori_loop(0, n_keys, weighted, ps * vn, unroll=ATTN_SAMPLE_UNROLL)
        parts.append((num, m, den))
    mx = functools.reduce(jnp.maximum, [p_[1] for p_ in parts])
    num = jnp.zeros((n_heads, dh), F32)
    den = jnp.zeros((n_heads, 1), F32)
    for num_g, m_g, den_g in parts:
        w = jnp.exp(m_g - mx)
        num = num + w * num_g
        den = den + w * den_g
    y_ref[...] = ((num / den) * _silu(gate_ref[...])).astype(BF16)


def attn_sample(qs, ks, vs, gate, k_caches, v_caches, bias_past, bias_self, n_heads):
    bs, width = gate.shape
    dh = A_HEAD_DIM
    ng = len(A_GROUPS)
    n_keys = A_BLOCK
    heads = lambda rows: jnp.stack(rows, axis=1).reshape(bs, ng, n_heads, dh)
    grp_spec = pl.BlockSpec((None, ng, n_heads, dh), lambda bi: (bi, 0, 0, 0))
    args = [heads(qs), heads(ks), heads(vs), gate.reshape(bs, n_heads, dh)]
    in_specs = [grp_spec, grp_spec, grp_spec, pl.BlockSpec((None, n_heads, dh), lambda bi: (bi, 0, 0))]
    for (_, dil), kc, vc in zip(A_GROUPS, k_caches, v_caches):
        for c in (kc, vc):
            args.append(c.reshape(bs, n_keys, dil, n_heads, dh))
            in_specs.append(pl.BlockSpec((None, n_keys, None, n_heads, dh), lambda bi: (bi, 0, 0, 0, 0)))
    args += [bias_past, bias_self]
    in_specs += [pl.BlockSpec(bias_past.shape, lambda bi: (0, 0, 0)),
                 pl.BlockSpec(bias_self.shape, lambda bi: (0, 0, 0))]
    y = pl.pallas_call(
        functools.partial(_attn_sample_kernel, n_heads=n_heads, n_keys=n_keys),
        grid=(bs,),
        in_specs=in_specs,
        out_specs=pl.BlockSpec((None, n_heads, dh), lambda bi: (bi, 0, 0)),
        out_shape=jax.ShapeDtypeStruct((bs, n_heads, dh), BF16),
        compiler_params=_params(1),
        name="attn_sample",
    )(*args)
    return y.reshape(bs, width)


def _lru_gates(xc, wa_ref, ba_ref, wx_ref, bx_ref, lam_ref):
    xb = xc.astype(BF16)
    r = _sigmoid(_dot(xb, wa_ref[...].astype(BF16)) + ba_ref[...])
    ig = _sigmoid(_dot(xb, wx_ref[...].astype(BF16)) + bx_ref[...])
    log_a = -C_RG * r * _softplus(-lam_ref[...])
    a = jnp.exp(log_a)
    mult = jnp.sqrt(-jnp.tanh(log_a) * (a * a + 1.0))
    return a, mult, ig


def _lru_prompt_kernel(x_ref, gate_ref, cw_ref, cb_ref, wa_ref, ba_ref, wx_ref, bx_ref, lam_ref,
                       y_ref, hl_ref, hc_ref, xp_ref, *, tt, conv_w):
    t = pl.program_id(2)

    @pl.when(t == 0)
    def _():
        hc_ref[...] = jnp.zeros_like(hc_ref)
        xp_ref[:SUBLANES, :] = jnp.zeros((SUBLANES, xp_ref.shape[1]), F32)

    x = x_ref[...]
    c = x.shape[1]
    xp_ref[SUBLANES:, :] = x
    xc = cb_ref[...] + cw_ref[conv_w - 1:conv_w, :] * x
    for k in range(1, conv_w):
        xc = xc + cw_ref[conv_w - 1 - k:conv_w - k, :] * xp_ref[SUBLANES - k:SUBLANES - k + tt, :]
    xp_ref[:SUBLANES, :] = x[tt - SUBLANES:]

    row = lax.broadcasted_iota(jnp.int32, (tt, c), 0)
    a, mult, ig = _lru_gates(xc, wa_ref, ba_ref, wx_ref, bx_ref, lam_ref)
    mult = jnp.where(row + t * tt == 0, 1.0, mult)
    bx = mult * ig * xc
    ng = tt // SUBLANES
    a = a.reshape(ng, SUBLANES, c)
    bx = bx.reshape(ng, SUBLANES, c)
    sub = lax.broadcasted_iota(jnp.int32, (ng, SUBLANES, c), 1)
    d = 1
    while d < SUBLANES:
        keep = sub >= d
        a_sh = jnp.where(keep, pltpu.roll(a, d, axis=1), 1.0)
        b_sh = jnp.where(keep, pltpu.roll(bx, d, axis=1), 0.0)
        bx = a * b_sh + bx
        a = a * a_sh
        d *= 2
    h = hc_ref[...]
    groups = []
    for g in range(ng):
        groups.append(a[g] * h + bx[g])
        h = groups[-1][SUBLANES - 1:]
    y_ref[...] = (jnp.concatenate(groups, axis=0) * _silu(gate_ref[...])).astype(BF16)
    hc_ref[...] = h
    hl_ref[...] = h


def lru_prompt(proj, conv_w, conv_b, wa, ba, wx, bx, lam):
    b, s, two_br = proj.shape
    br = two_br // 2
    nblk, bs_, _ = wa.shape
    cw = conv_w.shape[0]
    tt = min(s, LRU_TIME_TILE)
    vec = lambda a: a.reshape(1, br)
    vspec = pl.BlockSpec((1, bs_), lambda bi, ni, ti: (0, ni))
    wspec = pl.BlockSpec((None, bs_, bs_), lambda bi, ni, ti: (ni, 0, 0))
    y, hl = pl.pallas_call(
        functools.partial(_lru_prompt_kernel, tt=tt, conv_w=cw),
        grid=(b, nblk, s // tt),
        in_specs=[pl.BlockSpec((None, tt, bs_), lambda bi, ni, ti: (bi, ti, ni)),
                  pl.BlockSpec((None, tt, bs_), lambda bi, ni, ti: (bi, ti, nblk + ni)),
                  pl.BlockSpec((cw, bs_), lambda bi, ni, ti: (0, ni)),
                  vspec, wspec, vspec, wspec, vspec, vspec],
        out_specs=[pl.BlockSpec((None, tt, bs_), lambda bi, ni, ti: (bi, ti, ni)),
                   pl.BlockSpec((None, 1, bs_), lambda bi, ni, ti: (bi, 0, ni))],
        out_shape=[jax.ShapeDtypeStruct((b, s, br), BF16), jax.ShapeDtypeStruct((b, 1, br), F32)],
        scratch_shapes=[pltpu.VMEM((1, bs_), F32), pltpu.VMEM((SUBLANES + tt, bs_), F32)],
        compiler_params=_params(3),
        name="lru_prompt",
    )(proj, proj, conv_w, vec(conv_b), wa, vec(ba), wx, vec(bx), vec(lam))
    return y, hl.reshape(b, br)


def _lru_sample_kernel(x_ref, gate_ref, buf_ref, h0_ref, cw_ref, cb_ref, wa_ref, ba_ref, wx_ref, bx_ref,
                       lam_ref, y_ref, h_ref, *, conv_w):
    x = x_ref[...]
    xc = cb_ref[...] + cw_ref[conv_w - 1:conv_w, :] * x
    for w in range(conv_w - 1):
        xc = xc + cw_ref[w:w + 1, :] * buf_ref[:, w, :]
    a, mult, ig = _lru_gates(xc, wa_ref, ba_ref, wx_ref, bx_ref, lam_ref)
    h = a * h0_ref[...] + mult * ig * xc
    h_ref[...] = h
    y_ref[...] = (h * _silu(gate_ref[...])).astype(BF16)


def lru_sample(proj_s, conv_buf, h0, conv_w, conv_b, wa, ba, wx, bx, lam):
    bs, two_br = proj_s.shape
    br = two_br // 2
    nblk, bs_, _ = wa.shape
    cw = conv_w.shape[0]
    vec = lambda a: a.reshape(1, br)
    vspec = pl.BlockSpec((1, bs_), lambda ni: (0, ni))
    wspec = pl.BlockSpec((None, bs_, bs_), lambda ni: (ni, 0, 0))
    rspec = pl.BlockSpec((bs, bs_), lambda ni: (0, ni))
    return pl.pallas_call(
        functools.partial(_lru_sample_kernel, conv_w=cw),
        grid=(nblk,),
        in_specs=[rspec,
                  pl.BlockSpec((bs, bs_), lambda ni: (0, nblk + ni)),
                  pl.BlockSpec((bs, cw - 1, bs_), lambda ni: (0, 0, ni)),
                  rspec,
                  pl.BlockSpec((cw, bs_), lambda ni: (0, ni)),
                  vspec, wspec, vspec, wspec, vspec, vspec],
        out_specs=[rspec, rspec],
        out_shape=[jax.ShapeDtypeStruct((bs, br), BF16), jax.ShapeDtypeStruct((bs, br), F32)],
        compiler_params=_params(1),
        name="lru_sample",
    )(proj_s, proj_s, conv_buf, h0, conv_w, vec(conv_b), wa, vec(ba), wx, vec(bx), vec(lam))


def _rope(x, cos, sin):
    half = x.shape[-1] // 2
    x1, x2 = x[:, :half], x[:, half:]
    return jnp.concatenate([x1 * cos - x2 * sin, x1 * sin + x2 * cos], axis=-1)


def _groupnorm(o, gain):
    c = o - jnp.mean(o, axis=-1, keepdims=True)
    return c * lax.rsqrt(jnp.mean(c * c, axis=-1, keepdims=True) + EPS) * gain


def _ret_prompt_kernel(q_ref, k_ref, v_ref, gate_ref, cos_ref, sin_ref, lg_ref, gain_ref,
                       y_ref, s_ref, st_ref, *, chunk, n_chunks, group):
    lg = lg_ref[:, :1]
    idx = lax.broadcasted_iota(jnp.int32, (chunk, 1), 0).astype(F32)
    ii = lax.broadcasted_iota(jnp.int32, (chunk, chunk), 0)
    jj = lax.broadcasted_iota(jnp.int32, (chunk, chunk), 1)
    diff = (ii - jj).astype(F32)
    decay = jnp.where(diff >= 0, jnp.exp(diff * lg), 0.0)
    q_dec = jnp.exp((idx + 1.0) * lg)
    k_dec = jnp.exp((chunk - 1.0 - idx) * lg)
    chunk_dec = jnp.exp(chunk * lg)
    gain = gain_ref[...]
    st_ref[...] = jnp.zeros_like(st_ref)

    q_dec_rows = jnp.concatenate([q_dec] * group, axis=0)
    k_dec_rows = jnp.concatenate([k_dec] * group, axis=0)

    def body(it, carry):
        rows = pl.ds(pl.multiple_of(it * (group * chunk), group * chunk), group * chunk)
        cos = cos_ref[rows, :]
        sin = sin_ref[rows, :]
        qc = _rope(q_ref[rows, :], cos, sin)
        kc = _rope(k_ref[rows, :], cos, sin) * (RET_DK ** -0.5)
        vc = v_ref[rows, :].astype(BF16)
        qb = qc.astype(BF16)
        kb = kc.astype(BF16)
        qd = (qc * q_dec_rows).astype(BF16)
        kd = (kc * k_dec_rows).astype(BF16)
        sl = [slice(u * chunk, (u + 1) * chunk) for u in range(group)]
        intra, upd = [], []
        for u in range(group):
            scores = _dot_nt(qb[sl[u]], kb[sl[u]]) * decay
            intra.append(_dot(scores.astype(BF16), vc[sl[u]]))
            upd.append(_dot_tn(kd[sl[u]], vc[sl[u]]))
        st = st_ref[...]
        outs = []
        for u in range(group):
            outs.append(intra[u] + _dot(qd[sl[u]], st.astype(BF16)))
            st = chunk_dec * st + upd[u]
        st_ref[...] = st
        o = jnp.concatenate(outs, axis=0)
        y_ref[rows, :] = (_groupnorm(o, gain) * _silu(gate_ref[rows, :])).astype(BF16)
        return carry

    lax.fori_loop(0, n_chunks // group, body, 0)
    s_ref[...] = st_ref[...]


def _ret_log_gamma(n_heads):
    lg = np.log1p(-np.exp2(-5.0 - np.arange(n_heads, dtype=np.float32))).astype(np.float32)
    return jnp.asarray(np.broadcast_to(lg[:, None, None], (n_heads, 1, LANES)).copy())


def _rope_tables(pos):
    half = RET_DK // 2
    inv_freq = ROPE_BASE ** (-jnp.arange(half, dtype=F32) / half)
    ang = pos[:, None] * inv_freq[None, :]
    return jnp.cos(ang), jnp.sin(ang)


def ret_prompt(proj, gain, n_heads):
    b, s, _ = proj.shape
    dk, dv = RET_DK, RET_DV
    qk = n_heads * dk
    chunk = RET_CHUNK if s % RET_CHUNK == 0 else s
    cos, sin = _rope_tables(jnp.arange(s, dtype=F32))
    half = dk // 2
    koff = qk // dk
    voff = 2 * qk // dv
    goff = (2 * qk + n_heads * dv) // dv
    return pl.pallas_call(
        functools.partial(_ret_prompt_kernel, chunk=chunk, n_chunks=s // chunk,
                          group=_chunk_group(s // chunk, RET_GROUP)),
        grid=(b, n_heads),
        in_specs=[pl.BlockSpec((None, s, dk), lambda bi, hi: (bi, 0, hi)),
                  pl.BlockSpec((None, s, dk), lambda bi, hi: (bi, 0, koff + hi)),
                  pl.BlockSpec((None, s, dv), lambda bi, hi: (bi, 0, voff + hi)),
                  pl.BlockSpec((None, s, dv), lambda bi, hi: (bi, 0, goff + hi)),
                  pl.BlockSpec((s, half), lambda bi, hi: (0, 0)),
                  pl.BlockSpec((s, half), lambda bi, hi: (0, 0)),
                  pl.BlockSpec((None, 1, LANES), lambda bi, hi: (hi, 0, 0)),
                  pl.BlockSpec((1, dv), lambda bi, hi: (0, hi))],
        out_specs=[pl.BlockSpec((None, s, dv), lambda bi, hi: (bi, 0, hi)),
                   pl.BlockSpec((None, None, dk, dv), lambda bi, hi: (bi, hi, 0, 0))],
        out_shape=[jax.ShapeDtypeStruct((b, s, n_heads * dv), BF16),
                   jax.ShapeDtypeStruct((b, n_heads, dk, dv), F32)],
        scratch_shapes=[pltpu.VMEM((dk, dv), F32)],
        compiler_params=_params(2),
        name="ret_prompt",
    )(proj, proj, proj, proj, cos, sin, _ret_log_gamma(n_heads), gain.reshape(1, n_heads * dv))


def _ret_sample_kernel(q_ref, k_ref, v_ref, gate_ref, cos_ref, sin_ref, lg_ref, gain_ref, s0_ref,
                       y_ref, s_ref):
    dk, dv = RET_DK, RET_DV
    bs = q_ref.shape[0]
    gamma = jnp.exp(lg_ref[:, :1])
    cos = cos_ref[...]
    sin = sin_ref[...]
    q = _rope(q_ref[...], cos, sin)
    k = _rope(k_ref[...], cos, sin) * (dk ** -0.5)
    v = v_ref[...]
    o = jnp.sum(q * k, axis=-1, keepdims=True) * v
    qg = (q * gamma).astype(BF16)
    row = lax.broadcasted_iota(jnp.int32, (bs, dv), 0)
    for b in range(bs):
        s0 = s0_ref[b]
        o = o + jnp.where(row == b, _dot(qg, s0.astype(BF16)), 0.0)
        kcol = _col_bcast(k[b:b + 1], dk)
        for j in range(dv // dk):
            cols = slice(j * dk, (j + 1) * dk)
            s_ref[b, :, cols] = gamma * s0[:, cols] + kcol * v[b:b + 1, cols]
    y_ref[...] = (_groupnorm(o, gain_ref[...]) * _silu(gate_ref[...])).astype(BF16)


def ret_sample(proj_s, gain, s0, n_heads):
    bs = proj_s.shape[0]
    dk, dv = RET_DK, RET_DV
    qk = n_heads * dk
    cos, sin = _rope_tables(jnp.full((1,), PAST_LEN, F32))
    half = dk // 2
    koff = qk // dk
    voff = 2 * qk // dv
    goff = (2 * qk + n_heads * dv) // dv
    state_spec = pl.BlockSpec((bs, None, dk, dv), lambda hi: (0, hi, 0, 0))
    return pl.pallas_call(
        _ret_sample_kernel,
        grid=(n_heads,),
        in_specs=[pl.BlockSpec((bs, dk), lambda hi: (0, hi)),
                  pl.BlockSpec((bs, dk), lambda hi: (0, koff + hi)),
                  pl.BlockSpec((bs, dv), lambda hi: (0, voff + hi)),
                  pl.BlockSpec((bs, dv), lambda hi: (0, goff + hi)),
                  pl.BlockSpec((1, half), lambda hi: (0, 0)),
                  pl.BlockSpec((1, half), lambda hi: (0, 0)),
                  pl.BlockSpec((None, 1, LANES), lambda hi: (hi, 0, 0)),
                  pl.BlockSpec((1, dv), lambda hi: (0, hi)),
                  state_spec],
        out_specs=[pl.BlockSpec((bs, dv), lambda hi: (0, hi)), state_spec],
        out_shape=[jax.ShapeDtypeStruct((bs, n_heads * dv), BF16),
                   jax.ShapeDtypeStruct((bs, n_heads, dk, dv), F32)],
        compiler_params=_params(1),
        name="ret_sample",
    )(proj_s, proj_s, proj_s, proj_s, cos, sin, _ret_log_gamma(n_heads), gain.reshape(1, n_heads * dv), s0)


def _headnorm(o, gain):
    return o * lax.rsqrt(jnp.mean(o * o, axis=-1, keepdims=True) + EPS) * gain


def _gla_log_alpha(low_ref_val, gw_ref, gb_ref, rank):
    lane = lax.broadcasted_iota(jnp.int32, low_ref_val.shape, 1)
    low = jnp.where(lane < rank, low_ref_val, 0.0).astype(BF16)
    z = _dot(low, gw_ref[...].astype(BF16)) + gb_ref[...]
    return (jnp.minimum(z, 0.0) - jnp.log1p(jnp.exp(-jnp.abs(z)))) / GLA_TAU


def _gla_prompt_kernel(q_ref, k_ref, v_ref, gate_ref, low_ref, gw_ref, gb_ref, gain_ref,
                       y_ref, s_ref, st_ref, *, chunk, n_chunks, group, rank):
    dk, dv = GLA_DK, GLA_DV
    sub = lax.broadcasted_iota(jnp.int32, (group * chunk, dk), 0) % chunk
    ii = lax.broadcasted_iota(jnp.int32, (chunk, chunk), 0)
    jj = lax.broadcasted_iota(jnp.int32, (chunk, chunk), 1)
    causal = jj <= ii
    gain = gain_ref[...]
    st_ref[...] = jnp.zeros_like(st_ref)

    def body(it, carry):
        rows = pl.ds(pl.multiple_of(it * (group * chunk), group * chunk), group * chunk)
        bcum = _gla_log_alpha(low_ref[rows, :], gw_ref, gb_ref, rank)
        d = 1
        while d < chunk:
            bcum = bcum + jnp.where(sub >= d, pltpu.roll(bcum, d, axis=0), 0.0)
            d *= 2
        sl = [slice(u * chunk, (u + 1) * chunk) for u in range(group)]
        blast = [bcum[(u + 1) * chunk - 1:(u + 1) * chunk] for u in range(group)]
        blast_rows = jnp.concatenate([jnp.broadcast_to(b, (chunk, dk)) for b in blast], axis=0)
        kc = k_ref[rows, :]
        vc = v_ref[rows, :].astype(BF16)
        qe = (q_ref[rows, :] * (dk ** -0.5) * jnp.exp(bcum)).astype(BF16)
        ke = (kc * jnp.exp(-bcum)).astype(BF16)
        kd = (kc * jnp.exp(blast_rows - bcum)).astype(BF16)
        intra, upd, dec = [], [], []
        for u in range(group):
            scores = jnp.where(causal, _dot_nt(qe[sl[u]], ke[sl[u]]), 0.0)
            intra.append(_dot(scores.astype(BF16), vc[sl[u]]))
            upd.append(_dot_tn(kd[sl[u]], vc[sl[u]]))
            dec.append(_col_bcast(jnp.exp(blast[u]), dk))
        st = st_ref[...]
        outs = []
        for u in range(group):
            outs.append(intra[u] + _dot(qe[sl[u]], st.astype(BF16)))
            st = jnp.concatenate([dec[u] * st[:, j * dk:(j + 1) * dk] + upd[u][:, j * dk:(j + 1) * dk]
                                  for j in range(dv // dk)], axis=1)
        st_ref[...] = st
        o = jnp.concatenate(outs, axis=0)
        y_ref[rows, :] = (_headnorm(o, gain) * _silu(gate_ref[rows, :])).astype(BF16)
        return carry

    lax.fori_loop(0, n_chunks // group, body, 0)
    s_ref[...] = st_ref[...]


def _gla_offsets(n_heads):
    dk, dv = GLA_DK, GLA_DV
    qk = n_heads * dk
    koff = qk // dk
    voff = 2 * qk // dv
    goff = (2 * qk + n_heads * dv) // dv
    loff = (2 * qk + 2 * n_heads * dv) // LANES
    return koff, voff, goff, loff


def _pad_rank(gate_w):
    rank = gate_w.shape[0]
    return jnp.pad(gate_w, ((0, LANES - rank), (0, 0)))


def gla_prompt(proj, gate_w, gate_b, gain, n_heads):
    b, s, _ = proj.shape
    dk, dv = GLA_DK, GLA_DV
    rank = gate_w.shape[0]
    chunk = GLA_CHUNK if s % GLA_CHUNK == 0 else s
    koff, voff, goff, loff = _gla_offsets(n_heads)
    return pl.pallas_call(
        functools.partial(_gla_prompt_kernel, chunk=chunk, n_chunks=s // chunk,
                          group=_chunk_group(s // chunk, GLA_GROUP), rank=rank),
        grid=(b, n_heads),
        in_specs=[pl.BlockSpec((None, s, dk), lambda bi, hi: (bi, 0, hi)),
                  pl.BlockSpec((None, s, dk), lambda bi, hi: (bi, 0, koff + hi)),
                  pl.BlockSpec((None, s, dv), lambda bi, hi: (bi, 0, voff + hi)),
                  pl.BlockSpec((None, s, dv), lambda bi, hi: (bi, 0, goff + hi)),
                  pl.BlockSpec((None, s, LANES), lambda bi, hi: (bi, 0, loff)),
                  pl.BlockSpec((LANES, dk), lambda bi, hi: (0, hi)),
                  pl.BlockSpec((1, dk), lambda bi, hi: (0, hi)),
                  pl.BlockSpec((1, dv), lambda bi, hi: (0, hi))],
        out_specs=[pl.BlockSpec((None, s, dv), lambda bi, hi: (bi, 0, hi)),
                   pl.BlockSpec((None, None, dk, dv), lambda bi, hi: (bi, hi, 0, 0))],
        out_shape=[jax.ShapeDtypeStruct((b, s, n_heads * dv), BF16),
                   jax.ShapeDtypeStruct((b, n_heads, dk, dv), F32)],
        scratch_shapes=[pltpu.VMEM((dk, dv), F32)],
        compiler_params=_params(2),
        name="gla_prompt",
    )(proj, proj, proj, proj, proj, _pad_rank(gate_w), gate_b.reshape(1, -1), gain.reshape(1, -1))


def _gla_sample_kernel(q_ref, k_ref, v_ref, gate_ref, low_ref, gw_ref, gb_ref, gain_ref, s0_ref,
                       y_ref, s_ref, *, rank):
    dk, dv = GLA_DK, GLA_DV
    bs = q_ref.shape[0]
    g = _gla_log_alpha(low_ref[...], gw_ref, gb_ref, rank)
    q = q_ref[...] * (dk ** -0.5)
    k = k_ref[...]
    v = v_ref[...]
    qe = q * jnp.exp(g)
    ke = k * jnp.exp(-g)
    kd = k * jnp.exp(g - g)
    eg = jnp.exp(g)
    o = jnp.sum(qe * ke, axis=-1, keepdims=True) * v
    qeb = qe.astype(BF16)
    row = lax.broadcasted_iota(jnp.int32, (bs, dv), 0)
    for b in range(bs):
        s0 = s0_ref[b]
        o = o + jnp.where(row == b, _dot(qeb, s0.astype(BF16)), 0.0)
        dec = _col_bcast(eg[b:b + 1], dk)
        kcol = _col_bcast(kd[b:b + 1], dk)
        for j in range(dv // dk):
            cols = slice(j * dk, (j + 1) * dk)
            s_ref[b, :, cols] = dec * s0[:, cols] + kcol * v[b:b + 1, cols]
    y_ref[...] = (_headnorm(o, gain_ref[...]) * _silu(gate_ref[...])).astype(BF16)


def gla_sample(proj_s, gate_w, gate_b, gain, s0, n_heads):
    bs = proj_s.shape[0]
    dk, dv = GLA_DK, GLA_DV
    rank = gate_w.shape[0]
    koff, voff, goff, loff = _gla_offsets(n_heads)
    state_spec = pl.BlockSpec((bs, None, dk, dv), lambda hi: (0, hi, 0, 0))
    return pl.pallas_call(
        functools.partial(_gla_sample_kernel, rank=rank),
        grid=(n_heads,),
        in_specs=[pl.BlockSpec((bs, dk), lambda hi: (0, hi)),
                  pl.BlockSpec((bs, dk), lambda hi: (0, koff + hi)),
                  pl.BlockSpec((bs, dv), lambda hi: (0, voff + hi)),
                  pl.BlockSpec((bs, dv), lambda hi: (0, goff + hi)),
                  pl.BlockSpec((bs, LANES), lambda hi: (0, loff)),
                  pl.BlockSpec((LANES, dk), lambda hi: (0, hi)),
                  pl.BlockSpec((1, dk), lambda hi: (0, hi)),
                  pl.BlockSpec((1, dv), lambda hi: (0, hi)),
                  state_spec],
        out_specs=[pl.BlockSpec((bs, dv), lambda hi: (0, hi)), state_spec],
        out_shape=[jax.ShapeDtypeStruct((bs, n_heads * dv), BF16),
                   jax.ShapeDtypeStruct((bs, n_heads, dk, dv), F32)],
        compiler_params=_params(1),
        name="gla_sample",
    )(proj_s, proj_s, proj_s, proj_s, proj_s, _pad_rank(gate_w), gate_b.reshape(1, -1),
      gain.reshape(1, -1), s0)


def _attention_bias_tables(rel_bias, n_heads):
    blk = A_BLOCK
    ng = len(A_GROUPS)
    onehot = np.zeros((ng, 2 * blk, NUM_BUCKETS), np.float32)
    for g, (window, dil) in enumerate(A_GROUPS):
        n_keys = window // dil
        assert n_keys == blk, "one 128-key band per dilated stream is assumed"
        u = np.arange(n_keys + 1)
        onehot[g, u, _t5_bucket(dil * (n_keys - u))] = 1.0
    tab = jnp.einsum("gub,bgh->guh", jnp.asarray(onehot), rel_bias.astype(F32).reshape(NUM_BUCKETS, ng, n_heads),
                     precision=lax.Precision.HIGHEST)
    tab_t = tab.transpose(0, 2, 1)
    return tab.transpose(2, 0, 1), tab_t[:, :, :blk], tab_t[:, :, blk:blk + 1]


def kernel(x_prompt, x_sample, cache_k_w128, cache_v_w128, cache_k_w512, cache_v_w512, cache_k_w2048, cache_v_w2048, state_lru_h, state_lru_conv, state_ret, state_gla, norm_pre, norm_post, rel_bias, a_w_in, a_w_out, b_w_in, b_conv_w, b_conv_b, b_gate_a_w, b_gate_a_b, b_gate_x_w, b_gate_x_b, b_lambda, b_w_out, c_w_in, c_norm, c_w_out, d_w_in, d_gate_w, d_gate_b, d_norm, d_w_out):
    b, s, d = x_prompt.shape
    bs = x_sample.shape[0]
    assert x_sample.shape[1] == 1, "one new token per sequence"
    depth = norm_pre.shape[0]
    k_caches = (cache_k_w128, cache_k_w512, cache_k_w2048)
    v_caches = (cache_v_w128, cache_v_w512, cache_v_w2048)
    ng = len(A_GROUPS)
    for (window, dil), kc in zip(A_GROUPS, k_caches):
        assert kc.shape[2] == window and s % (dil * A_BLOCK) == 0

    xp = x_prompt.reshape(b * s, d)
    xs = x_sample.reshape(bs, d)
    hp = norm_cast(xp, norm_pre[0])
    hs = norm_cast(xs, norm_pre[0])

    kp_rows = [[] for _ in A_GROUPS]
    vp_rows = [[] for _ in A_GROUPS]
    ks_rows = [[] for _ in A_GROUPS]
    vs_rows = [[] for _ in A_GROUPS]
    lru_h_p, lru_h_s, lru_c_p, lru_c_s = [], [], [], []
    ret_p, ret_s, gla_p, gla_s = [], [], [], []

    for i in range(depth):
        kind, j = i % 4, i // 4
        if kind == 0:
            n_heads = a_w_out.shape[1] // A_HEAD_DIM
            width = n_heads * A_HEAD_DIM
            full = [min(window, s) == s for window, _ in A_GROUPS]
            own = [which * ng + g for which in (1, 2) for g in range(ng) if full[g]]
            shared = [u for u in range(3 * ng + 1) if u not in own]
            unit_p, unit_s = {}, {}
            for group in [shared] + [[u] for u in own]:
                arr_p, arr_s = matmul(hp, hs, a_w_in, j, units=tuple(group), unit=width)
                for pos, u in enumerate(group):
                    unit_p[u] = (arr_p.reshape(b, s, -1), pos)
                    unit_s[u] = arr_s[:, pos * width:(pos + 1) * width]
            bias_tab, bias_past, bias_self = _attention_bias_tables(rel_bias, n_heads)
            yp, *tails = attn_prompt([unit_p[g] for g in range(ng)], [unit_p[ng + g] for g in range(ng)],
                                     [unit_p[2 * ng + g] for g in range(ng)], unit_p[3 * ng],
                                     bias_tab, n_heads)
            yp = yp.reshape(b * s, width)
            ys = attn_sample([unit_s[g] for g in range(ng)], [unit_s[ng + g] for g in range(ng)],
                             [unit_s[2 * ng + g] for g in range(ng)], unit_s[3 * ng],
                             [c[j] for c in k_caches], [c[j] for c in v_caches],
                             bias_past, bias_self, n_heads)
            for g in range(ng):
                if full[g]:
                    k_rows, v_rows = unit_p[ng + g][0], unit_p[2 * ng + g][0]
                else:
                    k_rows, v_rows = tails.pop(0), tails.pop(0)
                kp_rows[g].append(k_rows.reshape(b, -1, n_heads, A_HEAD_DIM))
                vp_rows[g].append(v_rows.reshape(b, -1, n_heads, A_HEAD_DIM))
                ks_rows[g].append(unit_s[ng + g].reshape(bs, 1, n_heads, A_HEAD_DIM))
                vs_rows[g].append(unit_s[2 * ng + g].reshape(bs, 1, n_heads, A_HEAD_DIM))
            w_out = a_w_out
        elif kind == 1:
            br = b_w_out.shape[1]
            proj, proj_s = matmul(hp, hs, b_w_in, j)
            prm = (b_conv_w[j], b_conv_b[j], b_gate_a_w[j], b_gate_a_b[j], b_gate_x_w[j], b_gate_x_b[j], b_lambda[j])
            proj3 = proj.reshape(b, s, -1)
            yp, h_last = lru_prompt(proj3, *prm)
            yp = yp.reshape(b * s, br)
            ys, h_new = lru_sample(proj_s, state_lru_conv[j], state_lru_h[j], *prm)
            cw = b_conv_w.shape[1]
            lru_h_p.append(h_last)
            lru_c_p.append(proj3[:, s - (cw - 1):, :br])
            lru_h_s.append(h_new)
            lru_c_s.append(jnp.concatenate([state_lru_conv[j], proj_s[:, None, :br]], axis=1)[:, 1:])
            w_out = b_w_out
        elif kind == 2:
            br = c_w_out.shape[1]
            n_heads = br // RET_DV
            proj, proj_s = matmul(hp, hs, c_w_in, j)
            yp, st = ret_prompt(proj.reshape(b, s, -1), c_norm[j], n_heads)
            yp = yp.reshape(b * s, br)
            ys, st_s = ret_sample(proj_s, c_norm[j], state_ret[j], n_heads)
            ret_p.append(st)
            ret_s.append(st_s)
            w_out = c_w_out
        else:
            br = d_w_out.shape[1]
            n_heads = br // GLA_DV
            proj, proj_s = matmul(hp, hs, jnp.swapaxes(d_w_in, 1, 2), j, transposed=True)
            yp, st = gla_prompt(proj.reshape(b, s, -1), d_gate_w[j], d_gate_b[j], d_norm[j], n_heads)
            yp = yp.reshape(b * s, br)
            ys, st_s = gla_sample(proj_s, d_gate_w[j], d_gate_b[j], d_norm[j], state_gla[j], n_heads)
            gla_p.append(st)
            gla_s.append(st_s)
            w_out = d_w_out
        op, os_ = matmul(yp, ys, w_out, j)
        g_next = norm_pre[i + 1] if i + 1 < depth else None
        xp, hp = residual_norm(xp, op, norm_post[i], g_next)
        xs, hs = residual_norm(xs, os_, norm_post[i], g_next)

    return (xp.reshape(b, s, d), xs.reshape(bs, 1, d),
            jnp.stack(kp_rows[0]), jnp.stack(ks_rows[0]), jnp.stack(vp_rows[0]), jnp.stack(vs_rows[0]),
            jnp.stack(kp_rows[1]), jnp.stack(ks_rows[1]), jnp.stack(vp_rows[1]), jnp.stack(vs_rows[1]),
            jnp.stack(kp_rows[2]), jnp.stack(ks_rows[2]), jnp.stack(vp_rows[2]), jnp.stack(vs_rows[2]),
            jnp.stack(lru_h_p), jnp.stack(lru_h_s), jnp.stack(lru_c_p), jnp.stack(lru_c_s),
            jnp.stack(ret_p), jnp.stack(ret_s), jnp.stack(gla_p), jnp.stack(gla_s))
```

```python
import functools

import numpy as np
import jax
import jax.numpy as jnp
from jax import lax
from jax.experimental import pallas as pl
from jax.experimental.pallas import tpu as pltpu

F32 = jnp.float32
BF16 = jnp.bfloat16

PAST_LEN = 8192
EPS = 1e-6
NEG_INF = -1e30
A_GROUPS = ((128, 1), (512, 4), (2048, 16))
A_HEAD_DIM = 128
A_BLOCK = 128
ATTN_UNROLL = 16
ATTN_MERGE_ROWS = 256
ATTN_SAMPLE_UNROLL = 16
NUM_BUCKETS = 32
MAX_DISTANCE = 2048
C_RG = 8.0
LRU_TIME_TILE = 1024
RET_DK = 256
RET_DV = 512
RET_CHUNK = 128
ROPE_BASE = 10000.0
GLA_DK = 256
GLA_DV = 512
GLA_TAU = 16.0
GLA_CHUNK = 64
GLA_GROUP = 16
RET_GROUP = 8

LANES = 128
SUBLANES = 8
V7X_VMEM_LIMIT_BYTES = 56 * 1024 * 1024
V7X_VMEM_LIMIT_MATMUL_BYTES = 62 * 1024 * 1024

_ARB = pltpu.ARBITRARY


def _params(n_grid, vmem_limit_bytes=V7X_VMEM_LIMIT_BYTES):
    return pltpu.CompilerParams(dimension_semantics=(_ARB,) * n_grid,
                                vmem_limit_bytes=vmem_limit_bytes)


def _dot(a, b):
    return jnp.dot(a, b, preferred_element_type=F32)


def _dot_nt(a, b):
    return lax.dot_general(a, b, (((1,), (1,)), ((), ())), preferred_element_type=F32)


def _dot_tn(a, b):
    return lax.dot_general(a, b, (((0,), (0,)), ((), ())), preferred_element_type=F32)


def _sigmoid(x):
    return 0.5 * jnp.tanh(0.5 * x) + 0.5


def _silu(x):
    return x * _sigmoid(x)


def _softplus(x):
    return jnp.maximum(x, 0.0) + jnp.log1p(jnp.exp(-jnp.abs(x)))


def _rms(x, g):
    return x * lax.rsqrt(jnp.mean(x * x, axis=-1, keepdims=True) + EPS) * g


def _chunk_group(n_chunks, want):
    return want if n_chunks % want == 0 else 1


def _col_bcast(row, n):
    return jnp.broadcast_to(row, (n, n)).T


def _norm_kernel(x_ref, g_ref, h_ref):
    h_ref[...] = _rms(x_ref[...], g_ref[...]).astype(BF16)


def _resnorm_kernel(x_ref, y_ref, gpost_ref, gnext_ref, xo_ref, h_ref):
    xn = x_ref[...] + _rms(y_ref[...], gpost_ref[...])
    xo_ref[...] = xn
    h_ref[...] = _rms(xn, gnext_ref[...]).astype(BF16)


def _res_kernel(x_ref, y_ref, gpost_ref, xo_ref):
    xo_ref[...] = x_ref[...] + _rms(y_ref[...], gpost_ref[...])


def _row_tile(m):
    return min(m, 256)


def norm_cast(x, g):
    m, d = x.shape
    tm = _row_tile(m)
    return pl.pallas_call(
        _norm_kernel,
        grid=(m // tm,),
        in_specs=[pl.BlockSpec((tm, d), lambda i: (i, 0)),
                  pl.BlockSpec((1, d), lambda i: (0, 0))],
        out_specs=pl.BlockSpec((tm, d), lambda i: (i, 0)),
        out_shape=jax.ShapeDtypeStruct((m, d), BF16),
        compiler_params=_params(1),
        name="norm_cast",
    )(x, g.reshape(1, d))


def residual_norm(x, y, g_post, g_next):
    m, d = x.shape
    tm = _row_tile(m)
    row = pl.BlockSpec((tm, d), lambda i: (i, 0))
    vec = pl.BlockSpec((1, d), lambda i: (0, 0))
    if g_next is None:
        return pl.pallas_call(
            _res_kernel, grid=(m // tm,),
            in_specs=[row, row, vec], out_specs=row,
            out_shape=jax.ShapeDtypeStruct((m, d), F32),
            compiler_params=_params(1), name="residual",
        )(x, y, g_post.reshape(1, d)), None
    return pl.pallas_call(
        _resnorm_kernel, grid=(m // tm,),
        in_specs=[row, row, vec, vec], out_specs=[row, row],
        out_shape=[jax.ShapeDtypeStruct((m, d), F32), jax.ShapeDtypeStruct((m, d), BF16)],
        compiler_params=_params(1), name="residual_norm",
    )(x, y, g_post.reshape(1, d), g_next.reshape(1, d))


def _mm_kernel(x_ref, w_ref, xs_ref, o_ref, os_ref, *, transposed, tn):
    dot = _dot_nt if transposed else _dot
    k = x_ref.shape[1]
    kc = k // MM_K_CHUNKS

    def product(lhs_ref):
        acc = None
        for c in range(MM_K_CHUNKS):
            ks = slice(c * kc, (c + 1) * kc)
            w = (w_ref[:, ks] if transposed else w_ref[ks, :]).astype(BF16)
            part = dot(lhs_ref[:, ks], w)
            acc = part if acc is None else acc + part
        return acc

    o_ref[...] = product(x_ref)

    @pl.when(pl.program_id(0) == 0)
    def _():
        cols = pl.ds(pl.multiple_of(pl.program_id(1) * tn, tn), tn)
        os_ref[:, cols] = product(xs_ref)


MM_COL_TILE = 512
MM_K_CHUNKS = 4
MM_ROW_TILE = 2048
MM_ROW_TILE_LONG_K = 1024
MM_LONG_K = 4096


def matmul(x, xs, w, layer, units=None, unit=None, transposed=False):
    m, k = x.shape
    ms = xs.shape[0]
    n = w.shape[1] if transposed else w.shape[2]
    tn = MM_COL_TILE
    tm = min(m, MM_ROW_TILE if k <= MM_LONG_K else MM_ROW_TILE_LONG_K)
    assert m % tm == 0
    if units is None:
        ncols = n
        col_tile = lambda j: j
    else:
        assert unit % tn == 0
        per = unit // tn
        ncols = len(units) * unit

        def col_tile(j):
            pos = j // per
            src = units[0]
            for idx in range(1, len(units)):
                src = jnp.where(pos >= idx, units[idx], src)
            return src * per + j % per

    nj = pl.cdiv(ncols, tn)
    if transposed:
        w_spec = pl.BlockSpec((None, tn, k), lambda i, j: (layer, col_tile(j), 0))
    else:
        w_spec = pl.BlockSpec((None, k, tn), lambda i, j: (layer, 0, col_tile(j)))
    out, out_s = pl.pallas_call(
        functools.partial(_mm_kernel, transposed=transposed, tn=tn),
        grid=(m // tm, nj),
        in_specs=[pl.BlockSpec((tm, k), lambda i, j: (i, 0)),
                  w_spec,
                  pl.BlockSpec((ms, k), lambda i, j: (0, 0))],
        out_specs=[pl.BlockSpec((tm, tn), lambda i, j: (i, j)),
                   pl.BlockSpec((ms, nj * tn), lambda i, j: (0, 0))],
        out_shape=[jax.ShapeDtypeStruct((m, ncols), F32), jax.ShapeDtypeStruct((ms, nj * tn), F32)],
        compiler_params=_params(2, V7X_VMEM_LIMIT_MATMUL_BYTES),
        name="proj_matmul",
    )(x, w, xs)
    return out, (out_s if nj * tn == ncols else out_s[:, :ncols])


def _t5_bucket(dist):
    n = np.asarray(dist, dtype=np.int64)
    max_exact = NUM_BUCKETS // 2
    ratio = np.log(np.maximum(n, 1) / max_exact) / np.log(MAX_DISTANCE / max_exact)
    large = np.minimum(max_exact + (ratio * (NUM_BUCKETS - max_exact)).astype(np.int64), NUM_BUCKETS - 1)
    return np.where(n < max_exact, n, large).astype(np.int32)


def _attn_prompt_kernel(*refs, dilations, seq, tails):
    ng = len(dilations)
    qkv = refs[:3 * ng]
    gate_ref, bias_ref, y_ref = refs[3 * ng:3 * ng + 3]
    tail_refs = refs[3 * ng + 3:-2]
    o_s, lse_s = refs[-2:]
    blk = A_BLOCK
    for n, (g, keep) in enumerate(tails):
        tail_refs[2 * n][...] = qkv[3 * g + 1][seq - keep:, :]
        tail_refs[2 * n + 1][...] = qkv[3 * g + 2][seq - keep:, :]
    scale = A_HEAD_DIM ** -0.5
    ii = lax.broadcasted_iota(jnp.int32, (blk, blk), 0)
    jj = lax.broadcasted_iota(jnp.int32, (blk, blk), 1)
    cur_ok = jj <= ii
    prev_ok = jj >= ii

    for g, dil in enumerate(dilations):
        q_ref, k_ref, v_ref = qkv[3 * g:3 * g + 3]
        nb = seq // (dil * blk)
        band = pltpu.roll(jnp.broadcast_to(bias_ref[g:g + 1, :], (blk, 2 * blk)), 0, axis=1,
                          stride=1, stride_axis=0)
        b_prev = band[:, :blk]
        b_cur = band[:, blk:]

        def rows_at(start, dil=dil):
            if dil > 1:
                return pl.ds(start, blk, stride=dil)
            return pl.ds(start if isinstance(start, int) else pl.multiple_of(start, blk), blk)

        def scores(idx, dil=dil, nb=nb, q_ref=q_ref, k_ref=k_ref, b_prev=b_prev, b_cur=b_cur,
                   rows_at=rows_at):
            r = idx % dil
            bi = idx // dil
            start = r + bi * (blk * dil)
            rows = rows_at(start)
            q = q_ref[rows, :].astype(BF16)
            s = [jnp.where(cur_ok, _dot_nt(q, k_ref[rows, :].astype(BF16)) * scale + b_cur, NEG_INF)]
            prows = None
            if nb > 1 and not (isinstance(bi, int) and bi == 0):
                if isinstance(bi, int):
                    prows = rows_at(start - blk * dil)
                    prev_mask = prev_ok
                else:
                    prows = rows_at(jnp.maximum(start - blk * dil, 0))
                    prev_mask = prev_ok & (bi > 0)
                s.append(jnp.where(prev_mask, _dot_nt(q, k_ref[prows, :].astype(BF16)) * scale + b_prev,
                                   NEG_INF))
            return rows, prows, s

        def softmax(s):
            m = jnp.max(functools.reduce(jnp.maximum, s), axis=-1, keepdims=True)
            p = [jnp.exp(x - m) for x in s]
            den = jnp.sum(functools.reduce(jnp.add, p), axis=-1, keepdims=True)
            return m, den, [x.astype(BF16) for x in p]

        def values(rows, prows, p, v_ref=v_ref):
            num = _dot(p[0], v_ref[rows, :].astype(BF16))
            if prows is not None:
                num = num + _dot(p[1], v_ref[prows, :].astype(BF16))
            return num

        def body(it, carry, g=g, scores=scores, softmax=softmax, values=values):
            blocks = [scores(it * ATTN_UNROLL + u) for u in range(ATTN_UNROLL)]
            probs = [softmax(s) for _, _, s in blocks]
            nums = [values(rows, prows, p) for (rows, prows, _), (_, _, p) in zip(blocks, probs)]
            for (rows, _, _), (m, den, _), num in zip(blocks, probs, nums):
                o_s[g, rows, :] = num / den
                lse_s[g, rows, :] = jnp.broadcast_to(m + jnp.log(den), (blk, A_HEAD_DIM))
            return carry

        if seq // blk == ATTN_UNROLL:
            body(0, 0)
        else:
            lax.fori_loop(0, seq // (blk * ATTN_UNROLL), body, 0)

    tr = ATTN_MERGE_ROWS

    def merge(c, carry):
        rows = pl.ds(pl.multiple_of(c * tr, tr), tr)
        lses = [lse_s[g, rows, :] for g in range(ng)]
        mx = functools.reduce(jnp.maximum, lses)
        num = jnp.zeros((tr, A_HEAD_DIM), F32)
        den = jnp.zeros((tr, A_HEAD_DIM), F32)
        for g in range(ng):
            w = jnp.exp(lses[g] - mx)
            num = num + w * o_s[g, rows, :]
            den = den + w
        y_ref[rows, :] = ((num / den) * _silu(gate_ref[rows, :])).astype(BF16)
        return carry

    lax.fori_loop(0, seq // tr, merge, 0)


def attn_prompt(qs, ks, vs, gate, bias_tab, n_heads):
    b, s, _ = gate[0].shape
    ng = len(A_GROUPS)
    dh = A_HEAD_DIM
    in_specs = []
    args = []
    units = [u for g in range(ng) for u in (qs[g], ks[g], vs[g])] + [gate]
    for arr, unit in units:
        in_specs.append(pl.BlockSpec((None, s, dh), lambda bi, hi, unit=unit: (bi, 0, unit * n_heads + hi)))
        args.append(arr)
    in_specs.append(pl.BlockSpec((None, ng, 2 * A_BLOCK), lambda bi, hi: (hi, 0, 0)))
    args.append(bias_tab)
    tails = tuple((g, window) for g, (window, _) in enumerate(A_GROUPS) if window < s)
    kern = functools.partial(_attn_prompt_kernel, dilations=tuple(d for _, d in A_GROUPS), seq=s,
                             tails=tails)
    out_specs = [pl.BlockSpec((None, s, dh), lambda bi, hi: (bi, 0, hi))]
    out_shape = [jax.ShapeDtypeStruct((b, s, n_heads * dh), BF16)]
    for _, keep in tails:
        out_specs += [pl.BlockSpec((None, keep, dh), lambda bi, hi: (bi, 0, hi))] * 2
        out_shape += [jax.ShapeDtypeStruct((b, keep, n_heads * dh), F32)] * 2
    return pl.pallas_call(
        kern,
        grid=(b, n_heads),
        in_specs=in_specs,
        out_specs=out_specs,
        out_shape=out_shape,
        scratch_shapes=[pltpu.VMEM((ng, s, dh), F32)] * 2,
        compiler_params=_params(2),
        name="attn_prompt",
    )(*args)


def _attn_sample_kernel(q_ref, kn_ref, vn_ref, gate_ref, *refs, n_heads, n_keys):
    ng = len(A_GROUPS)
    caches = refs[:2 * ng]
    bpast_ref, bself_ref, y_ref = refs[2 * ng:]
    dh = A_HEAD_DIM
    scale = dh ** -0.5
    lane = lax.broadcasted_iota(jnp.int32, (n_heads, n_keys), 1)
    parts = []
    for g in range(ng):
        kc_ref, vc_ref = caches[2 * g], caches[2 * g + 1]
        q = q_ref[g]
        vn = vn_ref[g]

        def logits(j, acc, kc_ref=kc_ref, q=q):
            col = jnp.sum(kc_ref[j] * q, axis=-1, keepdims=True)
            return jnp.where(lane == j, col, acc)

        lp = lax.fori_loop(0, n_keys, logits, jnp.zeros((n_heads, n_keys), F32), unroll=ATTN_SAMPLE_UNROLL)
        lp = lp * scale + bpast_ref[g]
        ls = jnp.sum(kn_ref[g] * q, axis=-1, keepdims=True) * scale + bself_ref[g]
        m = jnp.maximum(jnp.max(lp, axis=-1, keepdims=True), ls)
        p = jnp.exp(lp - m)
        ps = jnp.exp(ls - m)
        den = jnp.sum(p, axis=-1, keepdims=True) + ps

        def weighted(j, acc, vc_ref=vc_ref, p=p):
            pj = jnp.sum(jnp.where(lane == j, p, 0.0), axis=-1, keepdims=True)
            return acc + pj * vc_ref[j]

        num = lax.fori_loop(0, n_keys, weighted, ps * vn, unroll=ATTN_SAMPLE_UNROLL)
        parts.append((num, m, den))
    mx = functools.reduce(jnp.maximum, [p_[1] for p_ in parts])
    num = jnp.zeros((n_heads, dh), F32)
    den = jnp.zeros((n_heads, 1), F32)
    for num_g, m_g, den_g in parts:
        w = jnp.exp(m_g - mx)
        num = num + w * num_g
        den = den + w * den_g
    y_ref[...] = ((num / den) * _silu(gate_ref[...])).astype(BF16)


def attn_sample(qs, ks, vs, gate, k_caches, v_caches, bias_past, bias_self, n_heads):
    bs, width = gate.shape
    dh = A_HEAD_DIM
    ng = len(A_GROUPS)
    n_keys = A_BLOCK
    heads = lambda rows: jnp.stack(rows, axis=1).reshape(bs, ng, n_heads, dh)
    grp_spec = pl.BlockSpec((None, ng, n_heads, dh), lambda bi: (bi, 0, 0, 0))
    args = [heads(qs), heads(ks), heads(vs), gate.reshape(bs, n_heads, dh)]
    in_specs = [grp_spec, grp_spec, grp_spec, pl.BlockSpec((None, n_heads, dh), lambda bi: (bi, 0, 0))]
    for (_, dil), kc, vc in zip(A_GROUPS, k_caches, v_caches):
        for c in (kc, vc):
            args.append(c.reshape(bs, n_keys, dil, n_heads, dh))
            in_specs.append(pl.BlockSpec((None, n_keys, None, n_heads, dh), lambda bi: (bi, 0, 0, 0, 0)))
    args += [bias_past, bias_self]
    in_specs += [pl.BlockSpec(bias_past.shape, lambda bi: (0, 0, 0)),
                 pl.BlockSpec(bias_self.shape, lambda bi: (0, 0, 0))]
    y = pl.pallas_call(
        functools.partial(_attn_sample_kernel, n_heads=n_heads, n_keys=n_keys),
        grid=(bs,),
        in_specs=in_specs,
        out_specs=pl.BlockSpec((None, n_heads, dh), lambda bi: (bi, 0, 0)),
        out_shape=jax.ShapeDtypeStruct((bs, n_heads, dh), BF16),
        compiler_params=_params(1),
        name="attn_sample",
    )(*args)
    return y.reshape(bs, width)


def _lru_gates(xc, wa_ref, ba_ref, wx_ref, bx_ref, lam_ref):
    xb = xc.astype(BF16)
    r = _sigmoid(_dot(xb, wa_ref[...].astype(BF16)) + ba_ref[...])
    ig = _sigmoid(_dot(xb, wx_ref[...].astype(BF16)) + bx_ref[...])
    log_a = -C_RG * r * _softplus(-lam_ref[...])
    a = jnp.exp(log_a)
    mult = jnp.sqrt(-jnp.tanh(log_a) * (a * a + 1.0))
    return a, mult, ig


def _lru_prompt_kernel(x_ref, gate_ref, cw_ref, cb_ref, wa_ref, ba_ref, wx_ref, bx_ref, lam_ref,
                       y_ref, hl_ref, hc_ref, xp_ref, *, tt, conv_w):
    t = pl.program_id(2)

    @pl.when(t == 0)
    def _():
        hc_ref[...] = jnp.zeros_like(hc_ref)
        xp_ref[:SUBLANES, :] = jnp.zeros((SUBLANES, xp_ref.shape[1]), F32)

    x = x_ref[...]
    c = x.shape[1]
    xp_ref[SUBLANES:, :] = x
    xc = cb_ref[...] + cw_ref[conv_w - 1:conv_w, :] * x
    for k in range(1, conv_w):
        xc = xc + cw_ref[conv_w - 1 - k:conv_w - k, :] * xp_ref[SUBLANES - k:SUBLANES - k + tt, :]
    xp_ref[:SUBLANES, :] = x[tt - SUBLANES:]

    row = lax.broadcasted_iota(jnp.int32, (tt, c), 0)
    a, mult, ig = _lru_gates(xc, wa_ref, ba_ref, wx_ref, bx_ref, lam_ref)
    mult = jnp.where(row + t * tt == 0, 1.0, mult)
    bx = mult * ig * xc
    ng = tt // SUBLANES
    a = a.reshape(ng, SUBLANES, c)
    bx = bx.reshape(ng, SUBLANES, c)
    sub = lax.broadcasted_iota(jnp.int32, (ng, SUBLANES, c), 1)
    d = 1
    while d < SUBLANES:
        keep = sub >= d
        a_sh = jnp.where(keep, pltpu.roll(a, d, axis=1), 1.0)
        b_sh = jnp.where(keep, pltpu.roll(bx, d, axis=1), 0.0)
        bx = a * b_sh + bx
        a = a * a_sh
        d *= 2
    h = hc_ref[...]
    groups = []
    for g in range(ng):
        groups.append(a[g] * h + bx[g])
        h = groups[-1][SUBLANES - 1:]
    y_ref[...] = (jnp.concatenate(groups, axis=0) * _silu(gate_ref[...])).astype(BF16)
    hc_ref[...] = h
    hl_ref[...] = h


def lru_prompt(proj, conv_w, conv_b, wa, ba, wx, bx, lam):
    b, s, two_br = proj.shape
    br = two_br // 2
    nblk, bs_, _ = wa.shape
    cw = conv_w.shape[0]
    tt = min(s, LRU_TIME_TILE)
    vec = lambda a: a.reshape(1, br)
    vspec = pl.BlockSpec((1, bs_), lambda bi, ni, ti: (0, ni))
    wspec = pl.BlockSpec((None, bs_, bs_), lambda bi, ni, ti: (ni, 0, 0))
    y, hl = pl.pallas_call(
        functools.partial(_lru_prompt_kernel, tt=tt, conv_w=cw),
        grid=(b, nblk, s // tt),
        in_specs=[pl.BlockSpec((None, tt, bs_), lambda bi, ni, ti: (bi, ti, ni)),
                  pl.BlockSpec((None, tt, bs_), lambda bi, ni, ti: (bi, ti, nblk + ni)),
                  pl.BlockSpec((cw, bs_), lambda bi, ni, ti: (0, ni)),
                  vspec, wspec, vspec, wspec, vspec, vspec],
        out_specs=[pl.BlockSpec((None, tt, bs_), lambda bi, ni, ti: (bi, ti, ni)),
                   pl.BlockSpec((None, 1, bs_), lambda bi, ni, ti: (bi, 0, ni))],
        out_shape=[jax.ShapeDtypeStruct((b, s, br), BF16), jax.ShapeDtypeStruct((b, 1, br), F32)],
        scratch_shapes=[pltpu.VMEM((1, bs_), F32), pltpu.VMEM((SUBLANES + tt, bs_), F32)],
        compiler_params=_params(3),
        name="lru_prompt",
    )(proj, proj, conv_w, vec(conv_b), wa, vec(ba), wx, vec(bx), vec(lam))
    return y, hl.reshape(b, br)


def _lru_sample_kernel(x_ref, gate_ref, buf_ref, h0_ref, cw_ref, cb_ref, wa_ref, ba_ref, wx_ref, bx_ref,
                       lam_ref, y_ref, h_ref, *, conv_w):
    x = x_ref[...]
    xc = cb_ref[...] + cw_ref[conv_w - 1:conv_w, :] * x
    for w in range(conv_w - 1):
        xc = xc + cw_ref[w:w + 1, :] * buf_ref[:, w, :]
    a, mult, ig = _lru_gates(xc, wa_ref, ba_ref, wx_ref, bx_ref, lam_ref)
    h = a * h0_ref[...] + mult * ig * xc
    h_ref[...] = h
    y_ref[...] = (h * _silu(gate_ref[...])).astype(BF16)


def lru_sample(proj_s, conv_buf, h0, conv_w, conv_b, wa, ba, wx, bx, lam):
    bs, two_br = proj_s.shape
    br = two_br // 2
    nblk, bs_, _ = wa.shape
    cw = conv_w.shape[0]
    vec = lambda a: a.reshape(1, br)
    vspec = pl.BlockSpec((1, bs_), lambda ni: (0, ni))
    wspec = pl.BlockSpec((None, bs_, bs_), lambda ni: (ni, 0, 0))
    rspec = pl.BlockSpec((bs, bs_), lambda ni: (0, ni))
    return pl.pallas_call(
        functools.partial(_lru_sample_kernel, conv_w=cw),
        grid=(nblk,),
        in_specs=[rspec,
                  pl.BlockSpec((bs, bs_), lambda ni: (0, nblk + ni)),
                  pl.BlockSpec((bs, cw - 1, bs_), lambda ni: (0, 0, ni)),
                  rspec,
                  pl.BlockSpec((cw, bs_), lambda ni: (0, ni)),
                  vspec, wspec, vspec, wspec, vspec, vspec],
        out_specs=[rspec, rspec],
        out_shape=[jax.ShapeDtypeStruct((bs, br), BF16), jax.ShapeDtypeStruct((bs, br), F32)],
        compiler_params=_params(1),
        name="lru_sample",
    )(proj_s, proj_s, conv_buf, h0, conv_w, vec(conv_b), wa, vec(ba), wx, vec(bx), vec(lam))


def _rope(x, cos, sin):
    half = x.shape[-1] // 2
    x1, x2 = x[:, :half], x[:, half:]
    return jnp.concatenate([x1 * cos - x2 * sin, x1 * sin + x2 * cos], axis=-1)


def _groupnorm(o, gain):
    c = o - jnp.mean(o, axis=-1, keepdims=True)
    return c * lax.rsqrt(jnp.mean(c * c, axis=-1, keepdims=True) + EPS) * gain


def _ret_prompt_kernel(q_ref, k_ref, v_ref, gate_ref, cos_ref, sin_ref, lg_ref, gain_ref,
                       y_ref, s_ref, st_ref, *, chunk, n_chunks, group):
    lg = lg_ref[:, :1]
    idx = lax.broadcasted_iota(jnp.int32, (chunk, 1), 0).astype(F32)
    ii = lax.broadcasted_iota(jnp.int32, (chunk, chunk), 0)
    jj = lax.broadcasted_iota(jnp.int32, (chunk, chunk), 1)
    diff = (ii - jj).astype(F32)
    decay = jnp.where(diff >= 0, jnp.exp(diff * lg), 0.0)
    q_dec = jnp.exp((idx + 1.0) * lg)
    k_dec = jnp.exp((chunk - 1.0 - idx) * lg)
    chunk_dec = jnp.exp(chunk * lg)
    gain = gain_ref[...]
    st_ref[...] = jnp.zeros_like(st_ref)

    q_dec_rows = jnp.concatenate([q_dec] * group, axis=0)
    k_dec_rows = jnp.concatenate([k_dec] * group, axis=0)

    def body(it, carry):
        rows = pl.ds(pl.multiple_of(it * (group * chunk), group * chunk), group * chunk)
        cos = cos_ref[rows, :]
        sin = sin_ref[rows, :]
        qc = _rope(q_ref[rows, :], cos, sin)
        kc = _rope(k_ref[rows, :], cos, sin) * (RET_DK ** -0.5)
        vc = v_ref[rows, :].astype(BF16)
        qb = qc.astype(BF16)
        kb = kc.astype(BF16)
        qd = (qc * q_dec_rows).astype(BF16)
        kd = (kc * k_dec_rows).astype(BF16)
        sl = [slice(u * chunk, (u + 1) * chunk) for u in range(group)]
        intra, upd = [], []
        for u in range(group):
            scores = _dot_nt(qb[sl[u]], kb[sl[u]]) * decay
            intra.append(_dot(scores.astype(BF16), vc[sl[u]]))
            upd.append(_dot_tn(kd[sl[u]], vc[sl[u]]))
        st = st_ref[...]
        outs = []
        for u in range(group):
            outs.append(intra[u] + _dot(qd[sl[u]], st.astype(BF16)))
            st = chunk_dec * st + upd[u]
        st_ref[...] = st
        o = jnp.concatenate(outs, axis=0)
        y_ref[rows, :] = (_groupnorm(o, gain) * _silu(gate_ref[rows, :])).astype(BF16)
        return carry

    lax.fori_loop(0, n_chunks // group, body, 0)
    s_ref[...] = st_ref[...]


def _ret_log_gamma(n_heads):
    lg = np.log1p(-np.exp2(-5.0 - np.arange(n_heads, dtype=np.float32))).astype(np.float32)
    return jnp.asarray(np.broadcast_to(lg[:, None, None], (n_heads, 1, LANES)).copy())


def _rope_tables(pos):
    half = RET_DK // 2
    inv_freq = ROPE_BASE ** (-jnp.arange(half, dtype=F32) / half)
    ang = pos[:, None] * inv_freq[None, :]
    return jnp.cos(ang), jnp.sin(ang)


def ret_prompt(proj, gain, n_heads):
    b, s, _ = proj.shape
    dk, dv = RET_DK, RET_DV
    qk = n_heads * dk
    chunk = RET_CHUNK if s % RET_CHUNK == 0 else s
    cos, sin = _rope_tables(jnp.arange(s, dtype=F32))
    half = dk // 2
    koff = qk // dk
    voff = 2 * qk // dv
    goff = (2 * qk + n_heads * dv) // dv
    return pl.pallas_call(
        functools.partial(_ret_prompt_kernel, chunk=chunk, n_chunks=s // chunk,
                          group=_chunk_group(s // chunk, RET_GROUP)),
        grid=(b, n_heads),
        in_specs=[pl.BlockSpec((None, s, dk), lambda bi, hi: (bi, 0, hi)),
                  pl.BlockSpec((None, s, dk), lambda bi, hi: (bi, 0, koff + hi)),
                  pl.BlockSpec((None, s, dv), lambda bi, hi: (bi, 0, voff + hi)),
                  pl.BlockSpec((None, s, dv), lambda bi, hi: (bi, 0, goff + hi)),
                  pl.BlockSpec((s, half), lambda bi, hi: (0, 0)),
                  pl.BlockSpec((s, half), lambda bi, hi: (0, 0)),
                  pl.BlockSpec((None, 1, LANES), lambda bi, hi: (hi, 0, 0)),
                  pl.BlockSpec((1, dv), lambda bi, hi: (0, hi))],
        out_specs=[pl.BlockSpec((None, s, dv), lambda bi, hi: (bi, 0, hi)),
                   pl.BlockSpec((None, None, dk, dv), lambda bi, hi: (bi, hi, 0, 0))],
        out_shape=[jax.ShapeDtypeStruct((b, s, n_heads * dv), BF16),
                   jax.ShapeDtypeStruct((b, n_heads, dk, dv), F32)],
        scratch_shapes=[pltpu.VMEM((dk, dv), F32)],
        compiler_params=_params(2),
        name="ret_prompt",
    )(proj, proj, proj, proj, cos, sin, _ret_log_gamma(n_heads), gain.reshape(1, n_heads * dv))


def _ret_sample_kernel(q_ref, k_ref, v_ref, gate_ref, cos_ref, sin_ref, lg_ref, gain_ref, s0_ref,
                       y_ref, s_ref):
    dk, dv = RET_DK, RET_DV
    bs = q_ref.shape[0]
    gamma = jnp.exp(lg_ref[:, :1])
    cos = cos_ref[...]
    sin = sin_ref[...]
    q = _rope(q_ref[...], cos, sin)
    k = _rope(k_ref[...], cos, sin) * (dk ** -0.5)
    v = v_ref[...]
    o = jnp.sum(q * k, axis=-1, keepdims=True) * v
    qg = (q * gamma).astype(BF16)
    row = lax.broadcasted_iota(jnp.int32, (bs, dv), 0)
    for b in range(bs):
        s0 = s0_ref[b]
        o = o + jnp.where(row == b, _dot(qg, s0.astype(BF16)), 0.0)
        kcol = _col_bcast(k[b:b + 1], dk)
        for j in range(dv // dk):
            cols = slice(j * dk, (j + 1) * dk)
            s_ref[b, :, cols] = gamma * s0[:, cols] + kcol * v[b:b + 1, cols]
    y_ref[...] = (_groupnorm(o, gain_ref[...]) * _silu(gate_ref[...])).astype(BF16)


def ret_sample(proj_s, gain, s0, n_heads):
    bs = proj_s.shape[0]
    dk, dv = RET_DK, RET_DV
    qk = n_heads * dk
    cos, sin = _rope_tables(jnp.full((1,), PAST_LEN, F32))
    half = dk // 2
    koff = qk // dk
    voff = 2 * qk // dv
    goff = (2 * qk + n_heads * dv) // dv
    state_spec = pl.BlockSpec((bs, None, dk, dv), lambda hi: (0, hi, 0, 0))
    return pl.pallas_call(
        _ret_sample_kernel,
        grid=(n_heads,),
        in_specs=[pl.BlockSpec((bs, dk), lambda hi: (0, hi)),
                  pl.BlockSpec((bs, dk), lambda hi: (0, koff + hi)),
                  pl.BlockSpec((bs, dv), lambda hi: (0, voff + hi)),
                  pl.BlockSpec((bs, dv), lambda hi: (0, goff + hi)),
                  pl.BlockSpec((1, half), lambda hi: (0, 0)),
                  pl.BlockSpec((1, half), lambda hi: (0, 0)),
                  pl.BlockSpec((None, 1, LANES), lambda hi: (hi, 0, 0)),
                  pl.BlockSpec((1, dv), lambda hi: (0, hi)),
                  state_spec],
        out_specs=[pl.BlockSpec((bs, dv), lambda hi: (0, hi)), state_spec],
        out_shape=[jax.ShapeDtypeStruct((bs, n_heads * dv), BF16),
                   jax.ShapeDtypeStruct((bs, n_heads, dk, dv), F32)],
        compiler_params=_params(1),
        name="ret_sample",
    )(proj_s, proj_s, proj_s, proj_s, cos, sin, _ret_log_gamma(n_heads), gain.reshape(1, n_heads * dv), s0)


def _headnorm(o, gain):
    return o * lax.rsqrt(jnp.mean(o * o, axis=-1, keepdims=True) + EPS) * gain


def _gla_log_alpha(low_ref_val, gw_ref, gb_ref, rank):
    lane = lax.broadcasted_iota(jnp.int32, low_ref_val.shape, 1)
    low = jnp.where(lane < rank, low_ref_val, 0.0).astype(BF16)
    z = _dot(low, gw_ref[...].astype(BF16)) + gb_ref[...]
    return (jnp.minimum(z, 0.0) - jnp.log1p(jnp.exp(-jnp.abs(z)))) / GLA_TAU


def _gla_prompt_kernel(q_ref, k_ref, v_ref, gate_ref, low_ref, gw_ref, gb_ref, gain_ref,
                       y_ref, s_ref, st_ref, *, chunk, n_chunks, group, rank):
    dk, dv = GLA_DK, GLA_DV
    sub = lax.broadcasted_iota(jnp.int32, (group * chunk, dk), 0) % chunk
    ii = lax.broadcasted_iota(jnp.int32, (chunk, chunk), 0)
    jj = lax.broadcasted_iota(jnp.int32, (chunk, chunk), 1)
    causal = jj <= ii
    gain = gain_ref[...]
    st_ref[...] = jnp.zeros_like(st_ref)

    def body(it, carry):
        rows = pl.ds(pl.multiple_of(it * (group * chunk), group * chunk), group * chunk)
        bcum = _gla_log_alpha(low_ref[rows, :], gw_ref, gb_ref, rank)
        d = 1
        while d < chunk:
            bcum = bcum + jnp.where(sub >= d, pltpu.roll(bcum, d, axis=0), 0.0)
            d *= 2
        sl = [slice(u * chunk, (u + 1) * chunk) for u in range(group)]
        blast = [bcum[(u + 1) * chunk - 1:(u + 1) * chunk] for u in range(group)]
        blast_rows = jnp.concatenate([jnp.broadcast_to(b, (chunk, dk)) for b in blast], axis=0)
        kc = k_ref[rows, :]
        vc = v_ref[rows, :].astype(BF16)
        qe = (q_ref[rows, :] * (dk ** -0.5) * jnp.exp(bcum)).astype(BF16)
        ke = (kc * jnp.exp(-bcum)).astype(BF16)
        kd = (kc * jnp.exp(blast_rows - bcum)).astype(BF16)
        intra, upd, dec = [], [], []
        for u in range(group):
            scores = jnp.where(causal, _dot_nt(qe[sl[u]], ke[sl[u]]), 0.0)
            intra.append(_dot(scores.astype(BF16), vc[sl[u]]))
            upd.append(_dot_tn(kd[sl[u]], vc[sl[u]]))
            dec.append(_col_bcast(jnp.exp(blast[u]), dk))
        st = st_ref[...]
        outs = []
        for u in range(group):
            outs.append(intra[u] + _dot(qe[sl[u]], st.astype(BF16)))
            st = jnp.concatenate([dec[u] * st[:, j * dk:(j + 1) * dk] + upd[u][:, j * dk:(j + 1) * dk]
                                  for j in range(dv // dk)], axis=1)
        st_ref[...] = st
        o = jnp.concatenate(outs, axis=0)
        y_ref[rows, :] = (_headnorm(o, gain) * _silu(gate_ref[rows, :])).astype(BF16)
        return carry

    lax.fori_loop(0, n_chunks // group, body, 0)
    s_ref[...] = st_ref[...]


def _gla_offsets(n_heads):
    dk, dv = GLA_DK, GLA_DV
    qk = n_heads * dk
    koff = qk // dk
    voff = 2 * qk // dv
    goff = (2 * qk + n_heads * dv) // dv
    loff = (2 * qk + 2 * n_heads * dv) // LANES
    return koff, voff, goff, loff


def _pad_rank(gate_w):
    rank = gate_w.shape[0]
    return jnp.pad(gate_w, ((0, LANES - rank), (0, 0)))


def gla_prompt(proj, gate_w, gate_b, gain, n_heads):
    b, s, _ = proj.shape
    dk, dv = GLA_DK, GLA_DV
    rank = gate_w.shape[0]
    chunk = GLA_CHUNK if s % GLA_CHUNK == 0 else s
    koff, voff, goff, loff = _gla_offsets(n_heads)
    return pl.pallas_call(
        functools.partial(_gla_prompt_kernel, chunk=chunk, n_chunks=s // chunk,
                          group=_chunk_group(s // chunk, GLA_GROUP), rank=rank),
        grid=(b, n_heads),
        in_specs=[pl.BlockSpec((None, s, dk), lambda bi, hi: (bi, 0, hi)),
                  pl.BlockSpec((None, s, dk), lambda bi, hi: (bi, 0, koff + hi)),
                  pl.BlockSpec((None, s, dv), lambda bi, hi: (bi, 0, voff + hi)),
                  pl.BlockSpec((None, s, dv), lambda bi, hi: (bi, 0, goff + hi)),
                  pl.BlockSpec((None, s, LANES), lambda bi, hi: (bi, 0, loff)),
                  pl.BlockSpec((LANES, dk), lambda bi, hi: (0, hi)),
                  pl.BlockSpec((1, dk), lambda bi, hi: (0, hi)),
                  pl.BlockSpec((1, dv), lambda bi, hi: (0, hi))],
        out_specs=[pl.BlockSpec((None, s, dv), lambda bi, hi: (bi, 0, hi)),
                   pl.BlockSpec((None, None, dk, dv), lambda bi, hi: (bi, hi, 0, 0))],
        out_shape=[jax.ShapeDtypeStruct((b, s, n_heads * dv), BF16),
                   jax.ShapeDtypeStruct((b, n_heads, dk, dv), F32)],
        scratch_shapes=[pltpu.VMEM((dk, dv), F32)],
        compiler_params=_params(2),
        name="gla_prompt",
    )(proj, proj, proj, proj, proj, _pad_rank(gate_w), gate_b.reshape(1, -1), gain.reshape(1, -1))


def _gla_sample_kernel(q_ref, k_ref, v_ref, gate_ref, low_ref, gw_ref, gb_ref, gain_ref, s0_ref,
                       y_ref, s_ref, *, rank):
    dk, dv = GLA_DK, GLA_DV
    bs = q_ref.shape[0]
    g = _gla_log_alpha(low_ref[...], gw_ref, gb_ref, rank)
    q = q_ref[...] * (dk ** -0.5)
    k = k_ref[...]
    v = v_ref[...]
    qe = q * jnp.exp(g)
    ke = k * jnp.exp(-g)
    kd = k * jnp.exp(g - g)
    eg = jnp.exp(g)
    o = jnp.sum(qe * ke, axis=-1, keepdims=True) * v
    qeb = qe.astype(BF16)
    row = lax.broadcasted_iota(jnp.int32, (bs, dv), 0)
    for b in range(bs):
        s0 = s0_ref[b]
        o = o + jnp.where(row == b, _dot(qeb, s0.astype(BF16)), 0.0)
        dec = _col_bcast(eg[b:b + 1], dk)
        kcol = _col_bcast(kd[b:b + 1], dk)
        for j in range(dv // dk):
            cols = slice(j * dk, (j + 1) * dk)
            s_ref[b, :, cols] = dec * s0[:, cols] + kcol * v[b:b + 1, cols]
    y_ref[...] = (_headnorm(o, gain_ref[...]) * _silu(gate_ref[...])).astype(BF16)


def gla_sample(proj_s, gate_w, gate_b, gain, s0, n_heads):
    bs = proj_s.shape[0]
    dk, dv = GLA_DK, GLA_DV
    rank = gate_w.shape[0]
    koff, voff, goff, loff = _gla_offsets(n_heads)
    state_spec = pl.BlockSpec((bs, None, dk, dv), lambda hi: (0, hi, 0, 0))
    return pl.pallas_call(
        functools.partial(_gla_sample_kernel, rank=rank),
        grid=(n_heads,),
        in_specs=[pl.BlockSpec((bs, dk), lambda hi: (0, hi)),
                  pl.BlockSpec((bs, dk), lambda hi: (0, koff + hi)),
                  pl.BlockSpec((bs, dv), lambda hi: (0, voff + hi)),
                  pl.BlockSpec((bs, dv), lambda hi: (0, goff + hi)),
                  pl.BlockSpec((bs, LANES), lambda hi: (0, loff)),
                  pl.BlockSpec((LANES, dk), lambda hi: (0, hi)),
                  pl.BlockSpec((1, dk), lambda hi: (0, hi)),
                  pl.BlockSpec((1, dv), lambda hi: (0, hi)),
                  state_spec],
        out_specs=[pl.BlockSpec((bs, dv), lambda hi: (0, hi)), state_spec],
        out_shape=[jax.ShapeDtypeStruct((bs, n_heads * dv), BF16),
                   jax.ShapeDtypeStruct((bs, n_heads, dk, dv), F32)],
        compiler_params=_params(1),
        name="gla_sample",
    )(proj_s, proj_s, proj_s, proj_s, proj_s, _pad_rank(gate_w), gate_b.reshape(1, -1),
      gain.reshape(1, -1), s0)


def _attention_bias_tables(rel_bias, n_heads):
    blk = A_BLOCK
    ng = len(A_GROUPS)
    onehot = np.zeros((ng, 2 * blk, NUM_BUCKETS), np.float32)
    for g, (window, dil) in enumerate(A_GROUPS):
        n_keys = window // dil
        assert n_keys == blk, "one 128-key band per dilated stream is assumed"
        u = np.arange(n_keys + 1)
        onehot[g, u, _t5_bucket(dil * (n_keys - u))] = 1.0
    tab = jnp.einsum("gub,bgh->guh", jnp.asarray(onehot), rel_bias.astype(F32).reshape(NUM_BUCKETS, ng, n_heads),
                     precision=lax.Precision.HIGHEST)
    tab_t = tab.transpose(0, 2, 1)
    return tab.transpose(2, 0, 1), tab_t[:, :, :blk], tab_t[:, :, blk:blk + 1]


def kernel(x_prompt, x_sample, cache_k_w128, cache_v_w128, cache_k_w512, cache_v_w512, cache_k_w2048, cache_v_w2048, state_lru_h, state_lru_conv, state_ret, state_gla, norm_pre, norm_post, rel_bias, a_w_in, a_w_out, b_w_in, b_conv_w, b_conv_b, b_gate_a_w, b_gate_a_b, b_gate_x_w, b_gate_x_b, b_lambda, b_w_out, c_w_in, c_norm, c_w_out, d_w_in, d_gate_w, d_gate_b, d_norm, d_w_out):
    b, s, d = x_prompt.shape
    bs = x_sample.shape[0]
    assert x_sample.shape[1] == 1, "one new token per sequence"
    depth = norm_pre.shape[0]
    k_caches = (cache_k_w128, cache_k_w512, cache_k_w2048)
    v_caches = (cache_v_w128, cache_v_w512, cache_v_w2048)
    ng = len(A_GROUPS)
    for (window, dil), kc in zip(A_GROUPS, k_caches):
        assert kc.shape[2] == window and s % (dil * A_BLOCK) == 0

    xp = x_prompt.reshape(b * s, d)
    xs = x_sample.reshape(bs, d)
    hp = norm_cast(xp, norm_pre[0])
    hs = norm_cast(xs, norm_pre[0])

    kp_rows = [[] for _ in A_GROUPS]
    vp_rows = [[] for _ in A_GROUPS]
    ks_rows = [[] for _ in A_GROUPS]
    vs_rows = [[] for _ in A_GROUPS]
    lru_h_p, lru_h_s, lru_c_p, lru_c_s = [], [], [], []
    ret_p, ret_s, gla_p, gla_s = [], [], [], []

    for i in range(depth):
        kind, j = i % 4, i // 4
        if kind == 0:
            n_heads = a_w_out.shape[1] // A_HEAD_DIM
            width = n_heads * A_HEAD_DIM
            full = [min(window, s) == s for window, _ in A_GROUPS]
            own = [which * ng + g for which in (1, 2) for g in range(ng) if full[g]]
            shared = [u for u in range(3 * ng + 1) if u not in own]
            unit_p, unit_s = {}, {}
            for group in [shared] + [[u] for u in own]:
                arr_p, arr_s = matmul(hp, hs, a_w_in, j, units=tuple(group), unit=width)
                for pos, u in enumerate(group):
                    unit_p[u] = (arr_p.reshape(b, s, -1), pos)
                    unit_s[u] = arr_s[:, pos * width:(pos + 1) * width]
            bias_tab, bias_past, bias_self = _attention_bias_tables(rel_bias, n_heads)
            yp, *tails = attn_prompt([unit_p[g] for g in range(ng)], [unit_p[ng + g] for g in range(ng)],
                                     [unit_p[2 * ng + g] for g in range(ng)], unit_p[3 * ng],
                                     bias_tab, n_heads)
            yp = yp.reshape(b * s, width)
            ys = attn_sample([unit_s[g] for g in range(ng)], [unit_s[ng + g] for g in range(ng)],
                             [unit_s[2 * ng + g] for g in range(ng)], unit_s[3 * ng],
                             [c[j] for c in k_caches], [c[j] for c in v_caches],
                             bias_past, bias_self, n_heads)
            for g in range(ng):
                if full[g]:
                    k_rows, v_rows = unit_p[ng + g][0], unit_p[2 * ng + g][0]
                else:
                    k_rows, v_rows = tails.pop(0), tails.pop(0)
                kp_rows[g].append(k_rows.reshape(b, -1, n_heads, A_HEAD_DIM))
                vp_rows[g].append(v_rows.reshape(b, -1, n_heads, A_HEAD_DIM))
                ks_rows[g].append(unit_s[ng + g].reshape(bs, 1, n_heads, A_HEAD_DIM))
                vs_rows[g].append(unit_s[2 * ng + g].reshape(bs, 1, n_heads, A_HEAD_DIM))
            w_out = a_w_out
        elif kind == 1:
            br = b_w_out.shape[1]
            proj, proj_s = matmul(hp, hs, b_w_in, j)
            prm = (b_conv_w[j], b_conv_b[j], b_gate_a_w[j], b_gate_a_b[j], b_gate_x_w[j], b_gate_x_b[j], b_lambda[j])
            proj3 = proj.reshape(b, s, -1)
            yp, h_last = lru_prompt(proj3, *prm)
            yp = yp.reshape(b * s, br)
            ys, h_new = lru_sample(proj_s, state_lru_conv[j], state_lru_h[j], *prm)
            cw = b_conv_w.shape[1]
            lru_h_p.append(h_last)
            lru_c_p.append(proj3[:, s - (cw - 1):, :br])
            lru_h_s.append(h_new)
            lru_c_s.append(jnp.concatenate([state_lru_conv[j], proj_s[:, None, :br]], axis=1)[:, 1:])
            w_out = b_w_out
        elif kind == 2:
            br = c_w_out.shape[1]
            n_heads = br // RET_DV
            proj, proj_s = matmul(hp, hs, c_w_in, j)
            yp, st = ret_prompt(proj.reshape(b, s, -1), c_norm[j], n_heads)
            yp = yp.reshape(b * s, br)
            ys, st_s = ret_sample(proj_s, c_norm[j], state_ret[j], n_heads)
            ret_p.append(st)
            ret_s.append(st_s)
            w_out = c_w_out
        else:
            br = d_w_out.shape[1]
            n_heads = br // GLA_DV
            proj, proj_s = matmul(hp, hs, jnp.swapaxes(d_w_in, 1, 2), j, transposed=True)
            yp, st = gla_prompt(proj.reshape(b, s, -1), d_gate_w[j], d_gate_b[j], d_norm[j], n_heads)
            yp = yp.reshape(b * s, br)
            ys, st_s = gla_sample(proj_s, d_gate_w[j], d_gate_b[j], d_norm[j], state_gla[j], n_heads)
            gla_p.append(st)
            gla_s.append(st_s)
            w_out = d_w_out
        op, os_ = matmul(yp, ys, w_out, j)
        g_next = norm_pre[i + 1] if i + 1 < depth else None
        xp, hp = residual_norm(xp, op, norm_post[i], g_next)
        xs, hs = residual_norm(xs, os_, norm_post[i], g_next)

    return (xp.reshape(b, s, d), xs.reshape(bs, 1, d),
            jnp.stack(kp_rows[0]), jnp.stack(ks_rows[0]), jnp.stack(vp_rows[0]), jnp.stack(vs_rows[0]),
            jnp.stack(kp_rows[1]), jnp.stack(ks_rows[1]), jnp.stack(vp_rows[1]), jnp.stack(vs_rows[1]),
            jnp.stack(kp_rows[2]), jnp.stack(ks_rows[2]), jnp.stack(vp_rows[2]), jnp.stack(vs_rows[2]),
            jnp.stack(lru_h_p), jnp.stack(lru_h_s), jnp.stack(lru_c_p), jnp.stack(lru_c_s),
            jnp.stack(ret_p), jnp.stack(ret_s), jnp.stack(gla_p), jnp.stack(gla_s))
```
